```python
import math
import numpy as np
import jax
import jax.numpy as jnp
from jax import lax

D_MODEL = 1024
BATCH = 4
SEQ = 4096
DEPTH = 1

CHUNK = 64
Q_BLOCK = 128
NORM_EPS = 1e-6
MLA_HEADS = 4
MLA_Q_RANK = 384
MLA_KV_RANK = 256
MLA_NOPE = 128
MLA_ROPE = 64
MLA_V = 128
ROPE_THETA = 10000.0
GDN_HEADS = 4
GDN_DK = 128
GDN_DV = 128
GDN_CONV = 4
GDN_QKV = GDN_HEADS * (2 * GDN_DK + GDN_DV)
MIX_WIDTH = MLA_HEADS * MLA_V + GDN_HEADS * GDN_DV
IN_SPLITS = (MLA_Q_RANK, MLA_KV_RANK, MLA_ROPE, GDN_QKV, GDN_HEADS * GDN_DV, GDN_HEADS, GDN_HEADS)
D_IN = MLA_Q_RANK + MLA_KV_RANK + MLA_ROPE + GDN_QKV + GDN_HEADS * GDN_DV + 2 * GDN_HEADS
N_EXPERTS = 64
N_GROUPS = 8
TOPK_GROUPS = 4
TOP_K = 8
D_EXPERT = 256
D_SHARED = 256
ROUTED_SCALE = 2.5
MOE_BLOCK = 128

kernel_name = 'hybrid_mla_gdn_moe_adaln'


def rms_norm(x, w):
    xf = x.astype(jnp.float32)
    y = xf * lax.rsqrt(jnp.mean(xf * xf, axis=-1, keepdims=True) + NORM_EPS)
    return (y * w.astype(jnp.float32)).astype(x.dtype)


def l2norm(x):
    return x * lax.rsqrt(jnp.sum(x * x, axis=-1, keepdims=True) + 1e-6)


def apply_rope(x, cos, sin):
    x1, x2 = jnp.split(x.astype(jnp.float32), 2, axis=-1)
    return jnp.concatenate([x1 * cos - x2 * sin, x2 * cos + x1 * sin], axis=-1).astype(x.dtype)


def causal_depthwise_conv(x, w):
    K, C = w.shape
    return lax.conv_general_dilated(x, w.reshape(K, 1, C).astype(x.dtype), window_strides=(1,),
                                    padding=((K - 1, 0),), dimension_numbers=('NWC', 'WIO', 'NWC'),
                                    feature_group_count=C)


def mla_group(c_q, c_kv, k_pe, positions, q_a_norm_w, wq_b, kv_a_norm_w, wkv_b):
    B, S, _ = c_q.shape
    q = (rms_norm(c_q, q_a_norm_w) @ wq_b).reshape(B, S, MLA_HEADS, MLA_NOPE + MLA_ROPE)
    q_nope, q_pe = q[..., :MLA_NOPE], q[..., MLA_NOPE:]
    kv = (rms_norm(c_kv, kv_a_norm_w) @ wkv_b).reshape(B, S, MLA_HEADS, MLA_NOPE + MLA_V)
    k_nope, v = kv[..., :MLA_NOPE], kv[..., MLA_NOPE:]
    inv_freq = 1.0 / (ROPE_THETA ** (jnp.arange(0, MLA_ROPE, 2, dtype=jnp.float32) / MLA_ROPE))
    ang = positions.astype(jnp.float32)[..., None] * inv_freq
    cos, sin = jnp.cos(ang), jnp.sin(ang)
    q_pe = apply_rope(q_pe, cos[:, :, None, :], sin[:, :, None, :])
    k_pe = apply_rope(k_pe, cos, sin)
    scale = (MLA_NOPE + MLA_ROPE) ** -0.5
    chunk_id = jnp.arange(S) // CHUNK
    outs = []
    for i in range(S // Q_BLOCK):
        q0, q1 = i * Q_BLOCK, (i + 1) * Q_BLOCK
        s = (jnp.einsum('bqhd,bkhd->bhqk', q_nope[:, q0:q1], k_nope[:, :q1])
             + jnp.einsum('bqhr,bkr->bhqk', q_pe[:, q0:q1], k_pe[:, :q1])).astype(jnp.float32) * scale
        mask = chunk_id[None, :q1] <= chunk_id[q0:q1, None]
        p = jax.nn.softmax(jnp.where(mask, s, -jnp.inf), axis=-1).astype(v.dtype)
        outs.append(jnp.einsum('bhqk,bkhd->bqhd', p, v[:, :q1]))
    o = jnp.concatenate(outs, axis=1)
    return o.reshape(B, S, MLA_HEADS * MLA_V)


def gated_delta_chunked(q, k, v, g, beta):
    B, S, H, DK = q.shape
    DV = v.shape[-1]
    N = S // CHUNK

    def to_chunks(t):
        return jnp.moveaxis(t.reshape((B, N, CHUNK, H) + t.shape[3:]), 3, 1)

    q = to_chunks(q) * (DK ** -0.5)
    k = to_chunks(k)
    v = to_chunks(v)
    beta = to_chunks(beta)
    g = jnp.cumsum(to_chunks(g), axis=-1)
    idx = jnp.arange(CHUNK)
    incl = idx[:, None] >= idx[None, :]
    strict = idx[:, None] > idx[None, :]
    decay = jnp.exp(jnp.where(incl, g[..., :, None] - g[..., None, :], -jnp.inf))
    kb = k * beta[..., None]
    vb = v * beta[..., None]
    L = jnp.where(strict, jnp.einsum('bhncd,bhnmd->bhncm', kb, k) * decay, 0.0)
    eye = jnp.eye(CHUNK, dtype=jnp.float32)
    T = lax.linalg.triangular_solve(eye + L, jnp.broadcast_to(eye, L.shape), left_side=True,
                                    lower=True, unit_diagonal=True)
    u = T @ vb
    w = T @ (kb * jnp.exp(g)[..., None])
    A = jnp.einsum('bhncd,bhnmd->bhncm', q, k) * decay
    qg = q * jnp.exp(g)[..., None]
    kd = k * jnp.exp(g[..., -1:] - g)[..., None]
    g_last = jnp.exp(g[..., -1])

    def step(state, xs):
        qg_c, kd_c, u_c, w_c, A_c, gl = xs
        v_new = u_c - jnp.einsum('bhcd,bhde->bhce', w_c, state)
        o = jnp.einsum('bhcd,bhde->bhce', qg_c, state) + jnp.einsum('bhcm,bhme->bhce', A_c, v_new)
        state = state * gl[..., None, None] + jnp.einsum('bhcd,bhce->bhde', kd_c, v_new)
        return state, o

    xs = tuple(jnp.moveaxis(t, 2, 0) for t in (qg, kd, u, w, A, g_last))
    state0 = jnp.zeros((B, H, DK, DV), jnp.float32)
    _, o = lax.scan(step, state0, xs)
    o = jnp.moveaxis(o, 0, 2)
    return jnp.moveaxis(o, 1, 3).reshape(B, S, H, DV)


def gdn_group(qkv, z, a, b, conv_w, a_log, dt_bias, norm_w):
    B, S, _ = qkv.shape
    qkv = jax.nn.silu(causal_depthwise_conv(qkv, conv_w)).astype(jnp.float32)
    q, k, v = jnp.split(qkv, [GDN_HEADS * GDN_DK, 2 * GDN_HEADS * GDN_DK], axis=-1)
    q = l2norm(q.reshape(B, S, GDN_HEADS, GDN_DK))
    k = l2norm(k.reshape(B, S, GDN_HEADS, GDN_DK))
    v = v.reshape(B, S, GDN_HEADS, GDN_DV)
    beta = jax.nn.sigmoid(b.astype(jnp.float32))
    g = -jnp.exp(a_log.astype(jnp.float32)) * jax.nn.softplus(a.astype(jnp.float32) + dt_bias.astype(jnp.float32))
    o = gated_delta_chunked(q, k, v, g, beta)
    o = rms_norm(o, norm_w) * jax.nn.silu(z.astype(jnp.float32).reshape(B, S, GDN_HEADS, GDN_DV))
    return o.reshape(B, S, GDN_HEADS * GDN_DV).astype(z.dtype)


def moe(h, router_w, router_bias, w_gate, w_up, w_down, sh_gate, sh_up, sh_down):
    B, S, D = h.shape
    xt = h.reshape(-1, D)
    N = xt.shape[0]
    scores = jax.nn.sigmoid((xt @ router_w).astype(jnp.float32))
    biased = scores + router_bias.astype(jnp.float32)
    grp = biased.reshape(N, N_GROUPS, N_EXPERTS // N_GROUPS)
    grp_score = lax.top_k(grp, 2)[0].sum(-1)
    _, top_g = lax.top_k(grp_score, TOPK_GROUPS)
    gmask = jnp.any(top_g[:, :, None] == jnp.arange(N_GROUPS)[None, None, :], axis=1)
    emask = jnp.repeat(gmask, N_EXPERTS // N_GROUPS, axis=1)
    _, top_e = lax.top_k(jnp.where(emask, biased, -jnp.inf), TOP_K)
    wts = jnp.take_along_axis(scores, top_e, axis=1)
    wts = wts / jnp.sum(wts, axis=-1, keepdims=True) * ROUTED_SCALE
    NK = N * TOP_K
    flat_e = top_e.reshape(-1).astype(jnp.int32)
    flat_tok = jnp.arange(NK, dtype=jnp.int32) // TOP_K
    flat_w = wts.reshape(-1)
    order = jnp.argsort(flat_e, stable=True)
    sorted_e = flat_e[order]
    counts = jnp.bincount(flat_e, length=N_EXPERTS).astype(jnp.int32)
    padded = (counts + MOE_BLOCK - 1) // MOE_BLOCK * MOE_BLOCK
    pad_end = jnp.cumsum(padded)
    pad_start = pad_end - padded
    start = jnp.cumsum(counts) - counts
    dest = pad_start[sorted_e] + jnp.arange(NK, dtype=jnp.int32) - start[sorted_e]
    n_blocks = -(-NK // MOE_BLOCK) + N_EXPERTS
    P = n_blocks * MOE_BLOCK
    buf_tok = jnp.zeros((P,), jnp.int32).at[dest].set(flat_tok[order])
    buf_w = jnp.zeros((P,), jnp.float32).at[dest].set(flat_w[order])
    block_e = jnp.clip(jnp.searchsorted(pad_end, jnp.arange(n_blocks, dtype=jnp.int32) * MOE_BLOCK, side='right'),
                       0, N_EXPERTS - 1).astype(jnp.int32)

    def body(acc, blk):
        tok, wt, e = blk
        xb = xt[tok]
        hb = jax.nn.silu(xb @ w_gate[e]) * (xb @ w_up[e])
        yb = (hb @ w_down[e]).astype(jnp.float32)
        return acc.at[tok].add(yb * wt[:, None]), None

    acc, _ = lax.scan(body, jnp.zeros((N, D), jnp.float32),
                      (buf_tok.reshape(n_blocks, MOE_BLOCK), buf_w.reshape(n_blocks, MOE_BLOCK), block_e))
    shared = (jax.nn.silu(xt @ sh_gate) * (xt @ sh_up)) @ sh_down
    return (acc + shared.astype(jnp.float32)).astype(h.dtype).reshape(B, S, D)


def setup_inputs(seed: int = 0) -> dict:
    key = jax.random.key(seed)
    ks = jax.random.split(key, 27)
    f32 = jnp.float32
    L, D = DEPTH, D_MODEL

    def nrm(k, shape, fan_in):
        return jax.random.normal(k, shape, f32) * (fan_in ** -0.5)

    def gain(k, shape):
        return 1.0 + 0.02 * jax.random.normal(k, shape, f32)

    x = jax.random.normal(ks[0], (BATCH, SEQ, D), f32)
    c = jax.random.normal(ks[1], (BATCH, D), f32)
    offset = jax.random.randint(ks[2], (BATCH, 1), 0, 64, dtype=jnp.int32) * CHUNK
    positions = offset + jnp.arange(SEQ, dtype=jnp.int32)[None, :]
    ada_w = nrm(ks[3], (L, D, 6 * D), D) * 0.5
    ada_b = 0.02 * jax.random.normal(ks[4], (L, 6 * D), f32)
    norm1_w = gain(ks[5], (L, D))
    w_in = nrm(ks[6], (L, D, D_IN), D)
    q_a_norm_w = gain(ks[7], (L, MLA_Q_RANK))
    wq_b = nrm(ks[8], (L, MLA_Q_RANK, MLA_HEADS * (MLA_NOPE + MLA_ROPE)), MLA_Q_RANK)
    kv_a_norm_w = gain(ks[9], (L, MLA_KV_RANK))
    wkv_b = nrm(ks[10], (L, MLA_KV_RANK, MLA_HEADS * (MLA_NOPE + MLA_V)), MLA_KV_RANK)
    gdn_conv_w = nrm(ks[11], (L, GDN_CONV, GDN_QKV), GDN_CONV)
    gdn_a_log = jnp.log(jax.random.uniform(ks[12], (L, GDN_HEADS), f32, 1.0, 16.0))
    dt = jnp.exp(jax.random.uniform(ks[13], (L, GDN_HEADS), f32, math.log(1e-3), math.log(1e-1)))
    gdn_dt_bias = dt + jnp.log(-jnp.expm1(-dt))
    gdn_norm_w = gain(ks[14], (L, GDN_DV))
    w_out = nrm(ks[15], (L, MIX_WIDTH, D), MIX_WIDTH)
    norm2_w = gain(ks[16], (L, D))
    router_w = nrm(ks[17], (L, D, N_EXPERTS), D)
    router_bias = 0.01 * jax.random.normal(ks[18], (L, N_EXPERTS), f32)
    exp_w_gate = nrm(ks[19], (L, N_EXPERTS, D, D_EXPERT), D)
    exp_w_up = nrm(ks[20], (L, N_EXPERTS, D, D_EXPERT), D)
    exp_w_down = nrm(ks[21], (L, N_EXPERTS, D_EXPERT, D), D_EXPERT)
    sh_w_gate = nrm(ks[22], (L, D, D_SHARED), D)
    sh_w_up = nrm(ks[23], (L, D, D_SHARED), D)
    sh_w_down = nrm(ks[24], (L, D_SHARED, D), D_SHARED)
    final_norm_w = gain(ks[25], (D,))
    return {'x': x, 'c': c, 'positions': positions, 'ada_w': ada_w, 'ada_b': ada_b, 'norm1_w': norm1_w,
            'w_in': w_in, 'q_a_norm_w': q_a_norm_w, 'wq_b': wq_b, 'kv_a_norm_w': kv_a_norm_w, 'wkv_b': wkv_b,
            'gdn_conv_w': gdn_conv_w, 'gdn_a_log': gdn_a_log, 'gdn_dt_bias': gdn_dt_bias, 'gdn_norm_w': gdn_norm_w,
            'w_out': w_out, 'norm2_w': norm2_w, 'router_w': router_w, 'router_bias': router_bias,
            'exp_w_gate': exp_w_gate, 'exp_w_up': exp_w_up, 'exp_w_down': exp_w_down,
            'sh_w_gate': sh_w_gate, 'sh_w_up': sh_w_up, 'sh_w_down': sh_w_down, 'final_norm_w': final_norm_w}


def reference(x, c, positions, ada_w, ada_b, norm1_w, w_in, q_a_norm_w, wq_b, kv_a_norm_w, wkv_b,
              gdn_conv_w, gdn_a_log, gdn_dt_bias, gdn_norm_w, w_out, norm2_w, router_w, router_bias,
              exp_w_gate, exp_w_up, exp_w_down, sh_w_gate, sh_w_up, sh_w_down, final_norm_w):
    split_at = np.cumsum(IN_SPLITS)[:-1].tolist()
    for l in range(DEPTH):
        mod = jax.nn.silu(c) @ ada_w[l] + ada_b[l]
        sh1, sc1, g1, sh2, sc2, g2 = [m[:, None, :] for m in jnp.split(mod, 6, axis=-1)]
        h = rms_norm(x, norm1_w[l]) * (1.0 + sc1) + sh1
        proj = h @ w_in[l]
        c_q, c_kv, k_pe, qkv, z, a, b = jnp.split(proj, split_at, axis=-1)
        out_a = mla_group(c_q, c_kv, k_pe, positions, q_a_norm_w[l], wq_b[l], kv_a_norm_w[l], wkv_b[l])
        out_b = gdn_group(qkv, z, a, b, gdn_conv_w[l], gdn_a_log[l], gdn_dt_bias[l], gdn_norm_w[l])
        mix = jnp.concatenate([out_a, out_b], axis=-1) @ w_out[l]
        x = x + g1 * mix
        h2 = rms_norm(x, norm2_w[l]) * (1.0 + sc2) + sh2
        x = x + g2 * moe(h2, router_w[l], router_bias[l], exp_w_gate[l], exp_w_up[l], exp_w_down[l],
                         sh_w_gate[l], sh_w_up[l], sh_w_down[l])
    return rms_norm(x, final_norm_w)
```

```python
import functools

import jax
import jax.numpy as jnp
from jax import lax
from jax.experimental import pallas as pl
from jax.experimental.pallas import tpu as pltpu

F32 = jnp.float32
BF16 = jnp.bfloat16
HIGHEST = lax.Precision.HIGHEST

CHUNK = 64
NORM_EPS = 1e-6
MLA_HEADS = 4
MLA_Q_RANK = 384
MLA_KV_RANK = 256
MLA_NOPE = 128
MLA_ROPE = 64
MLA_V = 128
ROPE_THETA = 10000.0
GDN_HEADS = 4
GDN_DK = 128
GDN_DV = 128
GDN_CONV = 4
N_EXPERTS = 64
N_GROUPS = 8
TOPK_GROUPS = 4
TOP_K = 8
ROUTED_SCALE = 2.5

LANES = 128
MLA_QK_PAD = 256
VMEM_LIMIT = 56 * 1024 * 1024


def _cparams(*sem):
    return pltpu.CompilerParams(dimension_semantics=sem, vmem_limit_bytes=VMEM_LIMIT)


def _dot(a, b):
    return jnp.dot(a, b, preferred_element_type=F32)


def _dot_f32(a, b):
    return jnp.dot(a, b, preferred_element_type=F32, precision=HIGHEST)


def _dot_nt(a, b, precision=None):
    return lax.dot_general(a, b, (((1,), (1,)), ((), ())), preferred_element_type=F32, precision=precision)


def _dot_tn(a, b):
    return lax.dot_general(a, b, (((0,), (0,)), ((), ())), preferred_element_type=F32)


def _silu(x):
    return x * jax.nn.sigmoid(x)


def _rms(x, w):
    return x * lax.rsqrt(jnp.mean(x * x, axis=-1, keepdims=True) + NORM_EPS) * w


def _adaln_kernel(c_ref, w_ref, b_ref, o_ref):
    a = _silu(c_ref[...]).astype(BF16)
    o_ref[...] = _dot(a, w_ref[...].astype(BF16)) + b_ref[...]


def _adaln(c8, ada_w, ada_b):
    d = c8.shape[1]
    n_out = ada_w.shape[1]
    return pl.pallas_call(
        _adaln_kernel,
        grid=(n_out // d,),
        in_specs=[pl.BlockSpec((8, d), lambda j: (0, 0)),
                  pl.BlockSpec((d, d), lambda j: (0, j)),
                  pl.BlockSpec((1, d), lambda j: (0, j))],
        out_specs=pl.BlockSpec((8, d), lambda j: (0, j)),
        out_shape=jax.ShapeDtypeStruct((8, n_out), F32),
        compiler_params=_cparams("parallel"),
        name="adaln",
    )(c8, ada_w, ada_b)


_C_Q = 0
_C_KV = _C_Q + MLA_Q_RANK
_C_KPE = _C_KV + MLA_KV_RANK
_C_QKV = _C_KPE + LANES
_GDN_QKV = GDN_HEADS * (2 * GDN_DK + GDN_DV)
_C_Z = _C_QKV + _GDN_QKV
_C_AB = _C_Z + GDN_HEADS * GDN_DV
_D_IN_PAD = _C_AB + LANES


def _rope(xb, cos, sin):
    half = MLA_ROPE // 2
    lane = lax.broadcasted_iota(jnp.int32, xb.shape, 1)
    rot = jnp.where(lane < half, -pltpu.roll(xb, LANES - half, 1), pltpu.roll(xb, half, 1))
    return xb * cos + rot * sin


def _proj_kernel(x_ref, mod_ref, pos_ref, n1_ref, win_ref, qan_ref, wq_ref, kvan_ref, wkv_ref, invf_ref,
                 q_ref, k_ref, v_ref, qkv_ref, z_ref, ab_ref):
    x = x_ref[...]
    mod = mod_ref[0]
    sh1, sc1 = mod[0:1], mod[1:2]
    h = (_rms(x, n1_ref[...]) * (1.0 + sc1) + sh1).astype(BF16)
    proj = _dot(h, win_ref[...])

    ang = pos_ref[...].astype(F32) * invf_ref[...]
    cos, sin = jnp.cos(ang), jnp.sin(ang)

    cq = proj[:, _C_Q:_C_Q + MLA_Q_RANK]
    qn = _rms(cq, qan_ref[...]).astype(BF16)
    q = _dot(qn, wq_ref[...]) * ((MLA_NOPE + MLA_ROPE) ** -0.5)
    for hh in range(MLA_HEADS):
        c0 = hh * MLA_QK_PAD
        q_ref[:, c0:c0 + MLA_NOPE] = q[:, c0:c0 + MLA_NOPE].astype(BF16)
        q_ref[:, c0 + MLA_NOPE:c0 + MLA_QK_PAD] = _rope(q[:, c0 + MLA_NOPE:c0 + MLA_QK_PAD], cos, sin).astype(BF16)

    ckv = proj[:, _C_KV:_C_KV + MLA_KV_RANK]
    kvn = _rms(ckv, kvan_ref[...]).astype(BF16)
    kv = _dot(kvn, wkv_ref[...])
    kpe = _rope(proj[:, _C_KPE:_C_KPE + LANES], cos, sin).astype(BF16)
    for hh in range(MLA_HEADS):
        c0 = hh * MLA_QK_PAD
        k_ref[:, c0:c0 + MLA_NOPE] = kv[:, hh * MLA_NOPE:(hh + 1) * MLA_NOPE].astype(BF16)
        k_ref[:, c0 + MLA_NOPE:c0 + MLA_QK_PAD] = kpe
    v_ref[...] = kv[:, MLA_HEADS * MLA_NOPE:].astype(BF16)

    qkv_ref[...] = proj[:, _C_QKV:_C_QKV + _GDN_QKV]
    z_ref[...] = proj[:, _C_Z:_C_Z + GDN_HEADS * GDN_DV]
    ab_ref[...] = proj[:, _C_AB:_C_AB + LANES]


def _proj(x2, mod8, pos2, n1, win, qan, wq, kvan, wkv, invf, *, seq, tm):
    n, d = x2.shape
    per_b = seq // tm
    row = lambda i: (i, 0)
    const = lambda i: (0, 0)
    hq = MLA_HEADS * MLA_QK_PAD
    hv = MLA_HEADS * MLA_V
    return pl.pallas_call(
        _proj_kernel,
        grid=(n // tm,),
        in_specs=[pl.BlockSpec((tm, d), row),
                  pl.BlockSpec((1, 8, d), lambda i: (i // per_b, 0, 0)),
                  pl.BlockSpec((tm, 1), row),
                  pl.BlockSpec((1, d), const),
                  pl.BlockSpec(win.shape, const),
                  pl.BlockSpec(qan.shape, const),
                  pl.BlockSpec(wq.shape, const),
                  pl.BlockSpec(kvan.shape, const),
                  pl.BlockSpec(wkv.shape, const),
                  pl.BlockSpec(invf.shape, const)],
        out_specs=[pl.BlockSpec((tm, hq), row), pl.BlockSpec((tm, hq), row), pl.BlockSpec((tm, hv), row),
                   pl.BlockSpec((tm, _GDN_QKV), row), pl.BlockSpec((tm, GDN_HEADS * GDN_DV), row),
                   pl.BlockSpec((tm, LANES), row)],
        out_shape=[jax.ShapeDtypeStruct((n, hq), BF16), jax.ShapeDtypeStruct((n, hq), BF16),
                   jax.ShapeDtypeStruct((n, hv), BF16), jax.ShapeDtypeStruct((n, _GDN_QKV), F32),
                   jax.ShapeDtypeStruct((n, GDN_HEADS * GDN_DV), F32), jax.ShapeDtypeStruct((n, LANES), F32)],
        compiler_params=_cparams("parallel"),
        name="proj",
    )(x2, mod8, pos2, n1, win, qan, wq, kvan, wkv, invf)


def _attn_kernel(q_ref, k_ref, v_ref, o_ref, *, tq):
    i = pl.program_id(2)
    q = q_ref[...]

    def step(kj, vj, carry, mask):
        m, l, acc = carry
        s = _dot_nt(q, kj)
        if mask is not None:
            s = jnp.where(mask, s, -jnp.inf)
        m_new = jnp.maximum(m, jnp.max(s, axis=-1, keepdims=True))
        alpha = jnp.exp(m - m_new)
        p = jnp.exp(s - m_new)
        l = alpha * l + jnp.sum(p, axis=-1, keepdims=True)
        acc = alpha * acc + _dot(p.astype(BF16), vj)
        return m_new, l, acc

    def body(j, carry):
        r0 = pl.multiple_of(j * tq, tq)
        return step(k_ref[pl.ds(r0, tq), :], v_ref[pl.ds(r0, tq), :], carry, None)

    init = (jnp.full((tq, 1), -jnp.inf, F32), jnp.zeros((tq, 1), F32), jnp.zeros((tq, MLA_V), F32))
    carry = lax.fori_loop(0, i, body, init)
    r0 = pl.multiple_of(i * tq, tq)
    rq = lax.broadcasted_iota(jnp.int32, (tq, tq), 0) // CHUNK
    ck = lax.broadcasted_iota(jnp.int32, (tq, tq), 1) // CHUNK
    m, l, acc = step(k_ref[pl.ds(r0, tq), :], v_ref[pl.ds(r0, tq), :], carry, ck <= rq)
    o_ref[...] = (acc / l).astype(o_ref.dtype)


def _attn(q, k, v, *, batch, seq, tq):
    n = q.shape[0]
    nq = seq // tq
    return pl.pallas_call(
        functools.partial(_attn_kernel, tq=tq),
        grid=(batch, MLA_HEADS, nq),
        in_specs=[pl.BlockSpec((tq, MLA_QK_PAD), lambda b, h, i: (b * nq + i, h)),
                  pl.BlockSpec((seq, MLA_QK_PAD), lambda b, h, i: (b, h)),
                  pl.BlockSpec((seq, MLA_V), lambda b, h, i: (b, h))],
        out_specs=pl.BlockSpec((tq, MLA_V), lambda b, h, i: (b * nq + i, h)),
        out_shape=jax.ShapeDtypeStruct((n, MLA_HEADS * MLA_V), BF16),
        compiler_params=_cparams("parallel", "parallel", "arbitrary"),
        name="attn",
    )(q, k, v)


_SUPER = 2 * CHUNK


def _unit_lower_inverse(low):
    n = low.shape[0]
    eye = (lax.broadcasted_iota(jnp.int32, (n, n), 0) == lax.broadcasted_iota(jnp.int32, (n, n), 1)).astype(F32)
    p = -low
    t = eye + p
    levels = CHUNK.bit_length() - 1
    for _ in range(levels - 1):
        p = _dot_f32(p, p)
        t = t + _dot_f32(t, p)
    return t


def _gdn_kernel(qkv_ref, z_ref, ab_ref, cw_ref, hp_ref, nw_ref, o_ref, xe_ref, st_ref, *, tt):
    j = pl.program_id(1)
    hd = GDN_HEADS * GDN_DK

    @pl.when(j == 0)
    def _():
        xe_ref[0:8, :] = jnp.zeros((8, xe_ref.shape[1]), F32)
        st_ref[...] = jnp.zeros(st_ref.shape, F32)

    xe_ref[8:8 + tt, :] = qkv_ref[...]
    cw = cw_ref[...]
    y = xe_ref[8:8 + tt, :] * cw[GDN_CONV - 1:GDN_CONV]
    for i in range(1, GDN_CONV):
        y = y + xe_ref[8 - i:8 - i + tt, :] * cw[GDN_CONV - 1 - i:GDN_CONV - i]
    xe_ref[0:8, :] = xe_ref[tt:tt + 8, :]
    act = _silu(y)

    ab = ab_ref[...]
    hp = hp_ref[...]
    pre = ab + hp[1:2]
    softplus = jnp.maximum(pre, 0.0) + jnp.log1p(jnp.exp(-jnp.abs(pre)))
    g_all = -jnp.exp(hp[0:1]) * softplus
    beta_all = jax.nn.sigmoid(ab)

    ri = lax.broadcasted_iota(jnp.int32, (tt, tt), 0)
    ci = lax.broadcasted_iota(jnp.int32, (tt, tt), 1)
    tri = ((ri // CHUNK == ci // CHUNK) & (ci <= ri)).astype(F32)
    gc = _dot_f32(tri, g_all)
    gct = gc.T

    rs = lax.broadcasted_iota(jnp.int32, (_SUPER, _SUPER), 0)
    cs = lax.broadcasted_iota(jnp.int32, (_SUPER, _SUPER), 1)
    same = rs // CHUNK == cs // CHUNK
    incl = same & (cs <= rs)
    strict = same & (cs < rs)

    for s in range(tt // _SUPER):
        r0 = s * _SUPER
        for hh in range(GDN_HEADS):
            qh = act[r0:r0 + _SUPER, hh * GDN_DK:(hh + 1) * GDN_DK]
            kh = act[r0:r0 + _SUPER, hd + hh * GDN_DK:hd + (hh + 1) * GDN_DK]
            vh = act[r0:r0 + _SUPER, 2 * hd + hh * GDN_DV:2 * hd + (hh + 1) * GDN_DV]
            qh = qh * lax.rsqrt(jnp.sum(qh * qh, axis=-1, keepdims=True) + 1e-6) * (GDN_DK ** -0.5)
            kh = kh * lax.rsqrt(jnp.sum(kh * kh, axis=-1, keepdims=True) + 1e-6)
            beta = beta_all[r0:r0 + _SUPER, GDN_HEADS + hh:GDN_HEADS + hh + 1]
            gcol = gc[r0:r0 + _SUPER, hh:hh + 1]
            grow = gct[hh:hh + 1, r0:r0 + _SUPER]
            decay = jnp.exp(jnp.where(incl, gcol - grow, -jnp.inf))
            kb = kh * beta
            vb = vh * beta
            low = jnp.where(strict, _dot_nt(kb, kh, HIGHEST) * decay, 0.0)
            tinv = _unit_lower_inverse(low)
            eg = jnp.exp(gcol)
            u = _dot_f32(tinv, vb)
            w = _dot_f32(tinv, kb * eg)
            att = _dot_nt(qh.astype(BF16), kh.astype(BF16)) * decay
            qg = qh * eg
            for c in range(_SUPER // CHUNK):
                a0 = c * CHUNK
                gl = gc[r0 + a0 + CHUNK - 1:r0 + a0 + CHUNK, hh:hh + 1]
                kd = kh[a0:a0 + CHUNK] * jnp.exp(gl - gcol[a0:a0 + CHUNK])
                state = st_ref[hh]
                sb = state.astype(BF16)
                v_new = u[a0:a0 + CHUNK] - _dot(w[a0:a0 + CHUNK].astype(BF16), sb)
                vnb = v_new.astype(BF16)
                o = (_dot(qg[a0:a0 + CHUNK].astype(BF16), sb)
                     + _dot(att[a0:a0 + CHUNK, a0:a0 + CHUNK].astype(BF16), vnb))
                st_ref[hh] = state * jnp.exp(gl) + _dot_tn(kd.astype(BF16), vnb)
                zz = z_ref[r0 + a0:r0 + a0 + CHUNK, hh * GDN_DV:(hh + 1) * GDN_DV]
                o_ref[r0 + a0:r0 + a0 + CHUNK, hh * GDN_DV:(hh + 1) * GDN_DV] = (
                    _rms(o, nw_ref[...]) * _silu(zz)).astype(o_ref.dtype)


def _gdn(qkv, z, ab, cw8, hp, nw, *, batch, seq, tt):
    n = qkv.shape[0]
    per_b = seq // tt
    row = lambda b, j: (b * per_b + j, 0)
    const = lambda b, j: (0, 0)
    hv = GDN_HEADS * GDN_DV
    return pl.pallas_call(
        functools.partial(_gdn_kernel, tt=tt),
        grid=(batch, per_b),
        in_specs=[pl.BlockSpec((tt, _GDN_QKV), row), pl.BlockSpec((tt, hv), row), pl.BlockSpec((tt, LANES), row),
                  pl.BlockSpec(cw8.shape, const), pl.BlockSpec(hp.shape, const), pl.BlockSpec(nw.shape, const)],
        out_specs=pl.BlockSpec((tt, hv), row),
        out_shape=jax.ShapeDtypeStruct((n, hv), BF16),
        scratch_shapes=[pltpu.VMEM((tt + 8, _GDN_QKV), F32), pltpu.VMEM((GDN_HEADS, GDN_DK, GDN_DV), F32)],
        compiler_params=_cparams("parallel", "arbitrary"),
        name="gdn",
    )(qkv, z, ab, cw8, hp, nw)


def _route(logits_t, bias_col):
    e, tm = logits_t.shape
    gsz = e // N_GROUPS
    scores = jax.nn.sigmoid(logits_t)
    biased = scores + bias_col
    sub = lax.broadcasted_iota(jnp.int32, (gsz, tm), 0)
    rows = []
    for g in range(N_GROUPS):
        blk = biased[g * gsz:(g + 1) * gsz]
        m1 = jnp.max(blk, axis=0, keepdims=True)
        first = jnp.min(jnp.where(blk == m1, sub, gsz), axis=0, keepdims=True)
        m2 = jnp.max(jnp.where(sub == first, -jnp.inf, blk), axis=0, keepdims=True)
        rows.append(m1 + m2)
    gs = jnp.concatenate(rows, axis=0)
    gi = lax.broadcasted_iota(jnp.int32, (N_GROUPS, tm), 0)
    grank = jnp.zeros((N_GROUPS, tm), F32)
    for g in range(N_GROUPS):
        r = gs[g:g + 1]
        grank = grank + jnp.where(r > gs, 1.0, jnp.where(r == gs, (gi > g).astype(F32), 0.0))
    gsel = grank < TOPK_GROUPS
    masked = jnp.concatenate(
        [jnp.where(gsel[g:g + 1], biased[g * gsz:(g + 1) * gsz], -jnp.inf) for g in range(N_GROUPS)], axis=0)
    ei = lax.broadcasted_iota(jnp.int32, (e, tm), 0)
    rank = jnp.zeros((e, tm), F32)
    for k in range(e):
        r = masked[k:k + 1]
        rank = rank + jnp.where(r > masked, 1.0, jnp.where(r == masked, (ei > k).astype(F32), 0.0))
    ids, wts = [], []
    eif = ei.astype(F32)
    for k in range(TOP_K):
        hit = rank == float(k)
        ids.append(jnp.sum(jnp.where(hit, eif, 0.0), axis=0, keepdims=True))
        wts.append(jnp.sum(jnp.where(hit, scores, 0.0), axis=0, keepdims=True))
    ids = jnp.concatenate(ids, axis=0)
    wts = jnp.concatenate(wts, axis=0)
    wts = wts / jnp.sum(wts, axis=0, keepdims=True) * ROUTED_SCALE
    return ids.astype(jnp.int32), wts


def _mix_kernel(oa_ref, ob_ref, x_ref, mod_ref, woa_ref, wob_ref, n2_ref, rwt_ref, rb_ref, sg_ref, su_ref, sd_ref,
                xs_ref, h2_ref, te_ref, tw_ref):
    mod = mod_ref[0]
    g1, sh2, sc2, g2 = mod[2:3], mod[3:4], mod[4:5], mod[5:6]
    mix = _dot(oa_ref[...], woa_ref[...]) + _dot(ob_ref[...], wob_ref[...])
    x1 = x_ref[...] + g1 * mix
    h2 = _rms(x1, n2_ref[...]) * (1.0 + sc2) + sh2
    h2_ref[...] = h2
    h2b = h2.astype(BF16)
    hs = (_silu(_dot(h2b, sg_ref[...])) * _dot(h2b, su_ref[...])).astype(BF16)
    xs_ref[...] = x1 + g2 * _dot(hs, sd_ref[...])
    ids, wts = _route(_dot_nt(rwt_ref[...], h2, HIGHEST), rb_ref[...])
    te_ref[...] = ids
    tw_ref[...] = wts


def _mix(oa, ob, x2, mod8, woa, wob, n2, rwt, rb, sg, su, sd, *, seq, tm):
    n, d = x2.shape
    per_b = seq // tm
    row = lambda i: (i, 0)
    col = lambda i: (0, i)
    const = lambda i: (0, 0)
    return pl.pallas_call(
        _mix_kernel,
        grid=(n // tm,),
        in_specs=[pl.BlockSpec((tm, oa.shape[1]), row), pl.BlockSpec((tm, ob.shape[1]), row),
                  pl.BlockSpec((tm, d), row), pl.BlockSpec((1, 8, d), lambda i: (i // per_b, 0, 0)),
                  pl.BlockSpec(woa.shape, const), pl.BlockSpec(wob.shape, const), pl.BlockSpec((1, d), const),
                  pl.BlockSpec(rwt.shape, const), pl.BlockSpec(rb.shape, const),
                  pl.BlockSpec(sg.shape, const), pl.BlockSpec(su.shape, const), pl.BlockSpec(sd.shape, const)],
        out_specs=[pl.BlockSpec((tm, d), row), pl.BlockSpec((tm, d), row),
                   pl.BlockSpec((TOP_K, tm), col), pl.BlockSpec((TOP_K, tm), col)],
        out_shape=[jax.ShapeDtypeStruct((n, d), F32), jax.ShapeDtypeStruct((n, d), F32),
                   jax.ShapeDtypeStruct((TOP_K, n), jnp.int32), jax.ShapeDtypeStruct((TOP_K, n), F32)],
        compiler_params=_cparams("parallel"),
        name="mix",
    )(oa, ob, x2, mod8, woa, wob, n2, rwt, rb, sg, su, sd)


def _gather_rows(idx_ref, count, src_hbm, dst, sem):
    def issue(r, carry):
        pltpu.make_async_copy(src_hbm.at[pl.ds(idx_ref[0, 0, r], 1), :], dst.at[pl.ds(r, 1), :], sem).start()
        return carry

    lax.fori_loop(0, count, issue, 0, unroll=8)
    pltpu.make_async_copy(src_hbm.at[pl.ds(0, count), :], dst, sem).wait()


def _experts_kernel(be_ref, tok_ref, w_ref, h2_hbm, wg_ref, wu_ref, wd_ref, y_ref, xbuf, sem, *, blk):
    del be_ref
    _gather_rows(tok_ref, blk, h2_hbm, xbuf, sem)
    xb = xbuf[...].astype(BF16)
    hb = (_silu(_dot(xb, wg_ref[0])) * _dot(xb, wu_ref[0])).astype(BF16)
    y_ref[...] = _dot(hb, wd_ref[0]) * w_ref[...]


def _experts(block_e, buf_tok, buf_w, h2, wg, wu, wd, *, blk):
    n_blocks = block_e.shape[0]
    d = h2.shape[1]
    wsel = lambda i, be: (be[i], 0, 0)
    grid_spec = pltpu.PrefetchScalarGridSpec(
        num_scalar_prefetch=1,
        grid=(n_blocks,),
        in_specs=[pl.BlockSpec((1, 1, blk), lambda i, be: (i, 0, 0), memory_space=pltpu.SMEM),
                  pl.BlockSpec((blk, 1), lambda i, be: (i, 0)),
                  pl.BlockSpec(memory_space=pl.ANY),
                  pl.BlockSpec((1,) + wg.shape[1:], wsel), pl.BlockSpec((1,) + wu.shape[1:], wsel),
                  pl.BlockSpec((1,) + wd.shape[1:], wsel)],
        out_specs=pl.BlockSpec((blk, d), lambda i, be: (i, 0)),
        scratch_shapes=[pltpu.VMEM((blk, d), F32), pltpu.SemaphoreType.DMA(())],
    )
    return pl.pallas_call(
        functools.partial(_experts_kernel, blk=blk),
        grid_spec=grid_spec,
        out_shape=jax.ShapeDtypeStruct((n_blocks * blk, d), F32),
        compiler_params=_cparams("arbitrary"),
        name="experts",
    )(block_e, buf_tok, buf_w, h2, wg, wu, wd)


def _combine_kernel(pos_ref, xs_ref, mod_ref, fn_ref, y_hbm, o_ref, gbuf, sem, *, tm):
    _gather_rows(pos_ref, TOP_K * tm, y_hbm, gbuf, sem)
    acc = gbuf[0:tm, :]
    for k in range(1, TOP_K):
        acc = acc + gbuf[k * tm:(k + 1) * tm, :]
    g2 = mod_ref[0][5:6]
    o_ref[...] = _rms(xs_ref[...] + g2 * acc, fn_ref[...])


def _combine(pos3, xs, mod8, fn, y, *, seq, tm):
    n, d = xs.shape
    per_b = seq // tm
    return pl.pallas_call(
        functools.partial(_combine_kernel, tm=tm),
        grid=(n // tm,),
        in_specs=[pl.BlockSpec((1, 1, TOP_K * tm), lambda i: (i, 0, 0), memory_space=pltpu.SMEM),
                  pl.BlockSpec((tm, d), lambda i: (i, 0)),
                  pl.BlockSpec((1, 8, d), lambda i: (i // per_b, 0, 0)),
                  pl.BlockSpec((1, d), lambda i: (0, 0)),
                  pl.BlockSpec(memory_space=pl.ANY)],
        out_specs=pl.BlockSpec((tm, d), lambda i: (i, 0)),
        out_shape=jax.ShapeDtypeStruct((n, d), F32),
        scratch_shapes=[pltpu.VMEM((TOP_K * tm, d), F32), pltpu.SemaphoreType.DMA(())],
        compiler_params=_cparams("arbitrary"),
        name="combine",
    )(pos3, xs, mod8, fn, y)


def _dispatch_plan(top_e, wts, *, blk):
    n = top_e.shape[0]
    nk = n * TOP_K
    flat_e = top_e.reshape(-1)
    order = jnp.argsort(flat_e, stable=True)
    sorted_e = flat_e[order]
    counts = jnp.bincount(flat_e, length=N_EXPERTS).astype(jnp.int32)
    padded = (counts + blk - 1) // blk * blk
    pad_end = jnp.cumsum(padded)
    pad_start = pad_end - padded
    start = jnp.cumsum(counts) - counts
    dest = pad_start[sorted_e] + jnp.arange(nk, dtype=jnp.int32) - start[sorted_e]
    n_blocks = nk // blk + N_EXPERTS
    p = n_blocks * blk
    buf_tok = jnp.zeros((p,), jnp.int32).at[dest].set((order // TOP_K).astype(jnp.int32))
    buf_w = jnp.zeros((p,), F32).at[dest].set(wts.reshape(-1)[order])
    slot = jnp.zeros((nk,), jnp.int32).at[order].set(dest)
    block_e = jnp.clip(jnp.searchsorted(pad_end, jnp.arange(n_blocks, dtype=jnp.int32) * blk, side='right'),
                       0, N_EXPERTS - 1).astype(jnp.int32)
    return block_e, buf_tok, buf_w, slot.reshape(n, TOP_K)


def _pad_lanes(w, width):
    return jnp.pad(w, ((0, 0), (0, width - w.shape[1])))


def _layer(x, c, positions, ada_w, ada_b, norm1_w, w_in, q_a_norm_w, wq_b, kv_a_norm_w, wkv_b,
           gdn_conv_w, gdn_a_log, gdn_dt_bias, gdn_norm_w, w_out, norm2_w, router_w, router_bias,
           exp_w_gate, exp_w_up, exp_w_down, sh_w_gate, sh_w_up, sh_w_down,
           *, tm_proj, tq, tt, tm_mix, blk):
    batch, seq, d = x.shape
    n = batch * seq
    x2 = x.reshape(n, d)

    c8 = jnp.zeros((8, d), F32).at[:batch].set(c)
    mod = _adaln(c8, ada_w, ada_b.reshape(1, -1))
    mod8 = jnp.pad(mod[:batch].reshape(batch, 6, d), ((0, 0), (0, 2), (0, 0)))

    hq = MLA_NOPE + MLA_ROPE
    w_cq, w_ckv, w_kpe, w_qkv, w_z, w_a, w_b = jnp.split(
        w_in, [_C_KV, _C_KV + MLA_KV_RANK, _C_KV + MLA_KV_RANK + MLA_ROPE,
               _C_KV + MLA_KV_RANK + MLA_ROPE + _GDN_QKV,
               _C_KV + MLA_KV_RANK + MLA_ROPE + _GDN_QKV + GDN_HEADS * GDN_DV,
               _C_KV + MLA_KV_RANK + MLA_ROPE + _GDN_QKV + GDN_HEADS * GDN_DV + GDN_HEADS], axis=1)
    win = jnp.concatenate([w_cq, w_ckv, _pad_lanes(w_kpe, LANES), w_qkv, w_z,
                           _pad_lanes(jnp.concatenate([w_a, w_b], axis=1), LANES)], axis=1).astype(BF16)
    wq = jnp.pad(wq_b.reshape(MLA_Q_RANK, MLA_HEADS, hq),
                 ((0, 0), (0, 0), (0, MLA_QK_PAD - hq))).reshape(MLA_Q_RANK, MLA_HEADS * MLA_QK_PAD).astype(BF16)
    wkv4 = wkv_b.reshape(MLA_KV_RANK, MLA_HEADS, MLA_NOPE + MLA_V)
    wkv = jnp.concatenate([wkv4[:, :, :MLA_NOPE].reshape(MLA_KV_RANK, -1),
                           wkv4[:, :, MLA_NOPE:].reshape(MLA_KV_RANK, -1)], axis=1).astype(BF16)
    inv_freq = 1.0 / (ROPE_THETA ** (jnp.arange(0, MLA_ROPE, 2, dtype=F32) / MLA_ROPE))
    invf = _pad_lanes(jnp.concatenate([inv_freq, inv_freq])[None, :], LANES)

    q, k, v, qkv, z, ab = _proj(x2, mod8, positions.reshape(n, 1), norm1_w.reshape(1, d), win,
                                q_a_norm_w.reshape(1, -1), wq, kv_a_norm_w.reshape(1, -1), wkv, invf,
                                seq=seq, tm=tm_proj)
    out_a = _attn(q, k, v, batch=batch, seq=seq, tq=tq)

    cw8 = jnp.pad(gdn_conv_w, ((0, 8 - GDN_CONV), (0, 0)))
    hp = _pad_lanes(jnp.stack([gdn_a_log, gdn_dt_bias]), LANES)
    hp = jnp.pad(hp, ((0, 6), (0, 0)))
    out_b = _gdn(qkv, z, ab, cw8, hp, gdn_norm_w.reshape(1, -1), batch=batch, seq=seq, tt=tt)

    ha = MLA_HEADS * MLA_V
    xs, h2, te_t, tw_t = _mix(out_a, out_b, x2, mod8, w_out[:ha].astype(BF16), w_out[ha:].astype(BF16),
                              norm2_w.reshape(1, d), router_w.T, router_bias.reshape(-1, 1),
                              sh_w_gate.astype(BF16), sh_w_up.astype(BF16), sh_w_down.astype(BF16),
                              seq=seq, tm=tm_mix)

    block_e, buf_tok, buf_w, slot = _dispatch_plan(te_t.T, tw_t.T, blk=blk)
    n_blocks = block_e.shape[0]
    y = _experts(block_e, buf_tok.reshape(n_blocks, 1, blk), buf_w.reshape(-1, 1), h2,
                 exp_w_gate.astype(BF16), exp_w_up.astype(BF16), exp_w_down.astype(BF16), blk=blk)
    return xs, mod8, y, slot


def kernel(x, c, positions, ada_w, ada_b, norm1_w, w_in, q_a_norm_w, wq_b, kv_a_norm_w, wkv_b, gdn_conv_w,
           gdn_a_log, gdn_dt_bias, gdn_norm_w, w_out, norm2_w, router_w, router_bias, exp_w_gate, exp_w_up,
           exp_w_down, sh_w_gate, sh_w_up, sh_w_down, final_norm_w):
    batch, seq, d = x.shape
    assert ada_w.shape[0] == 1, "single layer"
    tm_c = 128
    xs, mod8, y, slot = _layer(
        x, c, positions, ada_w[0], ada_b[0], norm1_w[0], w_in[0], q_a_norm_w[0], wq_b[0], kv_a_norm_w[0],
        wkv_b[0], gdn_conv_w[0], gdn_a_log[0], gdn_dt_bias[0], gdn_norm_w[0], w_out[0], norm2_w[0],
        router_w[0], router_bias[0], exp_w_gate[0], exp_w_up[0], exp_w_down[0], sh_w_gate[0], sh_w_up[0],
        sh_w_down[0], tm_proj=256, tq=256, tt=256, tm_mix=256, blk=256)
    n = batch * seq
    pos3 = slot.reshape(n // tm_c, tm_c, TOP_K).transpose(0, 2, 1).reshape(n // tm_c, 1, TOP_K * tm_c)
    out = _combine(pos3, xs, mod8, final_norm_w.reshape(1, d), y, seq=seq, tm=tm_c)
    return out.reshape(batch, seq, d)
```

```python
import functools

import jax
import jax.numpy as jnp
from jax import lax
from jax.experimental import pallas as pl
from jax.experimental.pallas import tpu as pltpu

F32 = jnp.float32
BF16 = jnp.bfloat16
HIGHEST = lax.Precision.HIGHEST

CHUNK = 64
NORM_EPS = 1e-6
MLA_HEADS = 4
MLA_Q_RANK = 384
MLA_KV_RANK = 256
MLA_NOPE = 128
MLA_ROPE = 64
MLA_V = 128
ROPE_THETA = 10000.0
GDN_HEADS = 4
GDN_DK = 128
GDN_DV = 128
GDN_CONV = 4
N_EXPERTS = 64
N_GROUPS = 8
TOPK_GROUPS = 4
TOP_K = 8
ROUTED_SCALE = 2.5

LANES = 128
MLA_QK_PAD = 256
VMEM_LIMIT = 56 * 1024 * 1024


def _cparams(*sem):
    return pltpu.CompilerParams(dimension_semantics=sem, vmem_limit_bytes=VMEM_LIMIT)


def _dot(a, b):
    return jnp.dot(a, b, preferred_element_type=F32)


def _dot_f32(a, b):
    return jnp.dot(a, b, preferred_element_type=F32, precision=HIGHEST)


def _dot_nt(a, b, precision=None):
    return lax.dot_general(a, b, (((1,), (1,)), ((), ())), preferred_element_type=F32, precision=precision)


def _dot_tn(a, b):
    return lax.dot_general(a, b, (((0,), (0,)), ((), ())), preferred_element_type=F32)


def _silu(x):
    return x * jax.nn.sigmoid(x)


def _rms(x, w):
    return x * lax.rsqrt(jnp.mean(x * x, axis=-1, keepdims=True) + NORM_EPS) * w


def _adaln_kernel(c_ref, w_ref, b_ref, o_ref):
    a = _silu(c_ref[...]).astype(BF16)
    o_ref[...] = _dot(a, w_ref[...].astype(BF16)) + b_ref[...]


def _adaln(c8, ada_w, ada_b):
    d = c8.shape[1]
    n_out = ada_w.shape[1]
    return pl.pallas_call(
        _adaln_kernel,
        grid=(n_out // d,),
        in_specs=[pl.BlockSpec((8, d), lambda j: (0, 0)),
                  pl.BlockSpec((d, d), lambda j: (0, j)),
                  pl.BlockSpec((1, d), lambda j: (0, j))],
        out_specs=pl.BlockSpec((8, d), lambda j: (0, j)),
        out_shape=jax.ShapeDtypeStruct((8, n_out), F32),
        compiler_params=_cparams("parallel"),
        name="adaln",
    )(c8, ada_w, ada_b)


_C_Q = 0
_C_KV = _C_Q + MLA_Q_RANK
_C_KPE = _C_KV + MLA_KV_RANK
_C_QKV = _C_KPE + LANES
_GDN_QKV = GDN_HEADS * (2 * GDN_DK + GDN_DV)
_C_Z = _C_QKV + _GDN_QKV
_C_AB = _C_Z + GDN_HEADS * GDN_DV
_D_IN_PAD = _C_AB + LANES


def _rope(xb, cos, sin):
    half = MLA_ROPE // 2
    lane = lax.broadcasted_iota(jnp.int32, xb.shape, 1)
    rot = jnp.where(lane < half, -pltpu.roll(xb, LANES - half, 1), pltpu.roll(xb, half, 1))
    return xb * cos + rot * sin


def _proj_kernel(x_ref, mod_ref, pos_ref, n1_ref, win_ref, qan_ref, wq_ref, kvan_ref, wkv_ref, invf_ref,
                 q_ref, k_ref, v_ref, qkv_ref, z_ref, ab_ref):
    x = x_ref[...]
    mod = mod_ref[0]
    sh1, sc1 = mod[0:1], mod[1:2]
    h = (_rms(x, n1_ref[...]) * (1.0 + sc1) + sh1).astype(BF16)
    proj = _dot(h, win_ref[...])

    ang = pos_ref[...].astype(F32) * invf_ref[...]
    cos, sin = jnp.cos(ang), jnp.sin(ang)

    cq = proj[:, _C_Q:_C_Q + MLA_Q_RANK]
    qn = _rms(cq, qan_ref[...]).astype(BF16)
    q = _dot(qn, wq_ref[...]) * ((MLA_NOPE + MLA_ROPE) ** -0.5)
    for hh in range(MLA_HEADS):
        c0 = hh * MLA_QK_PAD
        q_ref[:, c0:c0 + MLA_NOPE] = q[:, c0:c0 + MLA_NOPE].astype(BF16)
        q_ref[:, c0 + MLA_NOPE:c0 + MLA_QK_PAD] = _rope(q[:, c0 + MLA_NOPE:c0 + MLA_QK_PAD], cos, sin).astype(BF16)

    ckv = proj[:, _C_KV:_C_KV + MLA_KV_RANK]
    kvn = _rms(ckv, kvan_ref[...]).astype(BF16)
    kv = _dot(kvn, wkv_ref[...])
    kpe = _rope(proj[:, _C_KPE:_C_KPE + LANES], cos, sin).astype(BF16)
    for hh in range(MLA_HEADS):
        c0 = hh * MLA_QK_PAD
        k_ref[:, c0:c0 + MLA_NOPE] = kv[:, hh * MLA_NOPE:(hh + 1) * MLA_NOPE].astype(BF16)
        k_ref[:, c0 + MLA_NOPE:c0 + MLA_QK_PAD] = kpe
    v_ref[...] = kv[:, MLA_HEADS * MLA_NOPE:].astype(BF16)

    qkv_ref[...] = proj[:, _C_QKV:_C_QKV + _GDN_QKV]
    z_ref[...] = proj[:, _C_Z:_C_Z + GDN_HEADS * GDN_DV]
    ab_ref[...] = proj[:, _C_AB:_C_AB + LANES]


def _proj(x2, mod8, pos2, n1, win, qan, wq, kvan, wkv, invf, *, seq, tm):
    n, d = x2.shape
    per_b = seq // tm
    row = lambda i: (i, 0)
    const = lambda i: (0, 0)
    hq = MLA_HEADS * MLA_QK_PAD
    hv = MLA_HEADS * MLA_V
    return pl.pallas_call(
        _proj_kernel,
        grid=(n // tm,),
        in_specs=[pl.BlockSpec((tm, d), row),
                  pl.BlockSpec((1, 8, d), lambda i: (i // per_b, 0, 0)),
                  pl.BlockSpec((tm, 1), row),
                  pl.BlockSpec((1, d), const),
                  pl.BlockSpec(win.shape, const),
                  pl.BlockSpec(qan.shape, const),
                  pl.BlockSpec(wq.shape, const),
                  pl.BlockSpec(kvan.shape, const),
                  pl.BlockSpec(wkv.shape, const),
                  pl.BlockSpec(invf.shape, const)],
        out_specs=[pl.BlockSpec((tm, hq), row), pl.BlockSpec((tm, hq), row), pl.BlockSpec((tm, hv), row),
                   pl.BlockSpec((tm, _GDN_QKV), row), pl.BlockSpec((tm, GDN_HEADS * GDN_DV), row),
                   pl.BlockSpec((tm, LANES), row)],
        out_shape=[jax.ShapeDtypeStruct((n, hq), BF16), jax.ShapeDtypeStruct((n, hq), BF16),
                   jax.ShapeDtypeStruct((n, hv), BF16), jax.ShapeDtypeStruct((n, _GDN_QKV), F32),
                   jax.ShapeDtypeStruct((n, GDN_HEADS * GDN_DV), F32), jax.ShapeDtypeStruct((n, LANES), F32)],
        compiler_params=_cparams("parallel"),
        name="proj",
    )(x2, mod8, pos2, n1, win, qan, wq, kvan, wkv, invf)


def _attn_kernel(q_ref, k_ref, v_ref, o_ref, *, tq):
    i = pl.program_id(2)
    q = q_ref[...]

    def step(kj, vj, carry, mask):
        m, l, acc = carry
        s = _dot_nt(q, kj)
        if mask is not None:
            s = jnp.where(mask, s, -jnp.inf)
        m_new = jnp.maximum(m, jnp.max(s, axis=-1, keepdims=True))
        alpha = jnp.exp(m - m_new)
        p = jnp.exp(s - m_new)
        l = alpha * l + jnp.sum(p, axis=-1, keepdims=True)
        acc = alpha * acc + _dot(p.astype(BF16), vj)
        return m_new, l, acc

    def body(j, carry):
        r0 = pl.multiple_of(j * tq, tq)
        return step(k_ref[pl.ds(r0, tq), :], v_ref[pl.ds(r0, tq), :], carry, None)

    init = (jnp.full((tq, 1), -jnp.inf, F32), jnp.zeros((tq, 1), F32), jnp.zeros((tq, MLA_V), F32))
    carry = lax.fori_loop(0, i, body, init)
    r0 = pl.multiple_of(i * tq, tq)
    rq = lax.broadcasted_iota(jnp.int32, (tq, tq), 0) // CHUNK
    ck = lax.broadcasted_iota(jnp.int32, (tq, tq), 1) // CHUNK
    m, l, acc = step(k_ref[pl.ds(r0, tq), :], v_ref[pl.ds(r0, tq), :], carry, ck <= rq)
    o_ref[...] = (acc / l).astype(o_ref.dtype)


def _attn(q, k, v, *, batch, seq, tq):
    n = q.shape[0]
    nq = seq // tq
    return pl.pallas_call(
        functools.partial(_attn_kernel, tq=tq),
        grid=(batch, MLA_HEADS, nq),
        in_specs=[pl.BlockSpec((tq, MLA_QK_PAD), lambda b, h, i: (b * nq + i, h)),
                  pl.BlockSpec((seq, MLA_QK_PAD), lambda b, h, i: (b, h)),
                  pl.BlockSpec((seq, MLA_V), lambda b, h, i: (b, h))],
        out_specs=pl.BlockSpec((tq, MLA_V), lambda b, h, i: (b * nq + i, h)),
        out_shape=jax.ShapeDtypeStruct((n, MLA_HEADS * MLA_V), BF16),
        compiler_params=_cparams("parallel", "parallel", "arbitrary"),
        name="attn",
    )(q, k, v)


_SUPER = 2 * CHUNK


def _unit_lower_inverse(low):
    n = low.shape[0]
    eye = (lax.broadcasted_iota(jnp.int32, (n, n), 0) == lax.broadcasted_iota(jnp.int32, (n, n), 1)).astype(F32)
    p = -low
    t = eye + p
    levels = CHUNK.bit_length() - 1
    for _ in range(levels - 1):
        p = _dot_f32(p, p)
        t = t + _dot_f32(t, p)
    return t


def _gdn_kernel(qkv_ref, z_ref, ab_ref, cw_ref, hp_ref, nw_ref, o_ref, xe_ref, st_ref, *, tt):
    j = pl.program_id(1)
    hd = GDN_HEADS * GDN_DK

    @pl.when(j == 0)
    def _():
        xe_ref[0:8, :] = jnp.zeros((8, xe_ref.shape[1]), F32)
        st_ref[...] = jnp.zeros(st_ref.shape, F32)

    xe_ref[8:8 + tt, :] = qkv_ref[...]
    cw = cw_ref[...]
    y = xe_ref[8:8 + tt, :] * cw[GDN_CONV - 1:GDN_CONV]
    for i in range(1, GDN_CONV):
        y = y + xe_ref[8 - i:8 - i + tt, :] * cw[GDN_CONV - 1 - i:GDN_CONV - i]
    xe_ref[0:8, :] = xe_ref[tt:tt + 8, :]
    act = _silu(y)

    ab = ab_ref[...]
    hp = hp_ref[...]
    pre = ab + hp[1:2]
    softplus = jnp.maximum(pre, 0.0) + jnp.log1p(jnp.exp(-jnp.abs(pre)))
    g_all = -jnp.exp(hp[0:1]) * softplus
    beta_all = jax.nn.sigmoid(ab)

    ri = lax.broadcasted_iota(jnp.int32, (tt, tt), 0)
    ci = lax.broadcasted_iota(jnp.int32, (tt, tt), 1)
    tri = ((ri // CHUNK == ci // CHUNK) & (ci <= ri)).astype(F32)
    gc = _dot_f32(tri, g_all)
    gct = gc.T

    rs = lax.broadcasted_iota(jnp.int32, (_SUPER, _SUPER), 0)
    cs = lax.broadcasted_iota(jnp.int32, (_SUPER, _SUPER), 1)
    same = rs // CHUNK == cs // CHUNK
    incl = same & (cs <= rs)
    strict = same & (cs < rs)

    for s in range(tt // _SUPER):
        r0 = s * _SUPER
        for hh in range(GDN_HEADS):
            qh = act[r0:r0 + _SUPER, hh * GDN_DK:(hh + 1) * GDN_DK]
            kh = act[r0:r0 + _SUPER, hd + hh * GDN_DK:hd + (hh + 1) * GDN_DK]
            vh = act[r0:r0 + _SUPER, 2 * hd + hh * GDN_DV:2 * hd + (hh + 1) * GDN_DV]
            qh = qh * lax.rsqrt(jnp.sum(qh * qh, axis=-1, keepdims=True) + 1e-6) * (GDN_DK ** -0.5)
            kh = kh * lax.rsqrt(jnp.sum(kh * kh, axis=-1, keepdims=True) + 1e-6)
            beta = beta_all[r0:r0 + _SUPER, GDN_HEADS + hh:GDN_HEADS + hh + 1]
            gcol = gc[r0:r0 + _SUPER, hh:hh + 1]
            grow = gct[hh:hh + 1, r0:r0 + _SUPER]
            decay = jnp.exp(jnp.where(incl, gcol - grow, -jnp.inf))
            kb = kh * beta
            vb = vh * beta
            low = jnp.where(strict, _dot_nt(kb, kh, HIGHEST) * decay, 0.0)
            tinv = _unit_lower_inverse(low)
            eg = jnp.exp(gcol)
            u = _dot_f32(tinv, vb)
            w = _dot_f32(tinv, kb * eg)
            att = _dot_nt(qh.astype(BF16), kh.astype(BF16)) * decay
            qg = qh * eg
            for c in range(_SUPER // CHUNK):
                a0 = c * CHUNK
                gl = gc[r0 + a0 + CHUNK - 1:r0 + a0 + CHUNK, hh:hh + 1]
                kd = kh[a0:a0 + CHUNK] * jnp.exp(gl - gcol[a0:a0 + CHUNK])
                state = st_ref[hh]
                sb = state.astype(BF16)
                v_new = u[a0:a0 + CHUNK] - _dot(w[a0:a0 + CHUNK].astype(BF16), sb)
                vnb = v_new.astype(BF16)
                o = (_dot(qg[a0:a0 + CHUNK].astype(BF16), sb)
                     + _dot(att[a0:a0 + CHUNK, a0:a0 + CHUNK].astype(BF16), vnb))
                st_ref[hh] = state * jnp.exp(gl) + _dot_tn(kd.astype(BF16), vnb)
                zz = z_ref[r0 + a0:r0 + a0 + CHUNK, hh * GDN_DV:(hh + 1) * GDN_DV]
                o_ref[r0 + a0:r0 + a0 + CHUNK, hh * GDN_DV:(hh + 1) * GDN_DV] = (
                    _rms(o, nw_ref[...]) * _silu(zz)).astype(o_ref.dtype)


def _gdn(qkv, z, ab, cw8, hp, nw, *, batch, seq, tt):
    n = qkv.shape[0]
    per_b = seq // tt
    row = lambda b, j: (b * per_b + j, 0)
    const = lambda b, j: (0, 0)
    hv = GDN_HEADS * GDN_DV
    return pl.pallas_call(
        functools.partial(_gdn_kernel, tt=tt),
        grid=(batch, per_b),
        in_specs=[pl.BlockSpec((tt, _GDN_QKV), row), pl.BlockSpec((tt, hv), row), pl.BlockSpec((tt, LANES), row),
                  pl.BlockSpec(cw8.shape, const), pl.BlockSpec(hp.shape, const), pl.BlockSpec(nw.shape, const)],
        out_specs=pl.BlockSpec((tt, hv), row),
        out_shape=jax.ShapeDtypeStruct((n, hv), BF16),
        scratch_shapes=[pltpu.VMEM((tt + 8, _GDN_QKV), F32), pltpu.VMEM((GDN_HEADS, GDN_DK, GDN_DV), F32)],
        compiler_params=_cparams("parallel", "arbitrary"),
        name="gdn",
    )(qkv, z, ab, cw8, hp, nw)


def _route(logits_t, bias_col):
    e, tm = logits_t.shape
    gsz = e // N_GROUPS
    scores = jax.nn.sigmoid(logits_t)
    biased = scores + bias_col
    sub = lax.broadcasted_iota(jnp.int32, (gsz, tm), 0)
    rows = []
    for g in range(N_GROUPS):
        blk = biased[g * gsz:(g + 1) * gsz]
        m1 = jnp.max(blk, axis=0, keepdims=True)
        first = jnp.min(jnp.where(blk == m1, sub, gsz), axis=0, keepdims=True)
        m2 = jnp.max(jnp.where(sub == first, -jnp.inf, blk), axis=0, keepdims=True)
        rows.append(m1 + m2)
    gs = jnp.concatenate(rows, axis=0)
    gi = lax.broadcasted_iota(jnp.int32, (N_GROUPS, tm), 0)
    grank = jnp.zeros((N_GROUPS, tm), F32)
    for g in range(N_GROUPS):
        r = gs[g:g + 1]
        grank = grank + jnp.where(r > gs, 1.0, jnp.where(r == gs, (gi > g).astype(F32), 0.0))
    gsel = grank < TOPK_GROUPS
    masked = jnp.concatenate(
        [jnp.where(gsel[g:g + 1], biased[g * gsz:(g + 1) * gsz], -jnp.inf) for g in range(N_GROUPS)], axis=0)
    ei = lax.broadcasted_iota(jnp.int32, (e, tm), 0)
    rank = jnp.zeros((e, tm), F32)
    for k in range(e):
        r = masked[k:k + 1]
        rank = rank + jnp.where(r > masked, 1.0, jnp.where(r == masked, (ei > k).astype(F32), 0.0))
    return scores, rank


def _pack_pairs(x):
    c = x.shape[1] // 2
    lo = pltpu.bitcast(x[:, :c].astype(BF16).astype(F32), jnp.uint32)
    hi = pltpu.bitcast(x[:, c:].astype(BF16).astype(F32), jnp.uint32)
    return (hi & jnp.uint32(0xFFFF0000)) | (lo >> 16)


def _unpack_pairs(w):
    return pltpu.bitcast(w << 16, F32), pltpu.bitcast(w & jnp.uint32(0xFFFF0000), F32)


def _mix_kernel(oa_ref, ob_ref, x_ref, mod_ref, woa_ref, wob_ref, n2_ref, rwt_ref, rb_ref, sg_ref, su_ref, sd_ref,
                xs_ref, h2p_ref, te_ref, tw_ref, tp_ref, cnt_ref, run_ref):
    @pl.when(pl.program_id(0) == 0)
    def _():
        run_ref[...] = jnp.zeros(run_ref.shape, F32)

    mod = mod_ref[0]
    g1, sh2, sc2, g2 = mod[2:3], mod[3:4], mod[4:5], mod[5:6]
    mix = _dot(oa_ref[...], woa_ref[...]) + _dot(ob_ref[...], wob_ref[...])
    x1 = x_ref[...] + g1 * mix
    h2 = _rms(x1, n2_ref[...]) * (1.0 + sc2) + sh2
    h2p_ref[...] = _pack_pairs(h2)
    h2b = h2.astype(BF16)
    hs = (_silu(_dot(h2b, sg_ref[...])) * _dot(h2b, su_ref[...])).astype(BF16)
    xs_ref[...] = x1 + g2 * _dot(hs, sd_ref[...])

    scores, rank = _route(_dot_nt(rwt_ref[...], h2, HIGHEST), rb_ref[...])
    e, tm = rank.shape
    sel = jnp.where(rank < TOP_K, 1.0, 0.0)
    earlier = (lax.broadcasted_iota(jnp.int32, (tm, tm), 0) < lax.broadcasted_iota(jnp.int32, (tm, tm), 1))
    posmat = _dot(sel.astype(BF16), earlier.astype(BF16)) + run_ref[...]
    eif = lax.broadcasted_iota(jnp.int32, (e, tm), 0).astype(F32)
    ids, wts, pos = [], [], []
    for k in range(TOP_K):
        hit = rank == float(k)
        ids.append(jnp.sum(jnp.where(hit, eif, 0.0), axis=0, keepdims=True))
        wts.append(jnp.sum(jnp.where(hit, scores, 0.0), axis=0, keepdims=True))
        pos.append(jnp.sum(jnp.where(hit, posmat, 0.0), axis=0, keepdims=True))
    wts = jnp.concatenate(wts, axis=0)
    te_ref[0] = jnp.concatenate(ids, axis=0).astype(jnp.int32)
    tw_ref[0] = wts / jnp.sum(wts, axis=0, keepdims=True) * ROUTED_SCALE
    tp_ref[0] = jnp.concatenate(pos, axis=0).astype(jnp.int32)
    run_ref[...] = run_ref[...] + jnp.sum(sel, axis=1, keepdims=True)
    cnt_ref[...] = jnp.broadcast_to(run_ref[...], cnt_ref.shape)


def _mix(oa, ob, x2, mod8, woa, wob, n2, rwt, rb, sg, su, sd, *, seq, tm):
    n, d = x2.shape
    per_b = seq // tm
    nb = n // tm
    row = lambda i: (i, 0)
    blk3 = lambda i: (i, 0, 0)
    const = lambda i: (0, 0)
    return pl.pallas_call(
        _mix_kernel,
        grid=(nb,),
        in_specs=[pl.BlockSpec((tm, oa.shape[1]), row), pl.BlockSpec((tm, ob.shape[1]), row),
                  pl.BlockSpec((tm, d), row), pl.BlockSpec((1, 8, d), lambda i: (i // per_b, 0, 0)),
                  pl.BlockSpec(woa.shape, const), pl.BlockSpec(wob.shape, const), pl.BlockSpec((1, d), const),
                  pl.BlockSpec(rwt.shape, const), pl.BlockSpec(rb.shape, const),
                  pl.BlockSpec(sg.shape, const), pl.BlockSpec(su.shape, const), pl.BlockSpec(sd.shape, const)],
        out_specs=[pl.BlockSpec((tm, d), row), pl.BlockSpec((tm, d // 2), row),
                   pl.BlockSpec((1, TOP_K, tm), blk3), pl.BlockSpec((1, TOP_K, tm), blk3),
                   pl.BlockSpec((1, TOP_K, tm), blk3), pl.BlockSpec((N_EXPERTS, LANES), const)],
        out_shape=[jax.ShapeDtypeStruct((n, d), F32), jax.ShapeDtypeStruct((n, d // 2), jnp.uint32),
                   jax.ShapeDtypeStruct((nb, TOP_K, tm), jnp.int32), jax.ShapeDtypeStruct((nb, TOP_K, tm), F32),
                   jax.ShapeDtypeStruct((nb, TOP_K, tm), jnp.int32),
                   jax.ShapeDtypeStruct((N_EXPERTS, LANES), F32)],
        scratch_shapes=[pltpu.VMEM((N_EXPERTS, 1), F32)],
        compiler_params=_cparams("arbitrary"),
        name="mix",
    )(oa, ob, x2, mod8, woa, wob, n2, rwt, rb, sg, su, sd)


def _slot(ps_ref, ids_ref, pos_ref, k, t):
    return ps_ref[ids_ref[0, k, t]] + pos_ref[0, k, t]


def _dispatch_kernel(ps_ref, ids_ref, pos_ref, h2p_ref, xs_hbm, zbuf, zsem, sem, *, tm, blk):
    @pl.when(pl.program_id(0) == 0)
    def _():
        zbuf[...] = jnp.zeros(zbuf.shape, zbuf.dtype)

        def tail(e):
            end = ps_ref[e + 1]
            return end > ps_ref[e], pltpu.make_async_copy(
                zbuf, xs_hbm.at[pl.ds(pl.multiple_of(end - blk, blk), blk), :], zsem)

        def unused(e):
            row = pl.multiple_of(ps_ref[N_EXPERTS] + e * blk, blk)
            return row < xs_hbm.shape[0], pltpu.make_async_copy(zbuf, xs_hbm.at[pl.ds(row, blk), :], zsem)

        for fill in (tail, unused):
            for e in range(N_EXPERTS):
                needed, cp = fill(e)
                pl.when(needed)(cp.start)
        for fill in (tail, unused):
            for e in range(N_EXPERTS):
                needed, cp = fill(e)
                pl.when(needed)(cp.wait)

    def issue(t, carry):
        for k in range(TOP_K):
            pltpu.make_async_copy(h2p_ref.at[pl.ds(t, 1), :],
                                  xs_hbm.at[pl.ds(_slot(ps_ref, ids_ref, pos_ref, k, t), 1), :], sem).start()
        return carry

    lax.fori_loop(0, tm, issue, 0)
    for k in range(TOP_K):
        pltpu.make_async_copy(h2p_ref, xs_hbm.at[pl.ds(0, tm), :], sem).wait()


def _dispatch(ps, ids, pos, h2p, *, n_slots, tm, blk):
    n, c = h2p.shape
    blk3 = lambda i, ps: (i, 0, 0)
    grid_spec = pltpu.PrefetchScalarGridSpec(
        num_scalar_prefetch=1,
        grid=(n // tm,),
        in_specs=[pl.BlockSpec((1, TOP_K, tm), blk3, memory_space=pltpu.SMEM),
                  pl.BlockSpec((1, TOP_K, tm), blk3, memory_space=pltpu.SMEM),
                  pl.BlockSpec((tm, c), lambda i, ps: (i, 0))],
        out_specs=pl.BlockSpec(memory_space=pl.ANY),
        scratch_shapes=[pltpu.VMEM((blk, c), jnp.uint32), pltpu.SemaphoreType.DMA(()),
                        pltpu.SemaphoreType.DMA(())],
    )
    return pl.pallas_call(
        functools.partial(_dispatch_kernel, tm=tm, blk=blk),
        grid_spec=grid_spec,
        out_shape=jax.ShapeDtypeStruct((n_slots, c), jnp.uint32),
        compiler_params=_cparams("arbitrary"),
        name="dispatch",
    )(ps, ids, pos, h2p)


def _experts_kernel(be_ref, nu_ref, x_ref, wg_ref, wu_ref, wd_ref, y_ref):
    del be_ref
    used = pl.program_id(0) < nu_ref[0]

    @pl.when(used)
    def _():
        lo, hi = _unpack_pairs(x_ref[...])
        lo, hi = lo.astype(BF16), hi.astype(BF16)
        half = lo.shape[1]
        g = _dot(lo, wg_ref[0, :half, :]) + _dot(hi, wg_ref[0, half:, :])
        u = _dot(lo, wu_ref[0, :half, :]) + _dot(hi, wu_ref[0, half:, :])
        hb = (_silu(g) * u).astype(BF16)
        y_ref[...] = _pack_pairs(_dot(hb, wd_ref[0]))

    @pl.when(jnp.logical_not(used))
    def _():
        y_ref[...] = jnp.zeros(y_ref.shape, y_ref.dtype)


def _experts(block_e, n_used, xs, wg, wu, wd, *, blk):
    n_blocks = block_e.shape[0]
    c = xs.shape[1]
    wsel = lambda i, be, nu: (be[i], 0, 0)
    grid_spec = pltpu.PrefetchScalarGridSpec(
        num_scalar_prefetch=2,
        grid=(n_blocks,),
        in_specs=[pl.BlockSpec((blk, c), lambda i, be, nu: (jnp.minimum(i, nu[0] - 1), 0)),
                  pl.BlockSpec((1,) + wg.shape[1:], wsel), pl.BlockSpec((1,) + wu.shape[1:], wsel),
                  pl.BlockSpec((1,) + wd.shape[1:], wsel)],
        out_specs=pl.BlockSpec((blk, c), lambda i, be, nu: (i, 0)),
    )
    return pl.pallas_call(
        _experts_kernel,
        grid_spec=grid_spec,
        out_shape=jax.ShapeDtypeStruct((n_blocks * blk, c), jnp.uint32),
        compiler_params=_cparams("arbitrary"),
        name="experts",
    )(block_e, n_used, xs, wg, wu, wd)


def _combine_kernel(ps_ref, ids_ref, pos_ref, tw_ref, xs_ref, mod_ref, fn_ref, y_hbm, o_ref, gbuf, sem, *, tm):
    def issue(t, carry):
        for k in range(TOP_K):
            pltpu.make_async_copy(y_hbm.at[pl.ds(_slot(ps_ref, ids_ref, pos_ref, k, t), 1), :],
                                  gbuf.at[pl.ds(k * tm + t, 1), :], sem).start()
        return carry

    lax.fori_loop(0, tm, issue, 0)
    pltpu.make_async_copy(y_hbm.at[pl.ds(0, TOP_K * tm), :], gbuf, sem).wait()

    wt = jnp.concatenate([tw_ref[0], jnp.zeros((LANES - TOP_K, tm), F32)], axis=0).T
    half = gbuf.shape[1]
    acc_lo = jnp.zeros((tm, half), F32)
    acc_hi = jnp.zeros((tm, half), F32)
    for k in range(TOP_K):
        lo, hi = _unpack_pairs(gbuf[k * tm:(k + 1) * tm, :])
        acc_lo = acc_lo + wt[:, k:k + 1] * lo
        acc_hi = acc_hi + wt[:, k:k + 1] * hi
    g2 = mod_ref[0][5:6]
    x_lo = xs_ref[:, :half] + g2[:, :half] * acc_lo
    x_hi = xs_ref[:, half:] + g2[:, half:] * acc_hi
    ms = (jnp.sum(x_lo * x_lo, axis=-1, keepdims=True) + jnp.sum(x_hi * x_hi, axis=-1, keepdims=True)) / (2 * half)
    inv = lax.rsqrt(ms + NORM_EPS)
    o_ref[:, :half] = x_lo * inv * fn_ref[:, :half]
    o_ref[:, half:] = x_hi * inv * fn_ref[:, half:]


def _combine(ps, ids, pos, tw, xs, mod8, fn, y, *, seq, tm):
    n, d = xs.shape
    c = y.shape[1]
    per_b = seq // tm
    blk3 = lambda i, ps: (i, 0, 0)
    grid_spec = pltpu.PrefetchScalarGridSpec(
        num_scalar_prefetch=1,
        grid=(n // tm,),
        in_specs=[pl.BlockSpec((1, TOP_K, tm), blk3, memory_space=pltpu.SMEM),
                  pl.BlockSpec((1, TOP_K, tm), blk3, memory_space=pltpu.SMEM),
                  pl.BlockSpec((1, TOP_K, tm), blk3),
                  pl.BlockSpec((tm, d), lambda i, ps: (i, 0)),
                  pl.BlockSpec((1, 8, d), lambda i, ps: (i // per_b, 0, 0)),
                  pl.BlockSpec((1, d), lambda i, ps: (0, 0)),
                  pl.BlockSpec(memory_space=pl.ANY)],
        out_specs=pl.BlockSpec((tm, d), lambda i, ps: (i, 0)),
        scratch_shapes=[pltpu.VMEM((TOP_K * tm, c), jnp.uint32), pltpu.SemaphoreType.DMA(())],
    )
    return pl.pallas_call(
        functools.partial(_combine_kernel, tm=tm),
        grid_spec=grid_spec,
        out_shape=jax.ShapeDtypeStruct((n, d), F32),
        compiler_params=_cparams("arbitrary"),
        name="combine",
    )(ps, ids, pos, tw, xs, mod8, fn, y)


def _expert_runs(counts, n_blocks, blk):
    padded = (counts + blk - 1) // blk * blk
    pad_end = jnp.cumsum(padded)
    run_start = jnp.concatenate([pad_end - padded, pad_end[-1:]]).astype(jnp.int32)
    block_e = jnp.clip(jnp.searchsorted(pad_end, jnp.arange(n_blocks, dtype=jnp.int32) * blk, side='right'),
                       0, N_EXPERTS - 1).astype(jnp.int32)
    n_used = (pad_end[-1:] // blk).astype(jnp.int32)
    return run_start, block_e, n_used


def _pad_lanes(w, width):
    return jnp.pad(w, ((0, 0), (0, width - w.shape[1])))


def _tiles(seq):
    t = dict(tm_proj=256, tq=256, tt=256, tm_moe=256, blk=256)
    assert all(seq % v == 0 for k, v in t.items() if k != "blk")
    return t


def _layer(x, c, positions, ada_w, ada_b, norm1_w, w_in, q_a_norm_w, wq_b, kv_a_norm_w, wkv_b,
           gdn_conv_w, gdn_a_log, gdn_dt_bias, gdn_norm_w, w_out, norm2_w, router_w, router_bias,
           exp_w_gate, exp_w_up, exp_w_down, sh_w_gate, sh_w_up, sh_w_down, final_norm_w,
           *, tm_proj, tq, tt, tm_moe, blk):
    batch, seq, d = x.shape
    n = batch * seq
    x2 = x.reshape(n, d)

    c8 = jnp.zeros((8, d), F32).at[:batch].set(c)
    mod = _adaln(c8, ada_w, ada_b.reshape(1, -1))
    mod8 = jnp.pad(mod[:batch].reshape(batch, 6, d), ((0, 0), (0, 2), (0, 0)))

    hq = MLA_NOPE + MLA_ROPE
    w_cq, w_ckv, w_kpe, w_qkv, w_z, w_a, w_b = jnp.split(
        w_in, [_C_KV, _C_KV + MLA_KV_RANK, _C_KV + MLA_KV_RANK + MLA_ROPE,
               _C_KV + MLA_KV_RANK + MLA_ROPE + _GDN_QKV,
               _C_KV + MLA_KV_RANK + MLA_ROPE + _GDN_QKV + GDN_HEADS * GDN_DV,
               _C_KV + MLA_KV_RANK + MLA_ROPE + _GDN_QKV + GDN_HEADS * GDN_DV + GDN_HEADS], axis=1)
    win = jnp.concatenate([w_cq, w_ckv, _pad_lanes(w_kpe, LANES), w_qkv, w_z,
                           _pad_lanes(jnp.concatenate([w_a, w_b], axis=1), LANES)], axis=1).astype(BF16)
    wq = jnp.pad(wq_b.reshape(MLA_Q_RANK, MLA_HEADS, hq),
                 ((0, 0), (0, 0), (0, MLA_QK_PAD - hq))).reshape(MLA_Q_RANK, MLA_HEADS * MLA_QK_PAD).astype(BF16)
    wkv4 = wkv_b.reshape(MLA_KV_RANK, MLA_HEADS, MLA_NOPE + MLA_V)
    wkv = jnp.concatenate([wkv4[:, :, :MLA_NOPE].reshape(MLA_KV_RANK, -1),
                           wkv4[:, :, MLA_NOPE:].reshape(MLA_KV_RANK, -1)], axis=1).astype(BF16)
    inv_freq = 1.0 / (ROPE_THETA ** (jnp.arange(0, MLA_ROPE, 2, dtype=F32) / MLA_ROPE))
    invf = _pad_lanes(jnp.concatenate([inv_freq, inv_freq])[None, :], LANES)

    q, k, v, qkv, z, ab = _proj(x2, mod8, positions.reshape(n, 1), norm1_w.reshape(1, d), win,
                                q_a_norm_w.reshape(1, -1), wq, kv_a_norm_w.reshape(1, -1), wkv, invf,
                                seq=seq, tm=tm_proj)
    out_a = _attn(q, k, v, batch=batch, seq=seq, tq=tq)

    cw8 = jnp.pad(gdn_conv_w, ((0, 8 - GDN_CONV), (0, 0)))
    hp = _pad_lanes(jnp.stack([gdn_a_log, gdn_dt_bias]), LANES)
    hp = jnp.pad(hp, ((0, 6), (0, 0)))
    out_b = _gdn(qkv, z, ab, cw8, hp, gdn_norm_w.reshape(1, -1), batch=batch, seq=seq, tt=tt)

    ha = MLA_HEADS * MLA_V
    xs, h2p, ids, tw, pos, counts = _mix(
        out_a, out_b, x2, mod8, w_out[:ha].astype(BF16), w_out[ha:].astype(BF16), norm2_w.reshape(1, d),
        router_w.T, router_bias.reshape(-1, 1), sh_w_gate.astype(BF16), sh_w_up.astype(BF16),
        sh_w_down.astype(BF16), seq=seq, tm=tm_moe)

    n_blocks = n * TOP_K // blk + N_EXPERTS
    run_start, block_e, n_used = _expert_runs(counts[:, 0].astype(jnp.int32), n_blocks, blk)
    xsort = _dispatch(run_start, ids, pos, h2p, n_slots=n_blocks * blk, tm=tm_moe, blk=blk)
    y = _experts(block_e, n_used, xsort, exp_w_gate.astype(BF16), exp_w_up.astype(BF16),
                 exp_w_down.astype(BF16), blk=blk)
    out = _combine(run_start, ids, pos, tw, xs, mod8, final_norm_w.reshape(1, d), y, seq=seq, tm=tm_moe)
    return out.reshape(batch, seq, d)


def kernel(x, c, positions, ada_w, ada_b, norm1_w, w_in, q_a_norm_w, wq_b, kv_a_norm_w, wkv_b, gdn_conv_w,
           gdn_a_log, gdn_dt_bias, gdn_norm_w, w_out, norm2_w, router_w, router_bias, exp_w_gate, exp_w_up,
           exp_w_down, sh_w_gate, sh_w_up, sh_w_down, final_norm_w):
    assert ada_w.shape[0] == 1, "single layer"
    return _layer(
        x, c, positions, ada_w[0], ada_b[0], norm1_w[0], w_in[0], q_a_norm_w[0], wq_b[0], kv_a_norm_w[0],
        wkv_b[0], gdn_conv_w[0], gdn_a_log[0], gdn_dt_bias[0], gdn_norm_w[0], w_out[0], norm2_w[0],
        router_w[0], router_bias[0], exp_w_gate[0], exp_w_up[0], exp_w_down[0], sh_w_gate[0], sh_w_up[0],
        sh_w_down[0], final_norm_w, **_tiles(x.shape[1]))
```

```python
import functools

import jax
import jax.numpy as jnp
from jax import lax
from jax.experimental import pallas as pl
from jax.experimental.pallas import tpu as pltpu

F32 = jnp.float32
BF16 = jnp.bfloat16
HIGHEST = lax.Precision.HIGHEST

CHUNK = 64
NORM_EPS = 1e-6
MLA_HEADS = 4
MLA_Q_RANK = 384
MLA_KV_RANK = 256
MLA_NOPE = 128
MLA_ROPE = 64
MLA_V = 128
ROPE_THETA = 10000.0
GDN_HEADS = 4
GDN_DK = 128
GDN_DV = 128
GDN_CONV = 4
N_EXPERTS = 64
N_GROUPS = 8
TOPK_GROUPS = 4
TOP_K = 8
ROUTED_SCALE = 2.5

LANES = 128
MLA_QK_PAD = 256
VMEM_LIMIT = 56 * 1024 * 1024


def _cparams(*sem):
    return pltpu.CompilerParams(dimension_semantics=sem, vmem_limit_bytes=VMEM_LIMIT)


def _dot(a, b):
    return jnp.dot(a, b, preferred_element_type=F32)


def _split3(a):
    hi = a.astype(BF16)
    r = a - hi.astype(F32)
    mid = r.astype(BF16)
    return hi, mid, (r - mid.astype(F32)).astype(BF16)


def _dot_x3(a, b):
    ah, am, _ = a
    bh, bm, _ = b
    return _dot(ah, bh) + (_dot(ah, bm) + _dot(am, bh))


def _dot_nt(a, b, precision=None):
    return lax.dot_general(a, b, (((1,), (1,)), ((), ())), preferred_element_type=F32, precision=precision)


def _dot_tn(a, b):
    return lax.dot_general(a, b, (((0,), (0,)), ((), ())), preferred_element_type=F32)


def _silu(x):
    return x * jax.nn.sigmoid(x)


def _rms(x, w):
    return x * lax.rsqrt(jnp.mean(x * x, axis=-1, keepdims=True) + NORM_EPS) * w


def _adaln_kernel(c_ref, w_ref, b_ref, o_ref):
    a = _silu(c_ref[...]).astype(BF16)
    o_ref[...] = _dot(a, w_ref[...].astype(BF16)) + b_ref[...]


def _adaln(c8, ada_w, ada_b):
    d = c8.shape[1]
    n_out = ada_w.shape[1]
    return pl.pallas_call(
        _adaln_kernel,
        grid=(n_out // d,),
        in_specs=[pl.BlockSpec((8, d), lambda j: (0, 0)),
                  pl.BlockSpec((d, d), lambda j: (0, j)),
                  pl.BlockSpec((1, d), lambda j: (0, j))],
        out_specs=pl.BlockSpec((8, d), lambda j: (0, j)),
        out_shape=jax.ShapeDtypeStruct((8, n_out), F32),
        compiler_params=_cparams("parallel"),
        name="adaln",
    )(c8, ada_w, ada_b)


_C_Q = 0
_C_KV = _C_Q + MLA_Q_RANK
_C_KPE = _C_KV + MLA_KV_RANK
_C_QKV = _C_KPE + LANES
_GDN_QKV = GDN_HEADS * (2 * GDN_DK + GDN_DV)
_C_Z = _C_QKV + _GDN_QKV
_C_AB = _C_Z + GDN_HEADS * GDN_DV
_D_IN_PAD = _C_AB + LANES


def _rope(xb, cos, sin):
    half = MLA_ROPE // 2
    lane = lax.broadcasted_iota(jnp.int32, xb.shape, 1)
    rot = jnp.where(lane < half, -pltpu.roll(xb, LANES - half, 1), pltpu.roll(xb, half, 1))
    return xb * cos + rot * sin


def _proj_kernel(x_ref, mod_ref, pos_ref, n1_ref, win_ref, qan_ref, wq_ref, kvan_ref, wkv_ref, invf_ref,
                 q_ref, k_ref, v_ref, qkv_ref, z_ref, ab_ref):
    x = x_ref[...]
    mod = mod_ref[0]
    sh1, sc1 = mod[0:1], mod[1:2]
    h = (_rms(x, n1_ref[...]) * (1.0 + sc1) + sh1).astype(BF16)
    proj = _dot(h, win_ref[...])

    ang = pos_ref[...].astype(F32) * invf_ref[...]
    cos, sin = jnp.cos(ang), jnp.sin(ang)

    cq = proj[:, _C_Q:_C_Q + MLA_Q_RANK]
    qn = _rms(cq, qan_ref[...]).astype(BF16)
    q = _dot(qn, wq_ref[...]) * ((MLA_NOPE + MLA_ROPE) ** -0.5)
    for hh in range(MLA_HEADS):
        c0 = hh * MLA_QK_PAD
        q_ref[:, c0:c0 + MLA_NOPE] = q[:, c0:c0 + MLA_NOPE].astype(BF16)
        q_ref[:, c0 + MLA_NOPE:c0 + MLA_QK_PAD] = _rope(q[:, c0 + MLA_NOPE:c0 + MLA_QK_PAD], cos, sin).astype(BF16)

    ckv = proj[:, _C_KV:_C_KV + MLA_KV_RANK]
    kvn = _rms(ckv, kvan_ref[...]).astype(BF16)
    kv = _dot(kvn, wkv_ref[...])
    kpe = _rope(proj[:, _C_KPE:_C_KPE + LANES], cos, sin).astype(BF16)
    for hh in range(MLA_HEADS):
        c0 = hh * MLA_QK_PAD
        k_ref[:, c0:c0 + MLA_NOPE] = kv[:, hh * MLA_NOPE:(hh + 1) * MLA_NOPE].astype(BF16)
        k_ref[:, c0 + MLA_NOPE:c0 + MLA_QK_PAD] = kpe
    v_ref[...] = kv[:, MLA_HEADS * MLA_NOPE:].astype(BF16)

    qkv_ref[...] = proj[:, _C_QKV:_C_QKV + _GDN_QKV]
    z_ref[...] = proj[:, _C_Z:_C_Z + GDN_HEADS * GDN_DV]
    ab_ref[...] = proj[:, _C_AB:_C_AB + LANES]


def _proj(x2, mod8, pos2, n1, win, qan, wq, kvan, wkv, invf, *, seq, tm):
    n, d = x2.shape
    per_b = seq // tm
    row = lambda i: (i, 0)
    const = lambda i: (0, 0)
    hq = MLA_HEADS * MLA_QK_PAD
    hv = MLA_HEADS * MLA_V
    return pl.pallas_call(
        _proj_kernel,
        grid=(n // tm,),
        in_specs=[pl.BlockSpec((tm, d), row),
                  pl.BlockSpec((1, 8, d), lambda i: (i // per_b, 0, 0)),
                  pl.BlockSpec((tm, 1), row),
                  pl.BlockSpec((1, d), const),
                  pl.BlockSpec(win.shape, const),
                  pl.BlockSpec(qan.shape, const),
                  pl.BlockSpec(wq.shape, const),
                  pl.BlockSpec(kvan.shape, const),
                  pl.BlockSpec(wkv.shape, const),
                  pl.BlockSpec(invf.shape, const)],
        out_specs=[pl.BlockSpec((tm, hq), row), pl.BlockSpec((tm, hq), row), pl.BlockSpec((tm, hv), row),
                   pl.BlockSpec((tm, _GDN_QKV), row), pl.BlockSpec((tm, GDN_HEADS * GDN_DV), row),
                   pl.BlockSpec((tm, LANES), row)],
        out_shape=[jax.ShapeDtypeStruct((n, hq), BF16), jax.ShapeDtypeStruct((n, hq), BF16),
                   jax.ShapeDtypeStruct((n, hv), BF16), jax.ShapeDtypeStruct((n, _GDN_QKV), F32),
                   jax.ShapeDtypeStruct((n, GDN_HEADS * GDN_DV), F32), jax.ShapeDtypeStruct((n, LANES), F32)],
        compiler_params=_cparams("parallel"),
        name="proj",
    )(x2, mod8, pos2, n1, win, qan, wq, kvan, wkv, invf)


def _attn_kernel(q_ref, k_ref, v_ref, o_ref, *, tq):
    i = pl.program_id(1)

    def head_step(hh, r0, carry, mask):
        m, l, acc = carry
        s = _dot_nt(q_ref[:, hh * MLA_QK_PAD:(hh + 1) * MLA_QK_PAD],
                    k_ref[pl.ds(r0, tq), hh * MLA_QK_PAD:(hh + 1) * MLA_QK_PAD])
        if mask is not None:
            s = jnp.where(mask, s, -jnp.inf)
        m_new = jnp.maximum(m, jnp.max(s, axis=-1, keepdims=True))
        alpha = jnp.exp(m - m_new)
        p = jnp.exp(s - m_new)
        l = alpha * l + jnp.sum(p, axis=-1, keepdims=True)
        acc = alpha * acc + _dot(p.astype(BF16), v_ref[pl.ds(r0, tq), hh * MLA_V:(hh + 1) * MLA_V])
        return m_new, l, acc

    def block(r0, carry, mask):
        return tuple(head_step(hh, r0, carry[hh], mask) for hh in range(MLA_HEADS))

    init = tuple((jnp.full((tq, 1), -jnp.inf, F32), jnp.zeros((tq, 1), F32), jnp.zeros((tq, MLA_V), F32))
                 for _ in range(MLA_HEADS))
    carry = lax.fori_loop(0, i, lambda j, c: block(pl.multiple_of(j * tq, tq), c, None), init)
    rq = lax.broadcasted_iota(jnp.int32, (tq, tq), 0) // CHUNK
    ck = lax.broadcasted_iota(jnp.int32, (tq, tq), 1) // CHUNK
    carry = block(pl.multiple_of(i * tq, tq), carry, ck <= rq)
    for hh in range(MLA_HEADS):
        _, l, acc = carry[hh]
        o_ref[:, hh * MLA_V:(hh + 1) * MLA_V] = (acc / l).astype(o_ref.dtype)


def _attn(q, k, v, *, batch, seq, tq):
    n = q.shape[0]
    nq = seq // tq
    return pl.pallas_call(
        functools.partial(_attn_kernel, tq=tq),
        grid=(batch, nq),
        in_specs=[pl.BlockSpec((tq, q.shape[1]), lambda b, i: (b * nq + i, 0)),
                  pl.BlockSpec((seq, k.shape[1]), lambda b, i: (b, 0)),
                  pl.BlockSpec((seq, v.shape[1]), lambda b, i: (b, 0))],
        out_specs=pl.BlockSpec((tq, v.shape[1]), lambda b, i: (b * nq + i, 0)),
        out_shape=jax.ShapeDtypeStruct((n, v.shape[1]), BF16),
        compiler_params=_cparams("parallel", "arbitrary"),
        name="attn",
    )(q, k, v)


_SUPER = 2 * CHUNK


def _unit_lower_inverses(lows):
    n = lows[0].shape[0]
    eye = (lax.broadcasted_iota(jnp.int32, (n, n), 0) == lax.broadcasted_iota(jnp.int32, (n, n), 1)).astype(F32)
    ps = [-low for low in lows]
    ts = [eye + p for p in ps]
    levels = CHUNK.bit_length() - 1
    for _ in range(levels - 1):
        parts = [_split3(p) for p in ps]
        ps = [_dot_x3(s, s) for s in parts]
        ts = [t + _dot_x3(_split3(t), _split3(p)) for t, p in zip(ts, ps)]
    return ts


def _gdn_kernel(qkv_ref, z_ref, ab_ref, cw_ref, hp_ref, nw_ref, o_ref, xe_ref, st_ref, *, tt):
    j = pl.program_id(1)
    hd = GDN_HEADS * GDN_DK

    @pl.when(j == 0)
    def _():
        xe_ref[0:8, :] = jnp.zeros((8, xe_ref.shape[1]), F32)
        st_ref[...] = jnp.zeros(st_ref.shape, F32)

    xe_ref[8:8 + tt, :] = qkv_ref[...]
    cw = cw_ref[...]
    y = xe_ref[8:8 + tt, :] * cw[GDN_CONV - 1:GDN_CONV]
    for i in range(1, GDN_CONV):
        y = y + xe_ref[8 - i:8 - i + tt, :] * cw[GDN_CONV - 1 - i:GDN_CONV - i]
    xe_ref[0:8, :] = xe_ref[tt:tt + 8, :]
    act = _silu(y)

    ab = ab_ref[...]
    hp = hp_ref[...]
    pre = ab + hp[1:2]
    softplus = jnp.maximum(pre, 0.0) + jnp.log1p(jnp.exp(-jnp.abs(pre)))
    g_all = -jnp.exp(hp[0:1]) * softplus
    beta_all = jax.nn.sigmoid(ab)

    ri = lax.broadcasted_iota(jnp.int32, (tt, tt), 0)
    ci = lax.broadcasted_iota(jnp.int32, (tt, tt), 1)
    tri = ((ri // CHUNK == ci // CHUNK) & (ci <= ri)).astype(BF16)
    gc = sum(_dot(tri, part) for part in _split3(g_all))
    gct = gc.T

    rs = lax.broadcasted_iota(jnp.int32, (_SUPER, _SUPER), 0)
    cs = lax.broadcasted_iota(jnp.int32, (_SUPER, _SUPER), 1)
    same = rs // CHUNK == cs // CHUNK
    incl = same & (cs <= rs)
    strict = same & (cs < rs)

    tiles = [(s * _SUPER, hh) for s in range(tt // _SUPER) for hh in range(GDN_HEADS)]
    pre_t = []
    for r0, hh in tiles:
        qh = act[r0:r0 + _SUPER, hh * GDN_DK:(hh + 1) * GDN_DK]
        kh = act[r0:r0 + _SUPER, hd + hh * GDN_DK:hd + (hh + 1) * GDN_DK]
        vh = act[r0:r0 + _SUPER, 2 * hd + hh * GDN_DV:2 * hd + (hh + 1) * GDN_DV]
        qh = qh * lax.rsqrt(jnp.sum(qh * qh, axis=-1, keepdims=True) + 1e-6) * (GDN_DK ** -0.5)
        kh = kh * lax.rsqrt(jnp.sum(kh * kh, axis=-1, keepdims=True) + 1e-6)
        beta = beta_all[r0:r0 + _SUPER, GDN_HEADS + hh:GDN_HEADS + hh + 1]
        gcol = gc[r0:r0 + _SUPER, hh:hh + 1]
        grow = gct[hh:hh + 1, r0:r0 + _SUPER]
        decay = jnp.exp(jnp.where(incl, gcol - grow, -jnp.inf))
        eg = jnp.exp(gcol)
        kb = kh * beta
        pre_t.append(dict(kh=kh, khb=kh.astype(BF16), kbb=kb.astype(BF16), vbb=(vh * beta).astype(BF16),
                          kgb=(kb * eg).astype(BF16), qb=qh.astype(BF16), qgb=(qh * eg).astype(BF16),
                          gcol=gcol, decay=decay))
    lows = [jnp.where(strict, _dot_nt(p["kbb"], p["khb"]) * p["decay"], 0.0) for p in pre_t]
    tinvs = [t.astype(BF16) for t in _unit_lower_inverses(lows)]
    us = [_dot(t, p["vbb"]) for t, p in zip(tinvs, pre_t)]
    ws = [_dot(t, p["kgb"]).astype(BF16) for t, p in zip(tinvs, pre_t)]
    atts = [(_dot_nt(p["qb"], p["khb"]) * p["decay"]).astype(BF16) for p in pre_t]

    states = [st_ref[hh] for hh in range(GDN_HEADS)]
    for cc in range(tt // CHUNK):
        r0 = cc * CHUNK
        a0 = r0 % _SUPER
        for hh in range(GDN_HEADS):
            ti = (r0 // _SUPER) * GDN_HEADS + hh
            p = pre_t[ti]
            gl = gc[r0 + CHUNK - 1:r0 + CHUNK, hh:hh + 1]
            kd = (p["kh"][a0:a0 + CHUNK] * jnp.exp(gl - p["gcol"][a0:a0 + CHUNK])).astype(BF16)
            sb = states[hh].astype(BF16)
            vnb = (us[ti][a0:a0 + CHUNK] - _dot(ws[ti][a0:a0 + CHUNK], sb)).astype(BF16)
            o = _dot(p["qgb"][a0:a0 + CHUNK], sb) + _dot(atts[ti][a0:a0 + CHUNK, a0:a0 + CHUNK], vnb)
            states[hh] = states[hh] * jnp.exp(gl) + _dot_tn(kd, vnb)
            zz = z_ref[r0:r0 + CHUNK, hh * GDN_DV:(hh + 1) * GDN_DV]
            o_ref[r0:r0 + CHUNK, hh * GDN_DV:(hh + 1) * GDN_DV] = (
                _rms(o, nw_ref[...]) * _silu(zz)).astype(o_ref.dtype)
    for hh in range(GDN_HEADS):
        st_ref[hh] = states[hh]


def _gdn(qkv, z, ab, cw8, hp, nw, *, batch, seq, tt):
    n = qkv.shape[0]
    per_b = seq // tt
    row = lambda b, j: (b * per_b + j, 0)
    const = lambda b, j: (0, 0)
    hv = GDN_HEADS * GDN_DV
    return pl.pallas_call(
        functools.partial(_gdn_kernel, tt=tt),
        grid=(batch, per_b),
        in_specs=[pl.BlockSpec((tt, _GDN_QKV), row), pl.BlockSpec((tt, hv), row), pl.BlockSpec((tt, LANES), row),
                  pl.BlockSpec(cw8.shape, const), pl.BlockSpec(hp.shape, const), pl.BlockSpec(nw.shape, const)],
        out_specs=pl.BlockSpec((tt, hv), row),
        out_shape=jax.ShapeDtypeStruct((n, hv), BF16),
        scratch_shapes=[pltpu.VMEM((tt + 8, _GDN_QKV), F32), pltpu.VMEM((GDN_HEADS, GDN_DK, GDN_DV), F32)],
        compiler_params=_cparams("parallel", "arbitrary"),
        name="gdn",
    )(qkv, z, ab, cw8, hp, nw)


def _route(logits_t, bias_col):
    e, tm = logits_t.shape
    gsz = e // N_GROUPS
    scores = jax.nn.sigmoid(logits_t)
    biased = scores + bias_col
    sub = lax.broadcasted_iota(jnp.int32, (gsz, tm), 0)
    rows = []
    for g in range(N_GROUPS):
        blk = biased[g * gsz:(g + 1) * gsz]
        m1 = jnp.max(blk, axis=0, keepdims=True)
        first = jnp.min(jnp.where(blk == m1, sub, gsz), axis=0, keepdims=True)
        m2 = jnp.max(jnp.where(sub == first, -jnp.inf, blk), axis=0, keepdims=True)
        rows.append(m1 + m2)
    gs = jnp.concatenate(rows, axis=0)
    gi = lax.broadcasted_iota(jnp.int32, (N_GROUPS, tm), 0)
    grank = jnp.zeros((N_GROUPS, tm), F32)
    for g in range(N_GROUPS):
        r = gs[g:g + 1]
        grank = grank + jnp.where(r > gs, 1.0, jnp.where(r == gs, (gi > g).astype(F32), 0.0))
    gsel = grank < TOPK_GROUPS
    masked = jnp.concatenate(
        [jnp.where(gsel[g:g + 1], biased[g * gsz:(g + 1) * gsz], -jnp.inf) for g in range(N_GROUPS)], axis=0)
    ei = lax.broadcasted_iota(jnp.int32, (e, tm), 0)
    rank = jnp.zeros((e, tm), F32)
    for k in range(e):
        r = masked[k:k + 1]
        rank = rank + jnp.where(r > masked, 1.0, jnp.where(r == masked, (ei > k).astype(F32), 0.0))
    return scores, rank


def _pack_pairs(x):
    c = x.shape[1] // 2
    lo = pltpu.bitcast(x[:, :c].astype(BF16).astype(F32), jnp.uint32)
    hi = pltpu.bitcast(x[:, c:].astype(BF16).astype(F32), jnp.uint32)
    return (hi & jnp.uint32(0xFFFF0000)) | (lo >> 16)


def _unpack_pairs(w):
    return pltpu.bitcast(w << 16, F32), pltpu.bitcast(w & jnp.uint32(0xFFFF0000), F32)


def _mix_kernel(oa_ref, ob_ref, x_ref, mod_ref, woa_ref, wob_ref, n2_ref, rwt_ref, rb_ref, sg_ref, su_ref, sd_ref,
                xs_ref, h2p_ref, te_ref, tw_ref, tp_ref, cnt_ref, run_ref):
    @pl.when(pl.program_id(0) == 0)
    def _():
        run_ref[...] = jnp.zeros(run_ref.shape, F32)

    mod = mod_ref[0]
    g1, sh2, sc2, g2 = mod[2:3], mod[3:4], mod[4:5], mod[5:6]
    mix = _dot(oa_ref[...], woa_ref[...]) + _dot(ob_ref[...], wob_ref[...])
    x1 = x_ref[...] + g1 * mix
    h2 = _rms(x1, n2_ref[...]) * (1.0 + sc2) + sh2
    h2p_ref[...] = _pack_pairs(h2)
    h2b = h2.astype(BF16)
    hs = (_silu(_dot(h2b, sg_ref[...])) * _dot(h2b, su_ref[...])).astype(BF16)
    xs_ref[...] = x1 + g2 * _dot(hs, sd_ref[...])

    scores, rank = _route(_dot_nt(rwt_ref[...], h2, HIGHEST), rb_ref[...])
    e, tm = rank.shape
    sel = jnp.where(rank < TOP_K, 1.0, 0.0)
    earlier = (lax.broadcasted_iota(jnp.int32, (tm, tm), 0) < lax.broadcasted_iota(jnp.int32, (tm, tm), 1))
    posmat = _dot(sel.astype(BF16), earlier.astype(BF16)) + run_ref[...]
    eif = lax.broadcasted_iota(jnp.int32, (e, tm), 0).astype(F32)
    ids, wts, pos = [], [], []
    for k in range(TOP_K):
        hit = rank == float(k)
        ids.append(jnp.sum(jnp.where(hit, eif, 0.0), axis=0, keepdims=True))
        wts.append(jnp.sum(jnp.where(hit, scores, 0.0), axis=0, keepdims=True))
        pos.append(jnp.sum(jnp.where(hit, posmat, 0.0), axis=0, keepdims=True))
    wts = jnp.concatenate(wts, axis=0)
    te_ref[0] = jnp.concatenate(ids, axis=0).astype(jnp.int32)
    tw_ref[0] = wts / jnp.sum(wts, axis=0, keepdims=True) * ROUTED_SCALE
    tp_ref[0] = jnp.concatenate(pos, axis=0).astype(jnp.int32)
    run_ref[...] = run_ref[...] + jnp.sum(sel, axis=1, keepdims=True)
    cnt_ref[...] = jnp.broadcast_to(run_ref[...], cnt_ref.shape)


def _mix(oa, ob, x2, mod8, woa, wob, n2, rwt, rb, sg, su, sd, *, seq, tm):
    n, d = x2.shape
    per_b = seq // tm
    nb = n // tm
    row = lambda i: (i, 0)
    blk3 = lambda i: (i, 0, 0)
    const = lambda i: (0, 0)
    return pl.pallas_call(
        _mix_kernel,
        grid=(nb,),
        in_specs=[pl.BlockSpec((tm, oa.shape[1]), row), pl.BlockSpec((tm, ob.shape[1]), row),
                  pl.BlockSpec((tm, d), row), pl.BlockSpec((1, 8, d), lambda i: (i // per_b, 0, 0)),
                  pl.BlockSpec(woa.shape, const), pl.BlockSpec(wob.shape, const), pl.BlockSpec((1, d), const),
                  pl.BlockSpec(rwt.shape, const), pl.BlockSpec(rb.shape, const),
                  pl.BlockSpec(sg.shape, const), pl.BlockSpec(su.shape, const), pl.BlockSpec(sd.shape, const)],
        out_specs=[pl.BlockSpec((tm, d), row), pl.BlockSpec((tm, d // 2), row),
                   pl.BlockSpec((1, TOP_K, tm), blk3), pl.BlockSpec((1, TOP_K, tm), blk3),
                   pl.BlockSpec((1, TOP_K, tm), blk3), pl.BlockSpec((N_EXPERTS, LANES), const)],
        out_shape=[jax.ShapeDtypeStruct((n, d), F32), jax.ShapeDtypeStruct((n, d // 2), jnp.uint32),
                   jax.ShapeDtypeStruct((nb, TOP_K, tm), jnp.int32), jax.ShapeDtypeStruct((nb, TOP_K, tm), F32),
                   jax.ShapeDtypeStruct((nb, TOP_K, tm), jnp.int32),
                   jax.ShapeDtypeStruct((N_EXPERTS, LANES), F32)],
        scratch_shapes=[pltpu.VMEM((N_EXPERTS, 1), F32)],
        compiler_params=_cparams("arbitrary"),
        name="mix",
    )(oa, ob, x2, mod8, woa, wob, n2, rwt, rb, sg, su, sd)


def _slot(ps_ref, ids_ref, pos_ref, k, t):
    return ps_ref[ids_ref[0, k, t]] + pos_ref[0, k, t]


def _dispatch_kernel(ps_ref, ids_ref, pos_ref, h2p_ref, xs_hbm, zbuf, zsem, sem, *, tm, blk):
    @pl.when(pl.program_id(0) == 0)
    def _():
        zbuf[...] = jnp.zeros(zbuf.shape, zbuf.dtype)

        def tail(e):
            end = ps_ref[e + 1]
            return end > ps_ref[e], pltpu.make_async_copy(
                zbuf, xs_hbm.at[pl.ds(pl.multiple_of(end - blk, blk), blk), :], zsem)

        def unused(e):
            row = pl.multiple_of(ps_ref[N_EXPERTS] + e * blk, blk)
            return row < xs_hbm.shape[0], pltpu.make_async_copy(zbuf, xs_hbm.at[pl.ds(row, blk), :], zsem)

        for fill in (tail, unused):
            for e in range(N_EXPERTS):
                needed, cp = fill(e)
                pl.when(needed)(cp.start)
        for fill in (tail, unused):
            for e in range(N_EXPERTS):
                needed, cp = fill(e)
                pl.when(needed)(cp.wait)

    def issue(t, carry):
        for k in range(TOP_K):
            pltpu.make_async_copy(h2p_ref.at[pl.ds(t, 1), :],
                                  xs_hbm.at[pl.ds(_slot(ps_ref, ids_ref, pos_ref, k, t), 1), :], sem).start()
        return carry

    lax.fori_loop(0, tm, issue, 0)
    for k in range(TOP_K):
        pltpu.make_async_copy(h2p_ref, xs_hbm.at[pl.ds(0, tm), :], sem).wait()


def _dispatch(ps, ids, pos, h2p, *, n_slots, tm, blk):
    n, c = h2p.shape
    blk3 = lambda i, ps: (i, 0, 0)
    grid_spec = pltpu.PrefetchScalarGridSpec(
        num_scalar_prefetch=1,
        grid=(n // tm,),
        in_specs=[pl.BlockSpec((1, TOP_K, tm), blk3, memory_space=pltpu.SMEM),
                  pl.BlockSpec((1, TOP_K, tm), blk3, memory_space=pltpu.SMEM),
                  pl.BlockSpec((tm, c), lambda i, ps: (i, 0))],
        out_specs=pl.BlockSpec(memory_space=pl.ANY),
        scratch_shapes=[pltpu.VMEM((blk, c), jnp.uint32), pltpu.SemaphoreType.DMA(()),
                        pltpu.SemaphoreType.DMA(())],
    )
    return pl.pallas_call(
        functools.partial(_dispatch_kernel, tm=tm, blk=blk),
        grid_spec=grid_spec,
        out_shape=jax.ShapeDtypeStruct((n_slots, c), jnp.uint32),
        compiler_params=_cparams("arbitrary"),
        name="dispatch",
    )(ps, ids, pos, h2p)


def _experts_kernel(be_ref, nu_ref, x_ref, wg_ref, wu_ref, wd_ref, y_ref):
    del be_ref
    used = pl.program_id(0) < nu_ref[0]

    @pl.when(used)
    def _():
        lo, hi = _unpack_pairs(x_ref[...])
        lo, hi = lo.astype(BF16), hi.astype(BF16)
        half = lo.shape[1]
        g = _dot(lo, wg_ref[0, :half, :]) + _dot(hi, wg_ref[0, half:, :])
        u = _dot(lo, wu_ref[0, :half, :]) + _dot(hi, wu_ref[0, half:, :])
        hb = (_silu(g) * u).astype(BF16)
        y_ref[...] = _pack_pairs(_dot(hb, wd_ref[0]))

    @pl.when(jnp.logical_not(used))
    def _():
        y_ref[...] = jnp.zeros(y_ref.shape, y_ref.dtype)


def _experts(block_e, n_used, xs, wg, wu, wd, *, blk):
    n_blocks = block_e.shape[0]
    c = xs.shape[1]
    wsel = lambda i, be, nu: (be[i], 0, 0)
    grid_spec = pltpu.PrefetchScalarGridSpec(
        num_scalar_prefetch=2,
        grid=(n_blocks,),
        in_specs=[pl.BlockSpec((blk, c), lambda i, be, nu: (jnp.minimum(i, nu[0] - 1), 0)),
                  pl.BlockSpec((1,) + wg.shape[1:], wsel), pl.BlockSpec((1,) + wu.shape[1:], wsel),
                  pl.BlockSpec((1,) + wd.shape[1:], wsel)],
        out_specs=pl.BlockSpec((blk, c), lambda i, be, nu: (i, 0)),
    )
    return pl.pallas_call(
        _experts_kernel,
        grid_spec=grid_spec,
        out_shape=jax.ShapeDtypeStruct((n_blocks * blk, c), jnp.uint32),
        compiler_params=_cparams("arbitrary"),
        name="experts",
    )(block_e, n_used, xs, wg, wu, wd)


def _combine_kernel(ps_ref, ids_ref, pos_ref, tw_ref, xs_ref, mod_ref, fn_ref, y_hbm, o_ref, gbuf, sem, *, tm):
    def issue(t, carry):
        for k in range(TOP_K):
            pltpu.make_async_copy(y_hbm.at[pl.ds(_slot(ps_ref, ids_ref, pos_ref, k, t), 1), :],
                                  gbuf.at[pl.ds(k * tm + t, 1), :], sem).start()
        return carry

    lax.fori_loop(0, tm, issue, 0)
    pltpu.make_async_copy(y_hbm.at[pl.ds(0, TOP_K * tm), :], gbuf, sem).wait()

    wt = jnp.concatenate([tw_ref[0], jnp.zeros((LANES - TOP_K, tm), F32)], axis=0).T
    half = gbuf.shape[1]
    acc_lo = jnp.zeros((tm, half), F32)
    acc_hi = jnp.zeros((tm, half), F32)
    for k in range(TOP_K):
        lo, hi = _unpack_pairs(gbuf[k * tm:(k + 1) * tm, :])
        acc_lo = acc_lo + wt[:, k:k + 1] * lo
        acc_hi = acc_hi + wt[:, k:k + 1] * hi
    g2 = mod_ref[0][5:6]
    x_lo = xs_ref[:, :half] + g2[:, :half] * acc_lo
    x_hi = xs_ref[:, half:] + g2[:, half:] * acc_hi
    ms = (jnp.sum(x_lo * x_lo, axis=-1, keepdims=True) + jnp.sum(x_hi * x_hi, axis=-1, keepdims=True)) / (2 * half)
    inv = lax.rsqrt(ms + NORM_EPS)
    o_ref[:, :half] = x_lo * inv * fn_ref[:, :half]
    o_ref[:, half:] = x_hi * inv * fn_ref[:, half:]


def _combine(ps, ids, pos, tw, xs, mod8, fn, y, *, seq, tm):
    n, d = xs.shape
    c = y.shape[1]
    per_b = seq // tm
    blk3 = lambda i, ps: (i, 0, 0)
    grid_spec = pltpu.PrefetchScalarGridSpec(
        num_scalar_prefetch=1,
        grid=(n // tm,),
        in_specs=[pl.BlockSpec((1, TOP_K, tm), blk3, memory_space=pltpu.SMEM),
                  pl.BlockSpec((1, TOP_K, tm), blk3, memory_space=pltpu.SMEM),
                  pl.BlockSpec((1, TOP_K, tm), blk3),
                  pl.BlockSpec((tm, d), lambda i, ps: (i, 0)),
                  pl.BlockSpec((1, 8, d), lambda i, ps: (i // per_b, 0, 0)),
                  pl.BlockSpec((1, d), lambda i, ps: (0, 0)),
                  pl.BlockSpec(memory_space=pl.ANY)],
        out_specs=pl.BlockSpec((tm, d), lambda i, ps: (i, 0)),
        scratch_shapes=[pltpu.VMEM((TOP_K * tm, c), jnp.uint32), pltpu.SemaphoreType.DMA(())],
    )
    return pl.pallas_call(
        functools.partial(_combine_kernel, tm=tm),
        grid_spec=grid_spec,
        out_shape=jax.ShapeDtypeStruct((n, d), F32),
        compiler_params=_cparams("arbitrary"),
        name="combine",
    )(ps, ids, pos, tw, xs, mod8, fn, y)


def _expert_runs(counts, n_blocks, blk):
    padded = (counts + blk - 1) // blk * blk
    pad_end = jnp.cumsum(padded)
    run_start = jnp.concatenate([pad_end - padded, pad_end[-1:]]).astype(jnp.int32)
    block_row = jnp.arange(n_blocks, dtype=jnp.int32) * blk
    block_e = jnp.minimum(jnp.sum(pad_end[None, :] <= block_row[:, None], axis=1), N_EXPERTS - 1).astype(jnp.int32)
    n_used = (pad_end[-1:] // blk).astype(jnp.int32)
    return run_start, block_e, n_used


def _pad_lanes(w, width):
    return jnp.pad(w, ((0, 0), (0, width - w.shape[1])))


def _tiles(seq):
    t = dict(tm_proj=256, tq=512, tt=256, tm_moe=256, blk=256)
    assert all(seq % v == 0 for k, v in t.items() if k != "blk")
    return t


def _layer(x, c, positions, ada_w, ada_b, norm1_w, w_in, q_a_norm_w, wq_b, kv_a_norm_w, wkv_b,
           gdn_conv_w, gdn_a_log, gdn_dt_bias, gdn_norm_w, w_out, norm2_w, router_w, router_bias,
           exp_w_gate, exp_w_up, exp_w_down, sh_w_gate, sh_w_up, sh_w_down, final_norm_w,
           *, tm_proj, tq, tt, tm_moe, blk):
    batch, seq, d = x.shape
    n = batch * seq
    x2 = x.reshape(n, d)

    c8 = jnp.zeros((8, d), F32).at[:batch].set(c)
    mod = _adaln(c8, ada_w, ada_b.reshape(1, -1))
    mod8 = jnp.pad(mod[:batch].reshape(batch, 6, d), ((0, 0), (0, 2), (0, 0)))

    hq = MLA_NOPE + MLA_ROPE
    w_cq, w_ckv, w_kpe, w_qkv, w_z, w_a, w_b = jnp.split(
        w_in, [_C_KV, _C_KV + MLA_KV_RANK, _C_KV + MLA_KV_RANK + MLA_ROPE,
               _C_KV + MLA_KV_RANK + MLA_ROPE + _GDN_QKV,
               _C_KV + MLA_KV_RANK + MLA_ROPE + _GDN_QKV + GDN_HEADS * GDN_DV,
               _C_KV + MLA_KV_RANK + MLA_ROPE + _GDN_QKV + GDN_HEADS * GDN_DV + GDN_HEADS], axis=1)
    win = jnp.concatenate([w_cq, w_ckv, _pad_lanes(w_kpe, LANES), w_qkv, w_z,
                           _pad_lanes(jnp.concatenate([w_a, w_b], axis=1), LANES)], axis=1).astype(BF16)
    wq = jnp.pad(wq_b.reshape(MLA_Q_RANK, MLA_HEADS, hq),
                 ((0, 0), (0, 0), (0, MLA_QK_PAD - hq))).reshape(MLA_Q_RANK, MLA_HEADS * MLA_QK_PAD).astype(BF16)
    wkv4 = wkv_b.reshape(MLA_KV_RANK, MLA_HEADS, MLA_NOPE + MLA_V)
    wkv = jnp.concatenate([wkv4[:, :, :MLA_NOPE].reshape(MLA_KV_RANK, -1),
                           wkv4[:, :, MLA_NOPE:].reshape(MLA_KV_RANK, -1)], axis=1).astype(BF16)
    inv_freq = 1.0 / (ROPE_THETA ** (jnp.arange(0, MLA_ROPE, 2, dtype=F32) / MLA_ROPE))
    invf = _pad_lanes(jnp.concatenate([inv_freq, inv_freq])[None, :], LANES)

    q, k, v, qkv, z, ab = _proj(x2, mod8, positions.reshape(n, 1), norm1_w.reshape(1, d), win,
                                q_a_norm_w.reshape(1, -1), wq, kv_a_norm_w.reshape(1, -1), wkv, invf,
                                seq=seq, tm=tm_proj)
    out_a = _attn(q, k, v, batch=batch, seq=seq, tq=tq)

    cw8 = jnp.pad(gdn_conv_w, ((0, 8 - GDN_CONV), (0, 0)))
    hp = _pad_lanes(jnp.stack([gdn_a_log, gdn_dt_bias]), LANES)
    hp = jnp.pad(hp, ((0, 6), (0, 0)))
    out_b = _gdn(qkv, z, ab, cw8, hp, gdn_norm_w.reshape(1, -1), batch=batch, seq=seq, tt=tt)

    ha = MLA_HEADS * MLA_V
    xs, h2p, ids, tw, pos, counts = _mix(
        out_a, out_b, x2, mod8, w_out[:ha].astype(BF16), w_out[ha:].astype(BF16), norm2_w.reshape(1, d),
        router_w.T, router_bias.reshape(-1, 1), sh_w_gate.astype(BF16), sh_w_up.astype(BF16),
        sh_w_down.astype(BF16), seq=seq, tm=tm_moe)

    n_blocks = n * TOP_K // blk + N_EXPERTS
    run_start, block_e, n_used = _expert_runs(counts[:, 0].astype(jnp.int32), n_blocks, blk)
    xsort = _dispatch(run_start, ids, pos, h2p, n_slots=n_blocks * blk, tm=tm_moe, blk=blk)
    y = _experts(block_e, n_used, xsort, exp_w_gate.astype(BF16), exp_w_up.astype(BF16),
                 exp_w_down.astype(BF16), blk=blk)
    out = _combine(run_start, ids, pos, tw, xs, mod8, final_norm_w.reshape(1, d), y, seq=seq, tm=tm_moe)
    return out.reshape(batch, seq, d)


def kernel(x, c, positions, ada_w, ada_b, norm1_w, w_in, q_a_norm_w, wq_b, kv_a_norm_w, wkv_b, gdn_conv_w,
           gdn_a_log, gdn_dt_bias, gdn_norm_w, w_out, norm2_w, router_w, router_bias, exp_w_gate, exp_w_up,
           exp_w_down, sh_w_gate, sh_w_up, sh_w_down, final_norm_w):
    assert ada_w.shape[0] == 1, "single layer"
    return _layer(
        x, c, positions, ada_w[0], ada_b[0], norm1_w[0], w_in[0], q_a_norm_w[0], wq_b[0], kv_a_norm_w[0],
        wkv_b[0], gdn_conv_w[0], gdn_a_log[0], gdn_dt_bias[0], gdn_norm_w[0], w_out[0], norm2_w[0],
        router_w[0], router_bias[0], exp_w_gate[0], exp_w_up[0], exp_w_down[0], sh_w_gate[0], sh_w_up[0],
        sh_w_down[0], final_norm_w, **_tiles(x.shape[1]))
```

```python
import functools

import jax
import jax.numpy as jnp
from jax import lax
from jax.experimental import pallas as pl
from jax.experimental.pallas import tpu as pltpu

F32 = jnp.float32
BF16 = jnp.bfloat16
HIGHEST = lax.Precision.HIGHEST

CHUNK = 64
NORM_EPS = 1e-6
MLA_HEADS = 4
MLA_Q_RANK = 384
MLA_KV_RANK = 256
MLA_NOPE = 128
MLA_ROPE = 64
MLA_V = 128
ROPE_THETA = 10000.0
GDN_HEADS = 4
GDN_DK = 128
GDN_DV = 128
GDN_CONV = 4
N_EXPERTS = 64
N_GROUPS = 8
TOPK_GROUPS = 4
TOP_K = 8
ROUTED_SCALE = 2.5

LANES = 128
MLA_QK_PAD = 256
VMEM_LIMIT = 56 * 1024 * 1024


def _cparams(*sem):
    return pltpu.CompilerParams(dimension_semantics=sem, vmem_limit_bytes=VMEM_LIMIT)


def _dot(a, b):
    return jnp.dot(a, b, preferred_element_type=F32)


def _split3(a):
    hi = a.astype(BF16)
    r = a - hi.astype(F32)
    mid = r.astype(BF16)
    return hi, mid, (r - mid.astype(F32)).astype(BF16)


def _dot_x3(a, b):
    ah, am, _ = a
    bh, bm, _ = b
    return _dot(ah, bh) + (_dot(ah, bm) + _dot(am, bh))


def _dot_nt(a, b, precision=None):
    return lax.dot_general(a, b, (((1,), (1,)), ((), ())), preferred_element_type=F32, precision=precision)


def _dot_tn(a, b):
    return lax.dot_general(a, b, (((0,), (0,)), ((), ())), preferred_element_type=F32)


def _silu(x):
    return x * jax.nn.sigmoid(x)


def _rms(x, w):
    return x * lax.rsqrt(jnp.mean(x * x, axis=-1, keepdims=True) + NORM_EPS) * w


def _adaln_kernel(c_ref, w_ref, b_ref, o_ref):
    a = _silu(c_ref[...]).astype(BF16)
    o_ref[...] = _dot(a, w_ref[...].astype(BF16)) + b_ref[...]


def _adaln(c8, ada_w, ada_b):
    d = c8.shape[1]
    n_out = ada_w.shape[1]
    return pl.pallas_call(
        _adaln_kernel,
        grid=(n_out // d,),
        in_specs=[pl.BlockSpec((8, d), lambda j: (0, 0)),
                  pl.BlockSpec((d, d), lambda j: (0, j)),
                  pl.BlockSpec((1, d), lambda j: (0, j))],
        out_specs=pl.BlockSpec((8, d), lambda j: (0, j)),
        out_shape=jax.ShapeDtypeStruct((8, n_out), F32),
        compiler_params=_cparams("parallel"),
        name="adaln",
    )(c8, ada_w, ada_b)


_C_Q = 0
_C_KV = _C_Q + MLA_Q_RANK
_C_KPE = _C_KV + MLA_KV_RANK
_C_QKV = _C_KPE + LANES
_GDN_QKV = GDN_HEADS * (2 * GDN_DK + GDN_DV)
_C_Z = _C_QKV + _GDN_QKV
_C_AB = _C_Z + GDN_HEADS * GDN_DV
_D_IN_PAD = _C_AB + LANES


def _rope(xb, cos, sin):
    half = MLA_ROPE // 2
    lane = lax.broadcasted_iota(jnp.int32, xb.shape, 1)
    rot = jnp.where(lane < half, -pltpu.roll(xb, LANES - half, 1), pltpu.roll(xb, half, 1))
    return xb * cos + rot * sin


def _proj_kernel(x_ref, mod_ref, pos_ref, n1_ref, win_ref, qan_ref, wq_ref, kvan_ref, wkv_ref, invf_ref,
                 q_ref, k_ref, v_ref, qkv_ref, z_ref, ab_ref):
    x = x_ref[...]
    mod = mod_ref[0]
    sh1, sc1 = mod[0:1], mod[1:2]
    h = (_rms(x, n1_ref[...]) * (1.0 + sc1) + sh1).astype(BF16)
    proj = _dot(h, win_ref[...])

    ang = pos_ref[...].astype(F32) * invf_ref[...]
    cos, sin = jnp.cos(ang), jnp.sin(ang)

    cq = proj[:, _C_Q:_C_Q + MLA_Q_RANK]
    qn = _rms(cq, qan_ref[...]).astype(BF16)
    q = _dot(qn, wq_ref[...]) * ((MLA_NOPE + MLA_ROPE) ** -0.5)
    for hh in range(MLA_HEADS):
        c0 = hh * MLA_QK_PAD
        q_ref[:, c0:c0 + MLA_NOPE] = q[:, c0:c0 + MLA_NOPE].astype(BF16)
        q_ref[:, c0 + MLA_NOPE:c0 + MLA_QK_PAD] = _rope(q[:, c0 + MLA_NOPE:c0 + MLA_QK_PAD], cos, sin).astype(BF16)

    ckv = proj[:, _C_KV:_C_KV + MLA_KV_RANK]
    kvn = _rms(ckv, kvan_ref[...]).astype(BF16)
    kv = _dot(kvn, wkv_ref[...])
    kpe = _rope(proj[:, _C_KPE:_C_KPE + LANES], cos, sin).astype(BF16)
    for hh in range(MLA_HEADS):
        c0 = hh * MLA_QK_PAD
        k_ref[:, c0:c0 + MLA_NOPE] = kv[:, hh * MLA_NOPE:(hh + 1) * MLA_NOPE].astype(BF16)
        k_ref[:, c0 + MLA_NOPE:c0 + MLA_QK_PAD] = kpe
    v_ref[...] = kv[:, MLA_HEADS * MLA_NOPE:].astype(BF16)

    qkv_ref[...] = proj[:, _C_QKV:_C_QKV + _GDN_QKV]
    z_ref[...] = proj[:, _C_Z:_C_Z + GDN_HEADS * GDN_DV]
    ab_ref[...] = proj[:, _C_AB:_C_AB + LANES]


def _proj(x2, mod8, pos2, n1, win, qan, wq, kvan, wkv, invf, *, seq, tm):
    n, d = x2.shape
    per_b = seq // tm
    row = lambda i: (i, 0)
    const = lambda i: (0, 0)
    hq = MLA_HEADS * MLA_QK_PAD
    hv = MLA_HEADS * MLA_V
    return pl.pallas_call(
        _proj_kernel,
        grid=(n // tm,),
        in_specs=[pl.BlockSpec((tm, d), row),
                  pl.BlockSpec((1, 8, d), lambda i: (i // per_b, 0, 0)),
                  pl.BlockSpec((tm, 1), row),
                  pl.BlockSpec((1, d), const),
                  pl.BlockSpec(win.shape, const),
                  pl.BlockSpec(qan.shape, const),
                  pl.BlockSpec(wq.shape, const),
                  pl.BlockSpec(kvan.shape, const),
                  pl.BlockSpec(wkv.shape, const),
                  pl.BlockSpec(invf.shape, const)],
        out_specs=[pl.BlockSpec((tm, hq), row), pl.BlockSpec((tm, hq), row), pl.BlockSpec((tm, hv), row),
                   pl.BlockSpec((tm, _GDN_QKV), row), pl.BlockSpec((tm, GDN_HEADS * GDN_DV), row),
                   pl.BlockSpec((tm, LANES), row)],
        out_shape=[jax.ShapeDtypeStruct((n, hq), BF16), jax.ShapeDtypeStruct((n, hq), BF16),
                   jax.ShapeDtypeStruct((n, hv), BF16), jax.ShapeDtypeStruct((n, _GDN_QKV), F32),
                   jax.ShapeDtypeStruct((n, GDN_HEADS * GDN_DV), F32), jax.ShapeDtypeStruct((n, LANES), F32)],
        compiler_params=_cparams("parallel"),
        name="proj",
    )(x2, mod8, pos2, n1, win, qan, wq, kvan, wkv, invf)


def _attn_kernel(q_ref, k_ref, v_ref, o_ref, *, tq):
    i = pl.program_id(1)

    def head_step(hh, r0, carry, mask):
        m, l, acc = carry
        s = _dot_nt(q_ref[:, hh * MLA_QK_PAD:(hh + 1) * MLA_QK_PAD],
                    k_ref[pl.ds(r0, tq), hh * MLA_QK_PAD:(hh + 1) * MLA_QK_PAD])
        if mask is not None:
            s = jnp.where(mask, s, -jnp.inf)
        m_new = jnp.maximum(m, jnp.max(s, axis=-1, keepdims=True))
        alpha = jnp.exp(m - m_new)
        p = jnp.exp(s - m_new)
        l = alpha * l + jnp.sum(p, axis=-1, keepdims=True)
        acc = alpha * acc + _dot(p.astype(BF16), v_ref[pl.ds(r0, tq), hh * MLA_V:(hh + 1) * MLA_V])
        return m_new, l, acc

    def block(r0, carry, mask):
        return tuple(head_step(hh, r0, carry[hh], mask) for hh in range(MLA_HEADS))

    init = tuple((jnp.full((tq, 1), -jnp.inf, F32), jnp.zeros((tq, 1), F32), jnp.zeros((tq, MLA_V), F32))
                 for _ in range(MLA_HEADS))
    carry = lax.fori_loop(0, i, lambda j, c: block(pl.multiple_of(j * tq, tq), c, None), init)
    rq = lax.broadcasted_iota(jnp.int32, (tq, tq), 0) // CHUNK
    ck = lax.broadcasted_iota(jnp.int32, (tq, tq), 1) // CHUNK
    carry = block(pl.multiple_of(i * tq, tq), carry, ck <= rq)
    for hh in range(MLA_HEADS):
        _, l, acc = carry[hh]
        o_ref[:, hh * MLA_V:(hh + 1) * MLA_V] = (acc / l).astype(o_ref.dtype)


def _attn(q, k, v, *, batch, seq, tq):
    n = q.shape[0]
    nq = seq // tq
    return pl.pallas_call(
        functools.partial(_attn_kernel, tq=tq),
        grid=(batch, nq),
        in_specs=[pl.BlockSpec((tq, q.shape[1]), lambda b, i: (b * nq + i, 0)),
                  pl.BlockSpec((seq, k.shape[1]), lambda b, i: (b, 0)),
                  pl.BlockSpec((seq, v.shape[1]), lambda b, i: (b, 0))],
        out_specs=pl.BlockSpec((tq, v.shape[1]), lambda b, i: (b * nq + i, 0)),
        out_shape=jax.ShapeDtypeStruct((n, v.shape[1]), BF16),
        compiler_params=_cparams("parallel", "arbitrary"),
        name="attn",
    )(q, k, v)


_SUPER = 2 * CHUNK


def _unit_lower_inverses(lows):
    n = lows[0].shape[0]
    eye = (lax.broadcasted_iota(jnp.int32, (n, n), 0) == lax.broadcasted_iota(jnp.int32, (n, n), 1)).astype(F32)
    ps = [-low for low in lows]
    ts = [eye + p for p in ps]
    levels = CHUNK.bit_length() - 1
    for _ in range(levels - 1):
        parts = [_split3(p) for p in ps]
        ps = [_dot_x3(s, s) for s in parts]
        ts = [t + _dot_x3(_split3(t), _split3(p)) for t, p in zip(ts, ps)]
    return ts


def _gdn_kernel(qkv_ref, z_ref, ab_ref, cw_ref, hp_ref, nw_ref, o_ref, xe_ref, st_ref, *, tt):
    j = pl.program_id(1)
    hd = GDN_HEADS * GDN_DK

    @pl.when(j == 0)
    def _():
        xe_ref[0:8, :] = jnp.zeros((8, xe_ref.shape[1]), F32)
        st_ref[...] = jnp.zeros(st_ref.shape, F32)

    xe_ref[8:8 + tt, :] = qkv_ref[...]
    cw = cw_ref[...]
    y = xe_ref[8:8 + tt, :] * cw[GDN_CONV - 1:GDN_CONV]
    for i in range(1, GDN_CONV):
        y = y + xe_ref[8 - i:8 - i + tt, :] * cw[GDN_CONV - 1 - i:GDN_CONV - i]
    xe_ref[0:8, :] = xe_ref[tt:tt + 8, :]
    act = _silu(y)

    ab = ab_ref[...]
    hp = hp_ref[...]
    pre = ab + hp[1:2]
    softplus = jnp.maximum(pre, 0.0) + jnp.log1p(jnp.exp(-jnp.abs(pre)))
    g_all = -jnp.exp(hp[0:1]) * softplus
    beta_all = jax.nn.sigmoid(ab)

    ri = lax.broadcasted_iota(jnp.int32, (tt, tt), 0)
    ci = lax.broadcasted_iota(jnp.int32, (tt, tt), 1)
    tri = ((ri // CHUNK == ci // CHUNK) & (ci <= ri)).astype(BF16)
    gc = sum(_dot(tri, part) for part in _split3(g_all))
    gct = gc.T

    rs = lax.broadcasted_iota(jnp.int32, (_SUPER, _SUPER), 0)
    cs = lax.broadcasted_iota(jnp.int32, (_SUPER, _SUPER), 1)
    same = rs // CHUNK == cs // CHUNK
    incl = same & (cs <= rs)
    strict = same & (cs < rs)

    tiles = [(s * _SUPER, hh) for s in range(tt // _SUPER) for hh in range(GDN_HEADS)]
    pre_t = []
    for r0, hh in tiles:
        qh = act[r0:r0 + _SUPER, hh * GDN_DK:(hh + 1) * GDN_DK]
        kh = act[r0:r0 + _SUPER, hd + hh * GDN_DK:hd + (hh + 1) * GDN_DK]
        vh = act[r0:r0 + _SUPER, 2 * hd + hh * GDN_DV:2 * hd + (hh + 1) * GDN_DV]
        qh = qh * lax.rsqrt(jnp.sum(qh * qh, axis=-1, keepdims=True) + 1e-6) * (GDN_DK ** -0.5)
        kh = kh * lax.rsqrt(jnp.sum(kh * kh, axis=-1, keepdims=True) + 1e-6)
        beta = beta_all[r0:r0 + _SUPER, GDN_HEADS + hh:GDN_HEADS + hh + 1]
        gcol = gc[r0:r0 + _SUPER, hh:hh + 1]
        grow = gct[hh:hh + 1, r0:r0 + _SUPER]
        decay = jnp.exp(jnp.where(incl, gcol - grow, -jnp.inf))
        eg = jnp.exp(gcol)
        kb = kh * beta
        pre_t.append(dict(kh=kh, khb=kh.astype(BF16), kbb=kb.astype(BF16), vbb=(vh * beta).astype(BF16),
                          kgb=(kb * eg).astype(BF16), qb=qh.astype(BF16), qgb=(qh * eg).astype(BF16),
                          gcol=gcol, decay=decay))
    lows = [jnp.where(strict, _dot_nt(p["kbb"], p["khb"]) * p["decay"], 0.0) for p in pre_t]
    tinvs = [t.astype(BF16) for t in _unit_lower_inverses(lows)]
    us = [_dot(t, p["vbb"]) for t, p in zip(tinvs, pre_t)]
    ws = [_dot(t, p["kgb"]).astype(BF16) for t, p in zip(tinvs, pre_t)]
    atts = [(_dot_nt(p["qb"], p["khb"]) * p["decay"]).astype(BF16) for p in pre_t]

    states = [st_ref[hh] for hh in range(GDN_HEADS)]
    for cc in range(tt // CHUNK):
        r0 = cc * CHUNK
        a0 = r0 % _SUPER
        for hh in range(GDN_HEADS):
            ti = (r0 // _SUPER) * GDN_HEADS + hh
            p = pre_t[ti]
            gl = gc[r0 + CHUNK - 1:r0 + CHUNK, hh:hh + 1]
            kd = (p["kh"][a0:a0 + CHUNK] * jnp.exp(gl - p["gcol"][a0:a0 + CHUNK])).astype(BF16)
            sb = states[hh].astype(BF16)
            vnb = (us[ti][a0:a0 + CHUNK] - _dot(ws[ti][a0:a0 + CHUNK], sb)).astype(BF16)
            o = _dot(p["qgb"][a0:a0 + CHUNK], sb) + _dot(atts[ti][a0:a0 + CHUNK, a0:a0 + CHUNK], vnb)
            states[hh] = states[hh] * jnp.exp(gl) + _dot_tn(kd, vnb)
            zz = z_ref[r0:r0 + CHUNK, hh * GDN_DV:(hh + 1) * GDN_DV]
            o_ref[r0:r0 + CHUNK, hh * GDN_DV:(hh + 1) * GDN_DV] = (
                _rms(o, nw_ref[...]) * _silu(zz)).astype(o_ref.dtype)
    for hh in range(GDN_HEADS):
        st_ref[hh] = states[hh]


def _gdn(qkv, z, ab, cw8, hp, nw, *, batch, seq, tt):
    n = qkv.shape[0]
    per_b = seq // tt
    row = lambda b, j: (b * per_b + j, 0)
    const = lambda b, j: (0, 0)
    hv = GDN_HEADS * GDN_DV
    return pl.pallas_call(
        functools.partial(_gdn_kernel, tt=tt),
        grid=(batch, per_b),
        in_specs=[pl.BlockSpec((tt, _GDN_QKV), row), pl.BlockSpec((tt, hv), row), pl.BlockSpec((tt, LANES), row),
                  pl.BlockSpec(cw8.shape, const), pl.BlockSpec(hp.shape, const), pl.BlockSpec(nw.shape, const)],
        out_specs=pl.BlockSpec((tt, hv), row),
        out_shape=jax.ShapeDtypeStruct((n, hv), BF16),
        scratch_shapes=[pltpu.VMEM((tt + 8, _GDN_QKV), F32), pltpu.VMEM((GDN_HEADS, GDN_DK, GDN_DV), F32)],
        compiler_params=_cparams("parallel", "arbitrary"),
        name="gdn",
    )(qkv, z, ab, cw8, hp, nw)


def _route(logits_t, bias_col):
    e, tm = logits_t.shape
    gsz = e // N_GROUPS
    scores = jax.nn.sigmoid(logits_t)
    biased = scores + bias_col
    sub = lax.broadcasted_iota(jnp.int32, (gsz, tm), 0)
    rows = []
    for g in range(N_GROUPS):
        blk = biased[g * gsz:(g + 1) * gsz]
        m1 = jnp.max(blk, axis=0, keepdims=True)
        first = jnp.min(jnp.where(blk == m1, sub, gsz), axis=0, keepdims=True)
        m2 = jnp.max(jnp.where(sub == first, -jnp.inf, blk), axis=0, keepdims=True)
        rows.append(m1 + m2)
    gs = jnp.concatenate(rows, axis=0)
    gi = lax.broadcasted_iota(jnp.int32, (N_GROUPS, tm), 0)
    grank = jnp.zeros((N_GROUPS, tm), F32)
    for g in range(N_GROUPS):
        r = gs[g:g + 1]
        grank = grank + jnp.where(r > gs, 1.0, jnp.where(r == gs, (gi > g).astype(F32), 0.0))
    gsel = grank < TOPK_GROUPS
    masked = jnp.concatenate(
        [jnp.where(gsel[g:g + 1], biased[g * gsz:(g + 1) * gsz], -jnp.inf) for g in range(N_GROUPS)], axis=0)
    ei = lax.broadcasted_iota(jnp.int32, (e, tm), 0)
    rank = jnp.zeros((e, tm), F32)
    for k in range(e):
        r = masked[k:k + 1]
        rank = rank + jnp.where(r > masked, 1.0, jnp.where(r == masked, (ei > k).astype(F32), 0.0))
    return scores, rank


SUBLANES = 8


def _store_tile_rows(ref, x):
    for j in range(SUBLANES):
        ref[pl.ds(j, x.shape[0], stride=SUBLANES), :] = x[:, j * LANES:(j + 1) * LANES]


def _load_tile_rows(ref, r0, rows):
    return jnp.concatenate(
        [ref[pl.ds(r0 * SUBLANES + j, rows, stride=SUBLANES), :] for j in range(SUBLANES)], axis=1)


def _tile_row(ref, sublane_offset):
    return ref.at[pl.ds(pl.multiple_of(sublane_offset, SUBLANES), SUBLANES), :]


def _mix_kernel(oa_ref, ob_ref, x_ref, mod_ref, woa_ref, wob_ref, n2_ref, rwt_ref, rb_ref, sg_ref, su_ref, sd_ref,
                xs_ref, h2t_ref, te_ref, tw_ref, tp_ref, cnt_ref, run_ref):
    @pl.when(pl.program_id(0) == 0)
    def _():
        run_ref[...] = jnp.zeros(run_ref.shape, F32)

    mod = mod_ref[0]
    g1, sh2, sc2, g2 = mod[2:3], mod[3:4], mod[4:5], mod[5:6]
    mix = _dot(oa_ref[...], woa_ref[...]) + _dot(ob_ref[...], wob_ref[...])
    x1 = x_ref[...] + g1 * mix
    h2 = _rms(x1, n2_ref[...]) * (1.0 + sc2) + sh2
    _store_tile_rows(h2t_ref, h2)
    h2b = h2.astype(BF16)
    hs = (_silu(_dot(h2b, sg_ref[...])) * _dot(h2b, su_ref[...])).astype(BF16)
    xs_ref[...] = x1 + g2 * _dot(hs, sd_ref[...])

    scores, rank = _route(_dot_nt(rwt_ref[...], h2, HIGHEST), rb_ref[...])
    e, tm = rank.shape
    sel = jnp.where(rank < TOP_K, 1.0, 0.0)
    earlier = (lax.broadcasted_iota(jnp.int32, (tm, tm), 0) < lax.broadcasted_iota(jnp.int32, (tm, tm), 1))
    posmat = _dot(sel.astype(BF16), earlier.astype(BF16)) + run_ref[...]
    eif = lax.broadcasted_iota(jnp.int32, (e, tm), 0).astype(F32)
    ids, wts, pos = [], [], []
    for k in range(TOP_K):
        hit = rank == float(k)
        ids.append(jnp.sum(jnp.where(hit, eif, 0.0), axis=0, keepdims=True))
        wts.append(jnp.sum(jnp.where(hit, scores, 0.0), axis=0, keepdims=True))
        pos.append(jnp.sum(jnp.where(hit, posmat, 0.0), axis=0, keepdims=True))
    wts = jnp.concatenate(wts, axis=0)
    te_ref[0] = jnp.concatenate(ids, axis=0).astype(jnp.int32)
    tw_ref[0] = wts / jnp.sum(wts, axis=0, keepdims=True) * ROUTED_SCALE
    tp_ref[0] = jnp.concatenate(pos, axis=0).astype(jnp.int32)
    run_ref[...] = run_ref[...] + jnp.sum(sel, axis=1, keepdims=True)
    cnt_ref[...] = jnp.broadcast_to(run_ref[...], cnt_ref.shape)


def _mix(oa, ob, x2, mod8, woa, wob, n2, rwt, rb, sg, su, sd, *, seq, tm):
    n, d = x2.shape
    per_b = seq // tm
    nb = n // tm
    row = lambda i: (i, 0)
    blk3 = lambda i: (i, 0, 0)
    const = lambda i: (0, 0)
    return pl.pallas_call(
        _mix_kernel,
        grid=(nb,),
        in_specs=[pl.BlockSpec((tm, oa.shape[1]), row), pl.BlockSpec((tm, ob.shape[1]), row),
                  pl.BlockSpec((tm, d), row), pl.BlockSpec((1, 8, d), lambda i: (i // per_b, 0, 0)),
                  pl.BlockSpec(woa.shape, const), pl.BlockSpec(wob.shape, const), pl.BlockSpec((1, d), const),
                  pl.BlockSpec(rwt.shape, const), pl.BlockSpec(rb.shape, const),
                  pl.BlockSpec(sg.shape, const), pl.BlockSpec(su.shape, const), pl.BlockSpec(sd.shape, const)],
        out_specs=[pl.BlockSpec((tm, d), row), pl.BlockSpec((tm * SUBLANES, d // SUBLANES), row),
                   pl.BlockSpec((1, TOP_K, tm), blk3), pl.BlockSpec((1, TOP_K, tm), blk3),
                   pl.BlockSpec((1, TOP_K, tm), blk3), pl.BlockSpec((N_EXPERTS, LANES), const)],
        out_shape=[jax.ShapeDtypeStruct((n, d), F32), jax.ShapeDtypeStruct((n * SUBLANES, d // SUBLANES), F32),
                   jax.ShapeDtypeStruct((nb, TOP_K, tm), jnp.int32), jax.ShapeDtypeStruct((nb, TOP_K, tm), F32),
                   jax.ShapeDtypeStruct((nb, TOP_K, tm), jnp.int32),
                   jax.ShapeDtypeStruct((N_EXPERTS, LANES), F32)],
        scratch_shapes=[pltpu.VMEM((N_EXPERTS, 1), F32)],
        compiler_params=_cparams("arbitrary"),
        name="mix",
    )(oa, ob, x2, mod8, woa, wob, n2, rwt, rb, sg, su, sd)


def _slots_kernel(ps_ref, ids_ref, pos_ref, o_ref):
    ids = ids_ref[0]
    start = jnp.zeros(ids.shape, jnp.int32)
    for e in range(N_EXPERTS):
        start = jnp.where(ids == e, ps_ref[e], start)
    o_ref[0] = (start + pos_ref[0]) * SUBLANES


def _slots(ps, ids, pos):
    blk3 = lambda i, ps: (i, 0, 0)
    spec = pl.BlockSpec((1,) + ids.shape[1:], blk3)
    return pl.pallas_call(
        _slots_kernel,
        grid_spec=pltpu.PrefetchScalarGridSpec(num_scalar_prefetch=1, grid=(ids.shape[0],),
                                               in_specs=[spec, spec], out_specs=spec),
        out_shape=jax.ShapeDtypeStruct(ids.shape, jnp.int32),
        compiler_params=_cparams("parallel"),
        name="slots",
    )(ps, ids, pos)


def _dispatch_kernel(ps_ref, slot_ref, h2t_ref, xs_hbm, zbuf, zsem, sem, *, tm, blk):
    @pl.when(pl.program_id(0) == 0)
    def _():
        zbuf[...] = jnp.zeros(zbuf.shape, zbuf.dtype)

        def block_at(row):
            return xs_hbm.at[pl.ds(pl.multiple_of(row * SUBLANES, blk * SUBLANES), blk * SUBLANES), :]

        def tail(e):
            end = ps_ref[e + 1]
            return end > ps_ref[e], pltpu.make_async_copy(zbuf, block_at(end - blk), zsem)

        def unused(e):
            row = ps_ref[N_EXPERTS] + e * blk
            return row * SUBLANES < xs_hbm.shape[0], pltpu.make_async_copy(zbuf, block_at(row), zsem)

        for fill in (tail, unused):
            for e in range(N_EXPERTS):
                needed, cp = fill(e)
                pl.when(needed)(cp.start)
        for fill in (tail, unused):
            for e in range(N_EXPERTS):
                needed, cp = fill(e)
                pl.when(needed)(cp.wait)

    def issue(t, carry):
        for k in range(TOP_K):
            pltpu.make_async_copy(_tile_row(h2t_ref, t * SUBLANES), _tile_row(xs_hbm, slot_ref[0, k, t]), sem).start()
        return carry

    lax.fori_loop(0, tm, issue, 0)
    for k in range(TOP_K):
        pltpu.make_async_copy(h2t_ref, xs_hbm.at[pl.ds(0, tm * SUBLANES), :], sem).wait()


def _dispatch(ps, slot, h2t, *, n_slots, tm, blk):
    lanes = h2t.shape[1]
    grid_spec = pltpu.PrefetchScalarGridSpec(
        num_scalar_prefetch=1,
        grid=(h2t.shape[0] // (tm * SUBLANES),),
        in_specs=[pl.BlockSpec((1, TOP_K, tm), lambda i, ps: (i, 0, 0), memory_space=pltpu.SMEM),
                  pl.BlockSpec((tm * SUBLANES, lanes), lambda i, ps: (i, 0))],
        out_specs=pl.BlockSpec(memory_space=pl.ANY),
        scratch_shapes=[pltpu.VMEM((blk * SUBLANES, lanes), F32), pltpu.SemaphoreType.DMA(()),
                        pltpu.SemaphoreType.DMA(())],
    )
    return pl.pallas_call(
        functools.partial(_dispatch_kernel, tm=tm, blk=blk),
        grid_spec=grid_spec,
        out_shape=jax.ShapeDtypeStruct((n_slots * SUBLANES, lanes), F32),
        compiler_params=_cparams("arbitrary"),
        name="dispatch",
    )(ps, slot, h2t)


def _experts_kernel(be_ref, nu_ref, x_ref, wg_ref, wu_ref, wd_ref, y_ref):
    del be_ref
    used = pl.program_id(0) < nu_ref[0]

    @pl.when(used)
    def _():
        xb = _load_tile_rows(x_ref, 0, x_ref.shape[0] // SUBLANES).astype(BF16)
        hb = (_silu(_dot(xb, wg_ref[0])) * _dot(xb, wu_ref[0])).astype(BF16)
        _store_tile_rows(y_ref, _dot(hb, wd_ref[0]))

    @pl.when(jnp.logical_not(used))
    def _():
        y_ref[...] = jnp.zeros(y_ref.shape, y_ref.dtype)


def _experts(block_e, n_used, xs, wg, wu, wd, *, blk):
    n_blocks = block_e.shape[0]
    rows_blk = (blk * SUBLANES, xs.shape[1])
    wsel = lambda i, be, nu: (be[i], 0, 0)
    grid_spec = pltpu.PrefetchScalarGridSpec(
        num_scalar_prefetch=2,
        grid=(n_blocks,),
        in_specs=[pl.BlockSpec(rows_blk, lambda i, be, nu: (jnp.minimum(i, nu[0] - 1), 0)),
                  pl.BlockSpec((1,) + wg.shape[1:], wsel), pl.BlockSpec((1,) + wu.shape[1:], wsel),
                  pl.BlockSpec((1,) + wd.shape[1:], wsel)],
        out_specs=pl.BlockSpec(rows_blk, lambda i, be, nu: (i, 0)),
    )
    return pl.pallas_call(
        _experts_kernel,
        grid_spec=grid_spec,
        out_shape=jax.ShapeDtypeStruct(xs.shape, F32),
        compiler_params=_cparams("arbitrary"),
        name="experts",
    )(block_e, n_used, xs, wg, wu, wd)


def _combine_kernel(slot_ref, tw_ref, xs_ref, mod_ref, fn_ref, y_hbm, o_ref, gbuf, sem, *, tm):
    def issue(t, carry):
        for k in range(TOP_K):
            pltpu.make_async_copy(_tile_row(y_hbm, slot_ref[0, k, t]), _tile_row(gbuf, (k * tm + t) * SUBLANES),
                                  sem).start()
        return carry

    lax.fori_loop(0, tm, issue, 0)
    pltpu.make_async_copy(y_hbm.at[pl.ds(0, gbuf.shape[0]), :], gbuf, sem).wait()

    wt = jnp.concatenate([tw_ref[0], jnp.zeros((LANES - TOP_K, tm), F32)], axis=0).T
    acc = wt[:, 0:1] * _load_tile_rows(gbuf, 0, tm)
    for k in range(1, TOP_K):
        acc = acc + wt[:, k:k + 1] * _load_tile_rows(gbuf, k * tm, tm)
    g2 = mod_ref[0][5:6]
    o_ref[...] = _rms(xs_ref[...] + g2 * acc, fn_ref[...])


def _combine(slot, tw, xs, mod8, fn, y, *, seq, tm):
    n, d = xs.shape
    per_b = seq // tm
    blk3 = lambda i: (i, 0, 0)
    return pl.pallas_call(
        functools.partial(_combine_kernel, tm=tm),
        grid=(n // tm,),
        in_specs=[pl.BlockSpec((1, TOP_K, tm), blk3, memory_space=pltpu.SMEM),
                  pl.BlockSpec((1, TOP_K, tm), blk3),
                  pl.BlockSpec((tm, d), lambda i: (i, 0)),
                  pl.BlockSpec((1, 8, d), lambda i: (i // per_b, 0, 0)),
                  pl.BlockSpec((1, d), lambda i: (0, 0)),
                  pl.BlockSpec(memory_space=pl.ANY)],
        out_specs=pl.BlockSpec((tm, d), lambda i: (i, 0)),
        out_shape=jax.ShapeDtypeStruct((n, d), F32),
        scratch_shapes=[pltpu.VMEM((TOP_K * tm * SUBLANES, y.shape[1]), F32), pltpu.SemaphoreType.DMA(())],
        compiler_params=_cparams("arbitrary"),
        name="combine",
    )(slot, tw, xs, mod8, fn, y)


def _expert_runs(counts, n_blocks, blk):
    padded = (counts + blk - 1) // blk * blk
    pad_end = jnp.cumsum(padded)
    run_start = jnp.concatenate([pad_end - padded, pad_end[-1:]]).astype(jnp.int32)
    block_row = jnp.arange(n_blocks, dtype=jnp.int32) * blk
    block_e = jnp.minimum(jnp.sum(pad_end[None, :] <= block_row[:, None], axis=1), N_EXPERTS - 1).astype(jnp.int32)
    n_used = (pad_end[-1:] // blk).astype(jnp.int32)
    return run_start, block_e, n_used


def _pad_lanes(w, width):
    return jnp.pad(w, ((0, 0), (0, width - w.shape[1])))


def _tiles(seq):
    t = dict(tm_proj=256, tq=512, tt=256, tm_moe=256, blk=256)
    assert all(seq % v == 0 for k, v in t.items() if k != "blk")
    return t


def _layer(x, c, positions, ada_w, ada_b, norm1_w, w_in, q_a_norm_w, wq_b, kv_a_norm_w, wkv_b,
           gdn_conv_w, gdn_a_log, gdn_dt_bias, gdn_norm_w, w_out, norm2_w, router_w, router_bias,
           exp_w_gate, exp_w_up, exp_w_down, sh_w_gate, sh_w_up, sh_w_down, final_norm_w,
           *, tm_proj, tq, tt, tm_moe, blk):
    batch, seq, d = x.shape
    n = batch * seq
    x2 = x.reshape(n, d)

    c8 = jnp.zeros((8, d), F32).at[:batch].set(c)
    mod = _adaln(c8, ada_w, ada_b.reshape(1, -1))
    mod8 = jnp.pad(mod[:batch].reshape(batch, 6, d), ((0, 0), (0, 2), (0, 0)))

    hq = MLA_NOPE + MLA_ROPE
    w_cq, w_ckv, w_kpe, w_qkv, w_z, w_a, w_b = jnp.split(
        w_in, [_C_KV, _C_KV + MLA_KV_RANK, _C_KV + MLA_KV_RANK + MLA_ROPE,
               _C_KV + MLA_KV_RANK + MLA_ROPE + _GDN_QKV,
               _C_KV + MLA_KV_RANK + MLA_ROPE + _GDN_QKV + GDN_HEADS * GDN_DV,
               _C_KV + MLA_KV_RANK + MLA_ROPE + _GDN_QKV + GDN_HEADS * GDN_DV + GDN_HEADS], axis=1)
    win = jnp.concatenate([w_cq, w_ckv, _pad_lanes(w_kpe, LANES), w_qkv, w_z,
                           _pad_lanes(jnp.concatenate([w_a, w_b], axis=1), LANES)], axis=1).astype(BF16)
    wq = jnp.pad(wq_b.reshape(MLA_Q_RANK, MLA_HEADS, hq),
                 ((0, 0), (0, 0), (0, MLA_QK_PAD - hq))).reshape(MLA_Q_RANK, MLA_HEADS * MLA_QK_PAD).astype(BF16)
    wkv4 = wkv_b.reshape(MLA_KV_RANK, MLA_HEADS, MLA_NOPE + MLA_V)
    wkv = jnp.concatenate([wkv4[:, :, :MLA_NOPE].reshape(MLA_KV_RANK, -1),
                           wkv4[:, :, MLA_NOPE:].reshape(MLA_KV_RANK, -1)], axis=1).astype(BF16)
    inv_freq = 1.0 / (ROPE_THETA ** (jnp.arange(0, MLA_ROPE, 2, dtype=F32) / MLA_ROPE))
    invf = _pad_lanes(jnp.concatenate([inv_freq, inv_freq])[None, :], LANES)

    q, k, v, qkv, z, ab = _proj(x2, mod8, positions.reshape(n, 1), norm1_w.reshape(1, d), win,
                                q_a_norm_w.reshape(1, -1), wq, kv_a_norm_w.reshape(1, -1), wkv, invf,
                                seq=seq, tm=tm_proj)
    out_a = _attn(q, k, v, batch=batch, seq=seq, tq=tq)

    cw8 = jnp.pad(gdn_conv_w, ((0, 8 - GDN_CONV), (0, 0)))
    hp = _pad_lanes(jnp.stack([gdn_a_log, gdn_dt_bias]), LANES)
    hp = jnp.pad(hp, ((0, 6), (0, 0)))
    out_b = _gdn(qkv, z, ab, cw8, hp, gdn_norm_w.reshape(1, -1), batch=batch, seq=seq, tt=tt)

    ha = MLA_HEADS * MLA_V
    xs, h2t, ids, tw, pos, counts = _mix(
        out_a, out_b, x2, mod8, w_out[:ha].astype(BF16), w_out[ha:].astype(BF16), norm2_w.reshape(1, d),
        router_w.T, router_bias.reshape(-1, 1), sh_w_gate.astype(BF16), sh_w_up.astype(BF16),
        sh_w_down.astype(BF16), seq=seq, tm=tm_moe)

    n_blocks = n * TOP_K // blk + N_EXPERTS
    run_start, block_e, n_used = _expert_runs(counts[:, 0].astype(jnp.int32), n_blocks, blk)
    slot = _slots(run_start, ids, pos)
    xsort = _dispatch(run_start, slot, h2t, n_slots=n_blocks * blk, tm=tm_moe, blk=blk)
    y = _experts(block_e, n_used, xsort, exp_w_gate.astype(BF16), exp_w_up.astype(BF16),
                 exp_w_down.astype(BF16), blk=blk)
    out = _combine(slot, tw, xs, mod8, final_norm_w.reshape(1, d), y, seq=seq, tm=tm_moe)
    return out.reshape(batch, seq, d)


def kernel(x, c, positions, ada_w, ada_b, norm1_w, w_in, q_a_norm_w, wq_b, kv_a_norm_w, wkv_b, gdn_conv_w,
           gdn_a_log, gdn_dt_bias, gdn_norm_w, w_out, norm2_w, router_w, router_bias, exp_w_gate, exp_w_up,
           exp_w_down, sh_w_gate, sh_w_up, sh_w_down, final_norm_w):
    assert ada_w.shape[0] == 1, "single layer"
    return _layer(
        x, c, positions, ada_w[0], ada_b[0], norm1_w[0], w_in[0], q_a_norm_w[0], wq_b[0], kv_a_norm_w[0],
        wkv_b[0], gdn_conv_w[0], gdn_a_log[0], gdn_dt_bias[0], gdn_norm_w[0], w_out[0], norm2_w[0],
        router_w[0], router_bias[0], exp_w_gate[0], exp_w_up[0], exp_w_down[0], sh_w_gate[0], sh_w_up[0],
        sh_w_down[0], final_norm_w, **_tiles(x.shape[1]))
```

```python
import functools

import jax
import jax.numpy as jnp
from jax import lax
from jax.experimental import pallas as pl
from jax.experimental.pallas import tpu as pltpu

F32 = jnp.float32
BF16 = jnp.bfloat16
HIGHEST = lax.Precision.HIGHEST

CHUNK = 64
NORM_EPS = 1e-6
MLA_HEADS = 4
MLA_Q_RANK = 384
MLA_KV_RANK = 256
MLA_NOPE = 128
MLA_ROPE = 64
MLA_V = 128
ROPE_THETA = 10000.0
GDN_HEADS = 4
GDN_DK = 128
GDN_DV = 128
GDN_CONV = 4
N_EXPERTS = 64
N_GROUPS = 8
TOPK_GROUPS = 4
TOP_K = 8
ROUTED_SCALE = 2.5

LANES = 128
MLA_QK_PAD = 256
VMEM_LIMIT = 56 * 1024 * 1024


def _cparams(*sem):
    return pltpu.CompilerParams(dimension_semantics=sem, vmem_limit_bytes=VMEM_LIMIT)


def _dot(a, b):
    return jnp.dot(a, b, preferred_element_type=F32)


def _split3(a):
    hi = a.astype(BF16)
    r = a - hi.astype(F32)
    mid = r.astype(BF16)
    return hi, mid, (r - mid.astype(F32)).astype(BF16)


def _dot_x3(a, b):
    ah, am, _ = a
    bh, bm, _ = b
    return _dot(ah, bh) + (_dot(ah, bm) + _dot(am, bh))


def _dot_nt(a, b, precision=None):
    return lax.dot_general(a, b, (((1,), (1,)), ((), ())), preferred_element_type=F32, precision=precision)


def _dot_tn(a, b):
    return lax.dot_general(a, b, (((0,), (0,)), ((), ())), preferred_element_type=F32)


def _silu(x):
    return x * jax.nn.sigmoid(x)


def _rms(x, w):
    return x * lax.rsqrt(jnp.mean(x * x, axis=-1, keepdims=True) + NORM_EPS) * w


def _adaln_kernel(c_ref, w_ref, b_ref, o_ref):
    a = _silu(c_ref[...]).astype(BF16)
    o_ref[...] = _dot(a, w_ref[...].astype(BF16)) + b_ref[...]


def _adaln(c8, ada_w, ada_b):
    d = c8.shape[1]
    n_out = ada_w.shape[1]
    return pl.pallas_call(
        _adaln_kernel,
        grid=(n_out // d,),
        in_specs=[pl.BlockSpec((8, d), lambda j: (0, 0)),
                  pl.BlockSpec((d, d), lambda j: (0, j)),
                  pl.BlockSpec((1, d), lambda j: (0, j))],
        out_specs=pl.BlockSpec((8, d), lambda j: (0, j)),
        out_shape=jax.ShapeDtypeStruct((8, n_out), F32),
        compiler_params=_cparams("parallel"),
        name="adaln",
    )(c8, ada_w, ada_b)


_C_Q = 0
_C_KV = _C_Q + MLA_Q_RANK
_C_KPE = _C_KV + MLA_KV_RANK
_C_QKV = _C_KPE + LANES
_GDN_QKV = GDN_HEADS * (2 * GDN_DK + GDN_DV)
_C_Z = _C_QKV + _GDN_QKV
_C_AB = _C_Z + GDN_HEADS * GDN_DV
_D_IN_PAD = _C_AB + LANES


def _rope(xb, cos, sin):
    half = MLA_ROPE // 2
    lane = lax.broadcasted_iota(jnp.int32, xb.shape, 1)
    rot = jnp.where(lane < half, -pltpu.roll(xb, LANES - half, 1), pltpu.roll(xb, half, 1))
    return xb * cos + rot * sin


def _proj_kernel(x_ref, mod_ref, pos_ref, n1_ref, win_ref, qan_ref, wq_ref, kvan_ref, wkv_ref, invf_ref,
                 q_ref, k_ref, v_ref, qkv_ref, z_ref, ab_ref):
    x = x_ref[...]
    mod = mod_ref[0]
    sh1, sc1 = mod[0:1], mod[1:2]
    h = (_rms(x, n1_ref[...]) * (1.0 + sc1) + sh1).astype(BF16)
    proj = _dot(h, win_ref[...])

    ang = pos_ref[...].astype(F32) * invf_ref[...]
    cos, sin = jnp.cos(ang), jnp.sin(ang)

    cq = proj[:, _C_Q:_C_Q + MLA_Q_RANK]
    qn = _rms(cq, qan_ref[...]).astype(BF16)
    q = _dot(qn, wq_ref[...]) * ((MLA_NOPE + MLA_ROPE) ** -0.5)
    for hh in range(MLA_HEADS):
        c0 = hh * MLA_QK_PAD
        q_ref[:, c0:c0 + MLA_NOPE] = q[:, c0:c0 + MLA_NOPE].astype(BF16)
        q_ref[:, c0 + MLA_NOPE:c0 + MLA_QK_PAD] = _rope(q[:, c0 + MLA_NOPE:c0 + MLA_QK_PAD], cos, sin).astype(BF16)

    ckv = proj[:, _C_KV:_C_KV + MLA_KV_RANK]
    kvn = _rms(ckv, kvan_ref[...]).astype(BF16)
    kv = _dot(kvn, wkv_ref[...])
    kpe = _rope(proj[:, _C_KPE:_C_KPE + LANES], cos, sin).astype(BF16)
    for hh in range(MLA_HEADS):
        c0 = hh * MLA_QK_PAD
        k_ref[:, c0:c0 + MLA_NOPE] = kv[:, hh * MLA_NOPE:(hh + 1) * MLA_NOPE].astype(BF16)
        k_ref[:, c0 + MLA_NOPE:c0 + MLA_QK_PAD] = kpe
    v_ref[...] = kv[:, MLA_HEADS * MLA_NOPE:].astype(BF16)

    qkv_ref[...] = proj[:, _C_QKV:_C_QKV + _GDN_QKV]
    z_ref[...] = proj[:, _C_Z:_C_Z + GDN_HEADS * GDN_DV]
    ab_ref[...] = proj[:, _C_AB:_C_AB + LANES]


def _proj(x2, mod8, pos2, n1, win, qan, wq, kvan, wkv, invf, *, seq, tm):
    n, d = x2.shape
    per_b = seq // tm
    row = lambda i: (i, 0)
    const = lambda i: (0, 0)
    hq = MLA_HEADS * MLA_QK_PAD
    hv = MLA_HEADS * MLA_V
    return pl.pallas_call(
        _proj_kernel,
        grid=(n // tm,),
        in_specs=[pl.BlockSpec((tm, d), row),
                  pl.BlockSpec((1, 8, d), lambda i: (i // per_b, 0, 0)),
                  pl.BlockSpec((tm, 1), row),
                  pl.BlockSpec((1, d), const),
                  pl.BlockSpec(win.shape, const),
                  pl.BlockSpec(qan.shape, const),
                  pl.BlockSpec(wq.shape, const),
                  pl.BlockSpec(kvan.shape, const),
                  pl.BlockSpec(wkv.shape, const),
                  pl.BlockSpec(invf.shape, const)],
        out_specs=[pl.BlockSpec((tm, hq), row), pl.BlockSpec((tm, hq), row), pl.BlockSpec((tm, hv), row),
                   pl.BlockSpec((tm, _GDN_QKV), row), pl.BlockSpec((tm, GDN_HEADS * GDN_DV), row),
                   pl.BlockSpec((tm, LANES), row)],
        out_shape=[jax.ShapeDtypeStruct((n, hq), BF16), jax.ShapeDtypeStruct((n, hq), BF16),
                   jax.ShapeDtypeStruct((n, hv), BF16), jax.ShapeDtypeStruct((n, _GDN_QKV), F32),
                   jax.ShapeDtypeStruct((n, GDN_HEADS * GDN_DV), F32), jax.ShapeDtypeStruct((n, LANES), F32)],
        compiler_params=_cparams("parallel"),
        name="proj",
    )(x2, mod8, pos2, n1, win, qan, wq, kvan, wkv, invf)


def _attn_kernel(q_ref, k_ref, v_ref, o_ref, *, tq):
    i = pl.program_id(1)

    def head_step(hh, r0, carry, mask):
        m, l, acc = carry
        s = _dot_nt(q_ref[:, hh * MLA_QK_PAD:(hh + 1) * MLA_QK_PAD],
                    k_ref[pl.ds(r0, tq), hh * MLA_QK_PAD:(hh + 1) * MLA_QK_PAD])
        if mask is not None:
            s = jnp.where(mask, s, -jnp.inf)
        m_new = jnp.maximum(m, jnp.max(s, axis=-1, keepdims=True))
        alpha = jnp.exp(m - m_new)
        p = jnp.exp(s - m_new)
        l = alpha * l + jnp.sum(p, axis=-1, keepdims=True)
        acc = alpha * acc + _dot(p.astype(BF16), v_ref[pl.ds(r0, tq), hh * MLA_V:(hh + 1) * MLA_V])
        return m_new, l, acc

    def block(r0, carry, mask):
        return tuple(head_step(hh, r0, carry[hh], mask) for hh in range(MLA_HEADS))

    init = tuple((jnp.full((tq, 1), -jnp.inf, F32), jnp.zeros((tq, 1), F32), jnp.zeros((tq, MLA_V), F32))
                 for _ in range(MLA_HEADS))
    carry = lax.fori_loop(0, i, lambda j, c: block(pl.multiple_of(j * tq, tq), c, None), init)
    rq = lax.broadcasted_iota(jnp.int32, (tq, tq), 0) // CHUNK
    ck = lax.broadcasted_iota(jnp.int32, (tq, tq), 1) // CHUNK
    carry = block(pl.multiple_of(i * tq, tq), carry, ck <= rq)
    for hh in range(MLA_HEADS):
        _, l, acc = carry[hh]
        o_ref[:, hh * MLA_V:(hh + 1) * MLA_V] = (acc / l).astype(o_ref.dtype)


def _attn(q, k, v, *, batch, seq, tq):
    n = q.shape[0]
    nq = seq // tq
    return pl.pallas_call(
        functools.partial(_attn_kernel, tq=tq),
        grid=(batch, nq),
        in_specs=[pl.BlockSpec((tq, q.shape[1]), lambda b, i: (b * nq + i, 0)),
                  pl.BlockSpec((seq, k.shape[1]), lambda b, i: (b, 0)),
                  pl.BlockSpec((seq, v.shape[1]), lambda b, i: (b, 0))],
        out_specs=pl.BlockSpec((tq, v.shape[1]), lambda b, i: (b * nq + i, 0)),
        out_shape=jax.ShapeDtypeStruct((n, v.shape[1]), BF16),
        compiler_params=_cparams("parallel", "arbitrary"),
        name="attn",
    )(q, k, v)


_SUPER = 2 * CHUNK


def _unit_lower_inverses(lows):
    n = lows[0].shape[0]
    eye = (lax.broadcasted_iota(jnp.int32, (n, n), 0) == lax.broadcasted_iota(jnp.int32, (n, n), 1)).astype(F32)
    ps = [-low for low in lows]
    ts = [eye + p for p in ps]
    levels = CHUNK.bit_length() - 1
    for _ in range(levels - 1):
        parts = [_split3(p) for p in ps]
        ps = [_dot_x3(s, s) for s in parts]
        ts = [t + _dot_x3(_split3(t), _split3(p)) for t, p in zip(ts, ps)]
    return ts


def _gdn_kernel(qkv_ref, z_ref, ab_ref, cw_ref, hp_ref, nw_ref, o_ref, xe_ref, st_ref, *, tt):
    j = pl.program_id(1)
    hd = GDN_HEADS * GDN_DK

    @pl.when(j == 0)
    def _():
        xe_ref[0:8, :] = jnp.zeros((8, xe_ref.shape[1]), F32)
        st_ref[...] = jnp.zeros(st_ref.shape, F32)

    xe_ref[8:8 + tt, :] = qkv_ref[...]
    cw = cw_ref[...]
    y = xe_ref[8:8 + tt, :] * cw[GDN_CONV - 1:GDN_CONV]
    for i in range(1, GDN_CONV):
        y = y + xe_ref[8 - i:8 - i + tt, :] * cw[GDN_CONV - 1 - i:GDN_CONV - i]
    xe_ref[0:8, :] = xe_ref[tt:tt + 8, :]
    act = _silu(y)

    ab = ab_ref[...]
    hp = hp_ref[...]
    pre = ab + hp[1:2]
    softplus = jnp.maximum(pre, 0.0) + jnp.log1p(jnp.exp(-jnp.abs(pre)))
    g_all = -jnp.exp(hp[0:1]) * softplus
    beta_all = jax.nn.sigmoid(ab)

    ri = lax.broadcasted_iota(jnp.int32, (tt, tt), 0)
    ci = lax.broadcasted_iota(jnp.int32, (tt, tt), 1)
    tri = ((ri // CHUNK == ci // CHUNK) & (ci <= ri)).astype(BF16)
    gc = sum(_dot(tri, part) for part in _split3(g_all))
    gct = gc.T

    rs = lax.broadcasted_iota(jnp.int32, (_SUPER, _SUPER), 0)
    cs = lax.broadcasted_iota(jnp.int32, (_SUPER, _SUPER), 1)
    same = rs // CHUNK == cs // CHUNK
    incl = same & (cs <= rs)
    strict = same & (cs < rs)

    tiles = [(s * _SUPER, hh) for s in range(tt // _SUPER) for hh in range(GDN_HEADS)]
    pre_t = []
    for r0, hh in tiles:
        qh = act[r0:r0 + _SUPER, hh * GDN_DK:(hh + 1) * GDN_DK]
        kh = act[r0:r0 + _SUPER, hd + hh * GDN_DK:hd + (hh + 1) * GDN_DK]
        vh = act[r0:r0 + _SUPER, 2 * hd + hh * GDN_DV:2 * hd + (hh + 1) * GDN_DV]
        qh = qh * lax.rsqrt(jnp.sum(qh * qh, axis=-1, keepdims=True) + 1e-6) * (GDN_DK ** -0.5)
        kh = kh * lax.rsqrt(jnp.sum(kh * kh, axis=-1, keepdims=True) + 1e-6)
        beta = beta_all[r0:r0 + _SUPER, GDN_HEADS + hh:GDN_HEADS + hh + 1]
        gcol = gc[r0:r0 + _SUPER, hh:hh + 1]
        grow = gct[hh:hh + 1, r0:r0 + _SUPER]
        decay = jnp.exp(jnp.where(incl, gcol - grow, -jnp.inf))
        eg = jnp.exp(gcol)
        kb = kh * beta
        pre_t.append(dict(kh=kh, khb=kh.astype(BF16), kbb=kb.astype(BF16), vbb=(vh * beta).astype(BF16),
                          kgb=(kb * eg).astype(BF16), qb=qh.astype(BF16), qgb=(qh * eg).astype(BF16),
                          gcol=gcol, decay=decay))
    lows = [jnp.where(strict, _dot_nt(p["kbb"], p["khb"]) * p["decay"], 0.0) for p in pre_t]
    tinvs = [t.astype(BF16) for t in _unit_lower_inverses(lows)]
    us = [_dot(t, p["vbb"]) for t, p in zip(tinvs, pre_t)]
    ws = [_dot(t, p["kgb"]).astype(BF16) for t, p in zip(tinvs, pre_t)]
    atts = [(_dot_nt(p["qb"], p["khb"]) * p["decay"]).astype(BF16) for p in pre_t]

    states = [st_ref[hh] for hh in range(GDN_HEADS)]
    for cc in range(tt // CHUNK):
        r0 = cc * CHUNK
        a0 = r0 % _SUPER
        for hh in range(GDN_HEADS):
            ti = (r0 // _SUPER) * GDN_HEADS + hh
            p = pre_t[ti]
            gl = gc[r0 + CHUNK - 1:r0 + CHUNK, hh:hh + 1]
            kd = (p["kh"][a0:a0 + CHUNK] * jnp.exp(gl - p["gcol"][a0:a0 + CHUNK])).astype(BF16)
            sb = states[hh].astype(BF16)
            vnb = (us[ti][a0:a0 + CHUNK] - _dot(ws[ti][a0:a0 + CHUNK], sb)).astype(BF16)
            o = _dot(p["qgb"][a0:a0 + CHUNK], sb) + _dot(atts[ti][a0:a0 + CHUNK, a0:a0 + CHUNK], vnb)
            states[hh] = states[hh] * jnp.exp(gl) + _dot_tn(kd, vnb)
            zz = z_ref[r0:r0 + CHUNK, hh * GDN_DV:(hh + 1) * GDN_DV]
            o_ref[r0:r0 + CHUNK, hh * GDN_DV:(hh + 1) * GDN_DV] = (
                _rms(o, nw_ref[...]) * _silu(zz)).astype(o_ref.dtype)
    for hh in range(GDN_HEADS):
        st_ref[hh] = states[hh]


def _gdn(qkv, z, ab, cw8, hp, nw, *, batch, seq, tt):
    n = qkv.shape[0]
    per_b = seq // tt
    row = lambda b, j: (b * per_b + j, 0)
    const = lambda b, j: (0, 0)
    hv = GDN_HEADS * GDN_DV
    return pl.pallas_call(
        functools.partial(_gdn_kernel, tt=tt),
        grid=(batch, per_b),
        in_specs=[pl.BlockSpec((tt, _GDN_QKV), row), pl.BlockSpec((tt, hv), row), pl.BlockSpec((tt, LANES), row),
                  pl.BlockSpec(cw8.shape, const), pl.BlockSpec(hp.shape, const), pl.BlockSpec(nw.shape, const)],
        out_specs=pl.BlockSpec((tt, hv), row),
        out_shape=jax.ShapeDtypeStruct((n, hv), BF16),
        scratch_shapes=[pltpu.VMEM((tt + 8, _GDN_QKV), F32), pltpu.VMEM((GDN_HEADS, GDN_DK, GDN_DV), F32)],
        compiler_params=_cparams("parallel", "arbitrary"),
        name="gdn",
    )(qkv, z, ab, cw8, hp, nw)


def _route(logits_t, bias_col):
    e, tm = logits_t.shape
    gsz = e // N_GROUPS
    scores = jax.nn.sigmoid(logits_t)
    biased = scores + bias_col
    sub = lax.broadcasted_iota(jnp.int32, (gsz, tm), 0)
    rows = []
    for g in range(N_GROUPS):
        blk = biased[g * gsz:(g + 1) * gsz]
        m1 = jnp.max(blk, axis=0, keepdims=True)
        first = jnp.min(jnp.where(blk == m1, sub, gsz), axis=0, keepdims=True)
        m2 = jnp.max(jnp.where(sub == first, -jnp.inf, blk), axis=0, keepdims=True)
        rows.append(m1 + m2)
    gs = jnp.concatenate(rows, axis=0)
    gi = lax.broadcasted_iota(jnp.int32, (N_GROUPS, tm), 0)
    grank = jnp.zeros((N_GROUPS, tm), F32)
    for g in range(N_GROUPS):
        r = gs[g:g + 1]
        grank = grank + jnp.where(r > gs, 1.0, jnp.where(r == gs, (gi > g).astype(F32), 0.0))
    gsel = grank < TOPK_GROUPS
    masked = jnp.concatenate(
        [jnp.where(gsel[g:g + 1], biased[g * gsz:(g + 1) * gsz], -jnp.inf) for g in range(N_GROUPS)], axis=0)
    ei = lax.broadcasted_iota(jnp.int32, (e, tm), 0)
    rank = jnp.zeros((e, tm), F32)
    for k in range(e):
        r = masked[k:k + 1]
        rank = rank + jnp.where(r > masked, 1.0, jnp.where(r == masked, (ei > k).astype(F32), 0.0))
    return scores, rank


SUBLANES = 4


def _store_tile_rows(ref, x):
    c = x.shape[1] // 2
    lo = pltpu.bitcast(x[:, :c].astype(BF16).astype(F32), jnp.uint32)
    hi = pltpu.bitcast(x[:, c:].astype(BF16).astype(F32), jnp.uint32)
    w = (hi & jnp.uint32(0xFFFF0000)) | (lo >> 16)
    for j in range(SUBLANES):
        ref[pl.ds(j, x.shape[0], stride=SUBLANES), :] = w[:, j * LANES:(j + 1) * LANES]


def _load_tile_rows(ref, r0, rows):
    w = jnp.concatenate(
        [ref[pl.ds(r0 * SUBLANES + j, rows, stride=SUBLANES), :] for j in range(SUBLANES)], axis=1)
    return jnp.concatenate([pltpu.bitcast(w << 16, F32), pltpu.bitcast(w & jnp.uint32(0xFFFF0000), F32)], axis=1)


def _tile_row(ref, sublane_offset):
    return ref.at[pl.ds(pl.multiple_of(sublane_offset, SUBLANES), SUBLANES), :]


def _mix_kernel(oa_ref, ob_ref, x_ref, mod_ref, woa_ref, wob_ref, n2_ref, rwt_ref, rb_ref, sg_ref, su_ref, sd_ref,
                xs_ref, h2t_ref, te_ref, tw_ref, tp_ref, cnt_ref, run_ref):
    @pl.when(pl.program_id(0) == 0)
    def _():
        run_ref[...] = jnp.zeros(run_ref.shape, F32)

    mod = mod_ref[0]
    g1, sh2, sc2, g2 = mod[2:3], mod[3:4], mod[4:5], mod[5:6]
    mix = _dot(oa_ref[...], woa_ref[...]) + _dot(ob_ref[...], wob_ref[...])
    x1 = x_ref[...] + g1 * mix
    h2 = _rms(x1, n2_ref[...]) * (1.0 + sc2) + sh2
    _store_tile_rows(h2t_ref, h2)
    h2b = h2.astype(BF16)
    hs = (_silu(_dot(h2b, sg_ref[...])) * _dot(h2b, su_ref[...])).astype(BF16)
    xs_ref[...] = x1 + g2 * _dot(hs, sd_ref[...])

    scores, rank = _route(_dot_nt(rwt_ref[...], h2, HIGHEST), rb_ref[...])
    e, tm = rank.shape
    sel = jnp.where(rank < TOP_K, 1.0, 0.0)
    earlier = (lax.broadcasted_iota(jnp.int32, (tm, tm), 0) < lax.broadcasted_iota(jnp.int32, (tm, tm), 1))
    posmat = _dot(sel.astype(BF16), earlier.astype(BF16)) + run_ref[...]
    eif = lax.broadcasted_iota(jnp.int32, (e, tm), 0).astype(F32)
    ids, wts, pos = [], [], []
    for k in range(TOP_K):
        hit = rank == float(k)
        ids.append(jnp.sum(jnp.where(hit, eif, 0.0), axis=0, keepdims=True))
        wts.append(jnp.sum(jnp.where(hit, scores, 0.0), axis=0, keepdims=True))
        pos.append(jnp.sum(jnp.where(hit, posmat, 0.0), axis=0, keepdims=True))
    wts = jnp.concatenate(wts, axis=0)
    te_ref[0] = jnp.concatenate(ids, axis=0).astype(jnp.int32)
    tw_ref[0] = wts / jnp.sum(wts, axis=0, keepdims=True) * ROUTED_SCALE
    tp_ref[0] = jnp.concatenate(pos, axis=0).astype(jnp.int32)
    run_ref[...] = run_ref[...] + jnp.sum(sel, axis=1, keepdims=True)
    cnt_ref[...] = jnp.broadcast_to(run_ref[...], cnt_ref.shape)


def _mix(oa, ob, x2, mod8, woa, wob, n2, rwt, rb, sg, su, sd, *, seq, tm):
    n, d = x2.shape
    per_b = seq // tm
    nb = n // tm
    row = lambda i: (i, 0)
    blk3 = lambda i: (i, 0, 0)
    const = lambda i: (0, 0)
    return pl.pallas_call(
        _mix_kernel,
        grid=(nb,),
        in_specs=[pl.BlockSpec((tm, oa.shape[1]), row), pl.BlockSpec((tm, ob.shape[1]), row),
                  pl.BlockSpec((tm, d), row), pl.BlockSpec((1, 8, d), lambda i: (i // per_b, 0, 0)),
                  pl.BlockSpec(woa.shape, const), pl.BlockSpec(wob.shape, const), pl.BlockSpec((1, d), const),
                  pl.BlockSpec(rwt.shape, const), pl.BlockSpec(rb.shape, const),
                  pl.BlockSpec(sg.shape, const), pl.BlockSpec(su.shape, const), pl.BlockSpec(sd.shape, const)],
        out_specs=[pl.BlockSpec((tm, d), row), pl.BlockSpec((tm * SUBLANES, LANES), row),
                   pl.BlockSpec((1, TOP_K, tm), blk3), pl.BlockSpec((1, TOP_K, tm), blk3),
                   pl.BlockSpec((1, TOP_K, tm), blk3), pl.BlockSpec((N_EXPERTS, LANES), const)],
        out_shape=[jax.ShapeDtypeStruct((n, d), F32), jax.ShapeDtypeStruct((n * SUBLANES, LANES), jnp.uint32),
                   jax.ShapeDtypeStruct((nb, TOP_K, tm), jnp.int32), jax.ShapeDtypeStruct((nb, TOP_K, tm), F32),
                   jax.ShapeDtypeStruct((nb, TOP_K, tm), jnp.int32),
                   jax.ShapeDtypeStruct((N_EXPERTS, LANES), F32)],
        scratch_shapes=[pltpu.VMEM((N_EXPERTS, 1), F32)],
        compiler_params=_cparams("arbitrary"),
        name="mix",
    )(oa, ob, x2, mod8, woa, wob, n2, rwt, rb, sg, su, sd)


def _slots_kernel(ps_ref, ids_ref, pos_ref, o_ref):
    ids = ids_ref[0]
    start = jnp.zeros(ids.shape, jnp.int32)
    for e in range(N_EXPERTS):
        start = jnp.where(ids == e, ps_ref[e], start)
    o_ref[0] = (start + pos_ref[0]) * SUBLANES


def _slots(ps, ids, pos):
    blk3 = lambda i, ps: (i, 0, 0)
    spec = pl.BlockSpec((1,) + ids.shape[1:], blk3)
    return pl.pallas_call(
        _slots_kernel,
        grid_spec=pltpu.PrefetchScalarGridSpec(num_scalar_prefetch=1, grid=(ids.shape[0],),
                                               in_specs=[spec, spec], out_specs=spec),
        out_shape=jax.ShapeDtypeStruct(ids.shape, jnp.int32),
        compiler_params=_cparams("parallel"),
        name="slots",
    )(ps, ids, pos)


def _dispatch_kernel(ps_ref, slot_ref, h2t_ref, xs_hbm, zbuf, zsem, sem, *, tm, blk):
    @pl.when(pl.program_id(0) == 0)
    def _():
        zbuf[...] = jnp.zeros(zbuf.shape, zbuf.dtype)

        def block_at(row):
            return xs_hbm.at[pl.ds(pl.multiple_of(row * SUBLANES, blk * SUBLANES), blk * SUBLANES), :]

        def tail(e):
            end = ps_ref[e + 1]
            return end > ps_ref[e], pltpu.make_async_copy(zbuf, block_at(end - blk), zsem)

        def unused(e):
            row = ps_ref[N_EXPERTS] + e * blk
            return row * SUBLANES < xs_hbm.shape[0], pltpu.make_async_copy(zbuf, block_at(row), zsem)

        for fill in (tail, unused):
            for e in range(N_EXPERTS):
                needed, cp = fill(e)
                pl.when(needed)(cp.start)
        for fill in (tail, unused):
            for e in range(N_EXPERTS):
                needed, cp = fill(e)
                pl.when(needed)(cp.wait)

    def issue(t, carry):
        for k in range(TOP_K):
            pltpu.make_async_copy(_tile_row(h2t_ref, t * SUBLANES), _tile_row(xs_hbm, slot_ref[0, k, t]),
                                  sem).start(priority=k % 2)
        return carry

    lax.fori_loop(0, tm, issue, 0)
    for k in range(TOP_K):
        pltpu.make_async_copy(h2t_ref, xs_hbm.at[pl.ds(0, tm * SUBLANES), :], sem).wait()


def _dispatch(ps, slot, h2t, *, n_slots, tm, blk):
    lanes = h2t.shape[1]
    grid_spec = pltpu.PrefetchScalarGridSpec(
        num_scalar_prefetch=1,
        grid=(h2t.shape[0] // (tm * SUBLANES),),
        in_specs=[pl.BlockSpec((1, TOP_K, tm), lambda i, ps: (i, 0, 0), memory_space=pltpu.SMEM),
                  pl.BlockSpec((tm * SUBLANES, lanes), lambda i, ps: (i, 0))],
        out_specs=pl.BlockSpec(memory_space=pl.ANY),
        scratch_shapes=[pltpu.VMEM((blk * SUBLANES, lanes), jnp.uint32), pltpu.SemaphoreType.DMA(()),
                        pltpu.SemaphoreType.DMA(())],
    )
    return pl.pallas_call(
        functools.partial(_dispatch_kernel, tm=tm, blk=blk),
        grid_spec=grid_spec,
        out_shape=jax.ShapeDtypeStruct((n_slots * SUBLANES, lanes), jnp.uint32),
        compiler_params=_cparams("arbitrary"),
        name="dispatch",
    )(ps, slot, h2t)


def _experts_kernel(be_ref, nu_ref, x_ref, wg_ref, wu_ref, wd_ref, y_ref):
    del be_ref
    used = pl.program_id(0) < nu_ref[0]

    @pl.when(used)
    def _():
        xb = _load_tile_rows(x_ref, 0, x_ref.shape[0] // SUBLANES).astype(BF16)
        hb = (_silu(_dot(xb, wg_ref[0])) * _dot(xb, wu_ref[0])).astype(BF16)
        _store_tile_rows(y_ref, _dot(hb, wd_ref[0]))

    @pl.when(jnp.logical_not(used))
    def _():
        y_ref[...] = jnp.zeros(y_ref.shape, y_ref.dtype)


def _experts(block_e, n_used, xs, wg, wu, wd, *, blk):
    n_blocks = block_e.shape[0]
    rows_blk = (blk * SUBLANES, xs.shape[1])
    wsel = lambda i, be, nu: (be[i], 0, 0)
    grid_spec = pltpu.PrefetchScalarGridSpec(
        num_scalar_prefetch=2,
        grid=(n_blocks,),
        in_specs=[pl.BlockSpec(rows_blk, lambda i, be, nu: (jnp.minimum(i, nu[0] - 1), 0)),
                  pl.BlockSpec((1,) + wg.shape[1:], wsel), pl.BlockSpec((1,) + wu.shape[1:], wsel),
                  pl.BlockSpec((1,) + wd.shape[1:], wsel)],
        out_specs=pl.BlockSpec(rows_blk, lambda i, be, nu: (i, 0)),
    )
    return pl.pallas_call(
        _experts_kernel,
        grid_spec=grid_spec,
        out_shape=jax.ShapeDtypeStruct(xs.shape, xs.dtype),
        compiler_params=_cparams("arbitrary"),
        name="experts",
    )(block_e, n_used, xs, wg, wu, wd)


def _combine_kernel(slot_ref, tw_ref, xs_ref, mod_ref, fn_ref, y_hbm, o_ref, gbuf, sem, *, tm):
    def issue(t, carry):
        for k in range(TOP_K):
            pltpu.make_async_copy(_tile_row(y_hbm, slot_ref[0, k, t]), _tile_row(gbuf, (k * tm + t) * SUBLANES),
                                  sem).start(priority=k % 2)
        return carry

    lax.fori_loop(0, tm, issue, 0)
    pltpu.make_async_copy(y_hbm.at[pl.ds(0, gbuf.shape[0]), :], gbuf, sem).wait()

    wt = jnp.concatenate([tw_ref[0], jnp.zeros((LANES - TOP_K, tm), F32)], axis=0).T
    acc = wt[:, 0:1] * _load_tile_rows(gbuf, 0, tm)
    for k in range(1, TOP_K):
        acc = acc + wt[:, k:k + 1] * _load_tile_rows(gbuf, k * tm, tm)
    g2 = mod_ref[0][5:6]
    o_ref[...] = _rms(xs_ref[...] + g2 * acc, fn_ref[...])


def _combine(slot, tw, xs, mod8, fn, y, *, seq, tm):
    n, d = xs.shape
    per_b = seq // tm
    blk3 = lambda i: (i, 0, 0)
    return pl.pallas_call(
        functools.partial(_combine_kernel, tm=tm),
        grid=(n // tm,),
        in_specs=[pl.BlockSpec((1, TOP_K, tm), blk3, memory_space=pltpu.SMEM),
                  pl.BlockSpec((1, TOP_K, tm), blk3),
                  pl.BlockSpec((tm, d), lambda i: (i, 0)),
                  pl.BlockSpec((1, 8, d), lambda i: (i // per_b, 0, 0)),
                  pl.BlockSpec((1, d), lambda i: (0, 0)),
                  pl.BlockSpec(memory_space=pl.ANY)],
        out_specs=pl.BlockSpec((tm, d), lambda i: (i, 0)),
        out_shape=jax.ShapeDtypeStruct((n, d), F32),
        scratch_shapes=[pltpu.VMEM((TOP_K * tm * SUBLANES, y.shape[1]), y.dtype), pltpu.SemaphoreType.DMA(())],
        compiler_params=_cparams("arbitrary"),
        name="combine",
    )(slot, tw, xs, mod8, fn, y)


def _expert_runs(counts, n_blocks, blk):
    padded = (counts + blk - 1) // blk * blk
    pad_end = jnp.cumsum(padded)
    run_start = jnp.concatenate([pad_end - padded, pad_end[-1:]]).astype(jnp.int32)
    block_row = jnp.arange(n_blocks, dtype=jnp.int32) * blk
    block_e = jnp.minimum(jnp.sum(pad_end[None, :] <= block_row[:, None], axis=1), N_EXPERTS - 1).astype(jnp.int32)
    n_used = (pad_end[-1:] // blk).astype(jnp.int32)
    return run_start, block_e, n_used


def _pad_lanes(w, width):
    return jnp.pad(w, ((0, 0), (0, width - w.shape[1])))


def _tiles(seq):
    t = dict(tm_proj=256, tq=512, tt=256, tm_moe=256, blk=256)
    assert all(seq % v == 0 for k, v in t.items() if k != "blk")
    return t


def _layer(x, c, positions, ada_w, ada_b, norm1_w, w_in, q_a_norm_w, wq_b, kv_a_norm_w, wkv_b,
           gdn_conv_w, gdn_a_log, gdn_dt_bias, gdn_norm_w, w_out, norm2_w, router_w, router_bias,
           exp_w_gate, exp_w_up, exp_w_down, sh_w_gate, sh_w_up, sh_w_down, final_norm_w,
           *, tm_proj, tq, tt, tm_moe, blk):
    batch, seq, d = x.shape
    n = batch * seq
    x2 = x.reshape(n, d)

    c8 = jnp.zeros((8, d), F32).at[:batch].set(c)
    mod = _adaln(c8, ada_w, ada_b.reshape(1, -1))
    mod8 = jnp.pad(mod[:batch].reshape(batch, 6, d), ((0, 0), (0, 2), (0, 0)))

    hq = MLA_NOPE + MLA_ROPE
    w_cq, w_ckv, w_kpe, w_qkv, w_z, w_a, w_b = jnp.split(
        w_in, [_C_KV, _C_KV + MLA_KV_RANK, _C_KV + MLA_KV_RANK + MLA_ROPE,
               _C_KV + MLA_KV_RANK + MLA_ROPE + _GDN_QKV,
               _C_KV + MLA_KV_RANK + MLA_ROPE + _GDN_QKV + GDN_HEADS * GDN_DV,
               _C_KV + MLA_KV_RANK + MLA_ROPE + _GDN_QKV + GDN_HEADS * GDN_DV + GDN_HEADS], axis=1)
    win = jnp.concatenate([w_cq, w_ckv, _pad_lanes(w_kpe, LANES), w_qkv, w_z,
                           _pad_lanes(jnp.concatenate([w_a, w_b], axis=1), LANES)], axis=1).astype(BF16)
    wq = jnp.pad(wq_b.reshape(MLA_Q_RANK, MLA_HEADS, hq),
                 ((0, 0), (0, 0), (0, MLA_QK_PAD - hq))).reshape(MLA_Q_RANK, MLA_HEADS * MLA_QK_PAD).astype(BF16)
    wkv4 = wkv_b.reshape(MLA_KV_RANK, MLA_HEADS, MLA_NOPE + MLA_V)
    wkv = jnp.concatenate([wkv4[:, :, :MLA_NOPE].reshape(MLA_KV_RANK, -1),
                           wkv4[:, :, MLA_NOPE:].reshape(MLA_KV_RANK, -1)], axis=1).astype(BF16)
    inv_freq = 1.0 / (ROPE_THETA ** (jnp.arange(0, MLA_ROPE, 2, dtype=F32) / MLA_ROPE))
    invf = _pad_lanes(jnp.concatenate([inv_freq, inv_freq])[None, :], LANES)

    q, k, v, qkv, z, ab = _proj(x2, mod8, positions.reshape(n, 1), norm1_w.reshape(1, d), win,
                                q_a_norm_w.reshape(1, -1), wq, kv_a_norm_w.reshape(1, -1), wkv, invf,
                                seq=seq, tm=tm_proj)
    out_a = _attn(q, k, v, batch=batch, seq=seq, tq=tq)

    cw8 = jnp.pad(gdn_conv_w, ((0, 8 - GDN_CONV), (0, 0)))
    hp = _pad_lanes(jnp.stack([gdn_a_log, gdn_dt_bias]), LANES)
    hp = jnp.pad(hp, ((0, 6), (0, 0)))
    out_b = _gdn(qkv, z, ab, cw8, hp, gdn_norm_w.reshape(1, -1), batch=batch, seq=seq, tt=tt)

    ha = MLA_HEADS * MLA_V
    xs, h2t, ids, tw, pos, counts = _mix(
        out_a, out_b, x2, mod8, w_out[:ha].astype(BF16), w_out[ha:].astype(BF16), norm2_w.reshape(1, d),
        router_w.T, router_bias.reshape(-1, 1), sh_w_gate.astype(BF16), sh_w_up.astype(BF16),
        sh_w_down.astype(BF16), seq=seq, tm=tm_moe)

    n_blocks = n * TOP_K // blk + N_EXPERTS
    run_start, block_e, n_used = _expert_runs(counts[:, 0].astype(jnp.int32), n_blocks, blk)
    slot = _slots(run_start, ids, pos)
    xsort = _dispatch(run_start, slot, h2t, n_slots=n_blocks * blk, tm=tm_moe, blk=blk)
    y = _experts(block_e, n_used, xsort, exp_w_gate.astype(BF16), exp_w_up.astype(BF16),
                 exp_w_down.astype(BF16), blk=blk)
    out = _combine(slot, tw, xs, mod8, final_norm_w.reshape(1, d), y, seq=seq, tm=tm_moe)
    return out.reshape(batch, seq, d)


def kernel(x, c, positions, ada_w, ada_b, norm1_w, w_in, q_a_norm_w, wq_b, kv_a_norm_w, wkv_b, gdn_conv_w,
           gdn_a_log, gdn_dt_bias, gdn_norm_w, w_out, norm2_w, router_w, router_bias, exp_w_gate, exp_w_up,
           exp_w_down, sh_w_gate, sh_w_up, sh_w_down, final_norm_w):
    assert ada_w.shape[0] == 1, "single layer"
    return _layer(
        x, c, positions, ada_w[0], ada_b[0], norm1_w[0], w_in[0], q_a_norm_w[0], wq_b[0], kv_a_norm_w[0],
        wkv_b[0], gdn_conv_w[0], gdn_a_log[0], gdn_dt_bias[0], gdn_norm_w[0], w_out[0], norm2_w[0],
        router_w[0], router_bias[0], exp_w_gate[0], exp_w_up[0], exp_w_down[0], sh_w_gate[0], sh_w_up[0],
        sh_w_down[0], final_norm_w, **_tiles(x.shape[1]))
```

```python
import functools

import jax
import jax.numpy as jnp
from jax import lax
from jax.experimental import pallas as pl
from jax.experimental.pallas import tpu as pltpu

F32 = jnp.float32
BF16 = jnp.bfloat16
HIGHEST = lax.Precision.HIGHEST

CHUNK = 64
NORM_EPS = 1e-6
MLA_HEADS = 4
MLA_Q_RANK = 384
MLA_KV_RANK = 256
MLA_NOPE = 128
MLA_ROPE = 64
MLA_V = 128
ROPE_THETA = 10000.0
GDN_HEADS = 4
GDN_DK = 128
GDN_DV = 128
GDN_CONV = 4
N_EXPERTS = 64
N_GROUPS = 8
TOPK_GROUPS = 4
TOP_K = 8
ROUTED_SCALE = 2.5

LANES = 128
MLA_QK_PAD = 256
VMEM_LIMIT = 56 * 1024 * 1024


def _cparams(*sem):
    return pltpu.CompilerParams(dimension_semantics=sem, vmem_limit_bytes=VMEM_LIMIT)


def _dot(a, b):
    return jnp.dot(a, b, preferred_element_type=F32)


def _split3(a):
    hi = a.astype(BF16)
    r = a - hi.astype(F32)
    mid = r.astype(BF16)
    return hi, mid, (r - mid.astype(F32)).astype(BF16)


def _dot_x3(a, b):
    ah, am, _ = a
    bh, bm, _ = b
    return _dot(ah, bh) + (_dot(ah, bm) + _dot(am, bh))


def _dot_nt(a, b, precision=None):
    return lax.dot_general(a, b, (((1,), (1,)), ((), ())), preferred_element_type=F32, precision=precision)


def _dot_tn(a, b):
    return lax.dot_general(a, b, (((0,), (0,)), ((), ())), preferred_element_type=F32)


def _silu(x):
    return x * jax.nn.sigmoid(x)


def _rms(x, w):
    return x * lax.rsqrt(jnp.mean(x * x, axis=-1, keepdims=True) + NORM_EPS) * w


def _adaln_kernel(c_ref, w_ref, b_ref, o_ref):
    a = _silu(c_ref[...]).astype(BF16)
    o_ref[...] = _dot(a, w_ref[...].astype(BF16)) + b_ref[...]


def _adaln(c8, ada_w, ada_b):
    d = c8.shape[1]
    n_out = ada_w.shape[1]
    return pl.pallas_call(
        _adaln_kernel,
        grid=(n_out // d,),
        in_specs=[pl.BlockSpec((8, d), lambda j: (0, 0)),
                  pl.BlockSpec((d, d), lambda j: (0, j)),
                  pl.BlockSpec((1, d), lambda j: (0, j))],
        out_specs=pl.BlockSpec((8, d), lambda j: (0, j)),
        out_shape=jax.ShapeDtypeStruct((8, n_out), F32),
        compiler_params=_cparams("parallel"),
        name="adaln",
    )(c8, ada_w, ada_b)


_C_Q = 0
_C_KV = _C_Q + MLA_Q_RANK
_C_KPE = _C_KV + MLA_KV_RANK
_C_QKV = _C_KPE + LANES
_GDN_QKV = GDN_HEADS * (2 * GDN_DK + GDN_DV)
_C_Z = _C_QKV + _GDN_QKV
_C_AB = _C_Z + GDN_HEADS * GDN_DV
_D_IN_PAD = _C_AB + LANES


def _rope(xb, cos, sin):
    half = MLA_ROPE // 2
    lane = lax.broadcasted_iota(jnp.int32, xb.shape, 1)
    rot = jnp.where(lane < half, -pltpu.roll(xb, LANES - half, 1), pltpu.roll(xb, half, 1))
    return xb * cos + rot * sin


def _proj_kernel(x_ref, mod_ref, pos_ref, n1_ref, win_ref, qan_ref, wq_ref, kvan_ref, wkv_ref, invf_ref,
                 q_ref, k_ref, v_ref, qkv_ref, z_ref, ab_ref):
    x = x_ref[...]
    mod = mod_ref[0]
    sh1, sc1 = mod[0:1], mod[1:2]
    h = (_rms(x, n1_ref[...]) * (1.0 + sc1) + sh1).astype(BF16)
    proj = _dot(h, win_ref[...])

    ang = pos_ref[...].astype(F32) * invf_ref[...]
    cos, sin = jnp.cos(ang), jnp.sin(ang)

    cq = proj[:, _C_Q:_C_Q + MLA_Q_RANK]
    qn = _rms(cq, qan_ref[...]).astype(BF16)
    q = _dot(qn, wq_ref[...]) * ((MLA_NOPE + MLA_ROPE) ** -0.5)
    for hh in range(MLA_HEADS):
        c0 = hh * MLA_QK_PAD
        q_ref[:, c0:c0 + MLA_NOPE] = q[:, c0:c0 + MLA_NOPE].astype(BF16)
        q_ref[:, c0 + MLA_NOPE:c0 + MLA_QK_PAD] = _rope(q[:, c0 + MLA_NOPE:c0 + MLA_QK_PAD], cos, sin).astype(BF16)

    ckv = proj[:, _C_KV:_C_KV + MLA_KV_RANK]
    kvn = _rms(ckv, kvan_ref[...]).astype(BF16)
    kv = _dot(kvn, wkv_ref[...])
    kpe = _rope(proj[:, _C_KPE:_C_KPE + LANES], cos, sin).astype(BF16)
    for hh in range(MLA_HEADS):
        c0 = hh * MLA_QK_PAD
        k_ref[:, c0:c0 + MLA_NOPE] = kv[:, hh * MLA_NOPE:(hh + 1) * MLA_NOPE].astype(BF16)
        k_ref[:, c0 + MLA_NOPE:c0 + MLA_QK_PAD] = kpe
    v_ref[...] = kv[:, MLA_HEADS * MLA_NOPE:].astype(BF16)

    qkv_ref[...] = proj[:, _C_QKV:_C_QKV + _GDN_QKV]
    z_ref[...] = proj[:, _C_Z:_C_Z + GDN_HEADS * GDN_DV]
    ab_ref[...] = proj[:, _C_AB:_C_AB + LANES]


def _proj(x2, mod8, pos2, n1, win, qan, wq, kvan, wkv, invf, *, seq, tm):
    n, d = x2.shape
    per_b = seq // tm
    row = lambda i: (i, 0)
    const = lambda i: (0, 0)
    hq = MLA_HEADS * MLA_QK_PAD
    hv = MLA_HEADS * MLA_V
    return pl.pallas_call(
        _proj_kernel,
        grid=(n // tm,),
        in_specs=[pl.BlockSpec((tm, d), row),
                  pl.BlockSpec((1, 8, d), lambda i: (i // per_b, 0, 0)),
                  pl.BlockSpec((tm, 1), row),
                  pl.BlockSpec((1, d), const),
                  pl.BlockSpec(win.shape, const),
                  pl.BlockSpec(qan.shape, const),
                  pl.BlockSpec(wq.shape, const),
                  pl.BlockSpec(kvan.shape, const),
                  pl.BlockSpec(wkv.shape, const),
                  pl.BlockSpec(invf.shape, const)],
        out_specs=[pl.BlockSpec((tm, hq), row), pl.BlockSpec((tm, hq), row), pl.BlockSpec((tm, hv), row),
                   pl.BlockSpec((tm, _GDN_QKV), row), pl.BlockSpec((tm, GDN_HEADS * GDN_DV), row),
                   pl.BlockSpec((tm, LANES), row)],
        out_shape=[jax.ShapeDtypeStruct((n, hq), BF16), jax.ShapeDtypeStruct((n, hq), BF16),
                   jax.ShapeDtypeStruct((n, hv), BF16), jax.ShapeDtypeStruct((n, _GDN_QKV), F32),
                   jax.ShapeDtypeStruct((n, GDN_HEADS * GDN_DV), F32), jax.ShapeDtypeStruct((n, LANES), F32)],
        compiler_params=_cparams("parallel"),
        name="proj",
    )(x2, mod8, pos2, n1, win, qan, wq, kvan, wkv, invf)


def _attn_kernel(q_ref, k_ref, v_ref, o_ref, *, tq):
    i = pl.program_id(1)

    def head_step(hh, r0, carry, mask):
        m, l, acc = carry
        s = _dot_nt(q_ref[:, hh * MLA_QK_PAD:(hh + 1) * MLA_QK_PAD],
                    k_ref[pl.ds(r0, tq), hh * MLA_QK_PAD:(hh + 1) * MLA_QK_PAD])
        if mask is not None:
            s = jnp.where(mask, s, -jnp.inf)
        m_new = jnp.maximum(m, jnp.max(s, axis=-1, keepdims=True))
        alpha = jnp.exp(m - m_new)
        p = jnp.exp(s - m_new)
        l = alpha * l + jnp.sum(p, axis=-1, keepdims=True)
        acc = alpha * acc + _dot(p.astype(BF16), v_ref[pl.ds(r0, tq), hh * MLA_V:(hh + 1) * MLA_V])
        return m_new, l, acc

    def block(r0, carry, mask):
        return tuple(head_step(hh, r0, carry[hh], mask) for hh in range(MLA_HEADS))

    init = tuple((jnp.full((tq, 1), -jnp.inf, F32), jnp.zeros((tq, 1), F32), jnp.zeros((tq, MLA_V), F32))
                 for _ in range(MLA_HEADS))
    carry = lax.fori_loop(0, i, lambda j, c: block(pl.multiple_of(j * tq, tq), c, None), init)
    rq = lax.broadcasted_iota(jnp.int32, (tq, tq), 0) // CHUNK
    ck = lax.broadcasted_iota(jnp.int32, (tq, tq), 1) // CHUNK
    carry = block(pl.multiple_of(i * tq, tq), carry, ck <= rq)
    for hh in range(MLA_HEADS):
        _, l, acc = carry[hh]
        o_ref[:, hh * MLA_V:(hh + 1) * MLA_V] = (acc / l).astype(o_ref.dtype)


def _attn(q, k, v, *, batch, seq, tq):
    n = q.shape[0]
    nq = seq // tq
    return pl.pallas_call(
        functools.partial(_attn_kernel, tq=tq),
        grid=(batch, nq),
        in_specs=[pl.BlockSpec((tq, q.shape[1]), lambda b, i: (b * nq + i, 0)),
                  pl.BlockSpec((seq, k.shape[1]), lambda b, i: (b, 0)),
                  pl.BlockSpec((seq, v.shape[1]), lambda b, i: (b, 0))],
        out_specs=pl.BlockSpec((tq, v.shape[1]), lambda b, i: (b * nq + i, 0)),
        out_shape=jax.ShapeDtypeStruct((n, v.shape[1]), BF16),
        compiler_params=_cparams("parallel", "arbitrary"),
        name="attn",
    )(q, k, v)


_SUPER = 2 * CHUNK


def _unit_lower_inverses(lows):
    n = lows[0].shape[0]
    eye = (lax.broadcasted_iota(jnp.int32, (n, n), 0) == lax.broadcasted_iota(jnp.int32, (n, n), 1)).astype(F32)
    ps = [-low for low in lows]
    ts = [eye + p for p in ps]
    levels = CHUNK.bit_length() - 1
    for _ in range(levels - 1):
        parts = [_split3(p) for p in ps]
        ps = [_dot_x3(s, s) for s in parts]
        ts = [t + _dot_x3(_split3(t), _split3(p)) for t, p in zip(ts, ps)]
    return ts


def _gdn_kernel(qkv_ref, z_ref, ab_ref, cw_ref, hp_ref, nw_ref, o_ref, xe_ref, st_ref, *, tt):
    j = pl.program_id(1)
    hd = GDN_HEADS * GDN_DK

    @pl.when(j == 0)
    def _():
        xe_ref[0:8, :] = jnp.zeros((8, xe_ref.shape[1]), F32)
        st_ref[...] = jnp.zeros(st_ref.shape, F32)

    xe_ref[8:8 + tt, :] = qkv_ref[...]
    cw = cw_ref[...]
    y = xe_ref[8:8 + tt, :] * cw[GDN_CONV - 1:GDN_CONV]
    for i in range(1, GDN_CONV):
        y = y + xe_ref[8 - i:8 - i + tt, :] * cw[GDN_CONV - 1 - i:GDN_CONV - i]
    xe_ref[0:8, :] = xe_ref[tt:tt + 8, :]
    act = _silu(y)

    ab = ab_ref[...]
    hp = hp_ref[...]
    pre = ab + hp[1:2]
    softplus = jnp.maximum(pre, 0.0) + jnp.log1p(jnp.exp(-jnp.abs(pre)))
    g_all = -jnp.exp(hp[0:1]) * softplus
    beta_all = jax.nn.sigmoid(ab)

    ri = lax.broadcasted_iota(jnp.int32, (tt, tt), 0)
    ci = lax.broadcasted_iota(jnp.int32, (tt, tt), 1)
    tri = ((ri // CHUNK == ci // CHUNK) & (ci <= ri)).astype(BF16)
    gc = sum(_dot(tri, part) for part in _split3(g_all))
    gct = gc.T

    rs = lax.broadcasted_iota(jnp.int32, (_SUPER, _SUPER), 0)
    cs = lax.broadcasted_iota(jnp.int32, (_SUPER, _SUPER), 1)
    same = rs // CHUNK == cs // CHUNK
    incl = same & (cs <= rs)
    strict = same & (cs < rs)

    tiles = [(s * _SUPER, hh) for s in range(tt // _SUPER) for hh in range(GDN_HEADS)]
    pre_t = []
    for r0, hh in tiles:
        qh = act[r0:r0 + _SUPER, hh * GDN_DK:(hh + 1) * GDN_DK]
        kh = act[r0:r0 + _SUPER, hd + hh * GDN_DK:hd + (hh + 1) * GDN_DK]
        vh = act[r0:r0 + _SUPER, 2 * hd + hh * GDN_DV:2 * hd + (hh + 1) * GDN_DV]
        qh = qh * lax.rsqrt(jnp.sum(qh * qh, axis=-1, keepdims=True) + 1e-6) * (GDN_DK ** -0.5)
        kh = kh * lax.rsqrt(jnp.sum(kh * kh, axis=-1, keepdims=True) + 1e-6)
        beta = beta_all[r0:r0 + _SUPER, GDN_HEADS + hh:GDN_HEADS + hh + 1]
        gcol = gc[r0:r0 + _SUPER, hh:hh + 1]
        grow = gct[hh:hh + 1, r0:r0 + _SUPER]
        decay = jnp.exp(jnp.where(incl, gcol - grow, -jnp.inf))
        eg = jnp.exp(gcol)
        kb = kh * beta
        pre_t.append(dict(kh=kh, khb=kh.astype(BF16), kbb=kb.astype(BF16), vbb=(vh * beta).astype(BF16),
                          kgb=(kb * eg).astype(BF16), qb=qh.astype(BF16), qgb=(qh * eg).astype(BF16),
                          gcol=gcol, decay=decay))
    lows = [jnp.where(strict, _dot_nt(p["kbb"], p["khb"]) * p["decay"], 0.0) for p in pre_t]
    tinvs = [t.astype(BF16) for t in _unit_lower_inverses(lows)]
    us = [_dot(t, p["vbb"]) for t, p in zip(tinvs, pre_t)]
    ws = [_dot(t, p["kgb"]).astype(BF16) for t, p in zip(tinvs, pre_t)]
    atts = [(_dot_nt(p["qb"], p["khb"]) * p["decay"]).astype(BF16) for p in pre_t]

    states = [st_ref[hh] for hh in range(GDN_HEADS)]
    for cc in range(tt // CHUNK):
        r0 = cc * CHUNK
        a0 = r0 % _SUPER
        for hh in range(GDN_HEADS):
            ti = (r0 // _SUPER) * GDN_HEADS + hh
            p = pre_t[ti]
            gl = gc[r0 + CHUNK - 1:r0 + CHUNK, hh:hh + 1]
            kd = (p["kh"][a0:a0 + CHUNK] * jnp.exp(gl - p["gcol"][a0:a0 + CHUNK])).astype(BF16)
            sb = states[hh].astype(BF16)
            vnb = (us[ti][a0:a0 + CHUNK] - _dot(ws[ti][a0:a0 + CHUNK], sb)).astype(BF16)
            o = _dot(p["qgb"][a0:a0 + CHUNK], sb) + _dot(atts[ti][a0:a0 + CHUNK, a0:a0 + CHUNK], vnb)
            states[hh] = states[hh] * jnp.exp(gl) + _dot_tn(kd, vnb)
            zz = z_ref[r0:r0 + CHUNK, hh * GDN_DV:(hh + 1) * GDN_DV]
            o_ref[r0:r0 + CHUNK, hh * GDN_DV:(hh + 1) * GDN_DV] = (
                _rms(o, nw_ref[...]) * _silu(zz)).astype(o_ref.dtype)
    for hh in range(GDN_HEADS):
        st_ref[hh] = states[hh]


def _gdn(qkv, z, ab, cw8, hp, nw, *, batch, seq, tt):
    n = qkv.shape[0]
    per_b = seq // tt
    row = lambda b, j: (b * per_b + j, 0)
    const = lambda b, j: (0, 0)
    hv = GDN_HEADS * GDN_DV
    return pl.pallas_call(
        functools.partial(_gdn_kernel, tt=tt),
        grid=(batch, per_b),
        in_specs=[pl.BlockSpec((tt, _GDN_QKV), row), pl.BlockSpec((tt, hv), row), pl.BlockSpec((tt, LANES), row),
                  pl.BlockSpec(cw8.shape, const), pl.BlockSpec(hp.shape, const), pl.BlockSpec(nw.shape, const)],
        out_specs=pl.BlockSpec((tt, hv), row),
        out_shape=jax.ShapeDtypeStruct((n, hv), BF16),
        scratch_shapes=[pltpu.VMEM((tt + 8, _GDN_QKV), F32), pltpu.VMEM((GDN_HEADS, GDN_DK, GDN_DV), F32)],
        compiler_params=_cparams("parallel", "arbitrary"),
        name="gdn",
    )(qkv, z, ab, cw8, hp, nw)


def _route(logits_t, bias_col):
    e, tm = logits_t.shape
    gsz = e // N_GROUPS
    scores = jax.nn.sigmoid(logits_t)
    biased = scores + bias_col
    sub = lax.broadcasted_iota(jnp.int32, (gsz, tm), 0)
    rows = []
    for g in range(N_GROUPS):
        blk = biased[g * gsz:(g + 1) * gsz]
        m1 = jnp.max(blk, axis=0, keepdims=True)
        first = jnp.min(jnp.where(blk == m1, sub, gsz), axis=0, keepdims=True)
        m2 = jnp.max(jnp.where(sub == first, -jnp.inf, blk), axis=0, keepdims=True)
        rows.append(m1 + m2)
    gs = jnp.concatenate(rows, axis=0)
    gi = lax.broadcasted_iota(jnp.int32, (N_GROUPS, tm), 0)
    grank = jnp.zeros((N_GROUPS, tm), F32)
    for g in range(N_GROUPS):
        r = gs[g:g + 1]
        grank = grank + jnp.where(r > gs, 1.0, jnp.where(r == gs, (gi > g).astype(F32), 0.0))
    gsel = grank < TOPK_GROUPS
    masked = jnp.concatenate(
        [jnp.where(gsel[g:g + 1], biased[g * gsz:(g + 1) * gsz], -jnp.inf) for g in range(N_GROUPS)], axis=0)
    ei = lax.broadcasted_iota(jnp.int32, (e, tm), 0)
    rank = jnp.zeros((e, tm), F32)
    for k in range(e):
        r = masked[k:k + 1]
        rank = rank + jnp.where(r > masked, 1.0, jnp.where(r == masked, (ei > k).astype(F32), 0.0))
    return scores, rank


SUBLANES = 4


def _store_tile_rows(ref, x):
    c = x.shape[1] // 2
    lo = pltpu.bitcast(x[:, :c].astype(BF16).astype(F32), jnp.uint32)
    hi = pltpu.bitcast(x[:, c:].astype(BF16).astype(F32), jnp.uint32)
    w = (hi & jnp.uint32(0xFFFF0000)) | (lo >> 16)
    for j in range(SUBLANES):
        ref[pl.ds(j, x.shape[0], stride=SUBLANES), :] = w[:, j * LANES:(j + 1) * LANES]


def _load_tile_rows(ref, r0, rows):
    w = jnp.concatenate(
        [ref[pl.ds(r0 * SUBLANES + j, rows, stride=SUBLANES), :] for j in range(SUBLANES)], axis=1)
    return jnp.concatenate([pltpu.bitcast(w << 16, F32), pltpu.bitcast(w & jnp.uint32(0xFFFF0000), F32)], axis=1)


def _tile_row(ref, sublane_offset):
    return ref.at[pl.ds(pl.multiple_of(sublane_offset, SUBLANES), SUBLANES), :]


def _mix_kernel(oa_ref, ob_ref, x_ref, mod_ref, woa_ref, wob_ref, n2_ref, rwt_ref, rb_ref, sg_ref, su_ref, sd_ref,
                xs_ref, h2t_ref, te_ref, tw_ref, tp_ref, cnt_ref, run_ref):
    @pl.when(pl.program_id(0) == 0)
    def _():
        run_ref[...] = jnp.zeros(run_ref.shape, F32)

    mod = mod_ref[0]
    g1, sh2, sc2, g2 = mod[2:3], mod[3:4], mod[4:5], mod[5:6]
    mix = _dot(oa_ref[...], woa_ref[...]) + _dot(ob_ref[...], wob_ref[...])
    x1 = x_ref[...] + g1 * mix
    h2 = _rms(x1, n2_ref[...]) * (1.0 + sc2) + sh2
    _store_tile_rows(h2t_ref, h2)
    h2b = h2.astype(BF16)
    hs = (_silu(_dot(h2b, sg_ref[...])) * _dot(h2b, su_ref[...])).astype(BF16)
    xs_ref[...] = x1 + g2 * _dot(hs, sd_ref[...])

    scores, rank = _route(_dot_nt(rwt_ref[...], h2, HIGHEST), rb_ref[...])
    e, tm = rank.shape
    sel = jnp.where(rank < TOP_K, 1.0, 0.0)
    earlier = (lax.broadcasted_iota(jnp.int32, (tm, tm), 0) < lax.broadcasted_iota(jnp.int32, (tm, tm), 1))
    posmat = _dot(sel.astype(BF16), earlier.astype(BF16)) + run_ref[...]
    eif = lax.broadcasted_iota(jnp.int32, (e, tm), 0).astype(F32)
    ids, wts, pos = [], [], []
    for k in range(TOP_K):
        hit = rank == float(k)
        ids.append(jnp.sum(jnp.where(hit, eif, 0.0), axis=0, keepdims=True))
        wts.append(jnp.sum(jnp.where(hit, scores, 0.0), axis=0, keepdims=True))
        pos.append(jnp.sum(jnp.where(hit, posmat, 0.0), axis=0, keepdims=True))
    wts = jnp.concatenate(wts, axis=0)
    te_ref[0] = jnp.concatenate(ids, axis=0).astype(jnp.int32)
    tw_ref[0] = wts / jnp.sum(wts, axis=0, keepdims=True) * ROUTED_SCALE
    tp_ref[0] = jnp.concatenate(pos, axis=0).astype(jnp.int32)
    run_ref[...] = run_ref[...] + jnp.sum(sel, axis=1, keepdims=True)
    cnt_ref[...] = jnp.broadcast_to(run_ref[...], cnt_ref.shape)


def _mix(oa, ob, x2, mod8, woa, wob, n2, rwt, rb, sg, su, sd, *, seq, tm):
    n, d = x2.shape
    per_b = seq // tm
    nb = n // tm
    row = lambda i: (i, 0)
    blk3 = lambda i: (i, 0, 0)
    const = lambda i: (0, 0)
    return pl.pallas_call(
        _mix_kernel,
        grid=(nb,),
        in_specs=[pl.BlockSpec((tm, oa.shape[1]), row), pl.BlockSpec((tm, ob.shape[1]), row),
                  pl.BlockSpec((tm, d), row), pl.BlockSpec((1, 8, d), lambda i: (i // per_b, 0, 0)),
                  pl.BlockSpec(woa.shape, const), pl.BlockSpec(wob.shape, const), pl.BlockSpec((1, d), const),
                  pl.BlockSpec(rwt.shape, const), pl.BlockSpec(rb.shape, const),
                  pl.BlockSpec(sg.shape, const), pl.BlockSpec(su.shape, const), pl.BlockSpec(sd.shape, const)],
        out_specs=[pl.BlockSpec((tm, d), row), pl.BlockSpec((tm * SUBLANES, LANES), row),
                   pl.BlockSpec((1, TOP_K, tm), blk3), pl.BlockSpec((1, TOP_K, tm), blk3),
                   pl.BlockSpec((1, TOP_K, tm), blk3), pl.BlockSpec((N_EXPERTS, LANES), const)],
        out_shape=[jax.ShapeDtypeStruct((n, d), F32), jax.ShapeDtypeStruct((n * SUBLANES, LANES), jnp.uint32),
                   jax.ShapeDtypeStruct((nb, TOP_K, tm), jnp.int32), jax.ShapeDtypeStruct((nb, TOP_K, tm), F32),
                   jax.ShapeDtypeStruct((nb, TOP_K, tm), jnp.int32),
                   jax.ShapeDtypeStruct((N_EXPERTS, LANES), F32)],
        scratch_shapes=[pltpu.VMEM((N_EXPERTS, 1), F32)],
        compiler_params=_cparams("arbitrary"),
        name="mix",
    )(oa, ob, x2, mod8, woa, wob, n2, rwt, rb, sg, su, sd)


def _slots_kernel(ps_ref, ids_ref, pos_ref, o_ref):
    ids = ids_ref[...]
    start = jnp.zeros(ids.shape, jnp.int32)
    for e in range(N_EXPERTS):
        start = jnp.where(ids == e, ps_ref[e], start)
    o_ref[...] = (start + pos_ref[...]) * SUBLANES


def _slots(ps, ids, pos):
    spec = pl.BlockSpec(ids.shape, lambda i, ps: (0, 0, 0))
    return pl.pallas_call(
        _slots_kernel,
        grid_spec=pltpu.PrefetchScalarGridSpec(num_scalar_prefetch=1, grid=(1,),
                                               in_specs=[spec, spec], out_specs=spec),
        out_shape=jax.ShapeDtypeStruct(ids.shape, jnp.int32),
        compiler_params=_cparams("parallel"),
        name="slots",
    )(ps, ids, pos)


def _dispatch_kernel(ps_ref, slot_ref, h2t_ref, xs_hbm, zbuf, zsem, sem, *, tm, blk):
    @pl.when(pl.program_id(0) == 0)
    def _():
        zbuf[...] = jnp.zeros(zbuf.shape, zbuf.dtype)

        def block_at(row):
            return xs_hbm.at[pl.ds(pl.multiple_of(row * SUBLANES, blk * SUBLANES), blk * SUBLANES), :]

        def tail(e):
            end = ps_ref[e + 1]
            return end > ps_ref[e], pltpu.make_async_copy(zbuf, block_at(end - blk), zsem)

        def unused(e):
            row = ps_ref[N_EXPERTS] + e * blk
            return row * SUBLANES < xs_hbm.shape[0], pltpu.make_async_copy(zbuf, block_at(row), zsem)

        for fill in (tail, unused):
            for e in range(N_EXPERTS):
                needed, cp = fill(e)
                pl.when(needed)(cp.start)
        for fill in (tail, unused):
            for e in range(N_EXPERTS):
                needed, cp = fill(e)
                pl.when(needed)(cp.wait)

    def issue(t, carry):
        for k in range(TOP_K):
            pltpu.make_async_copy(_tile_row(h2t_ref, t * SUBLANES), _tile_row(xs_hbm, slot_ref[0, k, t]),
                                  sem).start(priority=k % 2)
        return carry

    lax.fori_loop(0, tm, issue, 0)
    for k in range(TOP_K):
        pltpu.make_async_copy(h2t_ref, xs_hbm.at[pl.ds(0, tm * SUBLANES), :], sem).wait()


def _dispatch(ps, slot, h2t, *, n_slots, tm, blk):
    lanes = h2t.shape[1]
    grid_spec = pltpu.PrefetchScalarGridSpec(
        num_scalar_prefetch=1,
        grid=(h2t.shape[0] // (tm * SUBLANES),),
        in_specs=[pl.BlockSpec((1, TOP_K, tm), lambda i, ps: (i, 0, 0), memory_space=pltpu.SMEM),
                  pl.BlockSpec((tm * SUBLANES, lanes), lambda i, ps: (i, 0))],
        out_specs=pl.BlockSpec(memory_space=pl.ANY),
        scratch_shapes=[pltpu.VMEM((blk * SUBLANES, lanes), jnp.uint32), pltpu.SemaphoreType.DMA(()),
                        pltpu.SemaphoreType.DMA(())],
    )
    return pl.pallas_call(
        functools.partial(_dispatch_kernel, tm=tm, blk=blk),
        grid_spec=grid_spec,
        out_shape=jax.ShapeDtypeStruct((n_slots * SUBLANES, lanes), jnp.uint32),
        compiler_params=_cparams("arbitrary"),
        name="dispatch",
    )(ps, slot, h2t)


def _experts_kernel(be_ref, nu_ref, x_ref, wg_ref, wu_ref, wd_ref, y_ref, wgb, wub, wdb):
    i = pl.program_id(0)
    used = i < nu_ref[0]
    new_expert = jnp.logical_or(i == 0, be_ref[i] != be_ref[jnp.maximum(i - 1, 0)])

    @pl.when(jnp.logical_and(used, new_expert))
    def _():
        wgb[...] = wg_ref[0].astype(BF16)
        wub[...] = wu_ref[0].astype(BF16)
        wdb[...] = wd_ref[0].astype(BF16)

    @pl.when(used)
    def _():
        xb = _load_tile_rows(x_ref, 0, x_ref.shape[0] // SUBLANES).astype(BF16)
        hb = (_silu(_dot(xb, wgb[...])) * _dot(xb, wub[...])).astype(BF16)
        _store_tile_rows(y_ref, _dot(hb, wdb[...]))

    @pl.when(jnp.logical_not(used))
    def _():
        y_ref[...] = jnp.zeros(y_ref.shape, y_ref.dtype)


def _experts(block_e, n_used, xs, wg, wu, wd, *, blk):
    n_blocks = block_e.shape[0]
    rows_blk = (blk * SUBLANES, xs.shape[1])
    wsel = lambda i, be, nu: (be[i], 0, 0)
    grid_spec = pltpu.PrefetchScalarGridSpec(
        num_scalar_prefetch=2,
        grid=(n_blocks,),
        in_specs=[pl.BlockSpec(rows_blk, lambda i, be, nu: (jnp.minimum(i, nu[0] - 1), 0)),
                  pl.BlockSpec((1,) + wg.shape[1:], wsel), pl.BlockSpec((1,) + wu.shape[1:], wsel),
                  pl.BlockSpec((1,) + wd.shape[1:], wsel)],
        out_specs=pl.BlockSpec(rows_blk, lambda i, be, nu: (i, 0)),
        scratch_shapes=[pltpu.VMEM(wg.shape[1:], BF16), pltpu.VMEM(wu.shape[1:], BF16),
                        pltpu.VMEM(wd.shape[1:], BF16)],
    )
    return pl.pallas_call(
        _experts_kernel,
        grid_spec=grid_spec,
        out_shape=jax.ShapeDtypeStruct(xs.shape, xs.dtype),
        compiler_params=_cparams("arbitrary"),
        name="experts",
    )(block_e, n_used, xs, wg, wu, wd)


def _combine_kernel(slot_ref, nxt_ref, tw_ref, xs_ref, mod_ref, fn_ref, y_hbm, o_ref, gbuf, sems, *, tm):
    j = pl.program_id(0)

    def gather(src_ref, bi, b):
        def issue(t, carry):
            for k in range(TOP_K):
                pltpu.make_async_copy(_tile_row(y_hbm, src_ref[bi, k, t]),
                                      _tile_row(gbuf.at[b], (k * tm + t) * SUBLANES),
                                      sems.at[b]).start(priority=k % 2)
            return carry

        lax.fori_loop(0, tm, issue, 0)

    def drain(b):
        pltpu.make_async_copy(y_hbm.at[pl.ds(0, gbuf.shape[1]), :], gbuf.at[b], sems.at[b]).wait()

    def finish(b):
        wt = jnp.concatenate([tw_ref[b], jnp.zeros((LANES - TOP_K, tm), F32)], axis=0).T
        acc = wt[:, 0:1] * _load_tile_rows(gbuf.at[b], 0, tm)
        for k in range(1, TOP_K):
            acc = acc + wt[:, k:k + 1] * _load_tile_rows(gbuf.at[b], k * tm, tm)
        g2 = mod_ref[0][5:6]
        o_ref[b * tm:(b + 1) * tm, :] = _rms(xs_ref[b * tm:(b + 1) * tm, :] + g2 * acc, fn_ref[...])

    @pl.when(j == 0)
    def _():
        gather(slot_ref, 0, 0)

    gather(slot_ref, 1, 1)
    drain(0)
    finish(0)

    @pl.when(j + 1 < pl.num_programs(0))
    def _():
        gather(nxt_ref, 0, 0)

    drain(1)
    finish(1)


def _combine(slot, tw, xs, mod8, fn, y, *, seq, tm):
    n, d = xs.shape
    nb = slot.shape[0]
    assert nb % 2 == 0 and (seq // tm) % 2 == 0
    per_b = seq // (2 * tm)
    pair = lambda j: (j, 0, 0)
    return pl.pallas_call(
        functools.partial(_combine_kernel, tm=tm),
        grid=(nb // 2,),
        in_specs=[pl.BlockSpec((2, TOP_K, tm), pair, memory_space=pltpu.SMEM),
                  pl.BlockSpec((1, TOP_K, tm), lambda j: (jnp.minimum(2 * j + 2, nb - 1), 0, 0),
                               memory_space=pltpu.SMEM),
                  pl.BlockSpec((2, TOP_K, tm), pair),
                  pl.BlockSpec((2 * tm, d), lambda j: (j, 0)),
                  pl.BlockSpec((1, 8, d), lambda j: (j // per_b, 0, 0)),
                  pl.BlockSpec((1, d), lambda j: (0, 0)),
                  pl.BlockSpec(memory_space=pl.ANY)],
        out_specs=pl.BlockSpec((2 * tm, d), lambda j: (j, 0)),
        out_shape=jax.ShapeDtypeStruct((n, d), F32),
        scratch_shapes=[pltpu.VMEM((2, TOP_K * tm * SUBLANES, y.shape[1]), y.dtype),
                        pltpu.SemaphoreType.DMA((2,))],
        compiler_params=_cparams("arbitrary"),
        name="combine",
    )(slot, slot, tw, xs, mod8, fn, y)


def _expert_runs(counts, n_blocks, blk):
    padded = (counts + blk - 1) // blk * blk
    pad_end = jnp.cumsum(padded)
    run_start = jnp.concatenate([pad_end - padded, pad_end[-1:]]).astype(jnp.int32)
    block_row = jnp.arange(n_blocks, dtype=jnp.int32) * blk
    block_e = jnp.minimum(jnp.sum(pad_end[None, :] <= block_row[:, None], axis=1), N_EXPERTS - 1).astype(jnp.int32)
    n_used = (pad_end[-1:] // blk).astype(jnp.int32)
    return run_start, block_e, n_used


def _pad_lanes(w, width):
    return jnp.pad(w, ((0, 0), (0, width - w.shape[1])))


def _tiles(seq):
    t = dict(tm_proj=256, tq=512, tt=256, tm_moe=256, blk=512)
    assert all(seq % v == 0 for k, v in t.items() if k != "blk")
    return t


def _layer(x, c, positions, ada_w, ada_b, norm1_w, w_in, q_a_norm_w, wq_b, kv_a_norm_w, wkv_b,
           gdn_conv_w, gdn_a_log, gdn_dt_bias, gdn_norm_w, w_out, norm2_w, router_w, router_bias,
           exp_w_gate, exp_w_up, exp_w_down, sh_w_gate, sh_w_up, sh_w_down, final_norm_w,
           *, tm_proj, tq, tt, tm_moe, blk):
    batch, seq, d = x.shape
    n = batch * seq
    x2 = x.reshape(n, d)

    c8 = jnp.zeros((8, d), F32).at[:batch].set(c)
    mod = _adaln(c8, ada_w, ada_b.reshape(1, -1))
    mod8 = jnp.pad(mod[:batch].reshape(batch, 6, d), ((0, 0), (0, 2), (0, 0)))

    hq = MLA_NOPE + MLA_ROPE
    w_cq, w_ckv, w_kpe, w_qkv, w_z, w_a, w_b = jnp.split(
        w_in, [_C_KV, _C_KV + MLA_KV_RANK, _C_KV + MLA_KV_RANK + MLA_ROPE,
               _C_KV + MLA_KV_RANK + MLA_ROPE + _GDN_QKV,
               _C_KV + MLA_KV_RANK + MLA_ROPE + _GDN_QKV + GDN_HEADS * GDN_DV,
               _C_KV + MLA_KV_RANK + MLA_ROPE + _GDN_QKV + GDN_HEADS * GDN_DV + GDN_HEADS], axis=1)
    win = jnp.concatenate([w_cq, w_ckv, _pad_lanes(w_kpe, LANES), w_qkv, w_z,
                           _pad_lanes(jnp.concatenate([w_a, w_b], axis=1), LANES)], axis=1).astype(BF16)
    wq = jnp.pad(wq_b.reshape(MLA_Q_RANK, MLA_HEADS, hq),
                 ((0, 0), (0, 0), (0, MLA_QK_PAD - hq))).reshape(MLA_Q_RANK, MLA_HEADS * MLA_QK_PAD).astype(BF16)
    wkv4 = wkv_b.reshape(MLA_KV_RANK, MLA_HEADS, MLA_NOPE + MLA_V)
    wkv = jnp.concatenate([wkv4[:, :, :MLA_NOPE].reshape(MLA_KV_RANK, -1),
                           wkv4[:, :, MLA_NOPE:].reshape(MLA_KV_RANK, -1)], axis=1).astype(BF16)
    inv_freq = 1.0 / (ROPE_THETA ** (jnp.arange(0, MLA_ROPE, 2, dtype=F32) / MLA_ROPE))
    invf = _pad_lanes(jnp.concatenate([inv_freq, inv_freq])[None, :], LANES)

    q, k, v, qkv, z, ab = _proj(x2, mod8, positions.reshape(n, 1), norm1_w.reshape(1, d), win,
                                q_a_norm_w.reshape(1, -1), wq, kv_a_norm_w.reshape(1, -1), wkv, invf,
                                seq=seq, tm=tm_proj)
    out_a = _attn(q, k, v, batch=batch, seq=seq, tq=tq)

    cw8 = jnp.pad(gdn_conv_w, ((0, 8 - GDN_CONV), (0, 0)))
    hp = _pad_lanes(jnp.stack([gdn_a_log, gdn_dt_bias]), LANES)
    hp = jnp.pad(hp, ((0, 6), (0, 0)))
    out_b = _gdn(qkv, z, ab, cw8, hp, gdn_norm_w.reshape(1, -1), batch=batch, seq=seq, tt=tt)

    ha = MLA_HEADS * MLA_V
    xs, h2t, ids, tw, pos, counts = _mix(
        out_a, out_b, x2, mod8, w_out[:ha].astype(BF16), w_out[ha:].astype(BF16), norm2_w.reshape(1, d),
        router_w.T, router_bias.reshape(-1, 1), sh_w_gate.astype(BF16), sh_w_up.astype(BF16),
        sh_w_down.astype(BF16), seq=seq, tm=tm_moe)

    n_blocks = n * TOP_K // blk + N_EXPERTS
    run_start, block_e, n_used = _expert_runs(counts[:, 0].astype(jnp.int32), n_blocks, blk)
    slot = _slots(run_start, ids, pos)
    xsort = _dispatch(run_start, slot, h2t, n_slots=n_blocks * blk, tm=tm_moe, blk=blk)
    y = _experts(block_e, n_used, xsort, exp_w_gate, exp_w_up, exp_w_down, blk=blk)
    out = _combine(slot, tw, xs, mod8, final_norm_w.reshape(1, d), y, seq=seq, tm=tm_moe)
    return out.reshape(batch, seq, d)


def kernel(x, c, positions, ada_w, ada_b, norm1_w, w_in, q_a_norm_w, wq_b, kv_a_norm_w, wkv_b, gdn_conv_w,
           gdn_a_log, gdn_dt_bias, gdn_norm_w, w_out, norm2_w, router_w, router_bias, exp_w_gate, exp_w_up,
           exp_w_down, sh_w_gate, sh_w_up, sh_w_down, final_norm_w):
    assert ada_w.shape[0] == 1, "single layer"
    return _layer(
        x, c, positions, ada_w[0], ada_b[0], norm1_w[0], w_in[0], q_a_norm_w[0], wq_b[0], kv_a_norm_w[0],
        wkv_b[0], gdn_conv_w[0], gdn_a_log[0], gdn_dt_bias[0], gdn_norm_w[0], w_out[0], norm2_w[0],
        router_w[0], router_bias[0], exp_w_gate[0], exp_w_up[0], exp_w_down[0], sh_w_gate[0], sh_w_up[0],
        sh_w_down[0], final_norm_w, **_tiles(x.shape[1]))
```

```python
import functools

import jax
import jax.numpy as jnp
from jax import lax
from jax.experimental import pallas as pl
from jax.experimental.pallas import tpu as pltpu

F32 = jnp.float32
BF16 = jnp.bfloat16
HIGHEST = lax.Precision.HIGHEST

CHUNK = 64
NORM_EPS = 1e-6
MLA_HEADS = 4
MLA_Q_RANK = 384
MLA_KV_RANK = 256
MLA_NOPE = 128
MLA_ROPE = 64
MLA_V = 128
ROPE_THETA = 10000.0
GDN_HEADS = 4
GDN_DK = 128
GDN_DV = 128
GDN_CONV = 4
N_EXPERTS = 64
N_GROUPS = 8
TOPK_GROUPS = 4
TOP_K = 8
ROUTED_SCALE = 2.5
LOG2E = 1.4426950408889634

LANES = 128
MLA_QK_PAD = 256
VMEM_LIMIT = 56 * 1024 * 1024


def _cparams(*sem):
    return pltpu.CompilerParams(dimension_semantics=sem, vmem_limit_bytes=VMEM_LIMIT)


def _dot(a, b):
    return jnp.dot(a, b, preferred_element_type=F32)


def _split3(a):
    hi = a.astype(BF16)
    r = a - hi.astype(F32)
    mid = r.astype(BF16)
    return hi, mid, (r - mid.astype(F32)).astype(BF16)


def _dot_x3(a, b):
    ah, am, _ = a
    bh, bm, _ = b
    return _dot(ah, bh) + (_dot(ah, bm) + _dot(am, bh))


def _dot_nt(a, b, precision=None):
    return lax.dot_general(a, b, (((1,), (1,)), ((), ())), preferred_element_type=F32, precision=precision)


def _dot_tn(a, b):
    return lax.dot_general(a, b, (((0,), (0,)), ((), ())), preferred_element_type=F32)


def _silu(x):
    return x * jax.nn.sigmoid(x)


def _rms(x, w):
    return x * lax.rsqrt(jnp.mean(x * x, axis=-1, keepdims=True) + NORM_EPS) * w


def _adaln_kernel(c_ref, w_ref, b_ref, o_ref):
    a = _silu(c_ref[...]).astype(BF16)
    o_ref[...] = _dot(a, w_ref[...].astype(BF16)) + b_ref[...]


def _adaln(c8, ada_w, ada_b):
    d = c8.shape[1]
    n_out = ada_w.shape[1]
    return pl.pallas_call(
        _adaln_kernel,
        grid=(n_out // d,),
        in_specs=[pl.BlockSpec((8, d), lambda j: (0, 0)),
                  pl.BlockSpec((d, d), lambda j: (0, j)),
                  pl.BlockSpec((1, d), lambda j: (0, j))],
        out_specs=pl.BlockSpec((8, d), lambda j: (0, j)),
        out_shape=jax.ShapeDtypeStruct((8, n_out), F32),
        compiler_params=_cparams("parallel"),
        name="adaln",
    )(c8, ada_w, ada_b)


_C_Q = 0
_C_KV = _C_Q + MLA_Q_RANK
_C_KPE = _C_KV + MLA_KV_RANK
_C_QKV = _C_KPE + LANES
_GDN_QKV = GDN_HEADS * (2 * GDN_DK + GDN_DV)
_C_Z = _C_QKV + _GDN_QKV
_C_AB = _C_Z + GDN_HEADS * GDN_DV
_D_IN_PAD = _C_AB + LANES


def _rope(xb, cos, sin):
    half = MLA_ROPE // 2
    lane = lax.broadcasted_iota(jnp.int32, xb.shape, 1)
    rot = jnp.where(lane < half, -pltpu.roll(xb, LANES - half, 1), pltpu.roll(xb, half, 1))
    return xb * cos + rot * sin


def _proj_kernel(x_ref, mod_ref, pos_ref, n1_ref, win_ref, qan_ref, wq_ref, kvan_ref, wkv_ref, invf_ref,
                 q_ref, k_ref, v_ref, qkv_ref, z_ref, ab_ref):
    x = x_ref[...]
    mod = mod_ref[0]
    sh1, sc1 = mod[0:1], mod[1:2]
    h = (_rms(x, n1_ref[...]) * (1.0 + sc1) + sh1).astype(BF16)
    proj = _dot(h, win_ref[...])

    ang = pos_ref[...].astype(F32) * invf_ref[...]
    cos, sin = jnp.cos(ang), jnp.sin(ang)

    cq = proj[:, _C_Q:_C_Q + MLA_Q_RANK]
    qn = _rms(cq, qan_ref[...]).astype(BF16)
    q = _dot(qn, wq_ref[...]) * ((MLA_NOPE + MLA_ROPE) ** -0.5 * LOG2E)
    for hh in range(MLA_HEADS):
        c0 = hh * MLA_QK_PAD
        q_ref[:, c0:c0 + MLA_NOPE] = q[:, c0:c0 + MLA_NOPE].astype(BF16)
        q_ref[:, c0 + MLA_NOPE:c0 + MLA_QK_PAD] = _rope(q[:, c0 + MLA_NOPE:c0 + MLA_QK_PAD], cos, sin).astype(BF16)

    ckv = proj[:, _C_KV:_C_KV + MLA_KV_RANK]
    kvn = _rms(ckv, kvan_ref[...]).astype(BF16)
    kv = _dot(kvn, wkv_ref[...])
    kpe = _rope(proj[:, _C_KPE:_C_KPE + LANES], cos, sin).astype(BF16)
    for hh in range(MLA_HEADS):
        c0 = hh * MLA_QK_PAD
        k_ref[:, c0:c0 + MLA_NOPE] = kv[:, hh * MLA_NOPE:(hh + 1) * MLA_NOPE].astype(BF16)
        k_ref[:, c0 + MLA_NOPE:c0 + MLA_QK_PAD] = kpe
    v_ref[...] = kv[:, MLA_HEADS * MLA_NOPE:].astype(BF16)

    qkv_ref[...] = proj[:, _C_QKV:_C_QKV + _GDN_QKV].astype(qkv_ref.dtype)
    z_ref[...] = proj[:, _C_Z:_C_Z + GDN_HEADS * GDN_DV].astype(z_ref.dtype)
    ab_ref[...] = proj[:, _C_AB:_C_AB + LANES]


def _proj(x2, mod8, pos2, n1, win, qan, wq, kvan, wkv, invf, *, seq, tm):
    n, d = x2.shape
    per_b = seq // tm
    row = lambda i: (i, 0)
    const = lambda i: (0, 0)
    hq = MLA_HEADS * MLA_QK_PAD
    hv = MLA_HEADS * MLA_V
    return pl.pallas_call(
        _proj_kernel,
        grid=(n // tm,),
        in_specs=[pl.BlockSpec((tm, d), row),
                  pl.BlockSpec((1, 8, d), lambda i: (i // per_b, 0, 0)),
                  pl.BlockSpec((tm, 1), row),
                  pl.BlockSpec((1, d), const),
                  pl.BlockSpec(win.shape, const),
                  pl.BlockSpec(qan.shape, const),
                  pl.BlockSpec(wq.shape, const),
                  pl.BlockSpec(kvan.shape, const),
                  pl.BlockSpec(wkv.shape, const),
                  pl.BlockSpec(invf.shape, const)],
        out_specs=[pl.BlockSpec((tm, hq), row), pl.BlockSpec((tm, hq), row), pl.BlockSpec((tm, hv), row),
                   pl.BlockSpec((tm, _GDN_QKV), row), pl.BlockSpec((tm, GDN_HEADS * GDN_DV), row),
                   pl.BlockSpec((tm, LANES), row)],
        out_shape=[jax.ShapeDtypeStruct((n, hq), BF16), jax.ShapeDtypeStruct((n, hq), BF16),
                   jax.ShapeDtypeStruct((n, hv), BF16), jax.ShapeDtypeStruct((n, _GDN_QKV), BF16),
                   jax.ShapeDtypeStruct((n, GDN_HEADS * GDN_DV), BF16), jax.ShapeDtypeStruct((n, LANES), F32)],
        compiler_params=_cparams("parallel"),
        name="proj",
    )(x2, mod8, pos2, n1, win, qan, wq, kvan, wkv, invf)


def _attn_kernel(q_ref, k_ref, v_ref, o_ref, *, tq):
    i = pl.program_id(1)

    def head_step(hh, r0, carry, mask):
        m, l, acc = carry
        s = _dot_nt(q_ref[:, hh * MLA_QK_PAD:(hh + 1) * MLA_QK_PAD],
                    k_ref[pl.ds(r0, tq), hh * MLA_QK_PAD:(hh + 1) * MLA_QK_PAD])
        if mask is not None:
            s = jnp.where(mask, s, -jnp.inf)
        m_new = jnp.maximum(m, jnp.max(s, axis=-1, keepdims=True))
        alpha = jnp.exp2(m - m_new)
        p = jnp.exp2(s - m_new)
        l = alpha * l + jnp.sum(p, axis=-1, keepdims=True)
        acc = alpha * acc + _dot(p.astype(BF16), v_ref[pl.ds(r0, tq), hh * MLA_V:(hh + 1) * MLA_V])
        return m_new, l, acc

    def block(r0, carry, mask):
        return tuple(head_step(hh, r0, carry[hh], mask) for hh in range(MLA_HEADS))

    init = tuple((jnp.full((tq, 1), -jnp.inf, F32), jnp.zeros((tq, 1), F32), jnp.zeros((tq, MLA_V), F32))
                 for _ in range(MLA_HEADS))
    carry = lax.fori_loop(0, i, lambda j, c: block(pl.multiple_of(j * tq, tq), c, None), init)
    rq = lax.broadcasted_iota(jnp.int32, (tq, tq), 0) // CHUNK
    ck = lax.broadcasted_iota(jnp.int32, (tq, tq), 1) // CHUNK
    carry = block(pl.multiple_of(i * tq, tq), carry, ck <= rq)
    for hh in range(MLA_HEADS):
        _, l, acc = carry[hh]
        o_ref[:, hh * MLA_V:(hh + 1) * MLA_V] = (acc / l).astype(o_ref.dtype)


def _attn(q, k, v, *, batch, seq, tq):
    n = q.shape[0]
    nq = seq // tq
    return pl.pallas_call(
        functools.partial(_attn_kernel, tq=tq),
        grid=(batch, nq),
        in_specs=[pl.BlockSpec((tq, q.shape[1]), lambda b, i: (b * nq + i, 0)),
                  pl.BlockSpec((seq, k.shape[1]), lambda b, i: (b, 0)),
                  pl.BlockSpec((seq, v.shape[1]), lambda b, i: (b, 0))],
        out_specs=pl.BlockSpec((tq, v.shape[1]), lambda b, i: (b * nq + i, 0)),
        out_shape=jax.ShapeDtypeStruct((n, v.shape[1]), BF16),
        compiler_params=_cparams("parallel", "arbitrary"),
        name="attn",
    )(q, k, v)


_SUPER = 2 * CHUNK


def _unit_lower_inverses(lows):
    n = lows[0].shape[0]
    eye = (lax.broadcasted_iota(jnp.int32, (n, n), 0) == lax.broadcasted_iota(jnp.int32, (n, n), 1)).astype(F32)
    ps = [-low for low in lows]
    ts = [eye + p for p in ps]
    levels = CHUNK.bit_length() - 1
    for _ in range(levels - 1):
        parts = [_split3(p) for p in ps]
        ps = [_dot_x3(s, s) for s in parts]
        ts = [t + _dot_x3(_split3(t), _split3(p)) for t, p in zip(ts, ps)]
    return ts


def _gdn_kernel(qkv_ref, z_ref, ab_ref, cw_ref, hp_ref, nw_ref, o_ref, xe_ref, st_ref, *, tt):
    j = pl.program_id(1)
    hd = GDN_HEADS * GDN_DK

    @pl.when(j == 0)
    def _():
        xe_ref[0:8, :] = jnp.zeros((8, xe_ref.shape[1]), F32)
        st_ref[...] = jnp.zeros(st_ref.shape, F32)

    xe_ref[8:8 + tt, :] = qkv_ref[...].astype(F32)
    cw = cw_ref[...]
    y = xe_ref[8:8 + tt, :] * cw[GDN_CONV - 1:GDN_CONV]
    for i in range(1, GDN_CONV):
        y = y + xe_ref[8 - i:8 - i + tt, :] * cw[GDN_CONV - 1 - i:GDN_CONV - i]
    xe_ref[0:8, :] = xe_ref[tt:tt + 8, :]
    act = _silu(y)

    ab = ab_ref[...]
    hp = hp_ref[...]
    pre = ab + hp[1:2]
    softplus = jnp.maximum(pre, 0.0) + jnp.log1p(jnp.exp(-jnp.abs(pre)))
    g_all = -jnp.exp(hp[0:1]) * softplus
    beta_all = jax.nn.sigmoid(ab)

    ri = lax.broadcasted_iota(jnp.int32, (tt, tt), 0)
    ci = lax.broadcasted_iota(jnp.int32, (tt, tt), 1)
    tri = ((ri // CHUNK == ci // CHUNK) & (ci <= ri)).astype(BF16)
    gc = sum(_dot(tri, part) for part in _split3(g_all))
    gct = gc.T

    rs = lax.broadcasted_iota(jnp.int32, (_SUPER, _SUPER), 0)
    cs = lax.broadcasted_iota(jnp.int32, (_SUPER, _SUPER), 1)
    same = rs // CHUNK == cs // CHUNK
    incl = same & (cs <= rs)
    strict = same & (cs < rs)

    tiles = [(s * _SUPER, hh) for s in range(tt // _SUPER) for hh in range(GDN_HEADS)]
    pre_t = []
    for r0, hh in tiles:
        qh = act[r0:r0 + _SUPER, hh * GDN_DK:(hh + 1) * GDN_DK]
        kh = act[r0:r0 + _SUPER, hd + hh * GDN_DK:hd + (hh + 1) * GDN_DK]
        vh = act[r0:r0 + _SUPER, 2 * hd + hh * GDN_DV:2 * hd + (hh + 1) * GDN_DV]
        qh = qh * lax.rsqrt(jnp.sum(qh * qh, axis=-1, keepdims=True) + 1e-6) * (GDN_DK ** -0.5)
        kh = kh * lax.rsqrt(jnp.sum(kh * kh, axis=-1, keepdims=True) + 1e-6)
        beta = beta_all[r0:r0 + _SUPER, GDN_HEADS + hh:GDN_HEADS + hh + 1]
        gcol = gc[r0:r0 + _SUPER, hh:hh + 1]
        grow = gct[hh:hh + 1, r0:r0 + _SUPER]
        decay = jnp.exp(jnp.where(incl, gcol - grow, -jnp.inf))
        eg = jnp.exp(gcol)
        kb = kh * beta
        pre_t.append(dict(kh=kh, khb=kh.astype(BF16), kbb=kb.astype(BF16), vbb=(vh * beta).astype(BF16),
                          kgb=(kb * eg).astype(BF16), qb=qh.astype(BF16), qgb=(qh * eg).astype(BF16),
                          gcol=gcol, decay=decay))
    lows = [jnp.where(strict, _dot_nt(p["kbb"], p["khb"]) * p["decay"], 0.0) for p in pre_t]
    tinvs = [t.astype(BF16) for t in _unit_lower_inverses(lows)]
    us = [_dot(t, p["vbb"]) for t, p in zip(tinvs, pre_t)]
    ws = [_dot(t, p["kgb"]).astype(BF16) for t, p in zip(tinvs, pre_t)]
    atts = [(_dot_nt(p["qb"], p["khb"]) * p["decay"]).astype(BF16) for p in pre_t]

    states = [st_ref[hh] for hh in range(GDN_HEADS)]
    for cc in range(tt // CHUNK):
        r0 = cc * CHUNK
        a0 = r0 % _SUPER
        for hh in range(GDN_HEADS):
            ti = (r0 // _SUPER) * GDN_HEADS + hh
            p = pre_t[ti]
            gl = gc[r0 + CHUNK - 1:r0 + CHUNK, hh:hh + 1]
            kd = (p["kh"][a0:a0 + CHUNK] * jnp.exp(gl - p["gcol"][a0:a0 + CHUNK])).astype(BF16)
            sb = states[hh].astype(BF16)
            vnb = (us[ti][a0:a0 + CHUNK] - _dot(ws[ti][a0:a0 + CHUNK], sb)).astype(BF16)
            o = _dot(p["qgb"][a0:a0 + CHUNK], sb) + _dot(atts[ti][a0:a0 + CHUNK, a0:a0 + CHUNK], vnb)
            states[hh] = states[hh] * jnp.exp(gl) + _dot_tn(kd, vnb)
            zz = z_ref[r0:r0 + CHUNK, hh * GDN_DV:(hh + 1) * GDN_DV].astype(F32)
            o_ref[r0:r0 + CHUNK, hh * GDN_DV:(hh + 1) * GDN_DV] = (
                _rms(o, nw_ref[...]) * _silu(zz)).astype(o_ref.dtype)
    for hh in range(GDN_HEADS):
        st_ref[hh] = states[hh]


def _gdn(qkv, z, ab, cw8, hp, nw, *, batch, seq, tt):
    n = qkv.shape[0]
    per_b = seq // tt
    row = lambda b, j: (b * per_b + j, 0)
    const = lambda b, j: (0, 0)
    hv = GDN_HEADS * GDN_DV
    return pl.pallas_call(
        functools.partial(_gdn_kernel, tt=tt),
        grid=(batch, per_b),
        in_specs=[pl.BlockSpec((tt, _GDN_QKV), row), pl.BlockSpec((tt, hv), row), pl.BlockSpec((tt, LANES), row),
                  pl.BlockSpec(cw8.shape, const), pl.BlockSpec(hp.shape, const), pl.BlockSpec(nw.shape, const)],
        out_specs=pl.BlockSpec((tt, hv), row),
        out_shape=jax.ShapeDtypeStruct((n, hv), BF16),
        scratch_shapes=[pltpu.VMEM((tt + 8, _GDN_QKV), F32), pltpu.VMEM((GDN_HEADS, GDN_DK, GDN_DV), F32)],
        compiler_params=_cparams("parallel", "arbitrary"),
        name="gdn",
    )(qkv, z, ab, cw8, hp, nw)


def _route(logits_t, bias_col):
    e, tm = logits_t.shape
    gsz = e // N_GROUPS
    scores = jax.nn.sigmoid(logits_t)
    biased = scores + bias_col
    sub = lax.broadcasted_iota(jnp.int32, (gsz, tm), 0)
    rows = []
    for g in range(N_GROUPS):
        blk = biased[g * gsz:(g + 1) * gsz]
        m1 = jnp.max(blk, axis=0, keepdims=True)
        first = jnp.min(jnp.where(blk == m1, sub, gsz), axis=0, keepdims=True)
        m2 = jnp.max(jnp.where(sub == first, -jnp.inf, blk), axis=0, keepdims=True)
        rows.append(m1 + m2)
    gs = jnp.concatenate(rows, axis=0)
    gi = lax.broadcasted_iota(jnp.int32, (N_GROUPS, tm), 0)
    grank = jnp.zeros((N_GROUPS, tm), F32)
    for g in range(N_GROUPS):
        r = gs[g:g + 1]
        grank = grank + jnp.where(r > gs, 1.0, jnp.where(r == gs, (gi > g).astype(F32), 0.0))
    gsel = grank < TOPK_GROUPS
    masked = jnp.concatenate(
        [jnp.where(gsel[g:g + 1], biased[g * gsz:(g + 1) * gsz], -jnp.inf) for g in range(N_GROUPS)], axis=0)
    ei = lax.broadcasted_iota(jnp.int32, (e, tm), 0)
    rank = jnp.zeros((e, tm), F32)
    for k in range(e):
        r = masked[k:k + 1]
        rank = rank + jnp.where(r > masked, 1.0, jnp.where(r == masked, (ei > k).astype(F32), 0.0))
    return scores, rank


SUBLANES = 4


def _store_tile_rows(ref, x):
    c = x.shape[1] // 2
    lo = pltpu.bitcast(x[:, :c].astype(BF16).astype(F32), jnp.uint32)
    hi = pltpu.bitcast(x[:, c:].astype(BF16).astype(F32), jnp.uint32)
    w = (hi & jnp.uint32(0xFFFF0000)) | (lo >> 16)
    for j in range(SUBLANES):
        ref[pl.ds(j, x.shape[0], stride=SUBLANES), :] = w[:, j * LANES:(j + 1) * LANES]


def _load_tile_rows(ref, r0, rows):
    w = jnp.concatenate(
        [ref[pl.ds(r0 * SUBLANES + j, rows, stride=SUBLANES), :] for j in range(SUBLANES)], axis=1)
    return jnp.concatenate([pltpu.bitcast(w << 16, F32), pltpu.bitcast(w & jnp.uint32(0xFFFF0000), F32)], axis=1)


def _tile_row(ref, sublane_offset):
    return ref.at[pl.ds(pl.multiple_of(sublane_offset, SUBLANES), SUBLANES), :]


def _mix_kernel(oa_ref, ob_ref, x_ref, mod_ref, woa_ref, wob_ref, n2_ref, rwt_ref, rb_ref, sg_ref, su_ref, sd_ref,
                xs_ref, h2t_ref, te_ref, tw_ref, tp_ref, cnt_ref, run_ref):
    @pl.when(pl.program_id(0) == 0)
    def _():
        run_ref[...] = jnp.zeros(run_ref.shape, F32)

    mod = mod_ref[0]
    g1, sh2, sc2, g2 = mod[2:3], mod[3:4], mod[4:5], mod[5:6]
    mix = _dot(oa_ref[...], woa_ref[...]) + _dot(ob_ref[...], wob_ref[...])
    x1 = x_ref[...] + g1 * mix
    h2 = _rms(x1, n2_ref[...]) * (1.0 + sc2) + sh2
    _store_tile_rows(h2t_ref, h2)
    h2b = h2.astype(BF16)
    hs = (_silu(_dot(h2b, sg_ref[...])) * _dot(h2b, su_ref[...])).astype(BF16)
    xs_ref[...] = x1 + g2 * _dot(hs, sd_ref[...])

    scores, rank = _route(_dot_nt(rwt_ref[...], h2, HIGHEST), rb_ref[...])
    e, tm = rank.shape
    sel = jnp.where(rank < TOP_K, 1.0, 0.0)
    earlier = (lax.broadcasted_iota(jnp.int32, (tm, tm), 0) < lax.broadcasted_iota(jnp.int32, (tm, tm), 1))
    posmat = _dot(sel.astype(BF16), earlier.astype(BF16)) + run_ref[...]
    eif = lax.broadcasted_iota(jnp.int32, (e, tm), 0).astype(F32)
    ids, wts, pos = [], [], []
    for k in range(TOP_K):
        hit = rank == float(k)
        ids.append(jnp.sum(jnp.where(hit, eif, 0.0), axis=0, keepdims=True))
        wts.append(jnp.sum(jnp.where(hit, scores, 0.0), axis=0, keepdims=True))
        pos.append(jnp.sum(jnp.where(hit, posmat, 0.0), axis=0, keepdims=True))
    wts = jnp.concatenate(wts, axis=0)
    te_ref[0] = jnp.concatenate(ids, axis=0).astype(jnp.int32)
    tw_ref[0] = wts / jnp.sum(wts, axis=0, keepdims=True) * ROUTED_SCALE
    tp_ref[0] = jnp.concatenate(pos, axis=0).astype(jnp.int32)
    run_ref[...] = run_ref[...] + jnp.sum(sel, axis=1, keepdims=True)
    cnt_ref[...] = jnp.broadcast_to(run_ref[...], cnt_ref.shape)


def _mix(oa, ob, x2, mod8, woa, wob, n2, rwt, rb, sg, su, sd, *, seq, tm):
    n, d = x2.shape
    per_b = seq // tm
    nb = n // tm
    row = lambda i: (i, 0)
    blk3 = lambda i: (i, 0, 0)
    const = lambda i: (0, 0)
    return pl.pallas_call(
        _mix_kernel,
        grid=(nb,),
        in_specs=[pl.BlockSpec((tm, oa.shape[1]), row), pl.BlockSpec((tm, ob.shape[1]), row),
                  pl.BlockSpec((tm, d), row), pl.BlockSpec((1, 8, d), lambda i: (i // per_b, 0, 0)),
                  pl.BlockSpec(woa.shape, const), pl.BlockSpec(wob.shape, const), pl.BlockSpec((1, d), const),
                  pl.BlockSpec(rwt.shape, const), pl.BlockSpec(rb.shape, const),
                  pl.BlockSpec(sg.shape, const), pl.BlockSpec(su.shape, const), pl.BlockSpec(sd.shape, const)],
        out_specs=[pl.BlockSpec((tm, d), row), pl.BlockSpec((tm * SUBLANES, LANES), row),
                   pl.BlockSpec((1, TOP_K, tm), blk3), pl.BlockSpec((1, TOP_K, tm), blk3),
                   pl.BlockSpec((1, TOP_K, tm), blk3), pl.BlockSpec((N_EXPERTS, LANES), const)],
        out_shape=[jax.ShapeDtypeStruct((n, d), F32), jax.ShapeDtypeStruct((n * SUBLANES, LANES), jnp.uint32),
                   jax.ShapeDtypeStruct((nb, TOP_K, tm), jnp.int32), jax.ShapeDtypeStruct((nb, TOP_K, tm), F32),
                   jax.ShapeDtypeStruct((nb, TOP_K, tm), jnp.int32),
                   jax.ShapeDtypeStruct((N_EXPERTS, LANES), F32)],
        scratch_shapes=[pltpu.VMEM((N_EXPERTS, 1), F32)],
        compiler_params=_cparams("arbitrary"),
        name="mix",
    )(oa, ob, x2, mod8, woa, wob, n2, rwt, rb, sg, su, sd)


def _slots_kernel(ps_ref, ids_ref, pos_ref, o_ref):
    ids = ids_ref[...]
    start = jnp.zeros(ids.shape, jnp.int32)
    for e in range(N_EXPERTS):
        start = jnp.where(ids == e, ps_ref[e], start)
    o_ref[...] = (start + pos_ref[...]) * SUBLANES


def _slots(ps, ids, pos):
    spec = pl.BlockSpec(ids.shape, lambda i, ps: (0, 0, 0))
    return pl.pallas_call(
        _slots_kernel,
        grid_spec=pltpu.PrefetchScalarGridSpec(num_scalar_prefetch=1, grid=(1,),
                                               in_specs=[spec, spec], out_specs=spec),
        out_shape=jax.ShapeDtypeStruct(ids.shape, jnp.int32),
        compiler_params=_cparams("parallel"),
        name="slots",
    )(ps, ids, pos)


def _dispatch_kernel(ps_ref, slot_ref, h2t_ref, xs_hbm, zbuf, zsem, sem, *, tm, blk):
    @pl.when(pl.program_id(0) == 0)
    def _():
        zbuf[...] = jnp.zeros(zbuf.shape, zbuf.dtype)

        def block_at(row):
            return xs_hbm.at[pl.ds(pl.multiple_of(row * SUBLANES, blk * SUBLANES), blk * SUBLANES), :]

        def tail(e):
            end = ps_ref[e + 1]
            return end > ps_ref[e], pltpu.make_async_copy(zbuf, block_at(end - blk), zsem)

        def unused(e):
            row = ps_ref[N_EXPERTS] + e * blk
            return row * SUBLANES < xs_hbm.shape[0], pltpu.make_async_copy(zbuf, block_at(row), zsem)

        for fill in (tail, unused):
            for e in range(N_EXPERTS):
                needed, cp = fill(e)
                pl.when(needed)(cp.start)
        for fill in (tail, unused):
            for e in range(N_EXPERTS):
                needed, cp = fill(e)
                pl.when(needed)(cp.wait)

    def issue(t, carry):
        for k in range(TOP_K):
            pltpu.make_async_copy(_tile_row(h2t_ref, t * SUBLANES), _tile_row(xs_hbm, slot_ref[0, k, t]),
                                  sem).start(priority=k % 2)
        return carry

    lax.fori_loop(0, tm, issue, 0)
    for k in range(TOP_K):
        pltpu.make_async_copy(h2t_ref, xs_hbm.at[pl.ds(0, tm * SUBLANES), :], sem).wait()


def _dispatch(ps, slot, h2t, *, n_slots, tm, blk):
    lanes = h2t.shape[1]
    grid_spec = pltpu.PrefetchScalarGridSpec(
        num_scalar_prefetch=1,
        grid=(h2t.shape[0] // (tm * SUBLANES),),
        in_specs=[pl.BlockSpec((1, TOP_K, tm), lambda i, ps: (i, 0, 0), memory_space=pltpu.SMEM),
                  pl.BlockSpec((tm * SUBLANES, lanes), lambda i, ps: (i, 0))],
        out_specs=pl.BlockSpec(memory_space=pl.ANY),
        scratch_shapes=[pltpu.VMEM((blk * SUBLANES, lanes), jnp.uint32), pltpu.SemaphoreType.DMA(()),
                        pltpu.SemaphoreType.DMA(())],
    )
    return pl.pallas_call(
        functools.partial(_dispatch_kernel, tm=tm, blk=blk),
        grid_spec=grid_spec,
        out_shape=jax.ShapeDtypeStruct((n_slots * SUBLANES, lanes), jnp.uint32),
        compiler_params=_cparams("arbitrary"),
        name="dispatch",
    )(ps, slot, h2t)


def _experts_kernel(be_ref, nu_ref, x_ref, wg_ref, wu_ref, wd_ref, y_ref, wgb, wub, wdb):
    i = pl.program_id(0)
    used = i < nu_ref[0]
    new_expert = jnp.logical_or(i == 0, be_ref[i] != be_ref[jnp.maximum(i - 1, 0)])

    @pl.when(jnp.logical_and(used, new_expert))
    def _():
        wgb[...] = wg_ref[0].astype(BF16)
        wub[...] = wu_ref[0].astype(BF16)
        wdb[...] = wd_ref[0].astype(BF16)

    @pl.when(used)
    def _():
        xb = _load_tile_rows(x_ref, 0, x_ref.shape[0] // SUBLANES).astype(BF16)
        hb = (_silu(_dot(xb, wgb[...])) * _dot(xb, wub[...])).astype(BF16)
        _store_tile_rows(y_ref, _dot(hb, wdb[...]))

    @pl.when(jnp.logical_not(used))
    def _():
        y_ref[...] = jnp.zeros(y_ref.shape, y_ref.dtype)


def _experts(block_e, n_used, xs, wg, wu, wd, *, blk):
    n_blocks = block_e.shape[0]
    rows_blk = (blk * SUBLANES, xs.shape[1])
    wsel = lambda i, be, nu: (be[i], 0, 0)
    grid_spec = pltpu.PrefetchScalarGridSpec(
        num_scalar_prefetch=2,
        grid=(n_blocks,),
        in_specs=[pl.BlockSpec(rows_blk, lambda i, be, nu: (jnp.minimum(i, nu[0] - 1), 0)),
                  pl.BlockSpec((1,) + wg.shape[1:], wsel), pl.BlockSpec((1,) + wu.shape[1:], wsel),
                  pl.BlockSpec((1,) + wd.shape[1:], wsel)],
        out_specs=pl.BlockSpec(rows_blk, lambda i, be, nu: (i, 0)),
        scratch_shapes=[pltpu.VMEM(wg.shape[1:], BF16), pltpu.VMEM(wu.shape[1:], BF16),
                        pltpu.VMEM(wd.shape[1:], BF16)],
    )
    return pl.pallas_call(
        _experts_kernel,
        grid_spec=grid_spec,
        out_shape=jax.ShapeDtypeStruct(xs.shape, xs.dtype),
        compiler_params=_cparams("arbitrary"),
        name="experts",
    )(block_e, n_used, xs, wg, wu, wd)


def _combine_kernel(slot_ref, nxt_ref, tw_ref, xs_ref, mod_ref, fn_ref, y_hbm, o_ref, gbuf, sems, *, tm):
    j = pl.program_id(0)

    def gather(src_ref, bi, b):
        def issue(t, carry):
            for k in range(TOP_K):
                pltpu.make_async_copy(_tile_row(y_hbm, src_ref[bi, k, t]),
                                      _tile_row(gbuf.at[b], (k * tm + t) * SUBLANES),
                                      sems.at[b]).start(priority=k % 2)
            return carry

        lax.fori_loop(0, tm, issue, 0)

    def drain(b):
        pltpu.make_async_copy(y_hbm.at[pl.ds(0, gbuf.shape[1]), :], gbuf.at[b], sems.at[b]).wait()

    def finish(b):
        wt = jnp.concatenate([tw_ref[b], jnp.zeros((LANES - TOP_K, tm), F32)], axis=0).T
        acc = wt[:, 0:1] * _load_tile_rows(gbuf.at[b], 0, tm)
        for k in range(1, TOP_K):
            acc = acc + wt[:, k:k + 1] * _load_tile_rows(gbuf.at[b], k * tm, tm)
        g2 = mod_ref[0][5:6]
        o_ref[b * tm:(b + 1) * tm, :] = _rms(xs_ref[b * tm:(b + 1) * tm, :] + g2 * acc, fn_ref[...])

    @pl.when(j == 0)
    def _():
        gather(slot_ref, 0, 0)

    gather(slot_ref, 1, 1)
    drain(0)
    finish(0)

    @pl.when(j + 1 < pl.num_programs(0))
    def _():
        gather(nxt_ref, 0, 0)

    drain(1)
    finish(1)


def _combine(slot, tw, xs, mod8, fn, y, *, seq, tm):
    n, d = xs.shape
    nb = slot.shape[0]
    assert nb % 2 == 0 and (seq // tm) % 2 == 0
    per_b = seq // (2 * tm)
    pair = lambda j: (j, 0, 0)
    return pl.pallas_call(
        functools.partial(_combine_kernel, tm=tm),
        grid=(nb // 2,),
        in_specs=[pl.BlockSpec((2, TOP_K, tm), pair, memory_space=pltpu.SMEM),
                  pl.BlockSpec((1, TOP_K, tm), lambda j: (jnp.minimum(2 * j + 2, nb - 1), 0, 0),
                               memory_space=pltpu.SMEM),
                  pl.BlockSpec((2, TOP_K, tm), pair),
                  pl.BlockSpec((2 * tm, d), lambda j: (j, 0)),
                  pl.BlockSpec((1, 8, d), lambda j: (j // per_b, 0, 0)),
                  pl.BlockSpec((1, d), lambda j: (0, 0)),
                  pl.BlockSpec(memory_space=pl.ANY)],
        out_specs=pl.BlockSpec((2 * tm, d), lambda j: (j, 0)),
        out_shape=jax.ShapeDtypeStruct((n, d), F32),
        scratch_shapes=[pltpu.VMEM((2, TOP_K * tm * SUBLANES, y.shape[1]), y.dtype),
                        pltpu.SemaphoreType.DMA((2,))],
        compiler_params=_cparams("arbitrary"),
        name="combine",
    )(slot, slot, tw, xs, mod8, fn, y)


def _expert_runs(counts, n_blocks, blk):
    padded = (counts + blk - 1) // blk * blk
    pad_end = jnp.cumsum(padded)
    run_start = jnp.concatenate([pad_end - padded, pad_end[-1:]]).astype(jnp.int32)
    block_row = jnp.arange(n_blocks, dtype=jnp.int32) * blk
    block_e = jnp.minimum(jnp.sum(pad_end[None, :] <= block_row[:, None], axis=1), N_EXPERTS - 1).astype(jnp.int32)
    n_used = (pad_end[-1:] // blk).astype(jnp.int32)
    return run_start, block_e, n_used


def _pad_lanes(w, width):
    return jnp.pad(w, ((0, 0), (0, width - w.shape[1])))


def _tiles(seq):
    t = dict(tm_proj=256, tq=512, tt=256, tm_moe=512, blk=512)
    assert all(seq % v == 0 for k, v in t.items() if k != "blk")
    return t


def _layer(x, c, positions, ada_w, ada_b, norm1_w, w_in, q_a_norm_w, wq_b, kv_a_norm_w, wkv_b,
           gdn_conv_w, gdn_a_log, gdn_dt_bias, gdn_norm_w, w_out, norm2_w, router_w, router_bias,
           exp_w_gate, exp_w_up, exp_w_down, sh_w_gate, sh_w_up, sh_w_down, final_norm_w,
           *, tm_proj, tq, tt, tm_moe, blk):
    batch, seq, d = x.shape
    n = batch * seq
    x2 = x.reshape(n, d)

    c8 = jnp.zeros((8, d), F32).at[:batch].set(c)
    mod = _adaln(c8, ada_w, ada_b.reshape(1, -1))
    mod8 = jnp.pad(mod[:batch].reshape(batch, 6, d), ((0, 0), (0, 2), (0, 0)))

    hq = MLA_NOPE + MLA_ROPE
    w_cq, w_ckv, w_kpe, w_qkv, w_z, w_a, w_b = jnp.split(
        w_in, [_C_KV, _C_KV + MLA_KV_RANK, _C_KV + MLA_KV_RANK + MLA_ROPE,
               _C_KV + MLA_KV_RANK + MLA_ROPE + _GDN_QKV,
               _C_KV + MLA_KV_RANK + MLA_ROPE + _GDN_QKV + GDN_HEADS * GDN_DV,
               _C_KV + MLA_KV_RANK + MLA_ROPE + _GDN_QKV + GDN_HEADS * GDN_DV + GDN_HEADS], axis=1)
    win = jnp.concatenate([w_cq, w_ckv, _pad_lanes(w_kpe, LANES), w_qkv, w_z,
                           _pad_lanes(jnp.concatenate([w_a, w_b], axis=1), LANES)], axis=1).astype(BF16)
    wq = jnp.pad(wq_b.reshape(MLA_Q_RANK, MLA_HEADS, hq),
                 ((0, 0), (0, 0), (0, MLA_QK_PAD - hq))).reshape(MLA_Q_RANK, MLA_HEADS * MLA_QK_PAD).astype(BF16)
    wkv4 = wkv_b.reshape(MLA_KV_RANK, MLA_HEADS, MLA_NOPE + MLA_V)
    wkv = jnp.concatenate([wkv4[:, :, :MLA_NOPE].reshape(MLA_KV_RANK, -1),
                           wkv4[:, :, MLA_NOPE:].reshape(MLA_KV_RANK, -1)], axis=1).astype(BF16)
    inv_freq = 1.0 / (ROPE_THETA ** (jnp.arange(0, MLA_ROPE, 2, dtype=F32) / MLA_ROPE))
    invf = _pad_lanes(jnp.concatenate([inv_freq, inv_freq])[None, :], LANES)

    q, k, v, qkv, z, ab = _proj(x2, mod8, positions.reshape(n, 1), norm1_w.reshape(1, d), win,
                                q_a_norm_w.reshape(1, -1), wq, kv_a_norm_w.reshape(1, -1), wkv, invf,
                                seq=seq, tm=tm_proj)
    out_a = _attn(q, k, v, batch=batch, seq=seq, tq=tq)

    cw8 = jnp.pad(gdn_conv_w, ((0, 8 - GDN_CONV), (0, 0)))
    hp = _pad_lanes(jnp.stack([gdn_a_log, gdn_dt_bias]), LANES)
    hp = jnp.pad(hp, ((0, 6), (0, 0)))
    out_b = _gdn(qkv, z, ab, cw8, hp, gdn_norm_w.reshape(1, -1), batch=batch, seq=seq, tt=tt)

    ha = MLA_HEADS * MLA_V
    xs, h2t, ids, tw, pos, counts = _mix(
        out_a, out_b, x2, mod8, w_out[:ha].astype(BF16), w_out[ha:].astype(BF16), norm2_w.reshape(1, d),
        router_w.T, router_bias.reshape(-1, 1), sh_w_gate.astype(BF16), sh_w_up.astype(BF16),
        sh_w_down.astype(BF16), seq=seq, tm=tm_moe)

    n_blocks = n * TOP_K // blk + N_EXPERTS
    run_start, block_e, n_used = _expert_runs(counts[:, 0].astype(jnp.int32), n_blocks, blk)
    slot = _slots(run_start, ids, pos)
    xsort = _dispatch(run_start, slot, h2t, n_slots=n_blocks * blk, tm=tm_moe, blk=blk)
    y = _experts(block_e, n_used, xsort, exp_w_gate, exp_w_up, exp_w_down, blk=blk)
    out = _combine(slot, tw, xs, mod8, final_norm_w.reshape(1, d), y, seq=seq, tm=tm_moe)
    return out.reshape(batch, seq, d)


def kernel(x, c, positions, ada_w, ada_b, norm1_w, w_in, q_a_norm_w, wq_b, kv_a_norm_w, wkv_b, gdn_conv_w,
           gdn_a_log, gdn_dt_bias, gdn_norm_w, w_out, norm2_w, router_w, router_bias, exp_w_gate, exp_w_up,
           exp_w_down, sh_w_gate, sh_w_up, sh_w_down, final_norm_w):
    assert ada_w.shape[0] == 1, "single layer"
    return _layer(
        x, c, positions, ada_w[0], ada_b[0], norm1_w[0], w_in[0], q_a_norm_w[0], wq_b[0], kv_a_norm_w[0],
        wkv_b[0], gdn_conv_w[0], gdn_a_log[0], gdn_dt_bias[0], gdn_norm_w[0], w_out[0], norm2_w[0],
        router_w[0], router_bias[0], exp_w_gate[0], exp_w_up[0], exp_w_down[0], sh_w_gate[0], sh_w_up[0],
        sh_w_down[0], final_norm_w, **_tiles(x.shape[1]))
```

```python
import functools

import jax
import jax.numpy as jnp
from jax import lax
from jax.experimental import pallas as pl
from jax.experimental.pallas import tpu as pltpu

F32 = jnp.float32
BF16 = jnp.bfloat16
HIGHEST = lax.Precision.HIGHEST

CHUNK = 64
NORM_EPS = 1e-6
MLA_HEADS = 4
MLA_Q_RANK = 384
MLA_KV_RANK = 256
MLA_NOPE = 128
MLA_ROPE = 64
MLA_V = 128
ROPE_THETA = 10000.0
GDN_HEADS = 4
GDN_DK = 128
GDN_DV = 128
GDN_CONV = 4
N_EXPERTS = 64
N_GROUPS = 8
TOPK_GROUPS = 4
TOP_K = 8
ROUTED_SCALE = 2.5
LOG2E = 1.4426950408889634

LANES = 128
MLA_QK_PAD = 256
VMEM_LIMIT = 56 * 1024 * 1024


def _cparams(*sem):
    return pltpu.CompilerParams(dimension_semantics=sem, vmem_limit_bytes=VMEM_LIMIT)


def _dot(a, b):
    return jnp.dot(a, b, preferred_element_type=F32)


def _split3(a):
    hi = a.astype(BF16)
    r = a - hi.astype(F32)
    mid = r.astype(BF16)
    return hi, mid, (r - mid.astype(F32)).astype(BF16)


def _dot_x3(a, b):
    ah, am, _ = a
    bh, bm, _ = b
    return _dot(ah, bh) + (_dot(ah, bm) + _dot(am, bh))


def _dot_nt(a, b, precision=None):
    return lax.dot_general(a, b, (((1,), (1,)), ((), ())), preferred_element_type=F32, precision=precision)


def _dot_tn(a, b):
    return lax.dot_general(a, b, (((0,), (0,)), ((), ())), preferred_element_type=F32)


def _silu(x):
    return x * jax.nn.sigmoid(x)


def _rms(x, w):
    return x * lax.rsqrt(jnp.mean(x * x, axis=-1, keepdims=True) + NORM_EPS) * w


def _adaln_kernel(c_ref, w_ref, b_ref, o_ref):
    a = _silu(c_ref[...]).astype(BF16)
    o_ref[...] = _dot(a, w_ref[...].astype(BF16)) + b_ref[...]


def _adaln(c8, ada_w, ada_b):
    d = c8.shape[1]
    n_out = ada_w.shape[1]
    return pl.pallas_call(
        _adaln_kernel,
        grid=(n_out // d,),
        in_specs=[pl.BlockSpec((8, d), lambda j: (0, 0)),
                  pl.BlockSpec((d, d), lambda j: (0, j)),
                  pl.BlockSpec((1, d), lambda j: (0, j))],
        out_specs=pl.BlockSpec((8, d), lambda j: (0, j)),
        out_shape=jax.ShapeDtypeStruct((8, n_out), F32),
        compiler_params=_cparams("parallel"),
        name="adaln",
    )(c8, ada_w, ada_b)


_C_Q = 0
_C_KV = _C_Q + MLA_Q_RANK
_C_KPE = _C_KV + MLA_KV_RANK
_C_QKV = _C_KPE + LANES
_GDN_QKV = GDN_HEADS * (2 * GDN_DK + GDN_DV)
_C_Z = _C_QKV + _GDN_QKV
_C_AB = _C_Z + GDN_HEADS * GDN_DV
_D_IN_PAD = _C_AB + LANES


def _rope(xb, cos, sin):
    half = MLA_ROPE // 2
    lane = lax.broadcasted_iota(jnp.int32, xb.shape, 1)
    rot = jnp.where(lane < half, -pltpu.roll(xb, LANES - half, 1), pltpu.roll(xb, half, 1))
    return xb * cos + rot * sin


def _proj_kernel(x_ref, mod_ref, pos_ref, n1_ref, win_ref, qan_ref, wq_ref, kvan_ref, wkv_ref, invf_ref,
                 q_ref, k_ref, v_ref, qkv_ref, z_ref, ab_ref):
    x = x_ref[...]
    mod = mod_ref[0]
    sh1, sc1 = mod[0:1], mod[1:2]
    h = (_rms(x, n1_ref[...]) * (1.0 + sc1) + sh1).astype(BF16)
    proj = _dot(h, win_ref[...])

    ang = pos_ref[...].astype(F32) * invf_ref[...]
    cos, sin = jnp.cos(ang), jnp.sin(ang)

    cq = proj[:, _C_Q:_C_Q + MLA_Q_RANK]
    qn = _rms(cq, qan_ref[...]).astype(BF16)
    q = _dot(qn, wq_ref[...]) * ((MLA_NOPE + MLA_ROPE) ** -0.5 * LOG2E)
    for hh in range(MLA_HEADS):
        c0 = hh * MLA_QK_PAD
        q_ref[:, c0:c0 + MLA_NOPE] = q[:, c0:c0 + MLA_NOPE].astype(BF16)
        q_ref[:, c0 + MLA_NOPE:c0 + MLA_QK_PAD] = _rope(q[:, c0 + MLA_NOPE:c0 + MLA_QK_PAD], cos, sin).astype(BF16)

    ckv = proj[:, _C_KV:_C_KV + MLA_KV_RANK]
    kvn = _rms(ckv, kvan_ref[...]).astype(BF16)
    kv = _dot(kvn, wkv_ref[...])
    kpe = _rope(proj[:, _C_KPE:_C_KPE + LANES], cos, sin).astype(BF16)
    for hh in range(MLA_HEADS):
        c0 = hh * MLA_QK_PAD
        k_ref[:, c0:c0 + MLA_NOPE] = kv[:, hh * MLA_NOPE:(hh + 1) * MLA_NOPE].astype(BF16)
        k_ref[:, c0 + MLA_NOPE:c0 + MLA_QK_PAD] = kpe
    v_ref[...] = kv[:, MLA_HEADS * MLA_NOPE:].astype(BF16)

    qkv_ref[...] = proj[:, _C_QKV:_C_QKV + _GDN_QKV].astype(qkv_ref.dtype)
    z_ref[...] = proj[:, _C_Z:_C_Z + GDN_HEADS * GDN_DV].astype(z_ref.dtype)
    ab_ref[...] = proj[:, _C_AB:_C_AB + LANES]


def _proj(x2, mod8, pos2, n1, win, qan, wq, kvan, wkv, invf, *, seq, tm):
    n, d = x2.shape
    per_b = seq // tm
    row = lambda i: (i, 0)
    const = lambda i: (0, 0)
    hq = MLA_HEADS * MLA_QK_PAD
    hv = MLA_HEADS * MLA_V
    return pl.pallas_call(
        _proj_kernel,
        grid=(n // tm,),
        in_specs=[pl.BlockSpec((tm, d), row),
                  pl.BlockSpec((1, 8, d), lambda i: (i // per_b, 0, 0)),
                  pl.BlockSpec((tm, 1), row),
                  pl.BlockSpec((1, d), const),
                  pl.BlockSpec(win.shape, const),
                  pl.BlockSpec(qan.shape, const),
                  pl.BlockSpec(wq.shape, const),
                  pl.BlockSpec(kvan.shape, const),
                  pl.BlockSpec(wkv.shape, const),
                  pl.BlockSpec(invf.shape, const)],
        out_specs=[pl.BlockSpec((tm, hq), row), pl.BlockSpec((tm, hq), row), pl.BlockSpec((tm, hv), row),
                   pl.BlockSpec((tm, _GDN_QKV), row), pl.BlockSpec((tm, GDN_HEADS * GDN_DV), row),
                   pl.BlockSpec((tm, LANES), row)],
        out_shape=[jax.ShapeDtypeStruct((n, hq), BF16), jax.ShapeDtypeStruct((n, hq), BF16),
                   jax.ShapeDtypeStruct((n, hv), BF16), jax.ShapeDtypeStruct((n, _GDN_QKV), BF16),
                   jax.ShapeDtypeStruct((n, GDN_HEADS * GDN_DV), BF16), jax.ShapeDtypeStruct((n, LANES), F32)],
        compiler_params=_cparams("parallel"),
        name="proj",
    )(x2, mod8, pos2, n1, win, qan, wq, kvan, wkv, invf)


_Q_TILES = 2


def _attn_kernel(q_ref, k_ref, v_ref, o_ref, *, tq):
    i = pl.program_id(1)
    chains = [(qt, hh) for qt in range(_Q_TILES) for hh in range(MLA_HEADS)]

    ones_col = jnp.where(lax.broadcasted_iota(jnp.int32, (tq, MLA_V), 1) == 0, 1.0, 0.0).astype(BF16)

    def chain_step(qt, hh, r0, carry, mask):
        m, acc = carry
        s = _dot_nt(q_ref[qt * tq:(qt + 1) * tq, hh * MLA_QK_PAD:(hh + 1) * MLA_QK_PAD],
                    k_ref[pl.ds(r0, tq), hh * MLA_QK_PAD:(hh + 1) * MLA_QK_PAD])
        if mask is not None:
            s = jnp.where(mask, s, -jnp.inf)
        m_new = jnp.maximum(m, jnp.max(s, axis=-1, keepdims=True))
        alpha = jnp.exp2(m - m_new)
        p = jnp.exp2((s - m_new).astype(BF16))
        v_ext = jnp.concatenate([v_ref[pl.ds(r0, tq), hh * MLA_V:(hh + 1) * MLA_V], ones_col], axis=1)
        return m_new, alpha * acc + _dot(p, v_ext)

    def block(r0, carry, masks):
        return tuple(carry[c] if masks[qt] is False else chain_step(qt, hh, r0, carry[c], masks[qt])
                     for c, (qt, hh) in enumerate(chains))

    init = tuple((jnp.full((tq, 1), -jnp.inf, F32), jnp.zeros((tq, 2 * MLA_V), F32)) for _ in chains)
    carry = lax.fori_loop(0, _Q_TILES * i, lambda j, c: block(pl.multiple_of(j * tq, tq), c, (None, None)), init)
    rq = lax.broadcasted_iota(jnp.int32, (tq, tq), 0) // CHUNK
    ck = lax.broadcasted_iota(jnp.int32, (tq, tq), 1) // CHUNK
    diag = ck <= rq
    carry = block(pl.multiple_of(_Q_TILES * i * tq, tq), carry, (diag, None))
    carry = block(pl.multiple_of((_Q_TILES * i + 1) * tq, tq), carry, (False, diag))
    for c, (qt, hh) in enumerate(chains):
        _, acc = carry[c]
        o_ref[qt * tq:(qt + 1) * tq, hh * MLA_V:(hh + 1) * MLA_V] = (
            acc[:, :MLA_V] / acc[:, MLA_V:MLA_V + 1]).astype(o_ref.dtype)


def _attn(q, k, v, *, batch, seq, tq):
    n = q.shape[0]
    rows = _Q_TILES * tq
    nq = seq // rows
    return pl.pallas_call(
        functools.partial(_attn_kernel, tq=tq),
        grid=(batch, nq),
        in_specs=[pl.BlockSpec((rows, q.shape[1]), lambda b, i: (b * nq + i, 0)),
                  pl.BlockSpec((seq, k.shape[1]), lambda b, i: (b, 0)),
                  pl.BlockSpec((seq, v.shape[1]), lambda b, i: (b, 0))],
        out_specs=pl.BlockSpec((rows, v.shape[1]), lambda b, i: (b * nq + i, 0)),
        out_shape=jax.ShapeDtypeStruct((n, v.shape[1]), BF16),
        compiler_params=_cparams("parallel", "arbitrary"),
        name="attn",
    )(q, k, v)


_SUPER = 2 * CHUNK


def _unit_lower_inverses(lows):
    n = lows[0].shape[0]
    eye = (lax.broadcasted_iota(jnp.int32, (n, n), 0) == lax.broadcasted_iota(jnp.int32, (n, n), 1)).astype(F32)
    levels = CHUNK.bit_length() - 1
    ps = [-low for low in lows]
    ts = [eye + p for p in ps]
    for level in range(1, levels):
        parts = [_split3(p) for p in ps]
        ps = [_dot_x3(s, s) for s in parts]
        if level == 1:
            continue
        ts = [t + _dot_x3(s, _split3(t)) for t, s in zip(ts, parts)]
    return [t + _dot_x3(_split3(p), _split3(t)) for t, p in zip(ts, ps)]


def _gdn_kernel(qkv_ref, z_ref, ab_ref, cw_ref, hp_ref, nw_ref, o_ref, xe_ref, st_ref, *, tt):
    j = pl.program_id(1)
    hd = GDN_HEADS * GDN_DK

    @pl.when(j == 0)
    def _():
        xe_ref[0:8, :] = jnp.zeros((8, xe_ref.shape[1]), F32)
        st_ref[...] = jnp.zeros(st_ref.shape, F32)

    xe_ref[8:8 + tt, :] = qkv_ref[...].astype(F32)
    cw = cw_ref[...]
    y = xe_ref[8:8 + tt, :] * cw[GDN_CONV - 1:GDN_CONV]
    for i in range(1, GDN_CONV):
        y = y + xe_ref[8 - i:8 - i + tt, :] * cw[GDN_CONV - 1 - i:GDN_CONV - i]
    xe_ref[0:8, :] = xe_ref[tt:tt + 8, :]
    act = _silu(y)

    ab = ab_ref[...]
    hp = hp_ref[...]
    pre = ab + hp[1:2]
    softplus = jnp.maximum(pre, 0.0) + jnp.log1p(jnp.exp(-jnp.abs(pre)))
    g_all = -jnp.exp(hp[0:1]) * softplus
    beta_all = jax.nn.sigmoid(ab)

    ri = lax.broadcasted_iota(jnp.int32, (tt, tt), 0)
    ci = lax.broadcasted_iota(jnp.int32, (tt, tt), 1)
    tri = ((ri // CHUNK == ci // CHUNK) & (ci <= ri)).astype(BF16)
    gc = sum(_dot(tri, part) for part in _split3(g_all))
    gct = gc.T

    rs = lax.broadcasted_iota(jnp.int32, (_SUPER, _SUPER), 0)
    cs = lax.broadcasted_iota(jnp.int32, (_SUPER, _SUPER), 1)
    same = rs // CHUNK == cs // CHUNK
    incl = same & (cs <= rs)
    strict = same & (cs < rs)

    tiles = [(s * _SUPER, hh) for s in range(tt // _SUPER) for hh in range(GDN_HEADS)]
    pre_t = []
    for r0, hh in tiles:
        qh = act[r0:r0 + _SUPER, hh * GDN_DK:(hh + 1) * GDN_DK]
        kh = act[r0:r0 + _SUPER, hd + hh * GDN_DK:hd + (hh + 1) * GDN_DK]
        vh = act[r0:r0 + _SUPER, 2 * hd + hh * GDN_DV:2 * hd + (hh + 1) * GDN_DV]
        qh = qh * lax.rsqrt(jnp.sum(qh * qh, axis=-1, keepdims=True) + 1e-6) * (GDN_DK ** -0.5)
        kh = kh * lax.rsqrt(jnp.sum(kh * kh, axis=-1, keepdims=True) + 1e-6)
        beta = beta_all[r0:r0 + _SUPER, GDN_HEADS + hh:GDN_HEADS + hh + 1]
        gcol = gc[r0:r0 + _SUPER, hh:hh + 1]
        grow = gct[hh:hh + 1, r0:r0 + _SUPER]
        decay = jnp.exp(jnp.where(incl, gcol - grow, -jnp.inf))
        eg = jnp.exp(gcol)
        kb = kh * beta
        pre_t.append(dict(kh=kh, khb=kh.astype(BF16), kbb=kb.astype(BF16), vbb=(vh * beta).astype(BF16),
                          kgb=(kb * eg).astype(BF16), qb=qh.astype(BF16), qg=qh * eg, gcol=gcol, decay=decay))
    lows = [jnp.where(strict, _dot_nt(p["kbb"], p["khb"]) * p["decay"], 0.0) for p in pre_t]
    tinvs = [t.astype(BF16) for t in _unit_lower_inverses(lows)]
    us = [_dot(t, p["vbb"]).astype(BF16) for t, p in zip(tinvs, pre_t)]
    ws = [_dot(t, p["kgb"]).astype(BF16) for t, p in zip(tinvs, pre_t)]
    atts = [(_dot_nt(p["qb"], p["khb"]) * p["decay"]).astype(BF16) for p in pre_t]

    steps = []
    for cc in range(tt // CHUNK):
        r0 = cc * CHUNK
        a0 = r0 % _SUPER
        for hh in range(GDN_HEADS):
            ti = (r0 // _SUPER) * GDN_HEADS + hh
            p = pre_t[ti]
            gl = gc[r0 + CHUNK - 1:r0 + CHUNK, hh:hh + 1]
            kd = (p["kh"][a0:a0 + CHUNK] * jnp.exp(gl - p["gcol"][a0:a0 + CHUNK])).astype(BF16)
            u_c, w_c = us[ti][a0:a0 + CHUNK], ws[ti][a0:a0 + CHUNK]
            att_c = atts[ti][a0:a0 + CHUNK, a0:a0 + CHUNK]
            steps.append(dict(hh=hh, r0=r0, gain=jnp.exp(gl), kw=_dot_tn(kd, w_c).astype(BF16), ku=_dot_tn(kd, u_c),
                              qs=(p["qg"][a0:a0 + CHUNK] - _dot(att_c, w_c)).astype(BF16), ou=_dot(att_c, u_c)))

    states = [st_ref[hh] for hh in range(GDN_HEADS)]
    for st in steps:
        hh, r0 = st["hh"], st["r0"]
        sb = states[hh].astype(BF16)
        o = _dot(st["qs"], sb) + st["ou"]
        states[hh] = states[hh] * st["gain"] + (st["ku"] - _dot(st["kw"], sb))
        zz = z_ref[r0:r0 + CHUNK, hh * GDN_DV:(hh + 1) * GDN_DV].astype(F32)
        o_ref[r0:r0 + CHUNK, hh * GDN_DV:(hh + 1) * GDN_DV] = (_rms(o, nw_ref[...]) * _silu(zz)).astype(o_ref.dtype)
    for hh in range(GDN_HEADS):
        st_ref[hh] = states[hh]


def _gdn(qkv, z, ab, cw8, hp, nw, *, batch, seq, tt):
    n = qkv.shape[0]
    per_b = seq // tt
    row = lambda b, j: (b * per_b + j, 0)
    const = lambda b, j: (0, 0)
    hv = GDN_HEADS * GDN_DV
    return pl.pallas_call(
        functools.partial(_gdn_kernel, tt=tt),
        grid=(batch, per_b),
        in_specs=[pl.BlockSpec((tt, _GDN_QKV), row), pl.BlockSpec((tt, hv), row), pl.BlockSpec((tt, LANES), row),
                  pl.BlockSpec(cw8.shape, const), pl.BlockSpec(hp.shape, const), pl.BlockSpec(nw.shape, const)],
        out_specs=pl.BlockSpec((tt, hv), row),
        out_shape=jax.ShapeDtypeStruct((n, hv), BF16),
        scratch_shapes=[pltpu.VMEM((tt + 8, _GDN_QKV), F32), pltpu.VMEM((GDN_HEADS, GDN_DK, GDN_DV), F32)],
        compiler_params=_cparams("parallel", "arbitrary"),
        name="gdn",
    )(qkv, z, ab, cw8, hp, nw)


def _route(logits_t, bias_col):
    e, tm = logits_t.shape
    gsz = e // N_GROUPS
    scores = jax.nn.sigmoid(logits_t)
    biased = scores + bias_col
    sub = lax.broadcasted_iota(jnp.int32, (gsz, tm), 0)
    rows = []
    for g in range(N_GROUPS):
        blk = biased[g * gsz:(g + 1) * gsz]
        m1 = jnp.max(blk, axis=0, keepdims=True)
        first = jnp.min(jnp.where(blk == m1, sub, gsz), axis=0, keepdims=True)
        m2 = jnp.max(jnp.where(sub == first, -jnp.inf, blk), axis=0, keepdims=True)
        rows.append(m1 + m2)
    gs = jnp.concatenate(rows, axis=0)
    gi = lax.broadcasted_iota(jnp.int32, (N_GROUPS, tm), 0)
    grank = jnp.zeros((N_GROUPS, tm), F32)
    for g in range(N_GROUPS):
        r = gs[g:g + 1]
        grank = grank + jnp.where(r > gs, 1.0, jnp.where(r == gs, (gi > g).astype(F32), 0.0))
    gsel = grank < TOPK_GROUPS
    masked = jnp.concatenate(
        [jnp.where(gsel[g:g + 1], biased[g * gsz:(g + 1) * gsz], -jnp.inf) for g in range(N_GROUPS)], axis=0)
    ei = lax.broadcasted_iota(jnp.int32, (e, tm), 0)
    rank = jnp.zeros((e, tm), F32)
    for k in range(e):
        r = masked[k:k + 1]
        rank = rank + jnp.where(r > masked, 1.0, jnp.where(r == masked, (ei > k).astype(F32), 0.0))
    return scores, rank


SUBLANES = 4


def _store_tile_rows(ref, x):
    c = x.shape[1] // 2
    lo = pltpu.bitcast(x[:, :c].astype(BF16).astype(F32), jnp.uint32)
    hi = pltpu.bitcast(x[:, c:].astype(BF16).astype(F32), jnp.uint32)
    w = (hi & jnp.uint32(0xFFFF0000)) | (lo >> 16)
    for j in range(SUBLANES):
        ref[pl.ds(j, x.shape[0], stride=SUBLANES), :] = w[:, j * LANES:(j + 1) * LANES]


def _load_tile_rows(ref, r0, rows):
    w = jnp.concatenate(
        [ref[pl.ds(r0 * SUBLANES + j, rows, stride=SUBLANES), :] for j in range(SUBLANES)], axis=1)
    return jnp.concatenate([pltpu.bitcast(w << 16, F32), pltpu.bitcast(w & jnp.uint32(0xFFFF0000), F32)], axis=1)


def _tile_row(ref, sublane_offset):
    return ref.at[pl.ds(pl.multiple_of(sublane_offset, SUBLANES), SUBLANES), :]


def _mix_kernel(oa_ref, ob_ref, x_ref, mod_ref, woa_ref, wob_ref, n2_ref, rwt_ref, rb_ref, sg_ref, su_ref, sd_ref,
                xs_ref, h2t_ref, te_ref, tw_ref, tp_ref, cnt_ref, run_ref):
    @pl.when(pl.program_id(0) == 0)
    def _():
        run_ref[...] = jnp.zeros(run_ref.shape, F32)

    mod = mod_ref[0]
    g1, sh2, sc2, g2 = mod[2:3], mod[3:4], mod[4:5], mod[5:6]
    mix = _dot(oa_ref[...], woa_ref[...]) + _dot(ob_ref[...], wob_ref[...])
    x1 = x_ref[...] + g1 * mix
    h2 = _rms(x1, n2_ref[...]) * (1.0 + sc2) + sh2
    _store_tile_rows(h2t_ref, h2)
    h2b = h2.astype(BF16)
    hs = (_silu(_dot(h2b, sg_ref[...])) * _dot(h2b, su_ref[...])).astype(BF16)
    xs_ref[...] = x1 + g2 * _dot(hs, sd_ref[...])

    scores, rank = _route(_dot_nt(rwt_ref[...], h2, HIGHEST), rb_ref[...])
    e, tm = rank.shape
    sel = jnp.where(rank < TOP_K, 1.0, 0.0)
    earlier = (lax.broadcasted_iota(jnp.int32, (tm, tm), 0) < lax.broadcasted_iota(jnp.int32, (tm, tm), 1))
    posmat = _dot(sel.astype(BF16), earlier.astype(BF16)) + run_ref[...]
    eif = lax.broadcasted_iota(jnp.int32, (e, tm), 0).astype(F32)
    ids, wts, pos = [], [], []
    for k in range(TOP_K):
        hit = rank == float(k)
        ids.append(jnp.sum(jnp.where(hit, eif, 0.0), axis=0, keepdims=True))
        wts.append(jnp.sum(jnp.where(hit, scores, 0.0), axis=0, keepdims=True))
        pos.append(jnp.sum(jnp.where(hit, posmat, 0.0), axis=0, keepdims=True))
    wts = jnp.concatenate(wts, axis=0)
    te_ref[0] = jnp.concatenate(ids, axis=0).astype(jnp.int32)
    tw_ref[0] = wts / jnp.sum(wts, axis=0, keepdims=True) * ROUTED_SCALE
    tp_ref[0] = jnp.concatenate(pos, axis=0).astype(jnp.int32)
    run_ref[...] = run_ref[...] + jnp.sum(sel, axis=1, keepdims=True)
    cnt_ref[...] = jnp.broadcast_to(run_ref[...], cnt_ref.shape)


def _mix(oa, ob, x2, mod8, woa, wob, n2, rwt, rb, sg, su, sd, *, seq, tm):
    n, d = x2.shape
    per_b = seq // tm
    nb = n // tm
    row = lambda i: (i, 0)
    blk3 = lambda i: (i, 0, 0)
    const = lambda i: (0, 0)
    return pl.pallas_call(
        _mix_kernel,
        grid=(nb,),
        in_specs=[pl.BlockSpec((tm, oa.shape[1]), row), pl.BlockSpec((tm, ob.shape[1]), row),
                  pl.BlockSpec((tm, d), row), pl.BlockSpec((1, 8, d), lambda i: (i // per_b, 0, 0)),
                  pl.BlockSpec(woa.shape, const), pl.BlockSpec(wob.shape, const), pl.BlockSpec((1, d), const),
                  pl.BlockSpec(rwt.shape, const), pl.BlockSpec(rb.shape, const),
                  pl.BlockSpec(sg.shape, const), pl.BlockSpec(su.shape, const), pl.BlockSpec(sd.shape, const)],
        out_specs=[pl.BlockSpec((tm, d), row), pl.BlockSpec((tm * SUBLANES, LANES), row),
                   pl.BlockSpec((1, TOP_K, tm), blk3), pl.BlockSpec((1, TOP_K, tm), blk3),
                   pl.BlockSpec((1, TOP_K, tm), blk3), pl.BlockSpec((N_EXPERTS, LANES), const)],
        out_shape=[jax.ShapeDtypeStruct((n, d), F32), jax.ShapeDtypeStruct((n * SUBLANES, LANES), jnp.uint32),
                   jax.ShapeDtypeStruct((nb, TOP_K, tm), jnp.int32), jax.ShapeDtypeStruct((nb, TOP_K, tm), F32),
                   jax.ShapeDtypeStruct((nb, TOP_K, tm), jnp.int32),
                   jax.ShapeDtypeStruct((N_EXPERTS, LANES), F32)],
        scratch_shapes=[pltpu.VMEM((N_EXPERTS, 1), F32)],
        compiler_params=_cparams("arbitrary"),
        name="mix",
    )(oa, ob, x2, mod8, woa, wob, n2, rwt, rb, sg, su, sd)


def _slots_kernel(ps_ref, ids_ref, pos_ref, o_ref):
    ids = ids_ref[...]
    start = jnp.zeros(ids.shape, jnp.int32)
    for e in range(N_EXPERTS):
        start = jnp.where(ids == e, ps_ref[e], start)
    o_ref[...] = (start + pos_ref[...]) * SUBLANES


def _slots(ps, ids, pos):
    spec = pl.BlockSpec(ids.shape, lambda i, ps: (0, 0, 0))
    return pl.pallas_call(
        _slots_kernel,
        grid_spec=pltpu.PrefetchScalarGridSpec(num_scalar_prefetch=1, grid=(1,),
                                               in_specs=[spec, spec], out_specs=spec),
        out_shape=jax.ShapeDtypeStruct(ids.shape, jnp.int32),
        compiler_params=_cparams("parallel"),
        name="slots",
    )(ps, ids, pos)


def _dispatch_kernel(ps_ref, slot_ref, h2t_ref, xs_hbm, zbuf, zsem, sem, *, tm, blk):
    @pl.when(pl.program_id(0) == 0)
    def _():
        zbuf[...] = jnp.zeros(zbuf.shape, zbuf.dtype)

        def block_at(row):
            return xs_hbm.at[pl.ds(pl.multiple_of(row * SUBLANES, blk * SUBLANES), blk * SUBLANES), :]

        def tail(e):
            end = ps_ref[e + 1]
            return end > ps_ref[e], pltpu.make_async_copy(zbuf, block_at(end - blk), zsem)

        def unused(e):
            row = ps_ref[N_EXPERTS] + e * blk
            return row * SUBLANES < xs_hbm.shape[0], pltpu.make_async_copy(zbuf, block_at(row), zsem)

        for fill in (tail, unused):
            for e in range(N_EXPERTS):
                needed, cp = fill(e)
                pl.when(needed)(cp.start)
        for fill in (tail, unused):
            for e in range(N_EXPERTS):
                needed, cp = fill(e)
                pl.when(needed)(cp.wait)

    def issue(t, carry):
        for k in range(TOP_K):
            pltpu.make_async_copy(_tile_row(h2t_ref, t * SUBLANES), _tile_row(xs_hbm, slot_ref[0, k, t]),
                                  sem).start(priority=k % 2)
        return carry

    lax.fori_loop(0, tm, issue, 0)
    for k in range(TOP_K):
        pltpu.make_async_copy(h2t_ref, xs_hbm.at[pl.ds(0, tm * SUBLANES), :], sem).wait()


def _dispatch(ps, slot, h2t, *, n_slots, tm, blk):
    lanes = h2t.shape[1]
    grid_spec = pltpu.PrefetchScalarGridSpec(
        num_scalar_prefetch=1,
        grid=(h2t.shape[0] // (tm * SUBLANES),),
        in_specs=[pl.BlockSpec((1, TOP_K, tm), lambda i, ps: (i, 0, 0), memory_space=pltpu.SMEM),
                  pl.BlockSpec((tm * SUBLANES, lanes), lambda i, ps: (i, 0))],
        out_specs=pl.BlockSpec(memory_space=pl.ANY),
        scratch_shapes=[pltpu.VMEM((blk * SUBLANES, lanes), jnp.uint32), pltpu.SemaphoreType.DMA(()),
                        pltpu.SemaphoreType.DMA(())],
    )
    return pl.pallas_call(
        functools.partial(_dispatch_kernel, tm=tm, blk=blk),
        grid_spec=grid_spec,
        out_shape=jax.ShapeDtypeStruct((n_slots * SUBLANES, lanes), jnp.uint32),
        compiler_params=_cparams("arbitrary"),
        name="dispatch",
    )(ps, slot, h2t)


def _experts_kernel(be_ref, nu_ref, x_ref, wg_ref, wu_ref, wd_ref, y_ref, wgb, wub, wdb):
    i = pl.program_id(0)
    used = i < nu_ref[0]
    new_expert = jnp.logical_or(i == 0, be_ref[i] != be_ref[jnp.maximum(i - 1, 0)])

    @pl.when(jnp.logical_and(used, new_expert))
    def _():
        wgb[...] = wg_ref[0].astype(BF16)
        wub[...] = wu_ref[0].astype(BF16)
        wdb[...] = wd_ref[0].astype(BF16)

    @pl.when(used)
    def _():
        xb = _load_tile_rows(x_ref, 0, x_ref.shape[0] // SUBLANES).astype(BF16)
        hb = (_silu(_dot(xb, wgb[...])) * _dot(xb, wub[...])).astype(BF16)
        _store_tile_rows(y_ref, _dot(hb, wdb[...]))

    @pl.when(jnp.logical_not(used))
    def _():
        y_ref[...] = jnp.zeros(y_ref.shape, y_ref.dtype)


def _experts(block_e, n_used, xs, wg, wu, wd, *, blk):
    n_blocks = block_e.shape[0]
    rows_blk = (blk * SUBLANES, xs.shape[1])
    wsel = lambda i, be, nu: (be[i], 0, 0)
    grid_spec = pltpu.PrefetchScalarGridSpec(
        num_scalar_prefetch=2,
        grid=(n_blocks,),
        in_specs=[pl.BlockSpec(rows_blk, lambda i, be, nu: (jnp.minimum(i, nu[0] - 1), 0)),
                  pl.BlockSpec((1,) + wg.shape[1:], wsel), pl.BlockSpec((1,) + wu.shape[1:], wsel),
                  pl.BlockSpec((1,) + wd.shape[1:], wsel)],
        out_specs=pl.BlockSpec(rows_blk, lambda i, be, nu: (i, 0)),
        scratch_shapes=[pltpu.VMEM(wg.shape[1:], BF16), pltpu.VMEM(wu.shape[1:], BF16),
                        pltpu.VMEM(wd.shape[1:], BF16)],
    )
    return pl.pallas_call(
        _experts_kernel,
        grid_spec=grid_spec,
        out_shape=jax.ShapeDtypeStruct(xs.shape, xs.dtype),
        compiler_params=_cparams("arbitrary"),
        name="experts",
    )(block_e, n_used, xs, wg, wu, wd)


_GROUP = 32


def _combine_kernel(slot_ref, nxt_ref, tw_ref, xs_ref, mod_ref, fn_ref, y_hbm, o_ref, gbuf0, gbuf1, wt_ref, sems, *, tm):
    gbuf = (gbuf0, gbuf1)
    j = pl.program_id(0)
    last = pl.num_programs(0) - 1
    d = xs_ref.shape[1]

    def issue(src_ref, bi, b, t):
        for k in range(TOP_K):
            pltpu.make_async_copy(_tile_row(y_hbm, src_ref[bi, k, t]), _tile_row(gbuf[b], (k * tm + t) * SUBLANES),
                                  sems.at[b]).start(priority=k % 2)

    def drain(b):
        pltpu.make_async_copy(y_hbm.at[pl.ds(0, gbuf[b].shape[0]), :], gbuf[b], sems.at[b]).wait()

    def reduce_rows(b, r0):
        w8 = wt_ref[b, pl.ds(r0, _GROUP), :]
        acc = [jnp.zeros((_GROUP, LANES), F32) for _ in range(d // LANES)]
        for k in range(TOP_K):
            wk = w8[:, k:k + 1]
            for c in range(SUBLANES):
                word = gbuf[b][pl.ds((k * tm + r0) * SUBLANES + c, _GROUP, stride=SUBLANES), :]
                acc[c] = acc[c] + wk * pltpu.bitcast(word << 16, F32)
                acc[SUBLANES + c] = acc[SUBLANES + c] + wk * pltpu.bitcast(word & jnp.uint32(0xFFFF0000), F32)
        rows = pl.ds(b * tm + r0, _GROUP)
        g2 = mod_ref[0][5:6]
        x2 = jnp.concatenate(acc, axis=1) * g2 + xs_ref[rows, :]
        o_ref[rows, :] = _rms(x2, fn_ref[...])

    def phase(fin_b, issue_args):
        def body(g, carry):
            r0 = pl.multiple_of(g * _GROUP, _GROUP)
            if issue_args is not None:
                for dt in range(_GROUP):
                    issue(*issue_args, r0 + dt)
            reduce_rows(fin_b, r0)
            return carry

        lax.fori_loop(0, tm // _GROUP, body, 0)

    @pl.when(j == 0)
    def _():
        lax.fori_loop(0, tm, lambda t, c: (issue(slot_ref, 0, 0, t), c)[1], 0)

    for b in range(2):
        wt_ref[b] = jnp.concatenate([tw_ref[b], jnp.zeros((LANES - TOP_K, tm), F32)], axis=0).T
    drain(0)
    phase(0, (slot_ref, 1, 1))
    drain(1)
    pl.when(j < last)(lambda: phase(1, (nxt_ref, 0, 0)))
    pl.when(j == last)(lambda: phase(1, None))


def _combine(slot, tw, xs, mod8, fn, y, *, seq, tm):
    n, d = xs.shape
    nb = slot.shape[0]
    assert nb % 2 == 0 and (seq // tm) % 2 == 0
    per_b = seq // (2 * tm)
    pair = lambda j: (j, 0, 0)
    return pl.pallas_call(
        functools.partial(_combine_kernel, tm=tm),
        grid=(nb // 2,),
        in_specs=[pl.BlockSpec((2, TOP_K, tm), pair, memory_space=pltpu.SMEM),
                  pl.BlockSpec((1, TOP_K, tm), lambda j: (jnp.minimum(2 * j + 2, nb - 1), 0, 0),
                               memory_space=pltpu.SMEM),
                  pl.BlockSpec((2, TOP_K, tm), pair),
                  pl.BlockSpec((2 * tm, d), lambda j: (j, 0)),
                  pl.BlockSpec((1, 8, d), lambda j: (j // per_b, 0, 0)),
                  pl.BlockSpec((1, d), lambda j: (0, 0)),
                  pl.BlockSpec(memory_space=pl.ANY)],
        out_specs=pl.BlockSpec((2 * tm, d), lambda j: (j, 0)),
        out_shape=jax.ShapeDtypeStruct((n, d), F32),
        scratch_shapes=[pltpu.VMEM((TOP_K * tm * SUBLANES, y.shape[1]), y.dtype),
                        pltpu.VMEM((TOP_K * tm * SUBLANES, y.shape[1]), y.dtype),
                        pltpu.VMEM((2, tm, LANES), F32), pltpu.SemaphoreType.DMA((2,))],
        compiler_params=_cparams("arbitrary"),
        name="combine",
    )(slot, slot, tw, xs, mod8, fn, y)


def _expert_runs(counts, n_blocks, blk):
    padded = (counts + blk - 1) // blk * blk
    pad_end = jnp.cumsum(padded)
    run_start = jnp.concatenate([pad_end - padded, pad_end[-1:]]).astype(jnp.int32)
    block_row = jnp.arange(n_blocks, dtype=jnp.int32) * blk
    block_e = jnp.minimum(jnp.sum(pad_end[None, :] <= block_row[:, None], axis=1), N_EXPERTS - 1).astype(jnp.int32)
    n_used = (pad_end[-1:] // blk).astype(jnp.int32)
    return run_start, block_e, n_used


def _pad_lanes(w, width):
    return jnp.pad(w, ((0, 0), (0, width - w.shape[1])))


def _tiles(seq):
    t = dict(tm_proj=256, tq=512, tt=256, tm_moe=512, blk=512)
    assert all(seq % v == 0 for k, v in t.items() if k != "blk")
    return t


def _layer(x, c, positions, ada_w, ada_b, norm1_w, w_in, q_a_norm_w, wq_b, kv_a_norm_w, wkv_b,
           gdn_conv_w, gdn_a_log, gdn_dt_bias, gdn_norm_w, w_out, norm2_w, router_w, router_bias,
           exp_w_gate, exp_w_up, exp_w_down, sh_w_gate, sh_w_up, sh_w_down, final_norm_w,
           *, tm_proj, tq, tt, tm_moe, blk):
    batch, seq, d = x.shape
    n = batch * seq
    x2 = x.reshape(n, d)

    c8 = jnp.zeros((8, d), F32).at[:batch].set(c)
    mod = _adaln(c8, ada_w, ada_b.reshape(1, -1))
    mod8 = jnp.pad(mod[:batch].reshape(batch, 6, d), ((0, 0), (0, 2), (0, 0)))

    hq = MLA_NOPE + MLA_ROPE
    w_cq, w_ckv, w_kpe, w_qkv, w_z, w_a, w_b = jnp.split(
        w_in, [_C_KV, _C_KV + MLA_KV_RANK, _C_KV + MLA_KV_RANK + MLA_ROPE,
               _C_KV + MLA_KV_RANK + MLA_ROPE + _GDN_QKV,
               _C_KV + MLA_KV_RANK + MLA_ROPE + _GDN_QKV + GDN_HEADS * GDN_DV,
               _C_KV + MLA_KV_RANK + MLA_ROPE + _GDN_QKV + GDN_HEADS * GDN_DV + GDN_HEADS], axis=1)
    win = jnp.concatenate([w_cq, w_ckv, _pad_lanes(w_kpe, LANES), w_qkv, w_z,
                           _pad_lanes(jnp.concatenate([w_a, w_b], axis=1), LANES)], axis=1).astype(BF16)
    wq = jnp.pad(wq_b.reshape(MLA_Q_RANK, MLA_HEADS, hq),
                 ((0, 0), (0, 0), (0, MLA_QK_PAD - hq))).reshape(MLA_Q_RANK, MLA_HEADS * MLA_QK_PAD).astype(BF16)
    wkv4 = wkv_b.reshape(MLA_KV_RANK, MLA_HEADS, MLA_NOPE + MLA_V)
    wkv = jnp.concatenate([wkv4[:, :, :MLA_NOPE].reshape(MLA_KV_RANK, -1),
                           wkv4[:, :, MLA_NOPE:].reshape(MLA_KV_RANK, -1)], axis=1).astype(BF16)
    inv_freq = 1.0 / (ROPE_THETA ** (jnp.arange(0, MLA_ROPE, 2, dtype=F32) / MLA_ROPE))
    invf = _pad_lanes(jnp.concatenate([inv_freq, inv_freq])[None, :], LANES)

    q, k, v, qkv, z, ab = _proj(x2, mod8, positions.reshape(n, 1), norm1_w.reshape(1, d), win,
                                q_a_norm_w.reshape(1, -1), wq, kv_a_norm_w.reshape(1, -1), wkv, invf,
                                seq=seq, tm=tm_proj)
    out_a = _attn(q, k, v, batch=batch, seq=seq, tq=tq)

    cw8 = jnp.pad(gdn_conv_w, ((0, 8 - GDN_CONV), (0, 0)))
    hp = _pad_lanes(jnp.stack([gdn_a_log, gdn_dt_bias]), LANES)
    hp = jnp.pad(hp, ((0, 6), (0, 0)))
    out_b = _gdn(qkv, z, ab, cw8, hp, gdn_norm_w.reshape(1, -1), batch=batch, seq=seq, tt=tt)

    ha = MLA_HEADS * MLA_V
    xs, h2t, ids, tw, pos, counts = _mix(
        out_a, out_b, x2, mod8, w_out[:ha].astype(BF16), w_out[ha:].astype(BF16), norm2_w.reshape(1, d),
        router_w.T, router_bias.reshape(-1, 1), sh_w_gate.astype(BF16), sh_w_up.astype(BF16),
        sh_w_down.astype(BF16), seq=seq, tm=tm_moe)

    n_blocks = n * TOP_K // blk + N_EXPERTS
    run_start, block_e, n_used = _expert_runs(counts[:, 0].astype(jnp.int32), n_blocks, blk)
    slot = _slots(run_start, ids, pos)
    xsort = _dispatch(run_start, slot, h2t, n_slots=n_blocks * blk, tm=tm_moe, blk=blk)
    y = _experts(block_e, n_used, xsort, exp_w_gate, exp_w_up, exp_w_down, blk=blk)
    out = _combine(slot, tw, xs, mod8, final_norm_w.reshape(1, d), y, seq=seq, tm=tm_moe)
    return out.reshape(batch, seq, d)


def kernel(x, c, positions, ada_w, ada_b, norm1_w, w_in, q_a_norm_w, wq_b, kv_a_norm_w, wkv_b, gdn_conv_w,
           gdn_a_log, gdn_dt_bias, gdn_norm_w, w_out, norm2_w, router_w, router_bias, exp_w_gate, exp_w_up,
           exp_w_down, sh_w_gate, sh_w_up, sh_w_down, final_norm_w):
    assert ada_w.shape[0] == 1, "single layer"
    return _layer(
        x, c, positions, ada_w[0], ada_b[0], norm1_w[0], w_in[0], q_a_norm_w[0], wq_b[0], kv_a_norm_w[0],
        wkv_b[0], gdn_conv_w[0], gdn_a_log[0], gdn_dt_bias[0], gdn_norm_w[0], w_out[0], norm2_w[0],
        router_w[0], router_bias[0], exp_w_gate[0], exp_w_up[0], exp_w_down[0], sh_w_gate[0], sh_w_up[0],
        sh_w_down[0], final_norm_w, **_tiles(x.shape[1]))
```

```python
import functools

import jax
import jax.numpy as jnp
from jax import lax
from jax.experimental import pallas as pl
from jax.experimental.pallas import tpu as pltpu

F32 = jnp.float32
BF16 = jnp.bfloat16

CHUNK = 64
NORM_EPS = 1e-6
MLA_HEADS = 4
MLA_Q_RANK = 384
MLA_KV_RANK = 256
MLA_NOPE = 128
MLA_ROPE = 64
MLA_V = 128
ROPE_THETA = 10000.0
GDN_HEADS = 4
GDN_DK = 128
GDN_DV = 128
GDN_CONV = 4
N_EXPERTS = 64
N_GROUPS = 8
TOPK_GROUPS = 4
TOP_K = 8
ROUTED_SCALE = 2.5
LOG2E = 1.4426950408889634

LANES = 128
MLA_QK_PAD = 256
VMEM_LIMIT = 56 * 1024 * 1024


def _cparams(*sem):
    return pltpu.CompilerParams(dimension_semantics=sem, vmem_limit_bytes=VMEM_LIMIT)


def _dot(a, b):
    return jnp.dot(a, b, preferred_element_type=F32)


def _split3(a):
    hi = a.astype(BF16)
    r = a - hi.astype(F32)
    mid = r.astype(BF16)
    return hi, mid, (r - mid.astype(F32)).astype(BF16)


def _dot_x3(a, b):
    ah, am, _ = a
    bh, bm, _ = b
    return _dot(ah, bh) + (_dot(ah, bm) + _dot(am, bh))


def _dot_nt(a, b):
    return lax.dot_general(a, b, (((1,), (1,)), ((), ())), preferred_element_type=F32)


def _dot_tn(a, b):
    return lax.dot_general(a, b, (((0,), (0,)), ((), ())), preferred_element_type=F32)


def _silu(x):
    return x * jax.nn.sigmoid(x)


def _rms(x, w):
    return x * lax.rsqrt(jnp.mean(x * x, axis=-1, keepdims=True) + NORM_EPS) * w


def _adaln_kernel(c_ref, w_ref, b_ref, o_ref):
    a = _silu(c_ref[...]).astype(BF16)
    o_ref[...] = _dot(a, w_ref[...].astype(BF16)) + b_ref[...]


def _adaln(c8, ada_w, ada_b):
    d = c8.shape[1]
    n_out = ada_w.shape[1]
    return pl.pallas_call(
        _adaln_kernel,
        grid=(n_out // d,),
        in_specs=[pl.BlockSpec((8, d), lambda j: (0, 0)),
                  pl.BlockSpec((d, d), lambda j: (0, j)),
                  pl.BlockSpec((1, d), lambda j: (0, j))],
        out_specs=pl.BlockSpec((8, d), lambda j: (0, j)),
        out_shape=jax.ShapeDtypeStruct((8, n_out), F32),
        compiler_params=_cparams("parallel"),
        name="adaln",
    )(c8, ada_w, ada_b)


_C_Q = 0
_C_KV = _C_Q + MLA_Q_RANK
_C_KPE = _C_KV + MLA_KV_RANK
_C_QKV = _C_KPE + LANES
_GDN_QKV = GDN_HEADS * (2 * GDN_DK + GDN_DV)
_C_Z = _C_QKV + _GDN_QKV
_C_AB = _C_Z + GDN_HEADS * GDN_DV
_D_IN_PAD = _C_AB + LANES


def _rope(xb, cos, sin):
    half = MLA_ROPE // 2
    lane = lax.broadcasted_iota(jnp.int32, xb.shape, 1)
    rot = jnp.where(lane < half, -pltpu.roll(xb, LANES - half, 1), pltpu.roll(xb, half, 1))
    return xb * cos + rot * sin


def _proj_kernel(x_ref, mod_ref, pos_ref, n1_ref, win_ref, qan_ref, wq_ref, kvan_ref, wkv_ref, invf_ref,
                 q_ref, k_ref, v_ref, qkv_ref, z_ref, ab_ref):
    x = x_ref[...]
    mod = mod_ref[0]
    sh1, sc1 = mod[0:1], mod[1:2]
    h = (_rms(x, n1_ref[...]) * (1.0 + sc1) + sh1).astype(BF16)
    proj = _dot(h, win_ref[...])

    ang = pos_ref[...].astype(F32) * invf_ref[...]
    cos, sin = jnp.cos(ang), jnp.sin(ang)

    cq = proj[:, _C_Q:_C_Q + MLA_Q_RANK]
    qn = _rms(cq, qan_ref[...]).astype(BF16)
    q = _dot(qn, wq_ref[...]) * ((MLA_NOPE + MLA_ROPE) ** -0.5 * LOG2E)
    for hh in range(MLA_HEADS):
        c0 = hh * MLA_QK_PAD
        q_ref[:, c0:c0 + MLA_NOPE] = q[:, c0:c0 + MLA_NOPE].astype(BF16)
        q_ref[:, c0 + MLA_NOPE:c0 + MLA_QK_PAD] = _rope(q[:, c0 + MLA_NOPE:c0 + MLA_QK_PAD], cos, sin).astype(BF16)

    ckv = proj[:, _C_KV:_C_KV + MLA_KV_RANK]
    kvn = _rms(ckv, kvan_ref[...]).astype(BF16)
    kv = _dot(kvn, wkv_ref[...])
    kpe = _rope(proj[:, _C_KPE:_C_KPE + LANES], cos, sin).astype(BF16)
    for hh in range(MLA_HEADS):
        c0 = hh * MLA_QK_PAD
        k_ref[:, c0:c0 + MLA_NOPE] = kv[:, hh * MLA_NOPE:(hh + 1) * MLA_NOPE].astype(BF16)
        k_ref[:, c0 + MLA_NOPE:c0 + MLA_QK_PAD] = kpe
    v_ref[...] = kv[:, MLA_HEADS * MLA_NOPE:].astype(BF16)

    qkv_ref[...] = proj[:, _C_QKV:_C_QKV + _GDN_QKV].astype(qkv_ref.dtype)
    z_ref[...] = proj[:, _C_Z:_C_Z + GDN_HEADS * GDN_DV].astype(z_ref.dtype)
    ab_ref[...] = proj[:, _C_AB:_C_AB + LANES]


def _proj(x2, mod8, pos2, n1, win, qan, wq, kvan, wkv, invf, *, seq, tm):
    n, d = x2.shape
    per_b = seq // tm
    row = lambda i: (i, 0)
    const = lambda i: (0, 0)
    hq = MLA_HEADS * MLA_QK_PAD
    hv = MLA_HEADS * MLA_V
    return pl.pallas_call(
        _proj_kernel,
        grid=(n // tm,),
        in_specs=[pl.BlockSpec((tm, d), row),
                  pl.BlockSpec((1, 8, d), lambda i: (i // per_b, 0, 0)),
                  pl.BlockSpec((tm, 1), row),
                  pl.BlockSpec((1, d), const),
                  pl.BlockSpec(win.shape, const),
                  pl.BlockSpec(qan.shape, const),
                  pl.BlockSpec(wq.shape, const),
                  pl.BlockSpec(kvan.shape, const),
                  pl.BlockSpec(wkv.shape, const),
                  pl.BlockSpec(invf.shape, const)],
        out_specs=[pl.BlockSpec((tm, hq), row), pl.BlockSpec((tm, hq), row), pl.BlockSpec((tm, hv), row),
                   pl.BlockSpec((tm, _GDN_QKV), row), pl.BlockSpec((tm, GDN_HEADS * GDN_DV), row),
                   pl.BlockSpec((tm, LANES), row)],
        out_shape=[jax.ShapeDtypeStruct((n, hq), BF16), jax.ShapeDtypeStruct((n, hq), BF16),
                   jax.ShapeDtypeStruct((n, hv), BF16), jax.ShapeDtypeStruct((n, _GDN_QKV), BF16),
                   jax.ShapeDtypeStruct((n, GDN_HEADS * GDN_DV), BF16), jax.ShapeDtypeStruct((n, LANES), F32)],
        compiler_params=_cparams("parallel"),
        name="proj",
    )(x2, mod8, pos2, n1, win, qan, wq, kvan, wkv, invf)


_Q_TILES = 2


def _attn_kernel(q_ref, k_ref, v_ref, o_ref, *, tq):
    i = pl.program_id(1)
    chains = [(qt, hh) for qt in range(_Q_TILES) for hh in range(MLA_HEADS)]

    ones_col = jnp.where(lax.broadcasted_iota(jnp.int32, (tq, MLA_V), 1) == 0, 1.0, 0.0).astype(BF16)

    def chain_step(qt, hh, r0, carry, mask):
        m, acc = carry
        s = _dot_nt(q_ref[qt * tq:(qt + 1) * tq, hh * MLA_QK_PAD:(hh + 1) * MLA_QK_PAD],
                    k_ref[pl.ds(r0, tq), hh * MLA_QK_PAD:(hh + 1) * MLA_QK_PAD])
        if mask is not None:
            s = jnp.where(mask, s, -jnp.inf)
        m_new = jnp.maximum(m, jnp.max(s, axis=-1, keepdims=True))
        alpha = jnp.exp2(m - m_new)
        p = jnp.exp2((s - m_new).astype(BF16))
        v_ext = jnp.concatenate([v_ref[pl.ds(r0, tq), hh * MLA_V:(hh + 1) * MLA_V], ones_col], axis=1)
        return m_new, alpha * acc + _dot(p, v_ext)

    def block(r0, carry, masks):
        return tuple(carry[c] if masks[qt] is False else chain_step(qt, hh, r0, carry[c], masks[qt])
                     for c, (qt, hh) in enumerate(chains))

    init = tuple((jnp.full((tq, 1), -jnp.inf, F32), jnp.zeros((tq, 2 * MLA_V), F32)) for _ in chains)
    carry = lax.fori_loop(0, _Q_TILES * i, lambda j, c: block(pl.multiple_of(j * tq, tq), c, (None, None)), init)
    rq = lax.broadcasted_iota(jnp.int32, (tq, tq), 0) // CHUNK
    ck = lax.broadcasted_iota(jnp.int32, (tq, tq), 1) // CHUNK
    diag = ck <= rq
    carry = block(pl.multiple_of(_Q_TILES * i * tq, tq), carry, (diag, None))
    carry = block(pl.multiple_of((_Q_TILES * i + 1) * tq, tq), carry, (False, diag))
    for c, (qt, hh) in enumerate(chains):
        _, acc = carry[c]
        o_ref[qt * tq:(qt + 1) * tq, hh * MLA_V:(hh + 1) * MLA_V] = (
            acc[:, :MLA_V] / acc[:, MLA_V:MLA_V + 1]).astype(o_ref.dtype)


def _attn(q, k, v, *, batch, seq, tq):
    n = q.shape[0]
    rows = _Q_TILES * tq
    nq = seq // rows
    return pl.pallas_call(
        functools.partial(_attn_kernel, tq=tq),
        grid=(batch, nq),
        in_specs=[pl.BlockSpec((rows, q.shape[1]), lambda b, i: (b * nq + i, 0)),
                  pl.BlockSpec((seq, k.shape[1]), lambda b, i: (b, 0)),
                  pl.BlockSpec((seq, v.shape[1]), lambda b, i: (b, 0))],
        out_specs=pl.BlockSpec((rows, v.shape[1]), lambda b, i: (b * nq + i, 0)),
        out_shape=jax.ShapeDtypeStruct((n, v.shape[1]), BF16),
        compiler_params=_cparams("parallel", "arbitrary"),
        name="attn",
    )(q, k, v)


_SUPER = 2 * CHUNK


def _unit_lower_inverses(lows):
    n = lows[0].shape[0]
    eye = (lax.broadcasted_iota(jnp.int32, (n, n), 0) == lax.broadcasted_iota(jnp.int32, (n, n), 1)).astype(F32)
    levels = CHUNK.bit_length() - 1
    ps = [-low for low in lows]
    ts = [eye + p for p in ps]
    for level in range(1, levels):
        parts = [_split3(p) for p in ps]
        ps = [_dot_x3(s, s) for s in parts]
        if level == 1:
            continue
        ts = [t + _dot_x3(s, _split3(t)) for t, s in zip(ts, parts)]
    return [t + _dot_x3(_split3(p), _split3(t)) for t, p in zip(ts, ps)]


def _gdn_kernel(qkv_ref, z_ref, ab_ref, cw_ref, hp_ref, nw_ref, o_ref, xe_ref, st_ref, *, tt):
    j = pl.program_id(1)
    hd = GDN_HEADS * GDN_DK

    @pl.when(j == 0)
    def _():
        xe_ref[0:8, :] = jnp.zeros((8, xe_ref.shape[1]), F32)
        st_ref[...] = jnp.zeros(st_ref.shape, F32)

    xe_ref[8:8 + tt, :] = qkv_ref[...].astype(F32)
    cw = cw_ref[...]
    y = xe_ref[8:8 + tt, :] * cw[GDN_CONV - 1:GDN_CONV]
    for i in range(1, GDN_CONV):
        y = y + xe_ref[8 - i:8 - i + tt, :] * cw[GDN_CONV - 1 - i:GDN_CONV - i]
    xe_ref[0:8, :] = xe_ref[tt:tt + 8, :]
    act = _silu(y)

    ab = ab_ref[...]
    hp = hp_ref[...]
    pre = ab + hp[1:2]
    softplus = jnp.maximum(pre, 0.0) + jnp.log1p(jnp.exp(-jnp.abs(pre)))
    g_all = -jnp.exp(hp[0:1]) * softplus
    beta_all = jax.nn.sigmoid(ab)

    ri = lax.broadcasted_iota(jnp.int32, (tt, tt), 0)
    ci = lax.broadcasted_iota(jnp.int32, (tt, tt), 1)
    tri = ((ri // CHUNK == ci // CHUNK) & (ci <= ri)).astype(BF16)
    gc = sum(_dot(tri, part) for part in _split3(g_all))
    gct = gc.T

    rs = lax.broadcasted_iota(jnp.int32, (_SUPER, _SUPER), 0)
    cs = lax.broadcasted_iota(jnp.int32, (_SUPER, _SUPER), 1)
    same = rs // CHUNK == cs // CHUNK
    incl = same & (cs <= rs)
    strict = same & (cs < rs)

    tiles = [(s * _SUPER, hh) for s in range(tt // _SUPER) for hh in range(GDN_HEADS)]
    pre_t = []
    for r0, hh in tiles:
        qh = act[r0:r0 + _SUPER, hh * GDN_DK:(hh + 1) * GDN_DK]
        kh = act[r0:r0 + _SUPER, hd + hh * GDN_DK:hd + (hh + 1) * GDN_DK]
        vh = act[r0:r0 + _SUPER, 2 * hd + hh * GDN_DV:2 * hd + (hh + 1) * GDN_DV]
        qh = qh * lax.rsqrt(jnp.sum(qh * qh, axis=-1, keepdims=True) + 1e-6) * (GDN_DK ** -0.5)
        kh = kh * lax.rsqrt(jnp.sum(kh * kh, axis=-1, keepdims=True) + 1e-6)
        beta = beta_all[r0:r0 + _SUPER, GDN_HEADS + hh:GDN_HEADS + hh + 1]
        gcol = gc[r0:r0 + _SUPER, hh:hh + 1]
        grow = gct[hh:hh + 1, r0:r0 + _SUPER]
        decay = jnp.exp(jnp.where(incl, gcol - grow, -jnp.inf))
        eg = jnp.exp(gcol)
        kb = kh * beta
        pre_t.append(dict(kh=kh, khb=kh.astype(BF16), kbb=kb.astype(BF16), vbb=(vh * beta).astype(BF16),
                          kgb=(kb * eg).astype(BF16), qb=qh.astype(BF16), qg=qh * eg, gcol=gcol, decay=decay))
    lows = [jnp.where(strict, _dot_nt(p["kbb"], p["khb"]) * p["decay"], 0.0) for p in pre_t]
    tinvs = [t.astype(BF16) for t in _unit_lower_inverses(lows)]
    us = [_dot(t, p["vbb"]).astype(BF16) for t, p in zip(tinvs, pre_t)]
    ws = [_dot(t, p["kgb"]).astype(BF16) for t, p in zip(tinvs, pre_t)]
    atts = [(_dot_nt(p["qb"], p["khb"]) * p["decay"]).astype(BF16) for p in pre_t]

    steps = []
    for cc in range(tt // CHUNK):
        r0 = cc * CHUNK
        a0 = r0 % _SUPER
        for hh in range(GDN_HEADS):
            ti = (r0 // _SUPER) * GDN_HEADS + hh
            p = pre_t[ti]
            gl = gc[r0 + CHUNK - 1:r0 + CHUNK, hh:hh + 1]
            kd = (p["kh"][a0:a0 + CHUNK] * jnp.exp(gl - p["gcol"][a0:a0 + CHUNK])).astype(BF16)
            u_c, w_c = us[ti][a0:a0 + CHUNK], ws[ti][a0:a0 + CHUNK]
            att_c = atts[ti][a0:a0 + CHUNK, a0:a0 + CHUNK]
            steps.append(dict(hh=hh, r0=r0, gain=jnp.exp(gl), kw=_dot_tn(kd, w_c).astype(BF16), ku=_dot_tn(kd, u_c),
                              qs=(p["qg"][a0:a0 + CHUNK] - _dot(att_c, w_c)).astype(BF16), ou=_dot(att_c, u_c)))

    states = [st_ref[hh] for hh in range(GDN_HEADS)]
    for st in steps:
        hh, r0 = st["hh"], st["r0"]
        sb = states[hh].astype(BF16)
        o = _dot(st["qs"], sb) + st["ou"]
        states[hh] = states[hh] * st["gain"] + (st["ku"] - _dot(st["kw"], sb))
        zz = z_ref[r0:r0 + CHUNK, hh * GDN_DV:(hh + 1) * GDN_DV].astype(F32)
        o_ref[r0:r0 + CHUNK, hh * GDN_DV:(hh + 1) * GDN_DV] = (_rms(o, nw_ref[...]) * _silu(zz)).astype(o_ref.dtype)
    for hh in range(GDN_HEADS):
        st_ref[hh] = states[hh]


def _gdn(qkv, z, ab, cw8, hp, nw, *, batch, seq, tt):
    n = qkv.shape[0]
    per_b = seq // tt
    row = lambda b, j: (b * per_b + j, 0)
    const = lambda b, j: (0, 0)
    hv = GDN_HEADS * GDN_DV
    return pl.pallas_call(
        functools.partial(_gdn_kernel, tt=tt),
        grid=(batch, per_b),
        in_specs=[pl.BlockSpec((tt, _GDN_QKV), row), pl.BlockSpec((tt, hv), row), pl.BlockSpec((tt, LANES), row),
                  pl.BlockSpec(cw8.shape, const), pl.BlockSpec(hp.shape, const), pl.BlockSpec(nw.shape, const)],
        out_specs=pl.BlockSpec((tt, hv), row),
        out_shape=jax.ShapeDtypeStruct((n, hv), BF16),
        scratch_shapes=[pltpu.VMEM((tt + 8, _GDN_QKV), F32), pltpu.VMEM((GDN_HEADS, GDN_DK, GDN_DV), F32)],
        compiler_params=_cparams("parallel", "arbitrary"),
        name="gdn",
    )(qkv, z, ab, cw8, hp, nw)


def _route(logits_t, bias_col):
    e, tm = logits_t.shape
    gsz = e // N_GROUPS
    scores = jax.nn.sigmoid(logits_t)
    biased = scores + bias_col
    sub = lax.broadcasted_iota(jnp.int32, (gsz, tm), 0)
    rows = []
    for g in range(N_GROUPS):
        blk = biased[g * gsz:(g + 1) * gsz]
        m1 = jnp.max(blk, axis=0, keepdims=True)
        first = jnp.min(jnp.where(blk == m1, sub, gsz), axis=0, keepdims=True)
        m2 = jnp.max(jnp.where(sub == first, -jnp.inf, blk), axis=0, keepdims=True)
        rows.append(m1 + m2)
    gs = jnp.concatenate(rows, axis=0)
    gi = lax.broadcasted_iota(jnp.int32, (N_GROUPS, tm), 0)
    grank = jnp.zeros((N_GROUPS, tm), F32)
    for g in range(N_GROUPS):
        r = gs[g:g + 1]
        grank = grank + jnp.where(r > gs, 1.0, jnp.where(r == gs, (gi > g).astype(F32), 0.0))
    gsel = grank < TOPK_GROUPS
    masked = jnp.concatenate(
        [jnp.where(gsel[g:g + 1], biased[g * gsz:(g + 1) * gsz], -jnp.inf) for g in range(N_GROUPS)], axis=0)
    ei = lax.broadcasted_iota(jnp.int32, (e, tm), 0)
    rank = jnp.zeros((e, tm), F32)
    for k in range(e):
        r = masked[k:k + 1]
        rank = rank + jnp.where(r > masked, 1.0, jnp.where(r == masked, (ei > k).astype(F32), 0.0))
    return scores, rank


SUBLANES = 4


def _store_tile_rows(ref, x):
    c = x.shape[1] // 2
    lo = pltpu.bitcast(x[:, :c].astype(BF16).astype(F32), jnp.uint32)
    hi = pltpu.bitcast(x[:, c:].astype(BF16).astype(F32), jnp.uint32)
    w = (hi & jnp.uint32(0xFFFF0000)) | (lo >> 16)
    for j in range(SUBLANES):
        ref[pl.ds(j, x.shape[0], stride=SUBLANES), :] = w[:, j * LANES:(j + 1) * LANES]


def _load_tile_rows(ref, r0, rows):
    w = jnp.concatenate(
        [ref[pl.ds(r0 * SUBLANES + j, rows, stride=SUBLANES), :] for j in range(SUBLANES)], axis=1)
    return jnp.concatenate([pltpu.bitcast(w << 16, F32), pltpu.bitcast(w & jnp.uint32(0xFFFF0000), F32)], axis=1)


def _tile_row(ref, sublane_offset):
    return ref.at[pl.ds(pl.multiple_of(sublane_offset, SUBLANES), SUBLANES), :]


def _mix_kernel(oa_ref, ob_ref, x_ref, mod_ref, woa_ref, wob_ref, n2_ref, rwt_ref, rb_ref, sg_ref, su_ref, sd_ref,
                xs_ref, h2t_ref, te_ref, tw_ref, tp_ref, cnt_ref, run_ref):
    @pl.when(pl.program_id(0) == 0)
    def _():
        run_ref[...] = jnp.zeros(run_ref.shape, F32)

    mod = mod_ref[0]
    g1, sh2, sc2, g2 = mod[2:3], mod[3:4], mod[4:5], mod[5:6]
    mix = _dot(oa_ref[...], woa_ref[...]) + _dot(ob_ref[...], wob_ref[...])
    x1 = x_ref[...] + g1 * mix
    h2 = _rms(x1, n2_ref[...]) * (1.0 + sc2) + sh2
    _store_tile_rows(h2t_ref, h2)
    h2b = h2.astype(BF16)
    hs = (_silu(_dot(h2b, sg_ref[...])) * _dot(h2b, su_ref[...])).astype(BF16)
    xs_ref[...] = x1 + g2 * _dot(hs, sd_ref[...])

    rh, rm, _ = _split3(rwt_ref[...])
    hh, hm, _ = _split3(h2)
    scores, rank = _route(_dot_nt(rh, hh) + (_dot_nt(rh, hm) + _dot_nt(rm, hh)), rb_ref[...])
    e, tm = rank.shape
    sel = jnp.where(rank < TOP_K, 1.0, 0.0)
    earlier = (lax.broadcasted_iota(jnp.int32, (tm, tm), 0) < lax.broadcasted_iota(jnp.int32, (tm, tm), 1))
    posmat = _dot(sel.astype(BF16), earlier.astype(BF16)) + run_ref[...]
    eif = lax.broadcasted_iota(jnp.int32, (e, tm), 0).astype(F32)
    ids, wts, pos = [], [], []
    for k in range(TOP_K):
        hit = rank == float(k)
        ids.append(jnp.sum(jnp.where(hit, eif, 0.0), axis=0, keepdims=True))
        wts.append(jnp.sum(jnp.where(hit, scores, 0.0), axis=0, keepdims=True))
        pos.append(jnp.sum(jnp.where(hit, posmat, 0.0), axis=0, keepdims=True))
    wts = jnp.concatenate(wts, axis=0)
    te_ref[0] = jnp.concatenate(ids, axis=0).astype(jnp.int32)
    tw_ref[0] = wts / jnp.sum(wts, axis=0, keepdims=True) * ROUTED_SCALE
    tp_ref[0] = jnp.concatenate(pos, axis=0).astype(jnp.int32)
    run_ref[...] = run_ref[...] + jnp.sum(sel, axis=1, keepdims=True)
    cnt_ref[...] = jnp.broadcast_to(run_ref[...], cnt_ref.shape)


def _mix(oa, ob, x2, mod8, woa, wob, n2, rwt, rb, sg, su, sd, *, seq, tm):
    n, d = x2.shape
    per_b = seq // tm
    nb = n // tm
    row = lambda i: (i, 0)
    blk3 = lambda i: (i, 0, 0)
    const = lambda i: (0, 0)
    return pl.pallas_call(
        _mix_kernel,
        grid=(nb,),
        in_specs=[pl.BlockSpec((tm, oa.shape[1]), row), pl.BlockSpec((tm, ob.shape[1]), row),
                  pl.BlockSpec((tm, d), row), pl.BlockSpec((1, 8, d), lambda i: (i // per_b, 0, 0)),
                  pl.BlockSpec(woa.shape, const), pl.BlockSpec(wob.shape, const), pl.BlockSpec((1, d), const),
                  pl.BlockSpec(rwt.shape, const), pl.BlockSpec(rb.shape, const),
                  pl.BlockSpec(sg.shape, const), pl.BlockSpec(su.shape, const), pl.BlockSpec(sd.shape, const)],
        out_specs=[pl.BlockSpec((tm, d), row), pl.BlockSpec((tm * SUBLANES, LANES), row),
                   pl.BlockSpec((1, TOP_K, tm), blk3), pl.BlockSpec((1, TOP_K, tm), blk3),
                   pl.BlockSpec((1, TOP_K, tm), blk3), pl.BlockSpec((N_EXPERTS, LANES), const)],
        out_shape=[jax.ShapeDtypeStruct((n, d), F32), jax.ShapeDtypeStruct((n * SUBLANES, LANES), jnp.uint32),
                   jax.ShapeDtypeStruct((nb, TOP_K, tm), jnp.int32), jax.ShapeDtypeStruct((nb, TOP_K, tm), F32),
                   jax.ShapeDtypeStruct((nb, TOP_K, tm), jnp.int32),
                   jax.ShapeDtypeStruct((N_EXPERTS, LANES), F32)],
        scratch_shapes=[pltpu.VMEM((N_EXPERTS, 1), F32)],
        compiler_params=_cparams("arbitrary"),
        name="mix",
    )(oa, ob, x2, mod8, woa, wob, n2, rwt, rb, sg, su, sd)


def _slots_kernel(ps_ref, ids_ref, pos_ref, o_ref):
    ids = ids_ref[...]
    start = jnp.zeros(ids.shape, jnp.int32)
    for e in range(N_EXPERTS):
        start = jnp.where(ids == e, ps_ref[e], start)
    o_ref[...] = (start + pos_ref[...]) * SUBLANES


def _slots(ps, ids, pos):
    spec = pl.BlockSpec(ids.shape, lambda i, ps: (0, 0, 0))
    return pl.pallas_call(
        _slots_kernel,
        grid_spec=pltpu.PrefetchScalarGridSpec(num_scalar_prefetch=1, grid=(1,),
                                               in_specs=[spec, spec], out_specs=spec),
        out_shape=jax.ShapeDtypeStruct(ids.shape, jnp.int32),
        compiler_params=_cparams("parallel"),
        name="slots",
    )(ps, ids, pos)


def _dispatch_kernel(ps_ref, slot_ref, h2t_ref, xs_hbm, zbuf, ring, zsem, sems, *, tm, blk, n_steps):
    i = pl.program_id(0)

    @pl.when(i == 0)
    def _():
        zbuf[...] = jnp.zeros(zbuf.shape, zbuf.dtype)

        def block_at(row):
            return xs_hbm.at[pl.ds(pl.multiple_of(row * SUBLANES, blk * SUBLANES), blk * SUBLANES), :]

        def tail(e):
            end = ps_ref[e + 1]
            return end > ps_ref[e], pltpu.make_async_copy(zbuf, block_at(end - blk), zsem)

        def unused(e):
            row = ps_ref[N_EXPERTS] + e * blk
            return row * SUBLANES < xs_hbm.shape[0], pltpu.make_async_copy(zbuf, block_at(row), zsem)

        for fill in (tail, unused):
            for e in range(N_EXPERTS):
                needed, cp = fill(e)
                pl.when(needed)(cp.start)
        for fill in (tail, unused):
            for e in range(N_EXPERTS):
                needed, cp = fill(e)
                pl.when(needed)(cp.wait)

    def send(b):
        ring[b] = h2t_ref[...]

        def issue(t, carry):
            for k in range(TOP_K):
                pltpu.make_async_copy(_tile_row(ring.at[b], t * SUBLANES), _tile_row(xs_hbm, slot_ref[0, k, t]),
                                      sems.at[b]).start(priority=k % 2)
            return carry

        lax.fori_loop(0, tm, issue, 0)

    def drain(b):
        for k in range(TOP_K):
            pltpu.make_async_copy(ring.at[b], xs_hbm.at[pl.ds(0, tm * SUBLANES), :], sems.at[b]).wait()

    for b in range(2):
        @pl.when(i % 2 == b)
        def _(b=b):
            send(b)

            @pl.when(i > 0)
            def _():
                drain(1 - b)

    @pl.when(i == n_steps - 1)
    def _():
        drain((n_steps - 1) % 2)


def _dispatch(ps, slot, h2t, *, n_slots, tm, blk):
    lanes = h2t.shape[1]
    n_steps = h2t.shape[0] // (tm * SUBLANES)
    grid_spec = pltpu.PrefetchScalarGridSpec(
        num_scalar_prefetch=1,
        grid=(n_steps,),
        in_specs=[pl.BlockSpec((1, TOP_K, tm), lambda i, ps: (i, 0, 0), memory_space=pltpu.SMEM),
                  pl.BlockSpec((tm * SUBLANES, lanes), lambda i, ps: (i, 0))],
        out_specs=pl.BlockSpec(memory_space=pl.ANY),
        scratch_shapes=[pltpu.VMEM((blk * SUBLANES, lanes), jnp.uint32),
                        pltpu.VMEM((2, tm * SUBLANES, lanes), jnp.uint32),
                        pltpu.SemaphoreType.DMA(()), pltpu.SemaphoreType.DMA((2,))],
    )
    return pl.pallas_call(
        functools.partial(_dispatch_kernel, tm=tm, blk=blk, n_steps=n_steps),
        grid_spec=grid_spec,
        out_shape=jax.ShapeDtypeStruct((n_slots * SUBLANES, lanes), jnp.uint32),
        compiler_params=_cparams("arbitrary"),
        name="dispatch",
    )(ps, slot, h2t)


def _experts_kernel(be_ref, nu_ref, x_ref, wg_ref, wu_ref, wd_ref, y_ref, wgb, wub, wdb):
    i = pl.program_id(0)
    used = i < nu_ref[0]
    new_expert = jnp.logical_or(i == 0, be_ref[i] != be_ref[jnp.maximum(i - 1, 0)])

    @pl.when(jnp.logical_and(used, new_expert))
    def _():
        wgb[...] = wg_ref[0].astype(BF16)
        wub[...] = wu_ref[0].astype(BF16)
        wdb[...] = wd_ref[0].astype(BF16)

    @pl.when(used)
    def _():
        xb = _load_tile_rows(x_ref, 0, x_ref.shape[0] // SUBLANES).astype(BF16)
        hb = (_silu(_dot(xb, wgb[...])) * _dot(xb, wub[...])).astype(BF16)
        _store_tile_rows(y_ref, _dot(hb, wdb[...]))

    @pl.when(jnp.logical_not(used))
    def _():
        y_ref[...] = jnp.zeros(y_ref.shape, y_ref.dtype)


def _experts(block_e, n_used, xs, wg, wu, wd, *, blk):
    n_blocks = block_e.shape[0]
    rows_blk = (blk * SUBLANES, xs.shape[1])
    wsel = lambda i, be, nu: (be[i], 0, 0)
    grid_spec = pltpu.PrefetchScalarGridSpec(
        num_scalar_prefetch=2,
        grid=(n_blocks,),
        in_specs=[pl.BlockSpec(rows_blk, lambda i, be, nu: (jnp.minimum(i, nu[0] - 1), 0)),
                  pl.BlockSpec((1,) + wg.shape[1:], wsel), pl.BlockSpec((1,) + wu.shape[1:], wsel),
                  pl.BlockSpec((1,) + wd.shape[1:], wsel)],
        out_specs=pl.BlockSpec(rows_blk, lambda i, be, nu: (i, 0)),
        scratch_shapes=[pltpu.VMEM(wg.shape[1:], BF16), pltpu.VMEM(wu.shape[1:], BF16),
                        pltpu.VMEM(wd.shape[1:], BF16)],
    )
    return pl.pallas_call(
        _experts_kernel,
        grid_spec=grid_spec,
        out_shape=jax.ShapeDtypeStruct(xs.shape, xs.dtype),
        compiler_params=_cparams("arbitrary"),
        name="experts",
    )(block_e, n_used, xs, wg, wu, wd)


_GROUP = 32


def _combine_kernel(slot_ref, nxt_ref, tw_ref, xs_ref, mod_ref, fn_ref, y_hbm, o_ref, gbuf0, gbuf1, wt_ref, sems, *, tm):
    gbuf = (gbuf0, gbuf1)
    j = pl.program_id(0)
    last = pl.num_programs(0) - 1
    d = xs_ref.shape[1]

    def issue(src_ref, bi, b, t):
        for k in range(TOP_K):
            pltpu.make_async_copy(_tile_row(y_hbm, src_ref[bi, k, t]), _tile_row(gbuf[b], (k * tm + t) * SUBLANES),
                                  sems.at[b]).start(priority=k % 2)

    def drain(b):
        pltpu.make_async_copy(y_hbm.at[pl.ds(0, gbuf[b].shape[0]), :], gbuf[b], sems.at[b]).wait()

    def reduce_rows(b, r0):
        w8 = wt_ref[b, pl.ds(r0, _GROUP), :]
        acc = [jnp.zeros((_GROUP, LANES), F32) for _ in range(d // LANES)]
        for k in range(TOP_K):
            wk = w8[:, k:k + 1]
            for c in range(SUBLANES):
                word = gbuf[b][pl.ds((k * tm + r0) * SUBLANES + c, _GROUP, stride=SUBLANES), :]
                acc[c] = acc[c] + wk * pltpu.bitcast(word << 16, F32)
                acc[SUBLANES + c] = acc[SUBLANES + c] + wk * pltpu.bitcast(word & jnp.uint32(0xFFFF0000), F32)
        rows = pl.ds(b * tm + r0, _GROUP)
        g2 = mod_ref[0][5:6]
        x2 = jnp.concatenate(acc, axis=1) * g2 + xs_ref[rows, :]
        o_ref[rows, :] = _rms(x2, fn_ref[...])

    def phase(fin_b, issue_args):
        def body(g, carry):
            r0 = pl.multiple_of(g * _GROUP, _GROUP)
            if issue_args is not None:
                for dt in range(_GROUP):
                    issue(*issue_args, r0 + dt)
            reduce_rows(fin_b, r0)
            return carry

        lax.fori_loop(0, tm // _GROUP, body, 0)

    @pl.when(j == 0)
    def _():
        lax.fori_loop(0, tm, lambda t, c: (issue(slot_ref, 0, 0, t), c)[1], 0)

    for b in range(2):
        wt_ref[b] = jnp.concatenate([tw_ref[b], jnp.zeros((LANES - TOP_K, tm), F32)], axis=0).T
    drain(0)
    phase(0, (slot_ref, 1, 1))
    drain(1)
    pl.when(j < last)(lambda: phase(1, (nxt_ref, 0, 0)))
    pl.when(j == last)(lambda: phase(1, None))


def _combine(slot, tw, xs, mod8, fn, y, *, seq, tm):
    n, d = xs.shape
    nb = slot.shape[0]
    assert nb % 2 == 0 and (seq // tm) % 2 == 0
    per_b = seq // (2 * tm)
    pair = lambda j: (j, 0, 0)
    return pl.pallas_call(
        functools.partial(_combine_kernel, tm=tm),
        grid=(nb // 2,),
        in_specs=[pl.BlockSpec((2, TOP_K, tm), pair, memory_space=pltpu.SMEM),
                  pl.BlockSpec((1, TOP_K, tm), lambda j: (jnp.minimum(2 * j + 2, nb - 1), 0, 0),
                               memory_space=pltpu.SMEM),
                  pl.BlockSpec((2, TOP_K, tm), pair),
                  pl.BlockSpec((2 * tm, d), lambda j: (j, 0)),
                  pl.BlockSpec((1, 8, d), lambda j: (j // per_b, 0, 0)),
                  pl.BlockSpec((1, d), lambda j: (0, 0)),
                  pl.BlockSpec(memory_space=pl.ANY)],
        out_specs=pl.BlockSpec((2 * tm, d), lambda j: (j, 0)),
        out_shape=jax.ShapeDtypeStruct((n, d), F32),
        scratch_shapes=[pltpu.VMEM((TOP_K * tm * SUBLANES, y.shape[1]), y.dtype),
                        pltpu.VMEM((TOP_K * tm * SUBLANES, y.shape[1]), y.dtype),
                        pltpu.VMEM((2, tm, LANES), F32), pltpu.SemaphoreType.DMA((2,))],
        compiler_params=_cparams("arbitrary"),
        name="combine",
    )(slot, slot, tw, xs, mod8, fn, y)


def _expert_runs(counts, n_blocks, blk):
    padded = (counts + blk - 1) // blk * blk
    pad_end = jnp.cumsum(padded)
    run_start = jnp.concatenate([pad_end - padded, pad_end[-1:]]).astype(jnp.int32)
    block_row = jnp.arange(n_blocks, dtype=jnp.int32) * blk
    block_e = jnp.minimum(jnp.sum(pad_end[None, :] <= block_row[:, None], axis=1), N_EXPERTS - 1).astype(jnp.int32)
    n_used = (pad_end[-1:] // blk).astype(jnp.int32)
    return run_start, block_e, n_used


def _pad_lanes(w, width):
    return jnp.pad(w, ((0, 0), (0, width - w.shape[1])))


def _tiles(seq):
    t = dict(tm_proj=256, tq=512, tt=256, tm_moe=512, blk=512)
    assert all(seq % v == 0 for k, v in t.items() if k != "blk")
    return t


def _layer(x, c, positions, ada_w, ada_b, norm1_w, w_in, q_a_norm_w, wq_b, kv_a_norm_w, wkv_b,
           gdn_conv_w, gdn_a_log, gdn_dt_bias, gdn_norm_w, w_out, norm2_w, router_w, router_bias,
           exp_w_gate, exp_w_up, exp_w_down, sh_w_gate, sh_w_up, sh_w_down, final_norm_w,
           *, tm_proj, tq, tt, tm_moe, blk):
    batch, seq, d = x.shape
    n = batch * seq
    x2 = x.reshape(n, d)

    c8 = jnp.zeros((8, d), F32).at[:batch].set(c)
    mod = _adaln(c8, ada_w, ada_b.reshape(1, -1))
    mod8 = jnp.pad(mod[:batch].reshape(batch, 6, d), ((0, 0), (0, 2), (0, 0)))

    hq = MLA_NOPE + MLA_ROPE
    w_cq, w_ckv, w_kpe, w_qkv, w_z, w_a, w_b = jnp.split(
        w_in, [_C_KV, _C_KV + MLA_KV_RANK, _C_KV + MLA_KV_RANK + MLA_ROPE,
               _C_KV + MLA_KV_RANK + MLA_ROPE + _GDN_QKV,
               _C_KV + MLA_KV_RANK + MLA_ROPE + _GDN_QKV + GDN_HEADS * GDN_DV,
               _C_KV + MLA_KV_RANK + MLA_ROPE + _GDN_QKV + GDN_HEADS * GDN_DV + GDN_HEADS], axis=1)
    win = jnp.concatenate([w_cq, w_ckv, _pad_lanes(w_kpe, LANES), w_qkv, w_z,
                           _pad_lanes(jnp.concatenate([w_a, w_b], axis=1), LANES)], axis=1).astype(BF16)
    wq = jnp.pad(wq_b.reshape(MLA_Q_RANK, MLA_HEADS, hq),
                 ((0, 0), (0, 0), (0, MLA_QK_PAD - hq))).reshape(MLA_Q_RANK, MLA_HEADS * MLA_QK_PAD).astype(BF16)
    wkv4 = wkv_b.reshape(MLA_KV_RANK, MLA_HEADS, MLA_NOPE + MLA_V)
    wkv = jnp.concatenate([wkv4[:, :, :MLA_NOPE].reshape(MLA_KV_RANK, -1),
                           wkv4[:, :, MLA_NOPE:].reshape(MLA_KV_RANK, -1)], axis=1).astype(BF16)
    inv_freq = 1.0 / (ROPE_THETA ** (jnp.arange(0, MLA_ROPE, 2, dtype=F32) / MLA_ROPE))
    invf = _pad_lanes(jnp.concatenate([inv_freq, inv_freq])[None, :], LANES)

    q, k, v, qkv, z, ab = _proj(x2, mod8, positions.reshape(n, 1), norm1_w.reshape(1, d), win,
                                q_a_norm_w.reshape(1, -1), wq, kv_a_norm_w.reshape(1, -1), wkv, invf,
                                seq=seq, tm=tm_proj)
    out_a = _attn(q, k, v, batch=batch, seq=seq, tq=tq)

    cw8 = jnp.pad(gdn_conv_w, ((0, 8 - GDN_CONV), (0, 0)))
    hp = _pad_lanes(jnp.stack([gdn_a_log, gdn_dt_bias]), LANES)
    hp = jnp.pad(hp, ((0, 6), (0, 0)))
    out_b = _gdn(qkv, z, ab, cw8, hp, gdn_norm_w.reshape(1, -1), batch=batch, seq=seq, tt=tt)

    ha = MLA_HEADS * MLA_V
    xs, h2t, ids, tw, pos, counts = _mix(
        out_a, out_b, x2, mod8, w_out[:ha].astype(BF16), w_out[ha:].astype(BF16), norm2_w.reshape(1, d),
        router_w.T, router_bias.reshape(-1, 1), sh_w_gate.astype(BF16), sh_w_up.astype(BF16),
        sh_w_down.astype(BF16), seq=seq, tm=tm_moe)

    n_blocks = n * TOP_K // blk + N_EXPERTS
    run_start, block_e, n_used = _expert_runs(counts[:, 0].astype(jnp.int32), n_blocks, blk)
    slot = _slots(run_start, ids, pos)
    xsort = _dispatch(run_start, slot, h2t, n_slots=n_blocks * blk, tm=tm_moe, blk=blk)
    y = _experts(block_e, n_used, xsort, exp_w_gate, exp_w_up, exp_w_down, blk=blk)
    out = _combine(slot, tw, xs, mod8, final_norm_w.reshape(1, d), y, seq=seq, tm=tm_moe)
    return out.reshape(batch, seq, d)


def kernel(x, c, positions, ada_w, ada_b, norm1_w, w_in, q_a_norm_w, wq_b, kv_a_norm_w, wkv_b, gdn_conv_w,
           gdn_a_log, gdn_dt_bias, gdn_norm_w, w_out, norm2_w, router_w, router_bias, exp_w_gate, exp_w_up,
           exp_w_down, sh_w_gate, sh_w_up, sh_w_down, final_norm_w):
    assert ada_w.shape[0] == 1, "single layer"
    return _layer(
        x, c, positions, ada_w[0], ada_b[0], norm1_w[0], w_in[0], q_a_norm_w[0], wq_b[0], kv_a_norm_w[0],
        wkv_b[0], gdn_conv_w[0], gdn_a_log[0], gdn_dt_bias[0], gdn_norm_w[0], w_out[0], norm2_w[0],
        router_w[0], router_bias[0], exp_w_gate[0], exp_w_up[0], exp_w_down[0], sh_w_gate[0], sh_w_up[0],
        sh_w_down[0], final_norm_w, **_tiles(x.shape[1]))
```

```python
import functools

import jax
import jax.numpy as jnp
from jax import lax
from jax.experimental import pallas as pl
from jax.experimental.pallas import tpu as pltpu

F32 = jnp.float32
BF16 = jnp.bfloat16

CHUNK = 64
NORM_EPS = 1e-6
MLA_HEADS = 4
MLA_Q_RANK = 384
MLA_KV_RANK = 256
MLA_NOPE = 128
MLA_ROPE = 64
MLA_V = 128
ROPE_THETA = 10000.0
GDN_HEADS = 4
GDN_DK = 128
GDN_DV = 128
GDN_CONV = 4
N_EXPERTS = 64
N_GROUPS = 8
TOPK_GROUPS = 4
TOP_K = 8
ROUTED_SCALE = 2.5
LOG2E = 1.4426950408889634

LANES = 128
MLA_QK_PAD = 256
VMEM_LIMIT = 56 * 1024 * 1024


def _cparams(*sem):
    return pltpu.CompilerParams(dimension_semantics=sem, vmem_limit_bytes=VMEM_LIMIT)


def _dot(a, b):
    return jnp.dot(a, b, preferred_element_type=F32)


def _split3(a):
    hi = a.astype(BF16)
    r = a - hi.astype(F32)
    mid = r.astype(BF16)
    return hi, mid, (r - mid.astype(F32)).astype(BF16)


def _dot_x3(a, b):
    ah, am, _ = a
    bh, bm, _ = b
    return _dot(ah, bh) + (_dot(ah, bm) + _dot(am, bh))


def _dot_nt(a, b):
    return lax.dot_general(a, b, (((1,), (1,)), ((), ())), preferred_element_type=F32)


def _dot_tn(a, b):
    return lax.dot_general(a, b, (((0,), (0,)), ((), ())), preferred_element_type=F32)


def _silu(x):
    return x * jax.nn.sigmoid(x)


def _rms(x, w):
    return x * lax.rsqrt(jnp.mean(x * x, axis=-1, keepdims=True) + NORM_EPS) * w


def _adaln_kernel(c_ref, w_ref, b_ref, o_ref):
    a = _silu(c_ref[...]).astype(BF16)
    o_ref[...] = _dot(a, w_ref[...].astype(BF16)) + b_ref[...]


def _adaln(c8, ada_w, ada_b):
    d = c8.shape[1]
    n_out = ada_w.shape[1]
    return pl.pallas_call(
        _adaln_kernel,
        grid=(n_out // d,),
        in_specs=[pl.BlockSpec((8, d), lambda j: (0, 0)),
                  pl.BlockSpec((d, d), lambda j: (0, j)),
                  pl.BlockSpec((1, d), lambda j: (0, j))],
        out_specs=pl.BlockSpec((8, d), lambda j: (0, j)),
        out_shape=jax.ShapeDtypeStruct((8, n_out), F32),
        compiler_params=_cparams("parallel"),
        name="adaln",
    )(c8, ada_w, ada_b)


_C_Q = 0
_C_KV = _C_Q + MLA_Q_RANK
_C_KPE = _C_KV + MLA_KV_RANK
_C_QKV = _C_KPE + LANES
_GDN_QKV = GDN_HEADS * (2 * GDN_DK + GDN_DV)
_C_Z = _C_QKV + _GDN_QKV
_C_AB = _C_Z + GDN_HEADS * GDN_DV
_D_IN_PAD = _C_AB + LANES


def _rope(xb, cos, sin):
    half = MLA_ROPE // 2
    lane = lax.broadcasted_iota(jnp.int32, xb.shape, 1)
    rot = jnp.where(lane < half, -pltpu.roll(xb, LANES - half, 1), pltpu.roll(xb, half, 1))
    return xb * cos + rot * sin


def _proj_kernel(x_ref, mod_ref, pos_ref, n1_ref, win_ref, qan_ref, wq_ref, kvan_ref, wkv_ref, invf_ref,
                 q_ref, k_ref, v_ref, qkv_ref, z_ref, ab_ref):
    x = x_ref[...]
    mod = mod_ref[0]
    sh1, sc1 = mod[0:1], mod[1:2]
    h = (_rms(x, n1_ref[...]) * (1.0 + sc1) + sh1).astype(BF16)
    proj = _dot(h, win_ref[...])

    ang = pos_ref[...].astype(F32) * invf_ref[...]
    cos, sin = jnp.cos(ang), jnp.sin(ang)

    cq = proj[:, _C_Q:_C_Q + MLA_Q_RANK]
    qn = _rms(cq, qan_ref[...]).astype(BF16)
    q = _dot(qn, wq_ref[...]) * ((MLA_NOPE + MLA_ROPE) ** -0.5 * LOG2E)
    for hh in range(MLA_HEADS):
        c0 = hh * MLA_QK_PAD
        q_ref[:, c0:c0 + MLA_NOPE] = q[:, c0:c0 + MLA_NOPE].astype(BF16)
        q_ref[:, c0 + MLA_NOPE:c0 + MLA_QK_PAD] = _rope(q[:, c0 + MLA_NOPE:c0 + MLA_QK_PAD], cos, sin).astype(BF16)

    ckv = proj[:, _C_KV:_C_KV + MLA_KV_RANK]
    kvn = _rms(ckv, kvan_ref[...]).astype(BF16)
    kv = _dot(kvn, wkv_ref[...])
    kpe = _rope(proj[:, _C_KPE:_C_KPE + LANES], cos, sin).astype(BF16)
    for hh in range(MLA_HEADS):
        c0 = hh * MLA_QK_PAD
        k_ref[:, c0:c0 + MLA_NOPE] = kv[:, hh * MLA_NOPE:(hh + 1) * MLA_NOPE].astype(BF16)
        k_ref[:, c0 + MLA_NOPE:c0 + MLA_QK_PAD] = kpe
    v_ref[...] = kv[:, MLA_HEADS * MLA_NOPE:].astype(BF16)

    qkv_ref[...] = proj[:, _C_QKV:_C_QKV + _GDN_QKV].astype(qkv_ref.dtype)
    z_ref[...] = proj[:, _C_Z:_C_Z + GDN_HEADS * GDN_DV].astype(z_ref.dtype)
    ab_ref[...] = proj[:, _C_AB:_C_AB + LANES]


def _proj(x2, mod8, pos2, n1, win, qan, wq, kvan, wkv, invf, *, seq, tm):
    n, d = x2.shape
    per_b = seq // tm
    row = lambda i: (i, 0)
    const = lambda i: (0, 0)
    hq = MLA_HEADS * MLA_QK_PAD
    hv = MLA_HEADS * MLA_V
    return pl.pallas_call(
        _proj_kernel,
        grid=(n // tm,),
        in_specs=[pl.BlockSpec((tm, d), row),
                  pl.BlockSpec((1, 8, d), lambda i: (i // per_b, 0, 0)),
                  pl.BlockSpec((tm, 1), row),
                  pl.BlockSpec((1, d), const),
                  pl.BlockSpec(win.shape, const),
                  pl.BlockSpec(qan.shape, const),
                  pl.BlockSpec(wq.shape, const),
                  pl.BlockSpec(kvan.shape, const),
                  pl.BlockSpec(wkv.shape, const),
                  pl.BlockSpec(invf.shape, const)],
        out_specs=[pl.BlockSpec((tm, hq), row), pl.BlockSpec((tm, hq), row), pl.BlockSpec((tm, hv), row),
                   pl.BlockSpec((tm, _GDN_QKV), row), pl.BlockSpec((tm, GDN_HEADS * GDN_DV), row),
                   pl.BlockSpec((tm, LANES), row)],
        out_shape=[jax.ShapeDtypeStruct((n, hq), BF16), jax.ShapeDtypeStruct((n, hq), BF16),
                   jax.ShapeDtypeStruct((n, hv), BF16), jax.ShapeDtypeStruct((n, _GDN_QKV), BF16),
                   jax.ShapeDtypeStruct((n, GDN_HEADS * GDN_DV), BF16), jax.ShapeDtypeStruct((n, LANES), F32)],
        compiler_params=_cparams("parallel"),
        name="proj",
    )(x2, mod8, pos2, n1, win, qan, wq, kvan, wkv, invf)


_Q_TILES = 2


def _attn_kernel(q_ref, k_ref, v_ref, o_ref, *, tq):
    i = pl.program_id(1)
    chains = [(qt, hh) for qt in range(_Q_TILES) for hh in range(MLA_HEADS)]

    ones_col = jnp.where(lax.broadcasted_iota(jnp.int32, (tq, MLA_V), 1) == 0, 1.0, 0.0).astype(BF16)

    def chain_step(qt, hh, r0, carry, mask):
        m, acc = carry
        s = _dot_nt(q_ref[qt * tq:(qt + 1) * tq, hh * MLA_QK_PAD:(hh + 1) * MLA_QK_PAD],
                    k_ref[pl.ds(r0, tq), hh * MLA_QK_PAD:(hh + 1) * MLA_QK_PAD])
        if mask is not None:
            s = jnp.where(mask, s, -jnp.inf)
        m_new = jnp.maximum(m, jnp.max(s, axis=-1, keepdims=True))
        alpha = jnp.exp2(m - m_new)
        p = jnp.exp2((s - m_new).astype(BF16))
        v_ext = jnp.concatenate([v_ref[pl.ds(r0, tq), hh * MLA_V:(hh + 1) * MLA_V], ones_col], axis=1)
        return m_new, alpha * acc + _dot(p, v_ext)

    def block(r0, carry, masks):
        return tuple(carry[c] if masks[qt] is False else chain_step(qt, hh, r0, carry[c], masks[qt])
                     for c, (qt, hh) in enumerate(chains))

    init = tuple((jnp.full((tq, 1), -jnp.inf, F32), jnp.zeros((tq, 2 * MLA_V), F32)) for _ in chains)
    carry = lax.fori_loop(0, _Q_TILES * i, lambda j, c: block(pl.multiple_of(j * tq, tq), c, (None, None)), init)
    rq = lax.broadcasted_iota(jnp.int32, (tq, tq), 0) // CHUNK
    ck = lax.broadcasted_iota(jnp.int32, (tq, tq), 1) // CHUNK
    diag = ck <= rq
    carry = block(pl.multiple_of(_Q_TILES * i * tq, tq), carry, (diag, None))
    carry = block(pl.multiple_of((_Q_TILES * i + 1) * tq, tq), carry, (False, diag))
    for c, (qt, hh) in enumerate(chains):
        _, acc = carry[c]
        o_ref[qt * tq:(qt + 1) * tq, hh * MLA_V:(hh + 1) * MLA_V] = (
            acc[:, :MLA_V] / acc[:, MLA_V:MLA_V + 1]).astype(o_ref.dtype)


def _attn(q, k, v, *, batch, seq, tq):
    n = q.shape[0]
    rows = _Q_TILES * tq
    nq = seq // rows
    return pl.pallas_call(
        functools.partial(_attn_kernel, tq=tq),
        grid=(batch, nq),
        in_specs=[pl.BlockSpec((rows, q.shape[1]), lambda b, i: (b * nq + i, 0)),
                  pl.BlockSpec((seq, k.shape[1]), lambda b, i: (b, 0)),
                  pl.BlockSpec((seq, v.shape[1]), lambda b, i: (b, 0))],
        out_specs=pl.BlockSpec((rows, v.shape[1]), lambda b, i: (b * nq + i, 0)),
        out_shape=jax.ShapeDtypeStruct((n, v.shape[1]), BF16),
        compiler_params=_cparams("parallel", "arbitrary"),
        name="attn",
    )(q, k, v)


_SUPER = 2 * CHUNK


def _unit_lower_inverses(lows):
    n = lows[0].shape[0]
    eye = (lax.broadcasted_iota(jnp.int32, (n, n), 0) == lax.broadcasted_iota(jnp.int32, (n, n), 1)).astype(F32)
    levels = CHUNK.bit_length() - 1
    ps = [-low for low in lows]
    ts = [eye + p for p in ps]
    for level in range(1, levels):
        parts = [_split3(p) for p in ps]
        ps = [_dot_x3(s, s) for s in parts]
        if level == 1:
            continue
        ts = [t + _dot_x3(s, _split3(t)) for t, s in zip(ts, parts)]
    return [t + _dot_x3(_split3(p), _split3(t)) for t, p in zip(ts, ps)]


def _gdn_kernel(qkv_ref, z_ref, ab_ref, cw_ref, hp_ref, nw_ref, o_ref, xe_ref, st_ref, *, tt):
    j = pl.program_id(1)
    hd = GDN_HEADS * GDN_DK

    @pl.when(j == 0)
    def _():
        xe_ref[0:8, :] = jnp.zeros((8, xe_ref.shape[1]), F32)
        st_ref[...] = jnp.zeros(st_ref.shape, F32)

    xe_ref[8:8 + tt, :] = qkv_ref[...].astype(F32)
    cw = cw_ref[...]
    y = xe_ref[8:8 + tt, :] * cw[GDN_CONV - 1:GDN_CONV]
    for i in range(1, GDN_CONV):
        y = y + xe_ref[8 - i:8 - i + tt, :] * cw[GDN_CONV - 1 - i:GDN_CONV - i]
    xe_ref[0:8, :] = xe_ref[tt:tt + 8, :]
    act = _silu(y)

    ab = ab_ref[...]
    hp = hp_ref[...]
    pre = ab + hp[1:2]
    softplus = jnp.maximum(pre, 0.0) + jnp.log1p(jnp.exp(-jnp.abs(pre)))
    g_all = -jnp.exp(hp[0:1]) * softplus
    beta_all = jax.nn.sigmoid(ab)

    ri = lax.broadcasted_iota(jnp.int32, (tt, tt), 0)
    ci = lax.broadcasted_iota(jnp.int32, (tt, tt), 1)
    tri = ((ri // CHUNK == ci // CHUNK) & (ci <= ri)).astype(BF16)
    gc = sum(_dot(tri, part) for part in _split3(g_all))
    gct = gc.T

    rs = lax.broadcasted_iota(jnp.int32, (_SUPER, _SUPER), 0)
    cs = lax.broadcasted_iota(jnp.int32, (_SUPER, _SUPER), 1)
    same = rs // CHUNK == cs // CHUNK
    incl = same & (cs <= rs)
    strict = same & (cs < rs)

    tiles = [(s * _SUPER, hh) for s in range(tt // _SUPER) for hh in range(GDN_HEADS)]
    pre_t = []
    for r0, hh in tiles:
        qh = act[r0:r0 + _SUPER, hh * GDN_DK:(hh + 1) * GDN_DK]
        kh = act[r0:r0 + _SUPER, hd + hh * GDN_DK:hd + (hh + 1) * GDN_DK]
        vh = act[r0:r0 + _SUPER, 2 * hd + hh * GDN_DV:2 * hd + (hh + 1) * GDN_DV]
        qh = qh * lax.rsqrt(jnp.sum(qh * qh, axis=-1, keepdims=True) + 1e-6) * (GDN_DK ** -0.5)
        kh = kh * lax.rsqrt(jnp.sum(kh * kh, axis=-1, keepdims=True) + 1e-6)
        beta = beta_all[r0:r0 + _SUPER, GDN_HEADS + hh:GDN_HEADS + hh + 1]
        gcol = gc[r0:r0 + _SUPER, hh:hh + 1]
        grow = gct[hh:hh + 1, r0:r0 + _SUPER]
        decay = jnp.exp(jnp.where(incl, gcol - grow, -jnp.inf))
        eg = jnp.exp(gcol)
        kb = kh * beta
        pre_t.append(dict(kh=kh, khb=kh.astype(BF16), kbb=kb.astype(BF16), vbb=(vh * beta).astype(BF16),
                          kgb=(kb * eg).astype(BF16), qb=qh.astype(BF16), qg=qh * eg, gcol=gcol, decay=decay))
    lows = [jnp.where(strict, _dot_nt(p["kbb"], p["khb"]) * p["decay"], 0.0) for p in pre_t]
    tinvs = [t.astype(BF16) for t in _unit_lower_inverses(lows)]
    us = [_dot(t, p["vbb"]).astype(BF16) for t, p in zip(tinvs, pre_t)]
    ws = [_dot(t, p["kgb"]).astype(BF16) for t, p in zip(tinvs, pre_t)]
    atts = [(_dot_nt(p["qb"], p["khb"]) * p["decay"]).astype(BF16) for p in pre_t]

    steps = []
    for cc in range(tt // CHUNK):
        r0 = cc * CHUNK
        a0 = r0 % _SUPER
        for hh in range(GDN_HEADS):
            ti = (r0 // _SUPER) * GDN_HEADS + hh
            p = pre_t[ti]
            gl = gc[r0 + CHUNK - 1:r0 + CHUNK, hh:hh + 1]
            kd = (p["kh"][a0:a0 + CHUNK] * jnp.exp(gl - p["gcol"][a0:a0 + CHUNK])).astype(BF16)
            u_c, w_c = us[ti][a0:a0 + CHUNK], ws[ti][a0:a0 + CHUNK]
            att_c = atts[ti][a0:a0 + CHUNK, a0:a0 + CHUNK]
            steps.append(dict(hh=hh, r0=r0, gain=jnp.exp(gl), kw=_dot_tn(kd, w_c).astype(BF16), ku=_dot_tn(kd, u_c),
                              qs=(p["qg"][a0:a0 + CHUNK] - _dot(att_c, w_c)).astype(BF16), ou=_dot(att_c, u_c)))

    states = [st_ref[hh] for hh in range(GDN_HEADS)]
    for st in steps:
        hh, r0 = st["hh"], st["r0"]
        sb = states[hh].astype(BF16)
        o = _dot(st["qs"], sb) + st["ou"]
        states[hh] = states[hh] * st["gain"] + (st["ku"] - _dot(st["kw"], sb))
        zz = z_ref[r0:r0 + CHUNK, hh * GDN_DV:(hh + 1) * GDN_DV].astype(F32)
        o_ref[r0:r0 + CHUNK, hh * GDN_DV:(hh + 1) * GDN_DV] = (_rms(o, nw_ref[...]) * _silu(zz)).astype(o_ref.dtype)
    for hh in range(GDN_HEADS):
        st_ref[hh] = states[hh]


def _gdn(qkv, z, ab, cw8, hp, nw, *, batch, seq, tt):
    n = qkv.shape[0]
    per_b = seq // tt
    row = lambda b, j: (b * per_b + j, 0)
    const = lambda b, j: (0, 0)
    hv = GDN_HEADS * GDN_DV
    return pl.pallas_call(
        functools.partial(_gdn_kernel, tt=tt),
        grid=(batch, per_b),
        in_specs=[pl.BlockSpec((tt, _GDN_QKV), row), pl.BlockSpec((tt, hv), row), pl.BlockSpec((tt, LANES), row),
                  pl.BlockSpec(cw8.shape, const), pl.BlockSpec(hp.shape, const), pl.BlockSpec(nw.shape, const)],
        out_specs=pl.BlockSpec((tt, hv), row),
        out_shape=jax.ShapeDtypeStruct((n, hv), BF16),
        scratch_shapes=[pltpu.VMEM((tt + 8, _GDN_QKV), F32), pltpu.VMEM((GDN_HEADS, GDN_DK, GDN_DV), F32)],
        compiler_params=_cparams("parallel", "arbitrary"),
        name="gdn",
    )(qkv, z, ab, cw8, hp, nw)


def _route(logits_t, bias_col):
    e, tm = logits_t.shape
    gsz = e // N_GROUPS
    scores = jax.nn.sigmoid(logits_t)
    biased = scores + bias_col
    sub = lax.broadcasted_iota(jnp.int32, (gsz, tm), 0)
    rows = []
    for g in range(N_GROUPS):
        blk = biased[g * gsz:(g + 1) * gsz]
        m1 = jnp.max(blk, axis=0, keepdims=True)
        first = jnp.min(jnp.where(blk == m1, sub, gsz), axis=0, keepdims=True)
        m2 = jnp.max(jnp.where(sub == first, -jnp.inf, blk), axis=0, keepdims=True)
        rows.append(m1 + m2)
    gs = jnp.concatenate(rows, axis=0)
    gi = lax.broadcasted_iota(jnp.int32, (N_GROUPS, tm), 0)
    grank = jnp.zeros((N_GROUPS, tm), F32)
    for g in range(N_GROUPS):
        r = gs[g:g + 1]
        grank = grank + jnp.where(r > gs, 1.0, jnp.where(r == gs, (gi > g).astype(F32), 0.0))
    gsel = grank < TOPK_GROUPS
    masked = jnp.concatenate(
        [jnp.where(gsel[g:g + 1], biased[g * gsz:(g + 1) * gsz], -jnp.inf) for g in range(N_GROUPS)], axis=0)
    ei = lax.broadcasted_iota(jnp.int32, (e, tm), 0)
    rank = jnp.zeros((e, tm), F32)
    for k in range(e):
        r = masked[k:k + 1]
        rank = rank + jnp.where(r > masked, 1.0, jnp.where(r == masked, (ei > k).astype(F32), 0.0))
    return scores, rank


SUBLANES = 4


def _store_tile_rows(ref, x):
    c = x.shape[1] // 2
    lo = pltpu.bitcast(x[:, :c].astype(BF16).astype(F32), jnp.uint32)
    hi = pltpu.bitcast(x[:, c:].astype(BF16).astype(F32), jnp.uint32)
    w = (hi & jnp.uint32(0xFFFF0000)) | (lo >> 16)
    for j in range(SUBLANES):
        ref[pl.ds(j, x.shape[0], stride=SUBLANES), :] = w[:, j * LANES:(j + 1) * LANES]


def _load_tile_rows(ref, r0, rows):
    w = jnp.concatenate(
        [ref[pl.ds(r0 * SUBLANES + j, rows, stride=SUBLANES), :] for j in range(SUBLANES)], axis=1)
    return jnp.concatenate([pltpu.bitcast(w << 16, F32), pltpu.bitcast(w & jnp.uint32(0xFFFF0000), F32)], axis=1)


def _tile_row(ref, sublane_offset):
    return ref.at[pl.ds(pl.multiple_of(sublane_offset, SUBLANES), SUBLANES), :]


def _mix_kernel(oa_ref, ob_ref, x_ref, mod_ref, woa_ref, wob_ref, n2_ref, rwt_ref, rb_ref, sg_ref, su_ref, sd_ref,
                xs_ref, h2t_ref, te_ref, tw_ref, tp_ref, cnt_ref, run_ref):
    @pl.when(pl.program_id(0) == 0)
    def _():
        run_ref[...] = jnp.zeros(run_ref.shape, F32)

    mod = mod_ref[0]
    g1, sh2, sc2, g2 = mod[2:3], mod[3:4], mod[4:5], mod[5:6]
    mix = _dot(oa_ref[...], woa_ref[...]) + _dot(ob_ref[...], wob_ref[...])
    x1 = x_ref[...] + g1 * mix
    h2 = _rms(x1, n2_ref[...]) * (1.0 + sc2) + sh2
    _store_tile_rows(h2t_ref, h2)
    h2b = h2.astype(BF16)
    hs = (_silu(_dot(h2b, sg_ref[...])) * _dot(h2b, su_ref[...])).astype(BF16)
    xs_ref[...] = x1 + g2 * _dot(hs, sd_ref[...])

    rh, rm, _ = _split3(rwt_ref[...])
    hh, hm, _ = _split3(h2)
    scores, rank = _route(_dot_nt(rh, hh) + (_dot_nt(rh, hm) + _dot_nt(rm, hh)), rb_ref[...])
    e, tm = rank.shape
    sel = jnp.where(rank < TOP_K, 1.0, 0.0)
    earlier = (lax.broadcasted_iota(jnp.int32, (tm, tm), 0) < lax.broadcasted_iota(jnp.int32, (tm, tm), 1))
    posmat = _dot(sel.astype(BF16), earlier.astype(BF16)) + run_ref[...]
    eif = lax.broadcasted_iota(jnp.int32, (e, tm), 0).astype(F32)
    ids, wts, pos = [], [], []
    for k in range(TOP_K):
        hit = rank == float(k)
        ids.append(jnp.sum(jnp.where(hit, eif, 0.0), axis=0, keepdims=True))
        wts.append(jnp.sum(jnp.where(hit, scores, 0.0), axis=0, keepdims=True))
        pos.append(jnp.sum(jnp.where(hit, posmat, 0.0), axis=0, keepdims=True))
    wts = jnp.concatenate(wts, axis=0)
    te_ref[0] = jnp.concatenate(ids, axis=0).astype(jnp.int32)
    tw_ref[0] = wts / jnp.sum(wts, axis=0, keepdims=True) * ROUTED_SCALE
    tp_ref[0] = jnp.concatenate(pos, axis=0).astype(jnp.int32)
    run_ref[...] = run_ref[...] + jnp.sum(sel, axis=1, keepdims=True)
    cnt_ref[...] = jnp.broadcast_to(run_ref[...], cnt_ref.shape)


def _mix(oa, ob, x2, mod8, woa, wob, n2, rwt, rb, sg, su, sd, *, seq, tm):
    n, d = x2.shape
    per_b = seq // tm
    nb = n // tm
    row = lambda i: (i, 0)
    blk3 = lambda i: (i, 0, 0)
    const = lambda i: (0, 0)
    return pl.pallas_call(
        _mix_kernel,
        grid=(nb,),
        in_specs=[pl.BlockSpec((tm, oa.shape[1]), row), pl.BlockSpec((tm, ob.shape[1]), row),
                  pl.BlockSpec((tm, d), row), pl.BlockSpec((1, 8, d), lambda i: (i // per_b, 0, 0)),
                  pl.BlockSpec(woa.shape, const), pl.BlockSpec(wob.shape, const), pl.BlockSpec((1, d), const),
                  pl.BlockSpec(rwt.shape, const), pl.BlockSpec(rb.shape, const),
                  pl.BlockSpec(sg.shape, const), pl.BlockSpec(su.shape, const), pl.BlockSpec(sd.shape, const)],
        out_specs=[pl.BlockSpec((tm, d), row), pl.BlockSpec((tm * SUBLANES, LANES), row),
                   pl.BlockSpec((1, TOP_K, tm), blk3), pl.BlockSpec((1, TOP_K, tm), blk3),
                   pl.BlockSpec((1, TOP_K, tm), blk3), pl.BlockSpec((N_EXPERTS, LANES), const)],
        out_shape=[jax.ShapeDtypeStruct((n, d), F32), jax.ShapeDtypeStruct((n * SUBLANES, LANES), jnp.uint32),
                   jax.ShapeDtypeStruct((nb, TOP_K, tm), jnp.int32), jax.ShapeDtypeStruct((nb, TOP_K, tm), F32),
                   jax.ShapeDtypeStruct((nb, TOP_K, tm), jnp.int32),
                   jax.ShapeDtypeStruct((N_EXPERTS, LANES), F32)],
        scratch_shapes=[pltpu.VMEM((N_EXPERTS, 1), F32)],
        compiler_params=_cparams("arbitrary"),
        name="mix",
    )(oa, ob, x2, mod8, woa, wob, n2, rwt, rb, sg, su, sd)


def _slots_kernel(ps_ref, ids_ref, pos_ref, o_ref):
    ids = ids_ref[...]
    start = jnp.zeros(ids.shape, jnp.int32)
    for e in range(N_EXPERTS):
        start = jnp.where(ids == e, ps_ref[e], start)
    o_ref[...] = (start + pos_ref[...]) * SUBLANES


def _slots(ps, ids, pos):
    spec = pl.BlockSpec(ids.shape, lambda i, ps: (0, 0, 0))
    return pl.pallas_call(
        _slots_kernel,
        grid_spec=pltpu.PrefetchScalarGridSpec(num_scalar_prefetch=1, grid=(1,),
                                               in_specs=[spec, spec], out_specs=spec),
        out_shape=jax.ShapeDtypeStruct(ids.shape, jnp.int32),
        compiler_params=_cparams("parallel"),
        name="slots",
    )(ps, ids, pos)


def _dispatch_kernel(ps_ref, slot_ref, h2t_ref, xs_hbm, zbuf, ring, zsem, sems, *, tm, blk, n_steps):
    i = pl.program_id(0)

    @pl.when(i == 0)
    def _():
        zbuf[...] = jnp.zeros(zbuf.shape, zbuf.dtype)

        def block_at(row):
            return xs_hbm.at[pl.ds(pl.multiple_of(row * SUBLANES, blk * SUBLANES), blk * SUBLANES), :]

        def tail(e):
            end = ps_ref[e + 1]
            return end > ps_ref[e], pltpu.make_async_copy(zbuf, block_at(end - blk), zsem)

        def unused(e):
            row = ps_ref[N_EXPERTS] + e * blk
            return row * SUBLANES < xs_hbm.shape[0], pltpu.make_async_copy(zbuf, block_at(row), zsem)

        for fill in (tail, unused):
            for e in range(N_EXPERTS):
                needed, cp = fill(e)
                pl.when(needed)(cp.start)
        for fill in (tail, unused):
            for e in range(N_EXPERTS):
                needed, cp = fill(e)
                pl.when(needed)(cp.wait)

    def send(b):
        ring[b] = h2t_ref[...]

        def issue(t, carry):
            for k in range(TOP_K):
                pltpu.make_async_copy(_tile_row(ring.at[b], t * SUBLANES), _tile_row(xs_hbm, slot_ref[0, k, t]),
                                      sems.at[b]).start(priority=k % 2)
            return carry

        lax.fori_loop(0, tm, issue, 0)

    def drain(b):
        for k in range(TOP_K):
            pltpu.make_async_copy(ring.at[b], xs_hbm.at[pl.ds(0, tm * SUBLANES), :], sems.at[b]).wait()

    for b in range(2):
        @pl.when(i % 2 == b)
        def _(b=b):
            send(b)

            @pl.when(i > 0)
            def _():
                drain(1 - b)

    @pl.when(i == n_steps - 1)
    def _():
        drain((n_steps - 1) % 2)


def _dispatch(ps, slot, h2t, *, n_slots, tm, blk):
    lanes = h2t.shape[1]
    n_steps = h2t.shape[0] // (tm * SUBLANES)
    grid_spec = pltpu.PrefetchScalarGridSpec(
        num_scalar_prefetch=1,
        grid=(n_steps,),
        in_specs=[pl.BlockSpec((1, TOP_K, tm), lambda i, ps: (i, 0, 0), memory_space=pltpu.SMEM),
                  pl.BlockSpec((tm * SUBLANES, lanes), lambda i, ps: (i, 0))],
        out_specs=pl.BlockSpec(memory_space=pl.ANY),
        scratch_shapes=[pltpu.VMEM((blk * SUBLANES, lanes), jnp.uint32),
                        pltpu.VMEM((2, tm * SUBLANES, lanes), jnp.uint32),
                        pltpu.SemaphoreType.DMA(()), pltpu.SemaphoreType.DMA((2,))],
    )
    return pl.pallas_call(
        functools.partial(_dispatch_kernel, tm=tm, blk=blk, n_steps=n_steps),
        grid_spec=grid_spec,
        out_shape=jax.ShapeDtypeStruct((n_slots * SUBLANES, lanes), jnp.uint32),
        compiler_params=_cparams("arbitrary"),
        name="dispatch",
    )(ps, slot, h2t)


def _experts_kernel(be_ref, nu_ref, x_ref, wg_ref, wu_ref, wd_ref, y_ref, wgb, wub, wdb):
    i = pl.program_id(0)
    used = i < nu_ref[0]
    new_expert = jnp.logical_or(i == 0, be_ref[i] != be_ref[jnp.maximum(i - 1, 0)])

    @pl.when(jnp.logical_and(used, new_expert))
    def _():
        wgb[...] = wg_ref[0].astype(BF16)
        wub[...] = wu_ref[0].astype(BF16)
        wdb[...] = wd_ref[0].astype(BF16)

    @pl.when(used)
    def _():
        xb = _load_tile_rows(x_ref, 0, x_ref.shape[0] // SUBLANES).astype(BF16)
        hb = (_silu(_dot(xb, wgb[...])) * _dot(xb, wub[...])).astype(BF16)
        _store_tile_rows(y_ref, _dot(hb, wdb[...]))

    @pl.when(jnp.logical_not(used))
    def _():
        y_ref[...] = jnp.zeros(y_ref.shape, y_ref.dtype)


def _experts(block_e, n_used, xs, wg, wu, wd, *, blk):
    n_blocks = block_e.shape[0]
    rows_blk = (blk * SUBLANES, xs.shape[1])
    wsel = lambda i, be, nu: (be[i], 0, 0)
    grid_spec = pltpu.PrefetchScalarGridSpec(
        num_scalar_prefetch=2,
        grid=(n_blocks,),
        in_specs=[pl.BlockSpec(rows_blk, lambda i, be, nu: (jnp.minimum(i, nu[0] - 1), 0)),
                  pl.BlockSpec((1,) + wg.shape[1:], wsel), pl.BlockSpec((1,) + wu.shape[1:], wsel),
                  pl.BlockSpec((1,) + wd.shape[1:], wsel)],
        out_specs=pl.BlockSpec(rows_blk, lambda i, be, nu: (i, 0)),
        scratch_shapes=[pltpu.VMEM(wg.shape[1:], BF16), pltpu.VMEM(wu.shape[1:], BF16),
                        pltpu.VMEM(wd.shape[1:], BF16)],
    )
    return pl.pallas_call(
        _experts_kernel,
        grid_spec=grid_spec,
        out_shape=jax.ShapeDtypeStruct(xs.shape, xs.dtype),
        compiler_params=_cparams("arbitrary"),
        name="experts",
    )(block_e, n_used, xs, wg, wu, wd)


_GROUP = 32


def _combine_kernel(slot_ref, nxt_ref, tw_ref, xs_ref, mod_ref, fn_ref, y_hbm, o_ref, gbuf0, gbuf1, wt_ref, sems, *, tm):
    gbuf = (gbuf0, gbuf1)
    j = pl.program_id(0)
    last = pl.num_programs(0) - 1
    d = xs_ref.shape[1]

    def issue(src_ref, bi, b, t):
        for k in range(TOP_K):
            pltpu.make_async_copy(_tile_row(y_hbm, src_ref[bi, k, t]), _tile_row(gbuf[b], (k * tm + t) * SUBLANES),
                                  sems.at[b]).start(priority=k % 2)

    def drain(b):
        pltpu.make_async_copy(y_hbm.at[pl.ds(0, gbuf[b].shape[0]), :], gbuf[b], sems.at[b]).wait()

    def reduce_rows(b, r0):
        w8 = wt_ref[b, pl.ds(r0, _GROUP), :]
        acc = [jnp.zeros((_GROUP, LANES), F32) for _ in range(d // LANES)]
        for k in range(TOP_K):
            wk = w8[:, k:k + 1]
            for c in range(SUBLANES):
                word = gbuf[b][pl.ds((k * tm + r0) * SUBLANES + c, _GROUP, stride=SUBLANES), :]
                acc[c] = acc[c] + wk * pltpu.bitcast(word << 16, F32)
                acc[SUBLANES + c] = acc[SUBLANES + c] + wk * pltpu.bitcast(word & jnp.uint32(0xFFFF0000), F32)
        rows = pl.ds(b * tm + r0, _GROUP)
        g2 = mod_ref[0][5:6]
        x2 = jnp.concatenate(acc, axis=1) * g2 + xs_ref[rows, :]
        o_ref[rows, :] = _rms(x2, fn_ref[...])

    def phase(fin_b, issue_args):
        def body(g, carry):
            r0 = pl.multiple_of(g * _GROUP, _GROUP)
            if issue_args is not None:
                for dt in range(_GROUP):
                    issue(*issue_args, r0 + dt)
            reduce_rows(fin_b, r0)
            return carry

        lax.fori_loop(0, tm // _GROUP, body, 0)

    @pl.when(j == 0)
    def _():
        lax.fori_loop(0, tm, lambda t, c: (issue(slot_ref, 0, 0, t), c)[1], 0)

    for b in range(2):
        wt_ref[b] = jnp.concatenate([tw_ref[b], jnp.zeros((LANES - TOP_K, tm), F32)], axis=0).T
    drain(0)
    phase(0, (slot_ref, 1, 1))
    drain(1)
    pl.when(j < last)(lambda: phase(1, (nxt_ref, 0, 0)))
    pl.when(j == last)(lambda: phase(1, None))


def _combine(slot, tw, xs, mod8, fn, y, *, seq, tm):
    n, d = xs.shape
    nb = slot.shape[0]
    assert nb % 2 == 0 and (seq // tm) % 2 == 0
    per_b = seq // (2 * tm)
    pair = lambda j: (j, 0, 0)
    return pl.pallas_call(
        functools.partial(_combine_kernel, tm=tm),
        grid=(nb // 2,),
        in_specs=[pl.BlockSpec((2, TOP_K, tm), pair, memory_space=pltpu.SMEM),
                  pl.BlockSpec((1, TOP_K, tm), lambda j: (jnp.minimum(2 * j + 2, nb - 1), 0, 0),
                               memory_space=pltpu.SMEM),
                  pl.BlockSpec((2, TOP_K, tm), pair),
                  pl.BlockSpec((2 * tm, d), lambda j: (j, 0)),
                  pl.BlockSpec((1, 8, d), lambda j: (j // per_b, 0, 0)),
                  pl.BlockSpec((1, d), lambda j: (0, 0)),
                  pl.BlockSpec(memory_space=pl.ANY)],
        out_specs=pl.BlockSpec((2 * tm, d), lambda j: (j, 0)),
        out_shape=jax.ShapeDtypeStruct((n, d), F32),
        scratch_shapes=[pltpu.VMEM((TOP_K * tm * SUBLANES, y.shape[1]), y.dtype),
                        pltpu.VMEM((TOP_K * tm * SUBLANES, y.shape[1]), y.dtype),
                        pltpu.VMEM((2, tm, LANES), F32), pltpu.SemaphoreType.DMA((2,))],
        compiler_params=_cparams("arbitrary"),
        name="combine",
    )(slot, slot, tw, xs, mod8, fn, y)


def _expert_runs(counts, n_blocks, blk):
    padded = (counts + blk - 1) // blk * blk
    pad_end = jnp.cumsum(padded)
    run_start = jnp.concatenate([pad_end - padded, pad_end[-1:]]).astype(jnp.int32)
    block_row = jnp.arange(n_blocks, dtype=jnp.int32) * blk
    block_e = jnp.minimum(jnp.sum(pad_end[None, :] <= block_row[:, None], axis=1), N_EXPERTS - 1).astype(jnp.int32)
    n_used = (pad_end[-1:] // blk).astype(jnp.int32)
    return run_start, block_e, n_used


def _pad_lanes(w, width):
    return jnp.pad(w, ((0, 0), (0, width - w.shape[1])))


def _tiles(seq):
    t = dict(tm_proj=512, tq=512, tt=256, tm_moe=512, blk=512)
    assert all(seq % v == 0 for k, v in t.items() if k != "blk")
    return t


def _layer(x, c, positions, ada_w, ada_b, norm1_w, w_in, q_a_norm_w, wq_b, kv_a_norm_w, wkv_b,
           gdn_conv_w, gdn_a_log, gdn_dt_bias, gdn_norm_w, w_out, norm2_w, router_w, router_bias,
           exp_w_gate, exp_w_up, exp_w_down, sh_w_gate, sh_w_up, sh_w_down, final_norm_w,
           *, tm_proj, tq, tt, tm_moe, blk):
    batch, seq, d = x.shape
    n = batch * seq
    x2 = x.reshape(n, d)

    c8 = jnp.zeros((8, d), F32).at[:batch].set(c)
    mod = _adaln(c8, ada_w, ada_b.reshape(1, -1))
    mod8 = jnp.pad(mod[:batch].reshape(batch, 6, d), ((0, 0), (0, 2), (0, 0)))

    hq = MLA_NOPE + MLA_ROPE
    w_cq, w_ckv, w_kpe, w_qkv, w_z, w_a, w_b = jnp.split(
        w_in, [_C_KV, _C_KV + MLA_KV_RANK, _C_KV + MLA_KV_RANK + MLA_ROPE,
               _C_KV + MLA_KV_RANK + MLA_ROPE + _GDN_QKV,
               _C_KV + MLA_KV_RANK + MLA_ROPE + _GDN_QKV + GDN_HEADS * GDN_DV,
               _C_KV + MLA_KV_RANK + MLA_ROPE + _GDN_QKV + GDN_HEADS * GDN_DV + GDN_HEADS], axis=1)
    win = jnp.concatenate([w_cq, w_ckv, _pad_lanes(w_kpe, LANES), w_qkv, w_z,
                           _pad_lanes(jnp.concatenate([w_a, w_b], axis=1), LANES)], axis=1).astype(BF16)
    wq = jnp.pad(wq_b.reshape(MLA_Q_RANK, MLA_HEADS, hq),
                 ((0, 0), (0, 0), (0, MLA_QK_PAD - hq))).reshape(MLA_Q_RANK, MLA_HEADS * MLA_QK_PAD).astype(BF16)
    wkv4 = wkv_b.reshape(MLA_KV_RANK, MLA_HEADS, MLA_NOPE + MLA_V)
    wkv = jnp.concatenate([wkv4[:, :, :MLA_NOPE].reshape(MLA_KV_RANK, -1),
                           wkv4[:, :, MLA_NOPE:].reshape(MLA_KV_RANK, -1)], axis=1).astype(BF16)
    inv_freq = 1.0 / (ROPE_THETA ** (jnp.arange(0, MLA_ROPE, 2, dtype=F32) / MLA_ROPE))
    invf = _pad_lanes(jnp.concatenate([inv_freq, inv_freq])[None, :], LANES)

    q, k, v, qkv, z, ab = _proj(x2, mod8, positions.reshape(n, 1), norm1_w.reshape(1, d), win,
                                q_a_norm_w.reshape(1, -1), wq, kv_a_norm_w.reshape(1, -1), wkv, invf,
                                seq=seq, tm=tm_proj)
    out_a = _attn(q, k, v, batch=batch, seq=seq, tq=tq)

    cw8 = jnp.pad(gdn_conv_w, ((0, 8 - GDN_CONV), (0, 0)))
    hp = _pad_lanes(jnp.stack([gdn_a_log, gdn_dt_bias]), LANES)
    hp = jnp.pad(hp, ((0, 6), (0, 0)))
    out_b = _gdn(qkv, z, ab, cw8, hp, gdn_norm_w.reshape(1, -1), batch=batch, seq=seq, tt=tt)

    ha = MLA_HEADS * MLA_V
    xs, h2t, ids, tw, pos, counts = _mix(
        out_a, out_b, x2, mod8, w_out[:ha].astype(BF16), w_out[ha:].astype(BF16), norm2_w.reshape(1, d),
        router_w.T, router_bias.reshape(-1, 1), sh_w_gate.astype(BF16), sh_w_up.astype(BF16),
        sh_w_down.astype(BF16), seq=seq, tm=tm_moe)

    n_blocks = n * TOP_K // blk + N_EXPERTS
    run_start, block_e, n_used = _expert_runs(counts[:, 0].astype(jnp.int32), n_blocks, blk)
    slot = _slots(run_start, ids, pos)
    xsort = _dispatch(run_start, slot, h2t, n_slots=n_blocks * blk, tm=tm_moe, blk=blk)
    y = _experts(block_e, n_used, xsort, exp_w_gate, exp_w_up, exp_w_down, blk=blk)
    out = _combine(slot, tw, xs, mod8, final_norm_w.reshape(1, d), y, seq=seq, tm=tm_moe)
    return out.reshape(batch, seq, d)


def kernel(x, c, positions, ada_w, ada_b, norm1_w, w_in, q_a_norm_w, wq_b, kv_a_norm_w, wkv_b, gdn_conv_w,
           gdn_a_log, gdn_dt_bias, gdn_norm_w, w_out, norm2_w, router_w, router_bias, exp_w_gate, exp_w_up,
           exp_w_down, sh_w_gate, sh_w_up, sh_w_down, final_norm_w):
    assert ada_w.shape[0] == 1, "single layer"
    return _layer(
        x, c, positions, ada_w[0], ada_b[0], norm1_w[0], w_in[0], q_a_norm_w[0], wq_b[0], kv_a_norm_w[0],
        wkv_b[0], gdn_conv_w[0], gdn_a_log[0], gdn_dt_bias[0], gdn_norm_w[0], w_out[0], norm2_w[0],
        router_w[0], router_bias[0], exp_w_gate[0], exp_w_up[0], exp_w_down[0], sh_w_gate[0], sh_w_up[0],
        sh_w_down[0], final_norm_w, **_tiles(x.shape[1]))
```

```python
import functools

import jax
import jax.numpy as jnp
from jax import lax
from jax.experimental import pallas as pl
from jax.experimental.pallas import tpu as pltpu

F32 = jnp.float32
BF16 = jnp.bfloat16

CHUNK = 64
NORM_EPS = 1e-6
MLA_HEADS = 4
MLA_Q_RANK = 384
MLA_KV_RANK = 256
MLA_NOPE = 128
MLA_ROPE = 64
MLA_V = 128
ROPE_THETA = 10000.0
GDN_HEADS = 4
GDN_DK = 128
GDN_DV = 128
GDN_CONV = 4
N_EXPERTS = 64
N_GROUPS = 8
TOPK_GROUPS = 4
TOP_K = 8
ROUTED_SCALE = 2.5
LOG2E = 1.4426950408889634

LANES = 128
MLA_QK_PAD = 256
VMEM_LIMIT = 56 * 1024 * 1024


def _cparams(*sem):
    return pltpu.CompilerParams(dimension_semantics=sem, vmem_limit_bytes=VMEM_LIMIT)


def _dot(a, b):
    return jnp.dot(a, b, preferred_element_type=F32)


def _split3(a):
    hi = a.astype(BF16)
    r = a - hi.astype(F32)
    mid = r.astype(BF16)
    return hi, mid, (r - mid.astype(F32)).astype(BF16)


def _dot_x3(a, b):
    ah, am, _ = a
    bh, bm, _ = b
    return _dot(ah, bh) + (_dot(ah, bm) + _dot(am, bh))


def _dot_nt(a, b):
    return lax.dot_general(a, b, (((1,), (1,)), ((), ())), preferred_element_type=F32)


def _dot_tn(a, b):
    return lax.dot_general(a, b, (((0,), (0,)), ((), ())), preferred_element_type=F32)


def _silu(x):
    return x * jax.nn.sigmoid(x)


def _rms(x, w):
    return x * lax.rsqrt(jnp.mean(x * x, axis=-1, keepdims=True) + NORM_EPS) * w


def _adaln_kernel(c_ref, w_ref, b_ref, o_ref):
    a = _silu(c_ref[...]).astype(BF16)
    o_ref[...] = _dot(a, w_ref[...].astype(BF16)) + b_ref[...]


def _adaln(c8, ada_w, ada_b):
    d = c8.shape[1]
    n_out = ada_w.shape[1]
    return pl.pallas_call(
        _adaln_kernel,
        grid=(n_out // d,),
        in_specs=[pl.BlockSpec((8, d), lambda j: (0, 0)),
                  pl.BlockSpec((d, d), lambda j: (0, j)),
                  pl.BlockSpec((1, d), lambda j: (0, j))],
        out_specs=pl.BlockSpec((8, d), lambda j: (0, j)),
        out_shape=jax.ShapeDtypeStruct((8, n_out), F32),
        compiler_params=_cparams("parallel"),
        name="adaln",
    )(c8, ada_w, ada_b)


_C_Q = 0
_C_KV = _C_Q + MLA_Q_RANK
_C_KPE = _C_KV + MLA_KV_RANK
_C_QKV = _C_KPE + LANES
_GDN_QKV = GDN_HEADS * (2 * GDN_DK + GDN_DV)
_C_Z = _C_QKV + _GDN_QKV
_C_AB = _C_Z + GDN_HEADS * GDN_DV
_D_IN_PAD = _C_AB + LANES


def _rope(xb, cos, sin):
    half = MLA_ROPE // 2
    lane = lax.broadcasted_iota(jnp.int32, xb.shape, 1)
    rot = jnp.where(lane < half, -pltpu.roll(xb, LANES - half, 1), pltpu.roll(xb, half, 1))
    return xb * cos + rot * sin


def _proj_kernel(x_ref, mod_ref, pos_ref, n1_ref, win_ref, qan_ref, wq_ref, kvan_ref, wkv_ref, invf_ref,
                 q_ref, k_ref, v_ref, qkv_ref, z_ref, ab_ref):
    x = x_ref[...]
    mod = mod_ref[0]
    sh1, sc1 = mod[0:1], mod[1:2]
    h = (_rms(x, n1_ref[...]) * (1.0 + sc1) + sh1).astype(BF16)
    proj = _dot(h, win_ref[...])

    ang = pos_ref[...].astype(F32) * invf_ref[...]
    cos, sin = jnp.cos(ang), jnp.sin(ang)

    cq = proj[:, _C_Q:_C_Q + MLA_Q_RANK]
    qn = _rms(cq, qan_ref[...]).astype(BF16)
    q = _dot(qn, wq_ref[...]) * ((MLA_NOPE + MLA_ROPE) ** -0.5 * LOG2E)
    for hh in range(MLA_HEADS):
        c0 = hh * MLA_QK_PAD
        q_ref[:, c0:c0 + MLA_NOPE] = q[:, c0:c0 + MLA_NOPE].astype(BF16)
        q_ref[:, c0 + MLA_NOPE:c0 + MLA_QK_PAD] = _rope(q[:, c0 + MLA_NOPE:c0 + MLA_QK_PAD], cos, sin).astype(BF16)

    ckv = proj[:, _C_KV:_C_KV + MLA_KV_RANK]
    kvn = _rms(ckv, kvan_ref[...]).astype(BF16)
    kv = _dot(kvn, wkv_ref[...])
    kpe = _rope(proj[:, _C_KPE:_C_KPE + LANES], cos, sin).astype(BF16)
    for hh in range(MLA_HEADS):
        c0 = hh * MLA_QK_PAD
        k_ref[:, c0:c0 + MLA_NOPE] = kv[:, hh * MLA_NOPE:(hh + 1) * MLA_NOPE].astype(BF16)
        k_ref[:, c0 + MLA_NOPE:c0 + MLA_QK_PAD] = kpe
    v_ref[...] = kv[:, MLA_HEADS * MLA_NOPE:].astype(BF16)

    qkv_ref[...] = proj[:, _C_QKV:_C_QKV + _GDN_QKV].astype(qkv_ref.dtype)
    z_ref[...] = proj[:, _C_Z:_C_Z + GDN_HEADS * GDN_DV].astype(z_ref.dtype)
    ab_ref[...] = proj[:, _C_AB:_C_AB + LANES]


def _proj(x2, mod8, pos2, n1, win, qan, wq, kvan, wkv, invf, *, seq, tm):
    n, d = x2.shape
    per_b = seq // tm
    row = lambda i: (i, 0)
    const = lambda i: (0, 0)
    hq = MLA_HEADS * MLA_QK_PAD
    hv = MLA_HEADS * MLA_V
    return pl.pallas_call(
        _proj_kernel,
        grid=(n // tm,),
        in_specs=[pl.BlockSpec((tm, d), row),
                  pl.BlockSpec((1, 8, d), lambda i: (i // per_b, 0, 0)),
                  pl.BlockSpec((tm, 1), row),
                  pl.BlockSpec((1, d), const),
                  pl.BlockSpec(win.shape, const),
                  pl.BlockSpec(qan.shape, const),
                  pl.BlockSpec(wq.shape, const),
                  pl.BlockSpec(kvan.shape, const),
                  pl.BlockSpec(wkv.shape, const),
                  pl.BlockSpec(invf.shape, const)],
        out_specs=[pl.BlockSpec((tm, hq), row), pl.BlockSpec((tm, hq), row), pl.BlockSpec((tm, hv), row),
                   pl.BlockSpec((tm, _GDN_QKV), row), pl.BlockSpec((tm, GDN_HEADS * GDN_DV), row),
                   pl.BlockSpec((tm, LANES), row)],
        out_shape=[jax.ShapeDtypeStruct((n, hq), BF16), jax.ShapeDtypeStruct((n, hq), BF16),
                   jax.ShapeDtypeStruct((n, hv), BF16), jax.ShapeDtypeStruct((n, _GDN_QKV), BF16),
                   jax.ShapeDtypeStruct((n, GDN_HEADS * GDN_DV), BF16), jax.ShapeDtypeStruct((n, LANES), F32)],
        compiler_params=_cparams("parallel"),
        name="proj",
    )(x2, mod8, pos2, n1, win, qan, wq, kvan, wkv, invf)


_Q_TILES = 2


def _attn_kernel(q_ref, k_ref, v_ref, o_ref, *, tq):
    i = pl.program_id(1)
    chains = [(qt, hh) for qt in range(_Q_TILES) for hh in range(MLA_HEADS)]

    ones_col = jnp.where(lax.broadcasted_iota(jnp.int32, (tq, MLA_V), 1) == 0, 1.0, 0.0).astype(BF16)

    def chain_step(qt, hh, r0, carry, mask):
        m, acc = carry
        s = _dot_nt(q_ref[qt * tq:(qt + 1) * tq, hh * MLA_QK_PAD:(hh + 1) * MLA_QK_PAD],
                    k_ref[pl.ds(r0, tq), hh * MLA_QK_PAD:(hh + 1) * MLA_QK_PAD])
        if mask is not None:
            s = jnp.where(mask, s, -jnp.inf)
        m_new = jnp.maximum(m, jnp.max(s, axis=-1, keepdims=True))
        alpha = jnp.exp2(m - m_new)
        p = jnp.exp2((s - m_new).astype(BF16))
        v_ext = jnp.concatenate([v_ref[pl.ds(r0, tq), hh * MLA_V:(hh + 1) * MLA_V], ones_col], axis=1)
        return m_new, alpha * acc + _dot(p, v_ext)

    def block(r0, carry, masks):
        return tuple(carry[c] if masks[qt] is False else chain_step(qt, hh, r0, carry[c], masks[qt])
                     for c, (qt, hh) in enumerate(chains))

    init = tuple((jnp.full((tq, 1), -jnp.inf, F32), jnp.zeros((tq, 2 * MLA_V), F32)) for _ in chains)
    carry = lax.fori_loop(0, _Q_TILES * i, lambda j, c: block(pl.multiple_of(j * tq, tq), c, (None, None)), init)
    rq = lax.broadcasted_iota(jnp.int32, (tq, tq), 0) // CHUNK
    ck = lax.broadcasted_iota(jnp.int32, (tq, tq), 1) // CHUNK
    diag = ck <= rq
    carry = block(pl.multiple_of(_Q_TILES * i * tq, tq), carry, (diag, None))
    carry = block(pl.multiple_of((_Q_TILES * i + 1) * tq, tq), carry, (False, diag))
    for c, (qt, hh) in enumerate(chains):
        _, acc = carry[c]
        o_ref[qt * tq:(qt + 1) * tq, hh * MLA_V:(hh + 1) * MLA_V] = (
            acc[:, :MLA_V] / acc[:, MLA_V:MLA_V + 1]).astype(o_ref.dtype)


def _attn(q, k, v, *, batch, seq, tq):
    n = q.shape[0]
    rows = _Q_TILES * tq
    nq = seq // rows
    return pl.pallas_call(
        functools.partial(_attn_kernel, tq=tq),
        grid=(batch, nq),
        in_specs=[pl.BlockSpec((rows, q.shape[1]), lambda b, i: (b * nq + i, 0)),
                  pl.BlockSpec((seq, k.shape[1]), lambda b, i: (b, 0)),
                  pl.BlockSpec((seq, v.shape[1]), lambda b, i: (b, 0))],
        out_specs=pl.BlockSpec((rows, v.shape[1]), lambda b, i: (b * nq + i, 0)),
        out_shape=jax.ShapeDtypeStruct((n, v.shape[1]), BF16),
        compiler_params=_cparams("parallel", "arbitrary"),
        name="attn",
    )(q, k, v)


_SUPER = 2 * CHUNK


def _unit_lower_inverses(lows):
    n = lows[0].shape[0]
    eye = (lax.broadcasted_iota(jnp.int32, (n, n), 0) == lax.broadcasted_iota(jnp.int32, (n, n), 1)).astype(F32)
    levels = CHUNK.bit_length() - 1
    ps = [-low for low in lows]
    ts = [eye + p for p in ps]
    for level in range(1, levels):
        parts = [_split3(p) for p in ps]
        ps = [_dot_x3(s, s) for s in parts]
        if level == 1:
            continue
        ts = [t + _dot_x3(s, _split3(t)) for t, s in zip(ts, parts)]
    return [t + _dot_x3(_split3(p), _split3(t)) for t, p in zip(ts, ps)]


def _gdn_kernel(qkv_ref, z_ref, ab_ref, cw_ref, hp_ref, nw_ref, o_ref, xe_ref, st_ref, *, tt):
    j = pl.program_id(1)
    hd = GDN_HEADS * GDN_DK

    @pl.when(j == 0)
    def _():
        xe_ref[0:8, :] = jnp.zeros((8, xe_ref.shape[1]), F32)
        st_ref[...] = jnp.zeros(st_ref.shape, F32)

    xe_ref[8:8 + tt, :] = qkv_ref[...].astype(F32)
    cw = cw_ref[...]
    y = xe_ref[8:8 + tt, :] * cw[GDN_CONV - 1:GDN_CONV]
    for i in range(1, GDN_CONV):
        y = y + xe_ref[8 - i:8 - i + tt, :] * cw[GDN_CONV - 1 - i:GDN_CONV - i]
    xe_ref[0:8, :] = xe_ref[tt:tt + 8, :]
    act = _silu(y)

    ab = ab_ref[...]
    hp = hp_ref[...]
    pre = ab + hp[1:2]
    softplus = jnp.maximum(pre, 0.0) + jnp.log1p(jnp.exp(-jnp.abs(pre)))
    g_all = -jnp.exp(hp[0:1]) * softplus
    beta_all = jax.nn.sigmoid(ab)

    ri = lax.broadcasted_iota(jnp.int32, (tt, tt), 0)
    ci = lax.broadcasted_iota(jnp.int32, (tt, tt), 1)
    tri = ((ri // CHUNK == ci // CHUNK) & (ci <= ri)).astype(BF16)
    gc = sum(_dot(tri, part) for part in _split3(g_all))
    gct = gc.T

    rs = lax.broadcasted_iota(jnp.int32, (_SUPER, _SUPER), 0)
    cs = lax.broadcasted_iota(jnp.int32, (_SUPER, _SUPER), 1)
    same = rs // CHUNK == cs // CHUNK
    incl = same & (cs <= rs)
    strict = same & (cs < rs)

    tiles = [(s * _SUPER, hh) for s in range(tt // _SUPER) for hh in range(GDN_HEADS)]
    pre_t = []
    for r0, hh in tiles:
        qh = act[r0:r0 + _SUPER, hh * GDN_DK:(hh + 1) * GDN_DK]
        kh = act[r0:r0 + _SUPER, hd + hh * GDN_DK:hd + (hh + 1) * GDN_DK]
        vh = act[r0:r0 + _SUPER, 2 * hd + hh * GDN_DV:2 * hd + (hh + 1) * GDN_DV]
        qh = qh * lax.rsqrt(jnp.sum(qh * qh, axis=-1, keepdims=True) + 1e-6) * (GDN_DK ** -0.5)
        kh = kh * lax.rsqrt(jnp.sum(kh * kh, axis=-1, keepdims=True) + 1e-6)
        beta = beta_all[r0:r0 + _SUPER, GDN_HEADS + hh:GDN_HEADS + hh + 1]
        gcol = gc[r0:r0 + _SUPER, hh:hh + 1]
        grow = gct[hh:hh + 1, r0:r0 + _SUPER]
        decay = jnp.exp(jnp.where(incl, gcol - grow, -jnp.inf))
        eg = jnp.exp(gcol)
        kb = kh * beta
        pre_t.append(dict(kh=kh, khb=kh.astype(BF16), kbb=kb.astype(BF16), vbb=(vh * beta).astype(BF16),
                          kgb=(kb * eg).astype(BF16), qb=qh.astype(BF16), qg=qh * eg, gcol=gcol, decay=decay))
    lows = [jnp.where(strict, _dot_nt(p["kbb"], p["khb"]) * p["decay"], 0.0) for p in pre_t]
    tinvs = [t.astype(BF16) for t in _unit_lower_inverses(lows)]
    us = [_dot(t, p["vbb"]).astype(BF16) for t, p in zip(tinvs, pre_t)]
    ws = [_dot(t, p["kgb"]).astype(BF16) for t, p in zip(tinvs, pre_t)]
    atts = [(_dot_nt(p["qb"], p["khb"]) * p["decay"]).astype(BF16) for p in pre_t]

    steps = []
    for cc in range(tt // CHUNK):
        r0 = cc * CHUNK
        a0 = r0 % _SUPER
        for hh in range(GDN_HEADS):
            ti = (r0 // _SUPER) * GDN_HEADS + hh
            p = pre_t[ti]
            gl = gc[r0 + CHUNK - 1:r0 + CHUNK, hh:hh + 1]
            kd = (p["kh"][a0:a0 + CHUNK] * jnp.exp(gl - p["gcol"][a0:a0 + CHUNK])).astype(BF16)
            u_c, w_c = us[ti][a0:a0 + CHUNK], ws[ti][a0:a0 + CHUNK]
            att_c = atts[ti][a0:a0 + CHUNK, a0:a0 + CHUNK]
            steps.append(dict(hh=hh, r0=r0, gain=jnp.exp(gl), kw=_dot_tn(kd, w_c).astype(BF16), ku=_dot_tn(kd, u_c),
                              qs=(p["qg"][a0:a0 + CHUNK] - _dot(att_c, w_c)).astype(BF16), ou=_dot(att_c, u_c)))

    states = [st_ref[hh] for hh in range(GDN_HEADS)]
    for st in steps:
        hh, r0 = st["hh"], st["r0"]
        sb = states[hh].astype(BF16)
        o = _dot(st["qs"], sb) + st["ou"]
        states[hh] = states[hh] * st["gain"] + (st["ku"] - _dot(st["kw"], sb))
        zz = z_ref[r0:r0 + CHUNK, hh * GDN_DV:(hh + 1) * GDN_DV].astype(F32)
        o_ref[r0:r0 + CHUNK, hh * GDN_DV:(hh + 1) * GDN_DV] = (_rms(o, nw_ref[...]) * _silu(zz)).astype(o_ref.dtype)
    for hh in range(GDN_HEADS):
        st_ref[hh] = states[hh]


def _gdn(qkv, z, ab, cw8, hp, nw, *, batch, seq, tt):
    n = qkv.shape[0]
    per_b = seq // tt
    row = lambda b, j: (b * per_b + j, 0)
    const = lambda b, j: (0, 0)
    hv = GDN_HEADS * GDN_DV
    return pl.pallas_call(
        functools.partial(_gdn_kernel, tt=tt),
        grid=(batch, per_b),
        in_specs=[pl.BlockSpec((tt, _GDN_QKV), row), pl.BlockSpec((tt, hv), row), pl.BlockSpec((tt, LANES), row),
                  pl.BlockSpec(cw8.shape, const), pl.BlockSpec(hp.shape, const), pl.BlockSpec(nw.shape, const)],
        out_specs=pl.BlockSpec((tt, hv), row),
        out_shape=jax.ShapeDtypeStruct((n, hv), BF16),
        scratch_shapes=[pltpu.VMEM((tt + 8, _GDN_QKV), F32), pltpu.VMEM((GDN_HEADS, GDN_DK, GDN_DV), F32)],
        compiler_params=_cparams("parallel", "arbitrary"),
        name="gdn",
    )(qkv, z, ab, cw8, hp, nw)


def _route(logits_t, bias_col):
    e, tm = logits_t.shape
    gsz = e // N_GROUPS
    scores = jax.nn.sigmoid(logits_t)
    biased = scores + bias_col
    sub = lax.broadcasted_iota(jnp.int32, (gsz, tm), 0)
    rows = []
    for g in range(N_GROUPS):
        blk = biased[g * gsz:(g + 1) * gsz]
        m1 = jnp.max(blk, axis=0, keepdims=True)
        first = jnp.min(jnp.where(blk == m1, sub, gsz), axis=0, keepdims=True)
        m2 = jnp.max(jnp.where(sub == first, -jnp.inf, blk), axis=0, keepdims=True)
        rows.append(m1 + m2)
    gs = jnp.concatenate(rows, axis=0)
    gi = lax.broadcasted_iota(jnp.int32, (N_GROUPS, tm), 0)
    grank = jnp.zeros((N_GROUPS, tm), F32)
    for g in range(N_GROUPS):
        r = gs[g:g + 1]
        grank = grank + jnp.where(r > gs, 1.0, jnp.where(r == gs, (gi > g).astype(F32), 0.0))
    gsel = grank < TOPK_GROUPS
    masked = jnp.concatenate(
        [jnp.where(gsel[g:g + 1], biased[g * gsz:(g + 1) * gsz], -jnp.inf) for g in range(N_GROUPS)], axis=0)
    ei = lax.broadcasted_iota(jnp.int32, (e, tm), 0)
    picks = []
    for _ in range(TOP_K):
        best = jnp.max(masked, axis=0, keepdims=True)
        pick = jnp.min(jnp.where(masked == best, ei, e), axis=0, keepdims=True)
        picks.append(pick)
        masked = jnp.where(ei == pick, -jnp.inf, masked)
    return scores, picks


SUBLANES = 4


def _store_tile_rows(ref, x):
    c = x.shape[1] // 2
    lo = pltpu.bitcast(x[:, :c].astype(BF16).astype(F32), jnp.uint32)
    hi = pltpu.bitcast(x[:, c:].astype(BF16).astype(F32), jnp.uint32)
    w = (hi & jnp.uint32(0xFFFF0000)) | (lo >> 16)
    for j in range(SUBLANES):
        ref[pl.ds(j, x.shape[0], stride=SUBLANES), :] = w[:, j * LANES:(j + 1) * LANES]


def _load_tile_rows(ref, r0, rows):
    w = jnp.concatenate(
        [ref[pl.ds(r0 * SUBLANES + j, rows, stride=SUBLANES), :] for j in range(SUBLANES)], axis=1)
    return jnp.concatenate([pltpu.bitcast(w << 16, F32), pltpu.bitcast(w & jnp.uint32(0xFFFF0000), F32)], axis=1)


def _tile_row(ref, sublane_offset):
    return ref.at[pl.ds(pl.multiple_of(sublane_offset, SUBLANES), SUBLANES), :]


def _mix_kernel(oa_ref, ob_ref, x_ref, mod_ref, woa_ref, wob_ref, n2_ref, rwt_ref, rb_ref, sg_ref, su_ref, sd_ref,
                xs_ref, h2t_ref, te_ref, tw_ref, tp_ref, cnt_ref, run_ref):
    @pl.when(pl.program_id(0) == 0)
    def _():
        run_ref[...] = jnp.zeros(run_ref.shape, F32)

    mod = mod_ref[0]
    g1, sh2, sc2, g2 = mod[2:3], mod[3:4], mod[4:5], mod[5:6]
    mix = _dot(oa_ref[...], woa_ref[...]) + _dot(ob_ref[...], wob_ref[...])
    x1 = x_ref[...] + g1 * mix
    h2 = _rms(x1, n2_ref[...]) * (1.0 + sc2) + sh2
    _store_tile_rows(h2t_ref, h2)
    h2b = h2.astype(BF16)
    hs = (_silu(_dot(h2b, sg_ref[...])) * _dot(h2b, su_ref[...])).astype(BF16)
    xs_ref[...] = x1 + g2 * _dot(hs, sd_ref[...])

    rh, rm, _ = _split3(rwt_ref[...])
    hh, hm, _ = _split3(h2)
    scores, picks = _route(_dot_nt(rh, hh) + (_dot_nt(rh, hm) + _dot_nt(rm, hh)), rb_ref[...])
    e, tm = scores.shape
    ei = lax.broadcasted_iota(jnp.int32, (e, tm), 0)
    hits = [ei == pick for pick in picks]
    sel = jnp.zeros((e, tm), F32)
    for hit in hits:
        sel = sel + jnp.where(hit, 1.0, 0.0)
    earlier = (lax.broadcasted_iota(jnp.int32, (tm, tm), 0) < lax.broadcasted_iota(jnp.int32, (tm, tm), 1))
    posmat = _dot(sel.astype(BF16), earlier.astype(BF16)) + run_ref[...]
    wts = jnp.concatenate([jnp.sum(jnp.where(hit, scores, 0.0), axis=0, keepdims=True) for hit in hits], axis=0)
    pos = [jnp.sum(jnp.where(hit, posmat, 0.0), axis=0, keepdims=True) for hit in hits]
    te_ref[0] = jnp.concatenate(picks, axis=0)
    tw_ref[0] = wts / jnp.sum(wts, axis=0, keepdims=True) * ROUTED_SCALE
    tp_ref[0] = jnp.concatenate(pos, axis=0).astype(jnp.int32)
    run_ref[...] = run_ref[...] + jnp.sum(sel, axis=1, keepdims=True)
    cnt_ref[...] = jnp.broadcast_to(run_ref[...], cnt_ref.shape)


def _mix(oa, ob, x2, mod8, woa, wob, n2, rwt, rb, sg, su, sd, *, seq, tm):
    n, d = x2.shape
    per_b = seq // tm
    nb = n // tm
    row = lambda i: (i, 0)
    blk3 = lambda i: (i, 0, 0)
    const = lambda i: (0, 0)
    return pl.pallas_call(
        _mix_kernel,
        grid=(nb,),
        in_specs=[pl.BlockSpec((tm, oa.shape[1]), row), pl.BlockSpec((tm, ob.shape[1]), row),
                  pl.BlockSpec((tm, d), row), pl.BlockSpec((1, 8, d), lambda i: (i // per_b, 0, 0)),
                  pl.BlockSpec(woa.shape, const), pl.BlockSpec(wob.shape, const), pl.BlockSpec((1, d), const),
                  pl.BlockSpec(rwt.shape, const), pl.BlockSpec(rb.shape, const),
                  pl.BlockSpec(sg.shape, const), pl.BlockSpec(su.shape, const), pl.BlockSpec(sd.shape, const)],
        out_specs=[pl.BlockSpec((tm, d), row), pl.BlockSpec((tm * SUBLANES, LANES), row),
                   pl.BlockSpec((1, TOP_K, tm), blk3), pl.BlockSpec((1, TOP_K, tm), blk3),
                   pl.BlockSpec((1, TOP_K, tm), blk3), pl.BlockSpec((N_EXPERTS, LANES), const)],
        out_shape=[jax.ShapeDtypeStruct((n, d), F32), jax.ShapeDtypeStruct((n * SUBLANES, LANES), jnp.uint32),
                   jax.ShapeDtypeStruct((nb, TOP_K, tm), jnp.int32), jax.ShapeDtypeStruct((nb, TOP_K, tm), F32),
                   jax.ShapeDtypeStruct((nb, TOP_K, tm), jnp.int32),
                   jax.ShapeDtypeStruct((N_EXPERTS, LANES), F32)],
        scratch_shapes=[pltpu.VMEM((N_EXPERTS, 1), F32)],
        compiler_params=_cparams("arbitrary"),
        name="mix",
    )(oa, ob, x2, mod8, woa, wob, n2, rwt, rb, sg, su, sd)


def _slots_kernel(ps_ref, ids_ref, pos_ref, o_ref):
    ids = ids_ref[...]
    start = jnp.zeros(ids.shape, jnp.int32)
    for e in range(N_EXPERTS):
        start = jnp.where(ids == e, ps_ref[e], start)
    o_ref[...] = (start + pos_ref[...]) * SUBLANES


def _slots(ps, ids, pos):
    spec = pl.BlockSpec(ids.shape, lambda i, ps: (0, 0, 0))
    return pl.pallas_call(
        _slots_kernel,
        grid_spec=pltpu.PrefetchScalarGridSpec(num_scalar_prefetch=1, grid=(1,),
                                               in_specs=[spec, spec], out_specs=spec),
        out_shape=jax.ShapeDtypeStruct(ids.shape, jnp.int32),
        compiler_params=_cparams("parallel"),
        name="slots",
    )(ps, ids, pos)


def _dispatch_kernel(ps_ref, slot_ref, h2t_ref, xs_hbm, zbuf, ring, zsem, sems, *, tm, blk, n_steps):
    i = pl.program_id(0)

    @pl.when(i == 0)
    def _():
        zbuf[...] = jnp.zeros(zbuf.shape, zbuf.dtype)

        def block_at(row):
            return xs_hbm.at[pl.ds(pl.multiple_of(row * SUBLANES, blk * SUBLANES), blk * SUBLANES), :]

        def tail(e):
            end = ps_ref[e + 1]
            return end > ps_ref[e], pltpu.make_async_copy(zbuf, block_at(end - blk), zsem)

        def unused(e):
            row = ps_ref[N_EXPERTS] + e * blk
            return row * SUBLANES < xs_hbm.shape[0], pltpu.make_async_copy(zbuf, block_at(row), zsem)

        for fill in (tail, unused):
            for e in range(N_EXPERTS):
                needed, cp = fill(e)
                pl.when(needed)(cp.start)
        for fill in (tail, unused):
            for e in range(N_EXPERTS):
                needed, cp = fill(e)
                pl.when(needed)(cp.wait)

    def send(b):
        ring[b] = h2t_ref[...]

        def issue(t, carry):
            for k in range(TOP_K):
                pltpu.make_async_copy(_tile_row(ring.at[b], t * SUBLANES), _tile_row(xs_hbm, slot_ref[0, k, t]),
                                      sems.at[b]).start(priority=k % 2)
            return carry

        lax.fori_loop(0, tm, issue, 0)

    def drain(b):
        for k in range(TOP_K):
            pltpu.make_async_copy(ring.at[b], xs_hbm.at[pl.ds(0, tm * SUBLANES), :], sems.at[b]).wait()

    for b in range(2):
        @pl.when(i % 2 == b)
        def _(b=b):
            send(b)

            @pl.when(i > 0)
            def _():
                drain(1 - b)

    @pl.when(i == n_steps - 1)
    def _():
        drain((n_steps - 1) % 2)


def _dispatch(ps, slot, h2t, *, n_slots, tm, blk):
    lanes = h2t.shape[1]
    n_steps = h2t.shape[0] // (tm * SUBLANES)
    grid_spec = pltpu.PrefetchScalarGridSpec(
        num_scalar_prefetch=1,
        grid=(n_steps,),
        in_specs=[pl.BlockSpec((1, TOP_K, tm), lambda i, ps: (i, 0, 0), memory_space=pltpu.SMEM),
                  pl.BlockSpec((tm * SUBLANES, lanes), lambda i, ps: (i, 0))],
        out_specs=pl.BlockSpec(memory_space=pl.ANY),
        scratch_shapes=[pltpu.VMEM((blk * SUBLANES, lanes), jnp.uint32),
                        pltpu.VMEM((2, tm * SUBLANES, lanes), jnp.uint32),
                        pltpu.SemaphoreType.DMA(()), pltpu.SemaphoreType.DMA((2,))],
    )
    return pl.pallas_call(
        functools.partial(_dispatch_kernel, tm=tm, blk=blk, n_steps=n_steps),
        grid_spec=grid_spec,
        out_shape=jax.ShapeDtypeStruct((n_slots * SUBLANES, lanes), jnp.uint32),
        compiler_params=_cparams("arbitrary"),
        name="dispatch",
    )(ps, slot, h2t)


def _experts_kernel(be_ref, nu_ref, x_ref, wg_ref, wu_ref, wd_ref, y_ref, wgb, wub, wdb):
    i = pl.program_id(0)
    used = i < nu_ref[0]
    new_expert = jnp.logical_or(i == 0, be_ref[i] != be_ref[jnp.maximum(i - 1, 0)])

    @pl.when(jnp.logical_and(used, new_expert))
    def _():
        wgb[...] = wg_ref[0].astype(BF16)
        wub[...] = wu_ref[0].astype(BF16)
        wdb[...] = wd_ref[0].astype(BF16)

    @pl.when(used)
    def _():
        xb = _load_tile_rows(x_ref, 0, x_ref.shape[0] // SUBLANES).astype(BF16)
        hb = (_silu(_dot(xb, wgb[...])) * _dot(xb, wub[...])).astype(BF16)
        _store_tile_rows(y_ref, _dot(hb, wdb[...]))

    @pl.when(jnp.logical_not(used))
    def _():
        y_ref[...] = jnp.zeros(y_ref.shape, y_ref.dtype)


def _experts(block_e, n_used, xs, wg, wu, wd, *, blk):
    n_blocks = block_e.shape[0]
    rows_blk = (blk * SUBLANES, xs.shape[1])
    wsel = lambda i, be, nu: (be[i], 0, 0)
    grid_spec = pltpu.PrefetchScalarGridSpec(
        num_scalar_prefetch=2,
        grid=(n_blocks,),
        in_specs=[pl.BlockSpec(rows_blk, lambda i, be, nu: (jnp.minimum(i, nu[0] - 1), 0)),
                  pl.BlockSpec((1,) + wg.shape[1:], wsel), pl.BlockSpec((1,) + wu.shape[1:], wsel),
                  pl.BlockSpec((1,) + wd.shape[1:], wsel)],
        out_specs=pl.BlockSpec(rows_blk, lambda i, be, nu: (i, 0)),
        scratch_shapes=[pltpu.VMEM(wg.shape[1:], BF16), pltpu.VMEM(wu.shape[1:], BF16),
                        pltpu.VMEM(wd.shape[1:], BF16)],
    )
    return pl.pallas_call(
        _experts_kernel,
        grid_spec=grid_spec,
        out_shape=jax.ShapeDtypeStruct(xs.shape, xs.dtype),
        compiler_params=_cparams("arbitrary"),
        name="experts",
    )(block_e, n_used, xs, wg, wu, wd)


_GROUP = 32


def _combine_kernel(slot_ref, nxt_ref, tw_ref, xs_ref, mod_ref, fn_ref, y_hbm, o_ref, gbuf0, gbuf1, wt_ref, sems, *, tm):
    gbuf = (gbuf0, gbuf1)
    j = pl.program_id(0)
    last = pl.num_programs(0) - 1
    d = xs_ref.shape[1]

    def issue(src_ref, bi, b, t):
        for k in range(TOP_K):
            pltpu.make_async_copy(_tile_row(y_hbm, src_ref[bi, k, t]), _tile_row(gbuf[b], (k * tm + t) * SUBLANES),
                                  sems.at[b]).start(priority=k % 2)

    def drain(b):
        pltpu.make_async_copy(y_hbm.at[pl.ds(0, gbuf[b].shape[0]), :], gbuf[b], sems.at[b]).wait()

    def reduce_rows(b, r0):
        w8 = wt_ref[b, pl.ds(r0, _GROUP), :]
        acc = [jnp.zeros((_GROUP, LANES), F32) for _ in range(d // LANES)]
        for k in range(TOP_K):
            wk = w8[:, k:k + 1]
            for c in range(SUBLANES):
                word = gbuf[b][pl.ds((k * tm + r0) * SUBLANES + c, _GROUP, stride=SUBLANES), :]
                acc[c] = acc[c] + wk * pltpu.bitcast(word << 16, F32)
                acc[SUBLANES + c] = acc[SUBLANES + c] + wk * pltpu.bitcast(word & jnp.uint32(0xFFFF0000), F32)
        rows = pl.ds(b * tm + r0, _GROUP)
        g2 = mod_ref[0][5:6]
        x2 = jnp.concatenate(acc, axis=1) * g2 + xs_ref[rows, :]
        o_ref[rows, :] = _rms(x2, fn_ref[...])

    def phase(fin_b, issue_args):
        def body(g, carry):
            r0 = pl.multiple_of(g * _GROUP, _GROUP)
            if issue_args is not None:
                for dt in range(_GROUP):
                    issue(*issue_args, r0 + dt)
            reduce_rows(fin_b, r0)
            return carry

        lax.fori_loop(0, tm // _GROUP, body, 0)

    @pl.when(j == 0)
    def _():
        lax.fori_loop(0, tm, lambda t, c: (issue(slot_ref, 0, 0, t), c)[1], 0)

    for b in range(2):
        wt_ref[b] = jnp.concatenate([tw_ref[b], jnp.zeros((LANES - TOP_K, tm), F32)], axis=0).T
    drain(0)
    phase(0, (slot_ref, 1, 1))
    drain(1)
    pl.when(j < last)(lambda: phase(1, (nxt_ref, 0, 0)))
    pl.when(j == last)(lambda: phase(1, None))


def _combine(slot, tw, xs, mod8, fn, y, *, seq, tm):
    n, d = xs.shape
    nb = slot.shape[0]
    assert nb % 2 == 0 and (seq // tm) % 2 == 0
    per_b = seq // (2 * tm)
    pair = lambda j: (j, 0, 0)
    return pl.pallas_call(
        functools.partial(_combine_kernel, tm=tm),
        grid=(nb // 2,),
        in_specs=[pl.BlockSpec((2, TOP_K, tm), pair, memory_space=pltpu.SMEM),
                  pl.BlockSpec((1, TOP_K, tm), lambda j: (jnp.minimum(2 * j + 2, nb - 1), 0, 0),
                               memory_space=pltpu.SMEM),
                  pl.BlockSpec((2, TOP_K, tm), pair),
                  pl.BlockSpec((2 * tm, d), lambda j: (j, 0)),
                  pl.BlockSpec((1, 8, d), lambda j: (j // per_b, 0, 0)),
                  pl.BlockSpec((1, d), lambda j: (0, 0)),
                  pl.BlockSpec(memory_space=pl.ANY)],
        out_specs=pl.BlockSpec((2 * tm, d), lambda j: (j, 0)),
        out_shape=jax.ShapeDtypeStruct((n, d), F32),
        scratch_shapes=[pltpu.VMEM((TOP_K * tm * SUBLANES, y.shape[1]), y.dtype),
                        pltpu.VMEM((TOP_K * tm * SUBLANES, y.shape[1]), y.dtype),
                        pltpu.VMEM((2, tm, LANES), F32), pltpu.SemaphoreType.DMA((2,))],
        compiler_params=_cparams("arbitrary"),
        name="combine",
    )(slot, slot, tw, xs, mod8, fn, y)


def _expert_runs(counts, n_blocks, blk):
    padded = (counts + blk - 1) // blk * blk
    pad_end = jnp.cumsum(padded)
    run_start = jnp.concatenate([pad_end - padded, pad_end[-1:]]).astype(jnp.int32)
    block_row = jnp.arange(n_blocks, dtype=jnp.int32) * blk
    block_e = jnp.minimum(jnp.sum(pad_end[None, :] <= block_row[:, None], axis=1), N_EXPERTS - 1).astype(jnp.int32)
    n_used = (pad_end[-1:] // blk).astype(jnp.int32)
    return run_start, block_e, n_used


def _pad_lanes(w, width):
    return jnp.pad(w, ((0, 0), (0, width - w.shape[1])))


def _tiles(seq):
    t = dict(tm_proj=512, tq=512, tt=256, tm_moe=512, blk=512)
    assert all(seq % v == 0 for k, v in t.items() if k != "blk")
    assert seq % (_Q_TILES * t["tq"]) == 0 and (seq // t["tm_moe"]) % 2 == 0
    return t


def _layer(x, c, positions, ada_w, ada_b, norm1_w, w_in, q_a_norm_w, wq_b, kv_a_norm_w, wkv_b,
           gdn_conv_w, gdn_a_log, gdn_dt_bias, gdn_norm_w, w_out, norm2_w, router_w, router_bias,
           exp_w_gate, exp_w_up, exp_w_down, sh_w_gate, sh_w_up, sh_w_down, final_norm_w,
           *, tm_proj, tq, tt, tm_moe, blk):
    batch, seq, d = x.shape
    n = batch * seq
    x2 = x.reshape(n, d)

    c8 = jnp.zeros((8, d), F32).at[:batch].set(c)
    mod = _adaln(c8, ada_w, ada_b.reshape(1, -1))
    mod8 = jnp.pad(mod[:batch].reshape(batch, 6, d), ((0, 0), (0, 2), (0, 0)))

    hq = MLA_NOPE + MLA_ROPE
    w_cq, w_ckv, w_kpe, w_qkv, w_z, w_a, w_b = jnp.split(
        w_in, [_C_KV, _C_KV + MLA_KV_RANK, _C_KV + MLA_KV_RANK + MLA_ROPE,
               _C_KV + MLA_KV_RANK + MLA_ROPE + _GDN_QKV,
               _C_KV + MLA_KV_RANK + MLA_ROPE + _GDN_QKV + GDN_HEADS * GDN_DV,
               _C_KV + MLA_KV_RANK + MLA_ROPE + _GDN_QKV + GDN_HEADS * GDN_DV + GDN_HEADS], axis=1)
    win = jnp.concatenate([w_cq, w_ckv, _pad_lanes(w_kpe, LANES), w_qkv, w_z,
                           _pad_lanes(jnp.concatenate([w_a, w_b], axis=1), LANES)], axis=1).astype(BF16)
    wq = jnp.pad(wq_b.reshape(MLA_Q_RANK, MLA_HEADS, hq),
                 ((0, 0), (0, 0), (0, MLA_QK_PAD - hq))).reshape(MLA_Q_RANK, MLA_HEADS * MLA_QK_PAD).astype(BF16)
    wkv4 = wkv_b.reshape(MLA_KV_RANK, MLA_HEADS, MLA_NOPE + MLA_V)
    wkv = jnp.concatenate([wkv4[:, :, :MLA_NOPE].reshape(MLA_KV_RANK, -1),
                           wkv4[:, :, MLA_NOPE:].reshape(MLA_KV_RANK, -1)], axis=1).astype(BF16)
    inv_freq = 1.0 / (ROPE_THETA ** (jnp.arange(0, MLA_ROPE, 2, dtype=F32) / MLA_ROPE))
    invf = _pad_lanes(jnp.concatenate([inv_freq, inv_freq])[None, :], LANES)

    q, k, v, qkv, z, ab = _proj(x2, mod8, positions.reshape(n, 1), norm1_w.reshape(1, d), win,
                                q_a_norm_w.reshape(1, -1), wq, kv_a_norm_w.reshape(1, -1), wkv, invf,
                                seq=seq, tm=tm_proj)
    out_a = _attn(q, k, v, batch=batch, seq=seq, tq=tq)

    cw8 = jnp.pad(gdn_conv_w, ((0, 8 - GDN_CONV), (0, 0)))
    hp = _pad_lanes(jnp.stack([gdn_a_log, gdn_dt_bias]), LANES)
    hp = jnp.pad(hp, ((0, 6), (0, 0)))
    out_b = _gdn(qkv, z, ab, cw8, hp, gdn_norm_w.reshape(1, -1), batch=batch, seq=seq, tt=tt)

    ha = MLA_HEADS * MLA_V
    xs, h2t, ids, tw, pos, counts = _mix(
        out_a, out_b, x2, mod8, w_out[:ha].astype(BF16), w_out[ha:].astype(BF16), norm2_w.reshape(1, d),
        router_w.T, router_bias.reshape(-1, 1), sh_w_gate.astype(BF16), sh_w_up.astype(BF16),
        sh_w_down.astype(BF16), seq=seq, tm=tm_moe)

    n_blocks = n * TOP_K // blk + N_EXPERTS
    run_start, block_e, n_used = _expert_runs(counts[:, 0].astype(jnp.int32), n_blocks, blk)
    slot = _slots(run_start, ids, pos)
    xsort = _dispatch(run_start, slot, h2t, n_slots=n_blocks * blk, tm=tm_moe, blk=blk)
    y = _experts(block_e, n_used, xsort, exp_w_gate, exp_w_up, exp_w_down, blk=blk)
    out = _combine(slot, tw, xs, mod8, final_norm_w.reshape(1, d), y, seq=seq, tm=tm_moe)
    return out.reshape(batch, seq, d)


def kernel(x, c, positions, ada_w, ada_b, norm1_w, w_in, q_a_norm_w, wq_b, kv_a_norm_w, wkv_b, gdn_conv_w,
           gdn_a_log, gdn_dt_bias, gdn_norm_w, w_out, norm2_w, router_w, router_bias, exp_w_gate, exp_w_up,
           exp_w_down, sh_w_gate, sh_w_up, sh_w_down, final_norm_w):
    assert ada_w.shape[0] == 1, "single layer"
    return _layer(
        x, c, positions, ada_w[0], ada_b[0], norm1_w[0], w_in[0], q_a_norm_w[0], wq_b[0], kv_a_norm_w[0],
        wkv_b[0], gdn_conv_w[0], gdn_a_log[0], gdn_dt_bias[0], gdn_norm_w[0], w_out[0], norm2_w[0],
        router_w[0], router_bias[0], exp_w_gate[0], exp_w_up[0], exp_w_down[0], sh_w_gate[0], sh_w_up[0],
        sh_w_down[0], final_norm_w, **_tiles(x.shape[1]))
```

```python
import functools

import jax
import jax.numpy as jnp
from jax import lax
from jax.experimental import pallas as pl
from jax.experimental.pallas import tpu as pltpu

F32 = jnp.float32
BF16 = jnp.bfloat16

CHUNK = 64
NORM_EPS = 1e-6
MLA_HEADS = 4
MLA_Q_RANK = 384
MLA_KV_RANK = 256
MLA_NOPE = 128
MLA_ROPE = 64
MLA_V = 128
ROPE_THETA = 10000.0
GDN_HEADS = 4
GDN_DK = 128
GDN_DV = 128
GDN_CONV = 4
N_EXPERTS = 64
N_GROUPS = 8
TOPK_GROUPS = 4
TOP_K = 8
ROUTED_SCALE = 2.5
LOG2E = 1.4426950408889634

LANES = 128
MLA_QK_PAD = 256
VMEM_LIMIT = 56 * 1024 * 1024


def _cparams(*sem):
    return pltpu.CompilerParams(dimension_semantics=sem, vmem_limit_bytes=VMEM_LIMIT)


def _dot(a, b):
    return jnp.dot(a, b, preferred_element_type=F32)


def _split3(a):
    hi = a.astype(BF16)
    r = a - hi.astype(F32)
    mid = r.astype(BF16)
    return hi, mid, (r - mid.astype(F32)).astype(BF16)


def _dot_x3(a, b):
    ah, am, _ = a
    bh, bm, _ = b
    return _dot(ah, bh) + (_dot(ah, bm) + _dot(am, bh))


def _dot_nt(a, b):
    return lax.dot_general(a, b, (((1,), (1,)), ((), ())), preferred_element_type=F32)


def _dot_tn(a, b):
    return lax.dot_general(a, b, (((0,), (0,)), ((), ())), preferred_element_type=F32)


def _silu(x):
    return x * jax.nn.sigmoid(x)


def _rms(x, w):
    return x * lax.rsqrt(jnp.mean(x * x, axis=-1, keepdims=True) + NORM_EPS) * w


def _adaln_kernel(c_ref, w_ref, b_ref, o_ref):
    a = _silu(c_ref[...]).astype(BF16)
    o_ref[...] = _dot(a, w_ref[...].astype(BF16)) + b_ref[...]


def _adaln(c8, ada_w, ada_b):
    d = c8.shape[1]
    n_out = ada_w.shape[1]
    return pl.pallas_call(
        _adaln_kernel,
        grid=(n_out // d,),
        in_specs=[pl.BlockSpec((8, d), lambda j: (0, 0)),
                  pl.BlockSpec((d, d), lambda j: (0, j)),
                  pl.BlockSpec((1, d), lambda j: (0, j))],
        out_specs=pl.BlockSpec((8, d), lambda j: (0, j)),
        out_shape=jax.ShapeDtypeStruct((8, n_out), F32),
        compiler_params=_cparams("parallel"),
        name="adaln",
    )(c8, ada_w, ada_b)


_C_Q = 0
_C_KV = _C_Q + MLA_Q_RANK
_C_KPE = _C_KV + MLA_KV_RANK
_C_QKV = _C_KPE + LANES
_GDN_QKV = GDN_HEADS * (2 * GDN_DK + GDN_DV)
_C_Z = _C_QKV + _GDN_QKV
_C_AB = _C_Z + GDN_HEADS * GDN_DV
_D_IN_PAD = _C_AB + LANES


def _rope(xb, cos, sin):
    half = MLA_ROPE // 2
    lane = lax.broadcasted_iota(jnp.int32, xb.shape, 1)
    rot = jnp.where(lane < half, -pltpu.roll(xb, LANES - half, 1), pltpu.roll(xb, half, 1))
    return xb * cos + rot * sin


def _proj_kernel(x_ref, mod_ref, pos_ref, n1_ref, win_ref, qan_ref, wq_ref, kvan_ref, wkv_ref, invf_ref,
                 q_ref, k_ref, v_ref, qkv_ref, z_ref, ab_ref):
    x = x_ref[...]
    mod = mod_ref[0]
    sh1, sc1 = mod[0:1], mod[1:2]
    h = (_rms(x, n1_ref[...]) * (1.0 + sc1) + sh1).astype(BF16)
    proj = _dot(h, win_ref[...])

    ang = pos_ref[...].astype(F32) * invf_ref[...]
    cos, sin = jnp.cos(ang), jnp.sin(ang)

    cq = proj[:, _C_Q:_C_Q + MLA_Q_RANK]
    qn = _rms(cq, qan_ref[...]).astype(BF16)
    q = _dot(qn, wq_ref[...]) * ((MLA_NOPE + MLA_ROPE) ** -0.5 * LOG2E)
    for hh in range(MLA_HEADS):
        c0 = hh * MLA_QK_PAD
        q_ref[:, c0:c0 + MLA_NOPE] = q[:, c0:c0 + MLA_NOPE].astype(BF16)
        q_ref[:, c0 + MLA_NOPE:c0 + MLA_QK_PAD] = _rope(q[:, c0 + MLA_NOPE:c0 + MLA_QK_PAD], cos, sin).astype(BF16)

    ckv = proj[:, _C_KV:_C_KV + MLA_KV_RANK]
    kvn = _rms(ckv, kvan_ref[...]).astype(BF16)
    kv = _dot(kvn, wkv_ref[...])
    kpe = _rope(proj[:, _C_KPE:_C_KPE + LANES], cos, sin).astype(BF16)
    for hh in range(MLA_HEADS):
        c0 = hh * MLA_QK_PAD
        k_ref[:, c0:c0 + MLA_NOPE] = kv[:, hh * MLA_NOPE:(hh + 1) * MLA_NOPE].astype(BF16)
        k_ref[:, c0 + MLA_NOPE:c0 + MLA_QK_PAD] = kpe
    v_ref[...] = kv[:, MLA_HEADS * MLA_NOPE:].astype(BF16)

    qkv_ref[...] = proj[:, _C_QKV:_C_QKV + _GDN_QKV].astype(qkv_ref.dtype)
    z_ref[...] = proj[:, _C_Z:_C_Z + GDN_HEADS * GDN_DV].astype(z_ref.dtype)
    ab_ref[...] = proj[:, _C_AB:_C_AB + LANES]


def _proj(x2, mod8, pos2, n1, win, qan, wq, kvan, wkv, invf, *, seq, tm):
    n, d = x2.shape
    per_b = seq // tm
    row = lambda i: (i, 0)
    const = lambda i: (0, 0)
    hq = MLA_HEADS * MLA_QK_PAD
    hv = MLA_HEADS * MLA_V
    return pl.pallas_call(
        _proj_kernel,
        grid=(n // tm,),
        in_specs=[pl.BlockSpec((tm, d), row),
                  pl.BlockSpec((1, 8, d), lambda i: (i // per_b, 0, 0)),
                  pl.BlockSpec((tm, 1), row),
                  pl.BlockSpec((1, d), const),
                  pl.BlockSpec(win.shape, const),
                  pl.BlockSpec(qan.shape, const),
                  pl.BlockSpec(wq.shape, const),
                  pl.BlockSpec(kvan.shape, const),
                  pl.BlockSpec(wkv.shape, const),
                  pl.BlockSpec(invf.shape, const)],
        out_specs=[pl.BlockSpec((tm, hq), row), pl.BlockSpec((tm, hq), row), pl.BlockSpec((tm, hv), row),
                   pl.BlockSpec((tm, _GDN_QKV), row), pl.BlockSpec((tm, GDN_HEADS * GDN_DV), row),
                   pl.BlockSpec((tm, LANES), row)],
        out_shape=[jax.ShapeDtypeStruct((n, hq), BF16), jax.ShapeDtypeStruct((n, hq), BF16),
                   jax.ShapeDtypeStruct((n, hv), BF16), jax.ShapeDtypeStruct((n, _GDN_QKV), BF16),
                   jax.ShapeDtypeStruct((n, GDN_HEADS * GDN_DV), BF16), jax.ShapeDtypeStruct((n, LANES), F32)],
        compiler_params=_cparams("parallel"),
        name="proj",
    )(x2, mod8, pos2, n1, win, qan, wq, kvan, wkv, invf)


_Q_TILES = 2


def _attn_kernel(q_ref, k_ref, v_ref, o_ref, *, tq):
    i = pl.program_id(1)
    chains = [(qt, hh) for qt in range(_Q_TILES) for hh in range(MLA_HEADS)]

    ones_col = jnp.where(lax.broadcasted_iota(jnp.int32, (tq, MLA_V), 1) == 0, 1.0, 0.0).astype(BF16)

    def chain_step(qt, hh, r0, carry, mask):
        m, acc = carry
        s = _dot_nt(q_ref[qt * tq:(qt + 1) * tq, hh * MLA_QK_PAD:(hh + 1) * MLA_QK_PAD],
                    k_ref[pl.ds(r0, tq), hh * MLA_QK_PAD:(hh + 1) * MLA_QK_PAD])
        if mask is not None:
            s = jnp.where(mask, s, -jnp.inf)
        m_new = jnp.maximum(m, jnp.max(s, axis=-1, keepdims=True))
        alpha = jnp.exp2(m - m_new)
        p = jnp.exp2((s - m_new).astype(BF16))
        v_ext = jnp.concatenate([v_ref[pl.ds(r0, tq), hh * MLA_V:(hh + 1) * MLA_V], ones_col], axis=1)
        return m_new, alpha * acc + _dot(p, v_ext)

    def block(r0, carry, masks):
        return tuple(carry[c] if masks[qt] is False else chain_step(qt, hh, r0, carry[c], masks[qt])
                     for c, (qt, hh) in enumerate(chains))

    init = tuple((jnp.full((tq, 1), -jnp.inf, F32), jnp.zeros((tq, 2 * MLA_V), F32)) for _ in chains)
    def full_tiles(j, c):
        for kt in range(_Q_TILES):
            c = block(pl.multiple_of((_Q_TILES * j + kt) * tq, tq), c, (None,) * _Q_TILES)
        return c

    carry = lax.fori_loop(0, i, full_tiles, init)
    rq = lax.broadcasted_iota(jnp.int32, (tq, tq), 0) // CHUNK
    ck = lax.broadcasted_iota(jnp.int32, (tq, tq), 1) // CHUNK
    diag = ck <= rq
    for kt in range(_Q_TILES):
        masks = tuple(False if qt < kt else (diag if qt == kt else None) for qt in range(_Q_TILES))
        carry = block(pl.multiple_of((_Q_TILES * i + kt) * tq, tq), carry, masks)
    for c, (qt, hh) in enumerate(chains):
        _, acc = carry[c]
        o_ref[qt * tq:(qt + 1) * tq, hh * MLA_V:(hh + 1) * MLA_V] = (
            acc[:, :MLA_V] / acc[:, MLA_V:MLA_V + 1]).astype(o_ref.dtype)


def _attn(q, k, v, *, batch, seq, tq):
    n = q.shape[0]
    rows = _Q_TILES * tq
    nq = seq // rows
    return pl.pallas_call(
        functools.partial(_attn_kernel, tq=tq),
        grid=(batch, nq),
        in_specs=[pl.BlockSpec((rows, q.shape[1]), lambda b, i: (b * nq + i, 0)),
                  pl.BlockSpec((seq, k.shape[1]), lambda b, i: (b, 0)),
                  pl.BlockSpec((seq, v.shape[1]), lambda b, i: (b, 0))],
        out_specs=pl.BlockSpec((rows, v.shape[1]), lambda b, i: (b * nq + i, 0)),
        out_shape=jax.ShapeDtypeStruct((n, v.shape[1]), BF16),
        compiler_params=_cparams("parallel", "arbitrary"),
        name="attn",
    )(q, k, v)


_SUPER = 2 * CHUNK


def _unit_lower_inverses(lows):
    n = lows[0].shape[0]
    eye = (lax.broadcasted_iota(jnp.int32, (n, n), 0) == lax.broadcasted_iota(jnp.int32, (n, n), 1)).astype(F32)
    levels = CHUNK.bit_length() - 1
    ps = [-low for low in lows]
    ts = [eye + p for p in ps]
    for level in range(1, levels):
        parts = [_split3(p) for p in ps]
        ps = [_dot_x3(s, s) for s in parts]
        if level == 1:
            continue
        ts = [t + _dot_x3(s, _split3(t)) for t, s in zip(ts, parts)]
    return [t + _dot_x3(_split3(p), _split3(t)) for t, p in zip(ts, ps)]


def _gdn_kernel(qkv_ref, z_ref, ab_ref, cw_ref, hp_ref, nw_ref, o_ref, xe_ref, st_ref, *, tt):
    j = pl.program_id(1)
    hd = GDN_HEADS * GDN_DK

    @pl.when(j == 0)
    def _():
        xe_ref[0:8, :] = jnp.zeros((8, xe_ref.shape[1]), F32)
        st_ref[...] = jnp.zeros(st_ref.shape, F32)

    xe_ref[8:8 + tt, :] = qkv_ref[...].astype(F32)
    cw = cw_ref[...]
    y = xe_ref[8:8 + tt, :] * cw[GDN_CONV - 1:GDN_CONV]
    for i in range(1, GDN_CONV):
        y = y + xe_ref[8 - i:8 - i + tt, :] * cw[GDN_CONV - 1 - i:GDN_CONV - i]
    xe_ref[0:8, :] = xe_ref[tt:tt + 8, :]
    act = _silu(y)

    ab = ab_ref[...]
    hp = hp_ref[...]
    pre = ab + hp[1:2]
    softplus = jnp.maximum(pre, 0.0) + jnp.log1p(jnp.exp(-jnp.abs(pre)))
    g_all = -jnp.exp(hp[0:1]) * softplus
    beta_all = jax.nn.sigmoid(ab)

    ri = lax.broadcasted_iota(jnp.int32, (tt, tt), 0)
    ci = lax.broadcasted_iota(jnp.int32, (tt, tt), 1)
    tri = ((ri // CHUNK == ci // CHUNK) & (ci <= ri)).astype(BF16)
    gc = sum(_dot(tri, part) for part in _split3(g_all))
    gct = gc.T

    rs = lax.broadcasted_iota(jnp.int32, (_SUPER, _SUPER), 0)
    cs = lax.broadcasted_iota(jnp.int32, (_SUPER, _SUPER), 1)
    same = rs // CHUNK == cs // CHUNK
    incl = same & (cs <= rs)
    strict = same & (cs < rs)

    tiles = [(s * _SUPER, hh) for s in range(tt // _SUPER) for hh in range(GDN_HEADS)]
    pre_t = []
    for r0, hh in tiles:
        qh = act[r0:r0 + _SUPER, hh * GDN_DK:(hh + 1) * GDN_DK]
        kh = act[r0:r0 + _SUPER, hd + hh * GDN_DK:hd + (hh + 1) * GDN_DK]
        vh = act[r0:r0 + _SUPER, 2 * hd + hh * GDN_DV:2 * hd + (hh + 1) * GDN_DV]
        qh = qh * lax.rsqrt(jnp.sum(qh * qh, axis=-1, keepdims=True) + 1e-6) * (GDN_DK ** -0.5)
        kh = kh * lax.rsqrt(jnp.sum(kh * kh, axis=-1, keepdims=True) + 1e-6)
        beta = beta_all[r0:r0 + _SUPER, GDN_HEADS + hh:GDN_HEADS + hh + 1]
        gcol = gc[r0:r0 + _SUPER, hh:hh + 1]
        grow = gct[hh:hh + 1, r0:r0 + _SUPER]
        decay = jnp.exp(jnp.where(incl, gcol - grow, -jnp.inf))
        eg = jnp.exp(gcol)
        kb = kh * beta
        pre_t.append(dict(kh=kh, khb=kh.astype(BF16), kbb=kb.astype(BF16), vbb=(vh * beta).astype(BF16),
                          kgb=(kb * eg).astype(BF16), qb=qh.astype(BF16), qg=qh * eg, gcol=gcol, decay=decay))
    lows = [jnp.where(strict, _dot_nt(p["kbb"], p["khb"]) * p["decay"], 0.0) for p in pre_t]
    tinvs = [t.astype(BF16) for t in _unit_lower_inverses(lows)]
    us = [_dot(t, p["vbb"]).astype(BF16) for t, p in zip(tinvs, pre_t)]
    ws = [_dot(t, p["kgb"]).astype(BF16) for t, p in zip(tinvs, pre_t)]
    atts = [(_dot_nt(p["qb"], p["khb"]) * p["decay"]).astype(BF16) for p in pre_t]

    steps = []
    for cc in range(tt // CHUNK):
        r0 = cc * CHUNK
        a0 = r0 % _SUPER
        for hh in range(GDN_HEADS):
            ti = (r0 // _SUPER) * GDN_HEADS + hh
            p = pre_t[ti]
            gl = gc[r0 + CHUNK - 1:r0 + CHUNK, hh:hh + 1]
            kd = (p["kh"][a0:a0 + CHUNK] * jnp.exp(gl - p["gcol"][a0:a0 + CHUNK])).astype(BF16)
            u_c, w_c = us[ti][a0:a0 + CHUNK], ws[ti][a0:a0 + CHUNK]
            att_c = atts[ti][a0:a0 + CHUNK, a0:a0 + CHUNK]
            steps.append(dict(hh=hh, r0=r0, gain=jnp.exp(gl), kw=_dot_tn(kd, w_c).astype(BF16), ku=_dot_tn(kd, u_c),
                              qs=(p["qg"][a0:a0 + CHUNK] - _dot(att_c, w_c)).astype(BF16), ou=_dot(att_c, u_c)))

    states = [st_ref[hh] for hh in range(GDN_HEADS)]
    for st in steps:
        hh, r0 = st["hh"], st["r0"]
        sb = states[hh].astype(BF16)
        o = _dot(st["qs"], sb) + st["ou"]
        states[hh] = states[hh] * st["gain"] + (st["ku"] - _dot(st["kw"], sb))
        zz = z_ref[r0:r0 + CHUNK, hh * GDN_DV:(hh + 1) * GDN_DV].astype(F32)
        o_ref[r0:r0 + CHUNK, hh * GDN_DV:(hh + 1) * GDN_DV] = (_rms(o, nw_ref[...]) * _silu(zz)).astype(o_ref.dtype)
    for hh in range(GDN_HEADS):
        st_ref[hh] = states[hh]


def _gdn(qkv, z, ab, cw8, hp, nw, *, batch, seq, tt):
    n = qkv.shape[0]
    per_b = seq // tt
    row = lambda b, j: (b * per_b + j, 0)
    const = lambda b, j: (0, 0)
    hv = GDN_HEADS * GDN_DV
    return pl.pallas_call(
        functools.partial(_gdn_kernel, tt=tt),
        grid=(batch, per_b),
        in_specs=[pl.BlockSpec((tt, _GDN_QKV), row), pl.BlockSpec((tt, hv), row), pl.BlockSpec((tt, LANES), row),
                  pl.BlockSpec(cw8.shape, const), pl.BlockSpec(hp.shape, const), pl.BlockSpec(nw.shape, const)],
        out_specs=pl.BlockSpec((tt, hv), row),
        out_shape=jax.ShapeDtypeStruct((n, hv), BF16),
        scratch_shapes=[pltpu.VMEM((tt + 8, _GDN_QKV), F32), pltpu.VMEM((GDN_HEADS, GDN_DK, GDN_DV), F32)],
        compiler_params=_cparams("parallel", "arbitrary"),
        name="gdn",
    )(qkv, z, ab, cw8, hp, nw)


def _route(logits_t, bias_col):
    e, tm = logits_t.shape
    gsz = e // N_GROUPS
    scores = jax.nn.sigmoid(logits_t)
    biased = scores + bias_col
    sub = lax.broadcasted_iota(jnp.int32, (gsz, tm), 0)
    rows = []
    for g in range(N_GROUPS):
        blk = biased[g * gsz:(g + 1) * gsz]
        m1 = jnp.max(blk, axis=0, keepdims=True)
        first = jnp.min(jnp.where(blk == m1, sub, gsz), axis=0, keepdims=True)
        m2 = jnp.max(jnp.where(sub == first, -jnp.inf, blk), axis=0, keepdims=True)
        rows.append(m1 + m2)
    gs = jnp.concatenate(rows, axis=0)
    gi = lax.broadcasted_iota(jnp.int32, (N_GROUPS, tm), 0)
    grank = jnp.zeros((N_GROUPS, tm), F32)
    for g in range(N_GROUPS):
        r = gs[g:g + 1]
        grank = grank + jnp.where(r > gs, 1.0, jnp.where(r == gs, (gi > g).astype(F32), 0.0))
    gsel = grank < TOPK_GROUPS
    masked = jnp.concatenate(
        [jnp.where(gsel[g:g + 1], biased[g * gsz:(g + 1) * gsz], -jnp.inf) for g in range(N_GROUPS)], axis=0)
    ei = lax.broadcasted_iota(jnp.int32, (e, tm), 0)
    picks = []
    for _ in range(TOP_K):
        best = jnp.max(masked, axis=0, keepdims=True)
        pick = jnp.min(jnp.where(masked == best, ei, e), axis=0, keepdims=True)
        picks.append(pick)
        masked = jnp.where(ei == pick, -jnp.inf, masked)
    return scores, picks


SUBLANES = 4


def _store_tile_rows(ref, x):
    c = x.shape[1] // 2
    lo = pltpu.bitcast(x[:, :c].astype(BF16).astype(F32), jnp.uint32)
    hi = pltpu.bitcast(x[:, c:].astype(BF16).astype(F32), jnp.uint32)
    w = (hi & jnp.uint32(0xFFFF0000)) | (lo >> 16)
    for j in range(SUBLANES):
        ref[pl.ds(j, x.shape[0], stride=SUBLANES), :] = w[:, j * LANES:(j + 1) * LANES]


def _load_tile_rows(ref, r0, rows):
    w = jnp.concatenate(
        [ref[pl.ds(r0 * SUBLANES + j, rows, stride=SUBLANES), :] for j in range(SUBLANES)], axis=1)
    return jnp.concatenate([pltpu.bitcast(w << 16, F32), pltpu.bitcast(w & jnp.uint32(0xFFFF0000), F32)], axis=1)


def _tile_row(ref, sublane_offset):
    return ref.at[pl.ds(pl.multiple_of(sublane_offset, SUBLANES), SUBLANES), :]


def _mix_kernel(oa_ref, ob_ref, x_ref, mod_ref, woa_ref, wob_ref, n2_ref, rwt_ref, rb_ref, sg_ref, su_ref, sd_ref,
                xs_ref, h2t_ref, te_ref, tw_ref, tp_ref, cnt_ref, run_ref):
    @pl.when(pl.program_id(0) == 0)
    def _():
        run_ref[...] = jnp.zeros(run_ref.shape, F32)

    mod = mod_ref[0]
    g1, sh2, sc2, g2 = mod[2:3], mod[3:4], mod[4:5], mod[5:6]
    mix = _dot(oa_ref[...], woa_ref[...]) + _dot(ob_ref[...], wob_ref[...])
    x1 = x_ref[...] + g1 * mix
    h2 = _rms(x1, n2_ref[...]) * (1.0 + sc2) + sh2
    _store_tile_rows(h2t_ref, h2)
    h2b = h2.astype(BF16)
    hs = (_silu(_dot(h2b, sg_ref[...])) * _dot(h2b, su_ref[...])).astype(BF16)
    xs_ref[...] = x1 + g2 * _dot(hs, sd_ref[...])

    rh, rm, _ = _split3(rwt_ref[...])
    hh, hm, _ = _split3(h2)
    scores, picks = _route(_dot_nt(rh, hh) + (_dot_nt(rh, hm) + _dot_nt(rm, hh)), rb_ref[...])
    e, tm = scores.shape
    ei = lax.broadcasted_iota(jnp.int32, (e, tm), 0)
    hits = [ei == pick for pick in picks]
    sel = jnp.zeros((e, tm), F32)
    for hit in hits:
        sel = sel + jnp.where(hit, 1.0, 0.0)
    earlier = (lax.broadcasted_iota(jnp.int32, (tm, tm), 0) < lax.broadcasted_iota(jnp.int32, (tm, tm), 1))
    posmat = _dot(sel.astype(BF16), earlier.astype(BF16)) + run_ref[...]
    wts = jnp.concatenate([jnp.sum(jnp.where(hit, scores, 0.0), axis=0, keepdims=True) for hit in hits], axis=0)
    pos = [jnp.sum(jnp.where(hit, posmat, 0.0), axis=0, keepdims=True) for hit in hits]
    te_ref[0] = jnp.concatenate(picks, axis=0)
    tw_ref[0] = wts / jnp.sum(wts, axis=0, keepdims=True) * ROUTED_SCALE
    tp_ref[0] = jnp.concatenate(pos, axis=0).astype(jnp.int32)
    run_ref[...] = run_ref[...] + jnp.sum(sel, axis=1, keepdims=True)
    cnt_ref[...] = jnp.broadcast_to(run_ref[...], cnt_ref.shape)


def _mix(oa, ob, x2, mod8, woa, wob, n2, rwt, rb, sg, su, sd, *, seq, tm):
    n, d = x2.shape
    per_b = seq // tm
    nb = n // tm
    row = lambda i: (i, 0)
    blk3 = lambda i: (i, 0, 0)
    const = lambda i: (0, 0)
    return pl.pallas_call(
        _mix_kernel,
        grid=(nb,),
        in_specs=[pl.BlockSpec((tm, oa.shape[1]), row), pl.BlockSpec((tm, ob.shape[1]), row),
                  pl.BlockSpec((tm, d), row), pl.BlockSpec((1, 8, d), lambda i: (i // per_b, 0, 0)),
                  pl.BlockSpec(woa.shape, const), pl.BlockSpec(wob.shape, const), pl.BlockSpec((1, d), const),
                  pl.BlockSpec(rwt.shape, const), pl.BlockSpec(rb.shape, const),
                  pl.BlockSpec(sg.shape, const), pl.BlockSpec(su.shape, const), pl.BlockSpec(sd.shape, const)],
        out_specs=[pl.BlockSpec((tm, d), row), pl.BlockSpec((tm * SUBLANES, LANES), row),
                   pl.BlockSpec((1, TOP_K, tm), blk3), pl.BlockSpec((1, TOP_K, tm), blk3),
                   pl.BlockSpec((1, TOP_K, tm), blk3), pl.BlockSpec((N_EXPERTS, LANES), const)],
        out_shape=[jax.ShapeDtypeStruct((n, d), F32), jax.ShapeDtypeStruct((n * SUBLANES, LANES), jnp.uint32),
                   jax.ShapeDtypeStruct((nb, TOP_K, tm), jnp.int32), jax.ShapeDtypeStruct((nb, TOP_K, tm), F32),
                   jax.ShapeDtypeStruct((nb, TOP_K, tm), jnp.int32),
                   jax.ShapeDtypeStruct((N_EXPERTS, LANES), F32)],
        scratch_shapes=[pltpu.VMEM((N_EXPERTS, 1), F32)],
        compiler_params=_cparams("arbitrary"),
        name="mix",
    )(oa, ob, x2, mod8, woa, wob, n2, rwt, rb, sg, su, sd)


def _slots_kernel(ps_ref, ids_ref, pos_ref, o_ref):
    ids = ids_ref[...]
    start = jnp.zeros(ids.shape, jnp.int32)
    for e in range(N_EXPERTS):
        start = jnp.where(ids == e, ps_ref[e], start)
    o_ref[...] = (start + pos_ref[...]) * SUBLANES


def _slots(ps, ids, pos):
    spec = pl.BlockSpec(ids.shape, lambda i, ps: (0, 0, 0))
    return pl.pallas_call(
        _slots_kernel,
        grid_spec=pltpu.PrefetchScalarGridSpec(num_scalar_prefetch=1, grid=(1,),
                                               in_specs=[spec, spec], out_specs=spec),
        out_shape=jax.ShapeDtypeStruct(ids.shape, jnp.int32),
        compiler_params=_cparams("parallel"),
        name="slots",
    )(ps, ids, pos)


def _dispatch_kernel(ps_ref, slot_ref, h2t_ref, xs_hbm, zbuf, ring, zsem, sems, *, tm, blk, n_steps):
    i = pl.program_id(0)

    @pl.when(i == 0)
    def _():
        zbuf[...] = jnp.zeros(zbuf.shape, zbuf.dtype)

        def block_at(row):
            return xs_hbm.at[pl.ds(pl.multiple_of(row * SUBLANES, blk * SUBLANES), blk * SUBLANES), :]

        def tail(e):
            end = ps_ref[e + 1]
            return end > ps_ref[e], pltpu.make_async_copy(zbuf, block_at(end - blk), zsem)

        def unused(e):
            row = ps_ref[N_EXPERTS] + e * blk
            return row * SUBLANES < xs_hbm.shape[0], pltpu.make_async_copy(zbuf, block_at(row), zsem)

        for fill in (tail, unused):
            for e in range(N_EXPERTS):
                needed, cp = fill(e)
                pl.when(needed)(cp.start)
        for fill in (tail, unused):
            for e in range(N_EXPERTS):
                needed, cp = fill(e)
                pl.when(needed)(cp.wait)

    def send(b):
        ring[b] = h2t_ref[...]

        def issue(t, carry):
            for k in range(TOP_K):
                pltpu.make_async_copy(_tile_row(ring.at[b], t * SUBLANES), _tile_row(xs_hbm, slot_ref[0, k, t]),
                                      sems.at[b]).start(priority=k % 2)
            return carry

        lax.fori_loop(0, tm, issue, 0)

    def drain(b):
        for k in range(TOP_K):
            pltpu.make_async_copy(ring.at[b], xs_hbm.at[pl.ds(0, tm * SUBLANES), :], sems.at[b]).wait()

    for b in range(2):
        @pl.when(i % 2 == b)
        def _(b=b):
            send(b)

            @pl.when(i > 0)
            def _():
                drain(1 - b)

    @pl.when(i == n_steps - 1)
    def _():
        drain((n_steps - 1) % 2)


def _dispatch(ps, slot, h2t, *, n_slots, tm, blk):
    lanes = h2t.shape[1]
    n_steps = h2t.shape[0] // (tm * SUBLANES)
    grid_spec = pltpu.PrefetchScalarGridSpec(
        num_scalar_prefetch=1,
        grid=(n_steps,),
        in_specs=[pl.BlockSpec((1, TOP_K, tm), lambda i, ps: (i, 0, 0), memory_space=pltpu.SMEM),
                  pl.BlockSpec((tm * SUBLANES, lanes), lambda i, ps: (i, 0))],
        out_specs=pl.BlockSpec(memory_space=pl.ANY),
        scratch_shapes=[pltpu.VMEM((blk * SUBLANES, lanes), jnp.uint32),
                        pltpu.VMEM((2, tm * SUBLANES, lanes), jnp.uint32),
                        pltpu.SemaphoreType.DMA(()), pltpu.SemaphoreType.DMA((2,))],
    )
    return pl.pallas_call(
        functools.partial(_dispatch_kernel, tm=tm, blk=blk, n_steps=n_steps),
        grid_spec=grid_spec,
        out_shape=jax.ShapeDtypeStruct((n_slots * SUBLANES, lanes), jnp.uint32),
        compiler_params=_cparams("arbitrary"),
        name="dispatch",
    )(ps, slot, h2t)


def _experts_kernel(be_ref, nu_ref, x_ref, wg_ref, wu_ref, wd_ref, y_ref, wgb, wub, wdb):
    i = pl.program_id(0)
    used = i < nu_ref[0]
    new_expert = jnp.logical_or(i == 0, be_ref[i] != be_ref[jnp.maximum(i - 1, 0)])

    @pl.when(jnp.logical_and(used, new_expert))
    def _():
        wgb[...] = wg_ref[0].astype(BF16)
        wub[...] = wu_ref[0].astype(BF16)
        wdb[...] = wd_ref[0].astype(BF16)

    @pl.when(used)
    def _():
        xb = _load_tile_rows(x_ref, 0, x_ref.shape[0] // SUBLANES).astype(BF16)
        hb = (_silu(_dot(xb, wgb[...])) * _dot(xb, wub[...])).astype(BF16)
        _store_tile_rows(y_ref, _dot(hb, wdb[...]))

    @pl.when(jnp.logical_not(used))
    def _():
        y_ref[...] = jnp.zeros(y_ref.shape, y_ref.dtype)


def _experts(block_e, n_used, xs, wg, wu, wd, *, blk):
    n_blocks = block_e.shape[0]
    rows_blk = (blk * SUBLANES, xs.shape[1])
    wsel = lambda i, be, nu: (be[i], 0, 0)
    grid_spec = pltpu.PrefetchScalarGridSpec(
        num_scalar_prefetch=2,
        grid=(n_blocks,),
        in_specs=[pl.BlockSpec(rows_blk, lambda i, be, nu: (jnp.minimum(i, nu[0] - 1), 0)),
                  pl.BlockSpec((1,) + wg.shape[1:], wsel), pl.BlockSpec((1,) + wu.shape[1:], wsel),
                  pl.BlockSpec((1,) + wd.shape[1:], wsel)],
        out_specs=pl.BlockSpec(rows_blk, lambda i, be, nu: (i, 0)),
        scratch_shapes=[pltpu.VMEM(wg.shape[1:], BF16), pltpu.VMEM(wu.shape[1:], BF16),
                        pltpu.VMEM(wd.shape[1:], BF16)],
    )
    return pl.pallas_call(
        _experts_kernel,
        grid_spec=grid_spec,
        out_shape=jax.ShapeDtypeStruct(xs.shape, xs.dtype),
        compiler_params=_cparams("arbitrary"),
        name="experts",
    )(block_e, n_used, xs, wg, wu, wd)


_GROUP = 32


def _combine_kernel(slot_ref, nxt_ref, tw_ref, xs_ref, mod_ref, fn_ref, y_hbm, o_ref, gbuf0, gbuf1, wt_ref, sems, *, tm):
    gbuf = (gbuf0, gbuf1)
    j = pl.program_id(0)
    last = pl.num_programs(0) - 1
    d = xs_ref.shape[1]

    def issue(src_ref, bi, b, t):
        for k in range(TOP_K):
            pltpu.make_async_copy(_tile_row(y_hbm, src_ref[bi, k, t]), _tile_row(gbuf[b], (k * tm + t) * SUBLANES),
                                  sems.at[b]).start(priority=k % 2)

    def drain(b):
        pltpu.make_async_copy(y_hbm.at[pl.ds(0, gbuf[b].shape[0]), :], gbuf[b], sems.at[b]).wait()

    def reduce_rows(b, r0):
        w8 = wt_ref[b, pl.ds(r0, _GROUP), :]
        acc = [jnp.zeros((_GROUP, LANES), F32) for _ in range(d // LANES)]
        for k in range(TOP_K):
            wk = w8[:, k:k + 1]
            for c in range(SUBLANES):
                word = gbuf[b][pl.ds((k * tm + r0) * SUBLANES + c, _GROUP, stride=SUBLANES), :]
                acc[c] = acc[c] + wk * pltpu.bitcast(word << 16, F32)
                acc[SUBLANES + c] = acc[SUBLANES + c] + wk * pltpu.bitcast(word & jnp.uint32(0xFFFF0000), F32)
        rows = pl.ds(b * tm + r0, _GROUP)
        g2 = mod_ref[0][5:6]
        x2 = jnp.concatenate(acc, axis=1) * g2 + xs_ref[rows, :]
        o_ref[rows, :] = _rms(x2, fn_ref[...])

    def phase(fin_b, issue_args):
        def body(g, carry):
            r0 = pl.multiple_of(g * _GROUP, _GROUP)
            if issue_args is not None:
                for dt in range(_GROUP):
                    issue(*issue_args, r0 + dt)
            reduce_rows(fin_b, r0)
            return carry

        lax.fori_loop(0, tm // _GROUP, body, 0)

    @pl.when(j == 0)
    def _():
        lax.fori_loop(0, tm, lambda t, c: (issue(slot_ref, 0, 0, t), c)[1], 0)

    for b in range(2):
        wt_ref[b] = jnp.concatenate([tw_ref[b], jnp.zeros((LANES - TOP_K, tm), F32)], axis=0).T
    drain(0)
    phase(0, (slot_ref, 1, 1))
    drain(1)
    pl.when(j < last)(lambda: phase(1, (nxt_ref, 0, 0)))
    pl.when(j == last)(lambda: phase(1, None))


def _combine(slot, tw, xs, mod8, fn, y, *, seq, tm):
    n, d = xs.shape
    nb = slot.shape[0]
    assert nb % 2 == 0 and (seq // tm) % 2 == 0
    per_b = seq // (2 * tm)
    pair = lambda j: (j, 0, 0)
    return pl.pallas_call(
        functools.partial(_combine_kernel, tm=tm),
        grid=(nb // 2,),
        in_specs=[pl.BlockSpec((2, TOP_K, tm), pair, memory_space=pltpu.SMEM),
                  pl.BlockSpec((1, TOP_K, tm), lambda j: (jnp.minimum(2 * j + 2, nb - 1), 0, 0),
                               memory_space=pltpu.SMEM),
                  pl.BlockSpec((2, TOP_K, tm), pair),
                  pl.BlockSpec((2 * tm, d), lambda j: (j, 0)),
                  pl.BlockSpec((1, 8, d), lambda j: (j // per_b, 0, 0)),
                  pl.BlockSpec((1, d), lambda j: (0, 0)),
                  pl.BlockSpec(memory_space=pl.ANY)],
        out_specs=pl.BlockSpec((2 * tm, d), lambda j: (j, 0)),
        out_shape=jax.ShapeDtypeStruct((n, d), F32),
        scratch_shapes=[pltpu.VMEM((TOP_K * tm * SUBLANES, y.shape[1]), y.dtype),
                        pltpu.VMEM((TOP_K * tm * SUBLANES, y.shape[1]), y.dtype),
                        pltpu.VMEM((2, tm, LANES), F32), pltpu.SemaphoreType.DMA((2,))],
        compiler_params=_cparams("arbitrary"),
        name="combine",
    )(slot, slot, tw, xs, mod8, fn, y)


def _expert_runs(counts, n_blocks, blk):
    padded = (counts + blk - 1) // blk * blk
    pad_end = jnp.cumsum(padded)
    run_start = jnp.concatenate([pad_end - padded, pad_end[-1:]]).astype(jnp.int32)
    block_row = jnp.arange(n_blocks, dtype=jnp.int32) * blk
    block_e = jnp.minimum(jnp.sum(pad_end[None, :] <= block_row[:, None], axis=1), N_EXPERTS - 1).astype(jnp.int32)
    n_used = (pad_end[-1:] // blk).astype(jnp.int32)
    return run_start, block_e, n_used


def _pad_lanes(w, width):
    return jnp.pad(w, ((0, 0), (0, width - w.shape[1])))


def _tiles(seq):
    t = dict(tm_proj=512, tq=512, tt=256, tm_moe=512, blk=512)
    assert all(seq % v == 0 for k, v in t.items() if k != "blk")
    assert seq % (_Q_TILES * t["tq"]) == 0 and (seq // t["tm_moe"]) % 2 == 0
    return t


def _layer(x, c, positions, ada_w, ada_b, norm1_w, w_in, q_a_norm_w, wq_b, kv_a_norm_w, wkv_b,
           gdn_conv_w, gdn_a_log, gdn_dt_bias, gdn_norm_w, w_out, norm2_w, router_w, router_bias,
           exp_w_gate, exp_w_up, exp_w_down, sh_w_gate, sh_w_up, sh_w_down, final_norm_w,
           *, tm_proj, tq, tt, tm_moe, blk):
    batch, seq, d = x.shape
    n = batch * seq
    x2 = x.reshape(n, d)

    c8 = jnp.zeros((8, d), F32).at[:batch].set(c)
    mod = _adaln(c8, ada_w, ada_b.reshape(1, -1))
    mod8 = jnp.pad(mod[:batch].reshape(batch, 6, d), ((0, 0), (0, 2), (0, 0)))

    hq = MLA_NOPE + MLA_ROPE
    w_cq, w_ckv, w_kpe, w_qkv, w_z, w_a, w_b = jnp.split(
        w_in, [_C_KV, _C_KV + MLA_KV_RANK, _C_KV + MLA_KV_RANK + MLA_ROPE,
               _C_KV + MLA_KV_RANK + MLA_ROPE + _GDN_QKV,
               _C_KV + MLA_KV_RANK + MLA_ROPE + _GDN_QKV + GDN_HEADS * GDN_DV,
               _C_KV + MLA_KV_RANK + MLA_ROPE + _GDN_QKV + GDN_HEADS * GDN_DV + GDN_HEADS], axis=1)
    win = jnp.concatenate([w_cq, w_ckv, _pad_lanes(w_kpe, LANES), w_qkv, w_z,
                           _pad_lanes(jnp.concatenate([w_a, w_b], axis=1), LANES)], axis=1).astype(BF16)
    wq = jnp.pad(wq_b.reshape(MLA_Q_RANK, MLA_HEADS, hq),
                 ((0, 0), (0, 0), (0, MLA_QK_PAD - hq))).reshape(MLA_Q_RANK, MLA_HEADS * MLA_QK_PAD).astype(BF16)
    wkv4 = wkv_b.reshape(MLA_KV_RANK, MLA_HEADS, MLA_NOPE + MLA_V)
    wkv = jnp.concatenate([wkv4[:, :, :MLA_NOPE].reshape(MLA_KV_RANK, -1),
                           wkv4[:, :, MLA_NOPE:].reshape(MLA_KV_RANK, -1)], axis=1).astype(BF16)
    inv_freq = 1.0 / (ROPE_THETA ** (jnp.arange(0, MLA_ROPE, 2, dtype=F32) / MLA_ROPE))
    invf = _pad_lanes(jnp.concatenate([inv_freq, inv_freq])[None, :], LANES)

    q, k, v, qkv, z, ab = _proj(x2, mod8, positions.reshape(n, 1), norm1_w.reshape(1, d), win,
                                q_a_norm_w.reshape(1, -1), wq, kv_a_norm_w.reshape(1, -1), wkv, invf,
                                seq=seq, tm=tm_proj)
    out_a = _attn(q, k, v, batch=batch, seq=seq, tq=tq)

    cw8 = jnp.pad(gdn_conv_w, ((0, 8 - GDN_CONV), (0, 0)))
    hp = _pad_lanes(jnp.stack([gdn_a_log, gdn_dt_bias]), LANES)
    hp = jnp.pad(hp, ((0, 6), (0, 0)))
    out_b = _gdn(qkv, z, ab, cw8, hp, gdn_norm_w.reshape(1, -1), batch=batch, seq=seq, tt=tt)

    ha = MLA_HEADS * MLA_V
    xs, h2t, ids, tw, pos, counts = _mix(
        out_a, out_b, x2, mod8, w_out[:ha].astype(BF16), w_out[ha:].astype(BF16), norm2_w.reshape(1, d),
        router_w.T, router_bias.reshape(-1, 1), sh_w_gate.astype(BF16), sh_w_up.astype(BF16),
        sh_w_down.astype(BF16), seq=seq, tm=tm_moe)

    n_blocks = n * TOP_K // blk + N_EXPERTS
    run_start, block_e, n_used = _expert_runs(counts[:, 0].astype(jnp.int32), n_blocks, blk)
    slot = _slots(run_start, ids, pos)
    xsort = _dispatch(run_start, slot, h2t, n_slots=n_blocks * blk, tm=tm_moe, blk=blk)
    y = _experts(block_e, n_used, xsort, exp_w_gate, exp_w_up, exp_w_down, blk=blk)
    out = _combine(slot, tw, xs, mod8, final_norm_w.reshape(1, d), y, seq=seq, tm=tm_moe)
    return out.reshape(batch, seq, d)


def kernel(x, c, positions, ada_w, ada_b, norm1_w, w_in, q_a_norm_w, wq_b, kv_a_norm_w, wkv_b, gdn_conv_w,
           gdn_a_log, gdn_dt_bias, gdn_norm_w, w_out, norm2_w, router_w, router_bias, exp_w_gate, exp_w_up,
           exp_w_down, sh_w_gate, sh_w_up, sh_w_down, final_norm_w):
    assert ada_w.shape[0] == 1, "single layer"
    return _layer(
        x, c, positions, ada_w[0], ada_b[0], norm1_w[0], w_in[0], q_a_norm_w[0], wq_b[0], kv_a_norm_w[0],
        wkv_b[0], gdn_conv_w[0], gdn_a_log[0], gdn_dt_bias[0], gdn_norm_w[0], w_out[0], norm2_w[0],
        router_w[0], router_bias[0], exp_w_gate[0], exp_w_up[0], exp_w_down[0], sh_w_gate[0], sh_w_up[0],
        sh_w_down[0], final_norm_w, **_tiles(x.shape[1]))
```

```python
import functools

import jax
import jax.numpy as jnp
from jax import lax
from jax.experimental import pallas as pl
from jax.experimental.pallas import tpu as pltpu

F32 = jnp.float32
BF16 = jnp.bfloat16

CHUNK = 64
NORM_EPS = 1e-6
MLA_HEADS = 4
MLA_Q_RANK = 384
MLA_KV_RANK = 256
MLA_NOPE = 128
MLA_ROPE = 64
MLA_V = 128
ROPE_THETA = 10000.0
GDN_HEADS = 4
GDN_DK = 128
GDN_DV = 128
GDN_CONV = 4
N_EXPERTS = 64
N_GROUPS = 8
TOPK_GROUPS = 4
TOP_K = 8
ROUTED_SCALE = 2.5
LOG2E = 1.4426950408889634

LANES = 128
MLA_QK_PAD = 256
VMEM_LIMIT = 56 * 1024 * 1024


def _cparams(*sem):
    return pltpu.CompilerParams(dimension_semantics=sem, vmem_limit_bytes=VMEM_LIMIT)


def _dot(a, b):
    return jnp.dot(a, b, preferred_element_type=F32)


def _split3(a):
    hi = a.astype(BF16)
    r = a - hi.astype(F32)
    mid = r.astype(BF16)
    return hi, mid, (r - mid.astype(F32)).astype(BF16)


def _dot_x3(a, b):
    ah, am, _ = a
    bh, bm, _ = b
    return _dot(ah, bh) + (_dot(ah, bm) + _dot(am, bh))


def _dot_nt(a, b):
    return lax.dot_general(a, b, (((1,), (1,)), ((), ())), preferred_element_type=F32)


def _dot_tn(a, b):
    return lax.dot_general(a, b, (((0,), (0,)), ((), ())), preferred_element_type=F32)


def _silu(x):
    return x * jax.nn.sigmoid(x)


def _rms(x, w):
    return x * lax.rsqrt(jnp.mean(x * x, axis=-1, keepdims=True) + NORM_EPS) * w


def _adaln_kernel(c_ref, w_ref, b_ref, o_ref):
    a = _silu(c_ref[...]).astype(BF16)
    o_ref[...] = _dot(a, w_ref[...].astype(BF16)) + b_ref[...]


def _adaln(c8, ada_w, ada_b):
    d = c8.shape[1]
    n_out = ada_w.shape[1]
    return pl.pallas_call(
        _adaln_kernel,
        grid=(n_out // d,),
        in_specs=[pl.BlockSpec((8, d), lambda j: (0, 0)),
                  pl.BlockSpec((d, d), lambda j: (0, j)),
                  pl.BlockSpec((1, d), lambda j: (0, j))],
        out_specs=pl.BlockSpec((8, d), lambda j: (0, j)),
        out_shape=jax.ShapeDtypeStruct((8, n_out), F32),
        compiler_params=_cparams("parallel"),
        name="adaln",
    )(c8, ada_w, ada_b)


_C_Q = 0
_C_KV = _C_Q + MLA_Q_RANK
_C_KPE = _C_KV + MLA_KV_RANK
_C_QKV = _C_KPE + LANES
_GDN_QKV = GDN_HEADS * (2 * GDN_DK + GDN_DV)
_C_Z = _C_QKV + _GDN_QKV
_C_AB = _C_Z + GDN_HEADS * GDN_DV
_D_IN_PAD = _C_AB + LANES


def _rope(xb, cos, sin):
    half = MLA_ROPE // 2
    lane = lax.broadcasted_iota(jnp.int32, xb.shape, 1)
    rot = jnp.where(lane < half, -pltpu.roll(xb, LANES - half, 1), pltpu.roll(xb, half, 1))
    return xb * cos + rot * sin


def _proj_kernel(x_ref, mod_ref, pos_ref, n1_ref, win_ref, qan_ref, wq_ref, kvan_ref, wkv_ref, invf_ref,
                 q_ref, k_ref, v_ref, qkv_ref, z_ref, ab_ref):
    x = x_ref[...]
    mod = mod_ref[0]
    sh1, sc1 = mod[0:1], mod[1:2]
    h = (_rms(x, n1_ref[...]) * (1.0 + sc1) + sh1).astype(BF16)
    proj = _dot(h, win_ref[...])

    ang = pos_ref[...].astype(F32) * invf_ref[...]
    cos, sin = jnp.cos(ang), jnp.sin(ang)

    cq = proj[:, _C_Q:_C_Q + MLA_Q_RANK]
    qn = _rms(cq, qan_ref[...]).astype(BF16)
    q = _dot(qn, wq_ref[...]) * ((MLA_NOPE + MLA_ROPE) ** -0.5 * LOG2E)
    for hh in range(MLA_HEADS):
        c0 = hh * MLA_QK_PAD
        q_ref[:, c0:c0 + MLA_NOPE] = q[:, c0:c0 + MLA_NOPE].astype(BF16)
        q_ref[:, c0 + MLA_NOPE:c0 + MLA_QK_PAD] = _rope(q[:, c0 + MLA_NOPE:c0 + MLA_QK_PAD], cos, sin).astype(BF16)

    ckv = proj[:, _C_KV:_C_KV + MLA_KV_RANK]
    kvn = _rms(ckv, kvan_ref[...]).astype(BF16)
    kv = _dot(kvn, wkv_ref[...])
    kpe = _rope(proj[:, _C_KPE:_C_KPE + LANES], cos, sin).astype(BF16)
    for hh in range(MLA_HEADS):
        c0 = hh * MLA_QK_PAD
        k_ref[:, c0:c0 + MLA_NOPE] = kv[:, hh * MLA_NOPE:(hh + 1) * MLA_NOPE].astype(BF16)
        k_ref[:, c0 + MLA_NOPE:c0 + MLA_QK_PAD] = kpe
    v_ref[...] = kv[:, MLA_HEADS * MLA_NOPE:].astype(BF16)

    qkv_ref[...] = proj[:, _C_QKV:_C_QKV + _GDN_QKV].astype(qkv_ref.dtype)
    z_ref[...] = proj[:, _C_Z:_C_Z + GDN_HEADS * GDN_DV].astype(z_ref.dtype)
    ab_ref[...] = proj[:, _C_AB:_C_AB + LANES]


def _proj(x2, mod8, pos2, n1, win, qan, wq, kvan, wkv, invf, *, seq, tm):
    n, d = x2.shape
    per_b = seq // tm
    row = lambda i: (i, 0)
    const = lambda i: (0, 0)
    hq = MLA_HEADS * MLA_QK_PAD
    hv = MLA_HEADS * MLA_V
    return pl.pallas_call(
        _proj_kernel,
        grid=(n // tm,),
        in_specs=[pl.BlockSpec((tm, d), row),
                  pl.BlockSpec((1, 8, d), lambda i: (i // per_b, 0, 0)),
                  pl.BlockSpec((tm, 1), row),
                  pl.BlockSpec((1, d), const),
                  pl.BlockSpec(win.shape, const),
                  pl.BlockSpec(qan.shape, const),
                  pl.BlockSpec(wq.shape, const),
                  pl.BlockSpec(kvan.shape, const),
                  pl.BlockSpec(wkv.shape, const),
                  pl.BlockSpec(invf.shape, const)],
        out_specs=[pl.BlockSpec((tm, hq), row), pl.BlockSpec((tm, hq), row), pl.BlockSpec((tm, hv), row),
                   pl.BlockSpec((tm, _GDN_QKV), row), pl.BlockSpec((tm, GDN_HEADS * GDN_DV), row),
                   pl.BlockSpec((tm, LANES), row)],
        out_shape=[jax.ShapeDtypeStruct((n, hq), BF16), jax.ShapeDtypeStruct((n, hq), BF16),
                   jax.ShapeDtypeStruct((n, hv), BF16), jax.ShapeDtypeStruct((n, _GDN_QKV), BF16),
                   jax.ShapeDtypeStruct((n, GDN_HEADS * GDN_DV), BF16), jax.ShapeDtypeStruct((n, LANES), F32)],
        compiler_params=_cparams("parallel"),
        name="proj",
    )(x2, mod8, pos2, n1, win, qan, wq, kvan, wkv, invf)


_Q_TILES = 2


def _attn_kernel(q_ref, k_ref, v_ref, o_ref, *, tq):
    i = pl.program_id(1)
    chains = [(qt, hh) for qt in range(_Q_TILES) for hh in range(MLA_HEADS)]

    ones_col = jnp.where(lax.broadcasted_iota(jnp.int32, (tq, MLA_V), 1) == 0, 1.0, 0.0).astype(BF16)

    def chain_step(qt, hh, r0, carry, mask):
        m, acc = carry
        s = _dot_nt(q_ref[qt * tq:(qt + 1) * tq, hh * MLA_QK_PAD:(hh + 1) * MLA_QK_PAD],
                    k_ref[pl.ds(r0, tq), hh * MLA_QK_PAD:(hh + 1) * MLA_QK_PAD])
        if mask is not None:
            s = jnp.where(mask, s, -jnp.inf)
        m_new = jnp.maximum(m, jnp.max(s, axis=-1, keepdims=True))
        alpha = jnp.exp2(m - m_new)
        p = jnp.exp2((s - m_new).astype(BF16))
        v_ext = jnp.concatenate([v_ref[pl.ds(r0, tq), hh * MLA_V:(hh + 1) * MLA_V], ones_col], axis=1)
        return m_new, alpha * acc + _dot(p, v_ext)

    def block(r0, carry, masks):
        return tuple(carry[c] if masks[qt] is False else chain_step(qt, hh, r0, carry[c], masks[qt])
                     for c, (qt, hh) in enumerate(chains))

    init = tuple((jnp.full((tq, 1), -jnp.inf, F32), jnp.zeros((tq, 2 * MLA_V), F32)) for _ in chains)
    def full_tiles(j, c):
        for kt in range(_Q_TILES):
            c = block(pl.multiple_of((_Q_TILES * j + kt) * tq, tq), c, (None,) * _Q_TILES)
        return c

    carry = lax.fori_loop(0, i, full_tiles, init)
    rq = lax.broadcasted_iota(jnp.int32, (tq, tq), 0) // CHUNK
    ck = lax.broadcasted_iota(jnp.int32, (tq, tq), 1) // CHUNK
    diag = ck <= rq
    for kt in range(_Q_TILES):
        masks = tuple(False if qt < kt else (diag if qt == kt else None) for qt in range(_Q_TILES))
        carry = block(pl.multiple_of((_Q_TILES * i + kt) * tq, tq), carry, masks)
    for c, (qt, hh) in enumerate(chains):
        _, acc = carry[c]
        o_ref[qt * tq:(qt + 1) * tq, hh * MLA_V:(hh + 1) * MLA_V] = (
            acc[:, :MLA_V] / acc[:, MLA_V:MLA_V + 1]).astype(o_ref.dtype)


def _attn(q, k, v, *, batch, seq, tq):
    n = q.shape[0]
    rows = _Q_TILES * tq
    nq = seq // rows
    return pl.pallas_call(
        functools.partial(_attn_kernel, tq=tq),
        grid=(batch, nq),
        in_specs=[pl.BlockSpec((rows, q.shape[1]), lambda b, i: (b * nq + i, 0)),
                  pl.BlockSpec((seq, k.shape[1]), lambda b, i: (b, 0)),
                  pl.BlockSpec((seq, v.shape[1]), lambda b, i: (b, 0))],
        out_specs=pl.BlockSpec((rows, v.shape[1]), lambda b, i: (b * nq + i, 0)),
        out_shape=jax.ShapeDtypeStruct((n, v.shape[1]), BF16),
        compiler_params=_cparams("parallel", "arbitrary"),
        name="attn",
    )(q, k, v)


_SUPER = 2 * CHUNK


def _unit_lower_inverses(lows):
    n = lows[0].shape[0]
    eye = (lax.broadcasted_iota(jnp.int32, (n, n), 0) == lax.broadcasted_iota(jnp.int32, (n, n), 1)).astype(F32)
    levels = CHUNK.bit_length() - 1
    ps = [-low for low in lows]
    ts = [eye + p for p in ps]
    for level in range(1, levels):
        parts = [_split3(p) for p in ps]
        ps = [_dot_x3(s, s) for s in parts]
        if level == 1:
            continue
        ts = [t + _dot_x3(s, _split3(t)) for t, s in zip(ts, parts)]
    return [t + _dot_x3(_split3(p), _split3(t)) for t, p in zip(ts, ps)]


def _gdn_kernel(qkv_ref, z_ref, ab_ref, cw_ref, hp_ref, nw_ref, wg_ref, wu_ref, wd_ref,
                o_ref, wgb_ref, wub_ref, wdb_ref, xe_ref, st_ref, *, tt):
    j = pl.program_id(1)
    hd = GDN_HEADS * GDN_DK
    for src, dst in ((wg_ref, wgb_ref), (wu_ref, wub_ref), (wd_ref, wdb_ref)):
        dst[...] = src[...].astype(BF16)

    @pl.when(j == 0)
    def _():
        xe_ref[0:8, :] = jnp.zeros((8, xe_ref.shape[1]), F32)
        st_ref[...] = jnp.zeros(st_ref.shape, F32)

    xe_ref[8:8 + tt, :] = qkv_ref[...].astype(F32)
    cw = cw_ref[...]
    y = xe_ref[8:8 + tt, :] * cw[GDN_CONV - 1:GDN_CONV]
    for i in range(1, GDN_CONV):
        y = y + xe_ref[8 - i:8 - i + tt, :] * cw[GDN_CONV - 1 - i:GDN_CONV - i]
    xe_ref[0:8, :] = xe_ref[tt:tt + 8, :]
    act = _silu(y)

    ab = ab_ref[...]
    hp = hp_ref[...]
    pre = ab + hp[1:2]
    softplus = jnp.maximum(pre, 0.0) + jnp.log1p(jnp.exp(-jnp.abs(pre)))
    g_all = -jnp.exp(hp[0:1]) * softplus
    beta_all = jax.nn.sigmoid(ab)

    ri = lax.broadcasted_iota(jnp.int32, (tt, tt), 0)
    ci = lax.broadcasted_iota(jnp.int32, (tt, tt), 1)
    tri = ((ri // CHUNK == ci // CHUNK) & (ci <= ri)).astype(BF16)
    gc = sum(_dot(tri, part) for part in _split3(g_all))
    gct = gc.T

    rs = lax.broadcasted_iota(jnp.int32, (_SUPER, _SUPER), 0)
    cs = lax.broadcasted_iota(jnp.int32, (_SUPER, _SUPER), 1)
    same = rs // CHUNK == cs // CHUNK
    incl = same & (cs <= rs)
    strict = same & (cs < rs)

    tiles = [(s * _SUPER, hh) for s in range(tt // _SUPER) for hh in range(GDN_HEADS)]
    pre_t = []
    for r0, hh in tiles:
        qh = act[r0:r0 + _SUPER, hh * GDN_DK:(hh + 1) * GDN_DK]
        kh = act[r0:r0 + _SUPER, hd + hh * GDN_DK:hd + (hh + 1) * GDN_DK]
        vh = act[r0:r0 + _SUPER, 2 * hd + hh * GDN_DV:2 * hd + (hh + 1) * GDN_DV]
        qh = qh * lax.rsqrt(jnp.sum(qh * qh, axis=-1, keepdims=True) + 1e-6) * (GDN_DK ** -0.5)
        kh = kh * lax.rsqrt(jnp.sum(kh * kh, axis=-1, keepdims=True) + 1e-6)
        beta = beta_all[r0:r0 + _SUPER, GDN_HEADS + hh:GDN_HEADS + hh + 1]
        gcol = gc[r0:r0 + _SUPER, hh:hh + 1]
        grow = gct[hh:hh + 1, r0:r0 + _SUPER]
        decay = jnp.exp(jnp.where(incl, gcol - grow, -jnp.inf))
        eg = jnp.exp(gcol)
        kb = kh * beta
        pre_t.append(dict(kh=kh, khb=kh.astype(BF16), kbb=kb.astype(BF16), vbb=(vh * beta).astype(BF16),
                          kgb=(kb * eg).astype(BF16), qb=qh.astype(BF16), qg=qh * eg, gcol=gcol, decay=decay))
    lows = [jnp.where(strict, _dot_nt(p["kbb"], p["khb"]) * p["decay"], 0.0) for p in pre_t]
    tinvs = [t.astype(BF16) for t in _unit_lower_inverses(lows)]
    us = [_dot(t, p["vbb"]).astype(BF16) for t, p in zip(tinvs, pre_t)]
    ws = [_dot(t, p["kgb"]).astype(BF16) for t, p in zip(tinvs, pre_t)]
    atts = [(_dot_nt(p["qb"], p["khb"]) * p["decay"]).astype(BF16) for p in pre_t]

    steps = []
    for cc in range(tt // CHUNK):
        r0 = cc * CHUNK
        a0 = r0 % _SUPER
        for hh in range(GDN_HEADS):
            ti = (r0 // _SUPER) * GDN_HEADS + hh
            p = pre_t[ti]
            gl = gc[r0 + CHUNK - 1:r0 + CHUNK, hh:hh + 1]
            kd = (p["kh"][a0:a0 + CHUNK] * jnp.exp(gl - p["gcol"][a0:a0 + CHUNK])).astype(BF16)
            u_c, w_c = us[ti][a0:a0 + CHUNK], ws[ti][a0:a0 + CHUNK]
            att_c = atts[ti][a0:a0 + CHUNK, a0:a0 + CHUNK]
            steps.append(dict(hh=hh, r0=r0, gain=jnp.exp(gl), kw=_dot_tn(kd, w_c).astype(BF16), ku=_dot_tn(kd, u_c),
                              qs=(p["qg"][a0:a0 + CHUNK] - _dot(att_c, w_c)).astype(BF16), ou=_dot(att_c, u_c)))

    states = [st_ref[hh] for hh in range(GDN_HEADS)]
    for st in steps:
        hh, r0 = st["hh"], st["r0"]
        sb = states[hh].astype(BF16)
        o = _dot(st["qs"], sb) + st["ou"]
        states[hh] = states[hh] * st["gain"] + (st["ku"] - _dot(st["kw"], sb))
        zz = z_ref[r0:r0 + CHUNK, hh * GDN_DV:(hh + 1) * GDN_DV].astype(F32)
        o_ref[r0:r0 + CHUNK, hh * GDN_DV:(hh + 1) * GDN_DV] = (_rms(o, nw_ref[...]) * _silu(zz)).astype(o_ref.dtype)
    for hh in range(GDN_HEADS):
        st_ref[hh] = states[hh]


def _gdn(qkv, z, ab, cw8, hp, nw, expert_ws, *, batch, seq, tt):
    n = qkv.shape[0]
    per_b = seq // tt
    n_steps = batch * per_b
    assert N_EXPERTS % n_steps == 0
    share = N_EXPERTS // n_steps
    row = lambda b, j: (b * per_b + j, 0)
    const = lambda b, j: (0, 0)
    wrow = lambda b, j: (b * per_b + j, 0, 0)
    hv = GDN_HEADS * GDN_DV
    wspecs = [pl.BlockSpec((share,) + w.shape[1:], wrow) for w in expert_ws]
    return pl.pallas_call(
        functools.partial(_gdn_kernel, tt=tt),
        grid=(batch, per_b),
        in_specs=[pl.BlockSpec((tt, _GDN_QKV), row), pl.BlockSpec((tt, hv), row), pl.BlockSpec((tt, LANES), row),
                  pl.BlockSpec(cw8.shape, const), pl.BlockSpec(hp.shape, const), pl.BlockSpec(nw.shape, const)]
        + wspecs,
        out_specs=[pl.BlockSpec((tt, hv), row)] + wspecs,
        out_shape=[jax.ShapeDtypeStruct((n, hv), BF16)] + [jax.ShapeDtypeStruct(w.shape, BF16) for w in expert_ws],
        scratch_shapes=[pltpu.VMEM((tt + 8, _GDN_QKV), F32), pltpu.VMEM((GDN_HEADS, GDN_DK, GDN_DV), F32)],
        compiler_params=_cparams("parallel", "arbitrary"),
        name="gdn",
    )(qkv, z, ab, cw8, hp, nw, *expert_ws)


def _route(logits_t, bias_col):
    e, tm = logits_t.shape
    gsz = e // N_GROUPS
    scores = jax.nn.sigmoid(logits_t)
    biased = scores + bias_col
    sub = lax.broadcasted_iota(jnp.int32, (gsz, tm), 0)
    rows = []
    for g in range(N_GROUPS):
        blk = biased[g * gsz:(g + 1) * gsz]
        m1 = jnp.max(blk, axis=0, keepdims=True)
        first = jnp.min(jnp.where(blk == m1, sub, gsz), axis=0, keepdims=True)
        m2 = jnp.max(jnp.where(sub == first, -jnp.inf, blk), axis=0, keepdims=True)
        rows.append(m1 + m2)
    gs = jnp.concatenate(rows, axis=0)
    gi = lax.broadcasted_iota(jnp.int32, (N_GROUPS, tm), 0)
    grank = jnp.zeros((N_GROUPS, tm), F32)
    for g in range(N_GROUPS):
        r = gs[g:g + 1]
        grank = grank + jnp.where(r > gs, 1.0, jnp.where(r == gs, (gi > g).astype(F32), 0.0))
    gsel = grank < TOPK_GROUPS
    masked = jnp.concatenate(
        [jnp.where(gsel[g:g + 1], biased[g * gsz:(g + 1) * gsz], -jnp.inf) for g in range(N_GROUPS)], axis=0)
    ei = lax.broadcasted_iota(jnp.int32, (e, tm), 0)
    picks = []
    for _ in range(TOP_K):
        best = jnp.max(masked, axis=0, keepdims=True)
        pick = jnp.min(jnp.where(masked == best, ei, e), axis=0, keepdims=True)
        picks.append(pick)
        masked = jnp.where(ei == pick, -jnp.inf, masked)
    return scores, picks


SUBLANES = 4


def _store_tile_rows(ref, x):
    c = x.shape[1] // 2
    lo = pltpu.bitcast(x[:, :c].astype(BF16).astype(F32), jnp.uint32)
    hi = pltpu.bitcast(x[:, c:].astype(BF16).astype(F32), jnp.uint32)
    w = (hi & jnp.uint32(0xFFFF0000)) | (lo >> 16)
    for j in range(SUBLANES):
        ref[pl.ds(j, x.shape[0], stride=SUBLANES), :] = w[:, j * LANES:(j + 1) * LANES]


def _load_tile_rows(ref, r0, rows):
    w = jnp.concatenate(
        [ref[pl.ds(r0 * SUBLANES + j, rows, stride=SUBLANES), :] for j in range(SUBLANES)], axis=1)
    return jnp.concatenate([pltpu.bitcast(w << 16, F32), pltpu.bitcast(w & jnp.uint32(0xFFFF0000), F32)], axis=1)


def _tile_row(ref, sublane_offset):
    return ref.at[pl.ds(pl.multiple_of(sublane_offset, SUBLANES), SUBLANES), :]


def _mix_kernel(oa_ref, ob_ref, x_ref, mod_ref, woa_ref, wob_ref, n2_ref, rwt_ref, rb_ref, sg_ref, su_ref, sd_ref,
                xs_ref, h2t_ref, te_ref, tw_ref, tp_ref, cnt_ref, run_ref):
    @pl.when(pl.program_id(0) == 0)
    def _():
        run_ref[...] = jnp.zeros(run_ref.shape, F32)

    mod = mod_ref[0]
    g1, sh2, sc2, g2 = mod[2:3], mod[3:4], mod[4:5], mod[5:6]
    mix = _dot(oa_ref[...], woa_ref[...]) + _dot(ob_ref[...], wob_ref[...])
    x1 = x_ref[...] + g1 * mix
    h2 = _rms(x1, n2_ref[...]) * (1.0 + sc2) + sh2
    _store_tile_rows(h2t_ref, h2)
    h2b = h2.astype(BF16)
    hs = (_silu(_dot(h2b, sg_ref[...])) * _dot(h2b, su_ref[...])).astype(BF16)
    xs_ref[...] = x1 + g2 * _dot(hs, sd_ref[...])

    rh, rm, _ = _split3(rwt_ref[...])
    hh, hm, _ = _split3(h2)
    scores, picks = _route(_dot_nt(rh, hh) + (_dot_nt(rh, hm) + _dot_nt(rm, hh)), rb_ref[...])
    e, tm = scores.shape
    ei = lax.broadcasted_iota(jnp.int32, (e, tm), 0)
    hits = [ei == pick for pick in picks]
    sel = jnp.zeros((e, tm), F32)
    for hit in hits:
        sel = sel + jnp.where(hit, 1.0, 0.0)
    earlier = (lax.broadcasted_iota(jnp.int32, (tm, tm), 0) < lax.broadcasted_iota(jnp.int32, (tm, tm), 1))
    posmat = _dot(sel.astype(BF16), earlier.astype(BF16)) + run_ref[...]
    wts = jnp.concatenate([jnp.sum(jnp.where(hit, scores, 0.0), axis=0, keepdims=True) for hit in hits], axis=0)
    pos = [jnp.sum(jnp.where(hit, posmat, 0.0), axis=0, keepdims=True) for hit in hits]
    te_ref[0] = jnp.concatenate(picks, axis=0)
    tw_ref[0] = wts / jnp.sum(wts, axis=0, keepdims=True) * ROUTED_SCALE
    tp_ref[0] = jnp.concatenate(pos, axis=0).astype(jnp.int32)
    run_ref[...] = run_ref[...] + jnp.sum(sel, axis=1, keepdims=True)
    cnt_ref[...] = jnp.broadcast_to(run_ref[...], cnt_ref.shape)


def _mix(oa, ob, x2, mod8, woa, wob, n2, rwt, rb, sg, su, sd, *, seq, tm):
    n, d = x2.shape
    per_b = seq // tm
    nb = n // tm
    row = lambda i: (i, 0)
    blk3 = lambda i: (i, 0, 0)
    const = lambda i: (0, 0)
    return pl.pallas_call(
        _mix_kernel,
        grid=(nb,),
        in_specs=[pl.BlockSpec((tm, oa.shape[1]), row), pl.BlockSpec((tm, ob.shape[1]), row),
                  pl.BlockSpec((tm, d), row), pl.BlockSpec((1, 8, d), lambda i: (i // per_b, 0, 0)),
                  pl.BlockSpec(woa.shape, const), pl.BlockSpec(wob.shape, const), pl.BlockSpec((1, d), const),
                  pl.BlockSpec(rwt.shape, const), pl.BlockSpec(rb.shape, const),
                  pl.BlockSpec(sg.shape, const), pl.BlockSpec(su.shape, const), pl.BlockSpec(sd.shape, const)],
        out_specs=[pl.BlockSpec((tm, d), row), pl.BlockSpec((tm * SUBLANES, LANES), row),
                   pl.BlockSpec((1, TOP_K, tm), blk3), pl.BlockSpec((1, TOP_K, tm), blk3),
                   pl.BlockSpec((1, TOP_K, tm), blk3), pl.BlockSpec((N_EXPERTS, LANES), const)],
        out_shape=[jax.ShapeDtypeStruct((n, d), F32), jax.ShapeDtypeStruct((n * SUBLANES, LANES), jnp.uint32),
                   jax.ShapeDtypeStruct((nb, TOP_K, tm), jnp.int32), jax.ShapeDtypeStruct((nb, TOP_K, tm), F32),
                   jax.ShapeDtypeStruct((nb, TOP_K, tm), jnp.int32),
                   jax.ShapeDtypeStruct((N_EXPERTS, LANES), F32)],
        scratch_shapes=[pltpu.VMEM((N_EXPERTS, 1), F32)],
        compiler_params=_cparams("arbitrary"),
        name="mix",
    )(oa, ob, x2, mod8, woa, wob, n2, rwt, rb, sg, su, sd)


def _slots_kernel(ps_ref, ids_ref, pos_ref, o_ref):
    ids = ids_ref[...]
    start = jnp.zeros(ids.shape, jnp.int32)
    for e in range(N_EXPERTS):
        start = jnp.where(ids == e, ps_ref[e], start)
    o_ref[...] = (start + pos_ref[...]) * SUBLANES


def _slots(ps, ids, pos):
    spec = pl.BlockSpec(ids.shape, lambda i, ps: (0, 0, 0))
    return pl.pallas_call(
        _slots_kernel,
        grid_spec=pltpu.PrefetchScalarGridSpec(num_scalar_prefetch=1, grid=(1,),
                                               in_specs=[spec, spec], out_specs=spec),
        out_shape=jax.ShapeDtypeStruct(ids.shape, jnp.int32),
        compiler_params=_cparams("parallel"),
        name="slots",
    )(ps, ids, pos)


def _dispatch_kernel(ps_ref, slot_ref, h2t_ref, xs_hbm, zbuf, ring, zsem, sems, *, tm, blk, n_steps):
    i = pl.program_id(0)

    @pl.when(i == 0)
    def _():
        zbuf[...] = jnp.zeros(zbuf.shape, zbuf.dtype)

        def block_at(row):
            return xs_hbm.at[pl.ds(pl.multiple_of(row * SUBLANES, blk * SUBLANES), blk * SUBLANES), :]

        def tail(e):
            end = ps_ref[e + 1]
            return end > ps_ref[e], pltpu.make_async_copy(zbuf, block_at(end - blk), zsem)

        def unused(e):
            row = ps_ref[N_EXPERTS] + e * blk
            return row * SUBLANES < xs_hbm.shape[0], pltpu.make_async_copy(zbuf, block_at(row), zsem)

        for fill in (tail, unused):
            for e in range(N_EXPERTS):
                needed, cp = fill(e)
                pl.when(needed)(cp.start)
        for fill in (tail, unused):
            for e in range(N_EXPERTS):
                needed, cp = fill(e)
                pl.when(needed)(cp.wait)

    def send(b):
        ring[b] = h2t_ref[...]

        def issue(t, carry):
            for k in range(TOP_K):
                pltpu.make_async_copy(_tile_row(ring.at[b], t * SUBLANES), _tile_row(xs_hbm, slot_ref[0, k, t]),
                                      sems.at[b]).start(priority=k % 2)
            return carry

        lax.fori_loop(0, tm, issue, 0)

    def drain(b):
        for k in range(TOP_K):
            pltpu.make_async_copy(ring.at[b], xs_hbm.at[pl.ds(0, tm * SUBLANES), :], sems.at[b]).wait()

    for b in range(2):
        @pl.when(i % 2 == b)
        def _(b=b):
            send(b)

            @pl.when(i > 0)
            def _():
                drain(1 - b)

    @pl.when(i == n_steps - 1)
    def _():
        drain((n_steps - 1) % 2)


def _dispatch(ps, slot, h2t, *, n_slots, tm, blk):
    lanes = h2t.shape[1]
    n_steps = h2t.shape[0] // (tm * SUBLANES)
    grid_spec = pltpu.PrefetchScalarGridSpec(
        num_scalar_prefetch=1,
        grid=(n_steps,),
        in_specs=[pl.BlockSpec((1, TOP_K, tm), lambda i, ps: (i, 0, 0), memory_space=pltpu.SMEM),
                  pl.BlockSpec((tm * SUBLANES, lanes), lambda i, ps: (i, 0))],
        out_specs=pl.BlockSpec(memory_space=pl.ANY),
        scratch_shapes=[pltpu.VMEM((blk * SUBLANES, lanes), jnp.uint32),
                        pltpu.VMEM((2, tm * SUBLANES, lanes), jnp.uint32),
                        pltpu.SemaphoreType.DMA(()), pltpu.SemaphoreType.DMA((2,))],
    )
    return pl.pallas_call(
        functools.partial(_dispatch_kernel, tm=tm, blk=blk, n_steps=n_steps),
        grid_spec=grid_spec,
        out_shape=jax.ShapeDtypeStruct((n_slots * SUBLANES, lanes), jnp.uint32),
        compiler_params=_cparams("arbitrary"),
        name="dispatch",
    )(ps, slot, h2t)


def _experts_kernel(be_ref, nu_ref, x_ref, wg_ref, wu_ref, wd_ref, y_ref):
    del be_ref
    used = pl.program_id(0) < nu_ref[0]

    @pl.when(used)
    def _():
        xb = _load_tile_rows(x_ref, 0, x_ref.shape[0] // SUBLANES).astype(BF16)
        hb = (_silu(_dot(xb, wg_ref[0])) * _dot(xb, wu_ref[0])).astype(BF16)
        _store_tile_rows(y_ref, _dot(hb, wd_ref[0]))

    @pl.when(jnp.logical_not(used))
    def _():
        y_ref[...] = jnp.zeros(y_ref.shape, y_ref.dtype)


def _experts(block_e, n_used, xs, wg, wu, wd, *, blk):
    n_blocks = block_e.shape[0]
    rows_blk = (blk * SUBLANES, xs.shape[1])
    wsel = lambda i, be, nu: (be[i], 0, 0)
    grid_spec = pltpu.PrefetchScalarGridSpec(
        num_scalar_prefetch=2,
        grid=(n_blocks,),
        in_specs=[pl.BlockSpec(rows_blk, lambda i, be, nu: (jnp.minimum(i, nu[0] - 1), 0)),
                  pl.BlockSpec((1,) + wg.shape[1:], wsel), pl.BlockSpec((1,) + wu.shape[1:], wsel),
                  pl.BlockSpec((1,) + wd.shape[1:], wsel)],
        out_specs=pl.BlockSpec(rows_blk, lambda i, be, nu: (i, 0)),
    )
    return pl.pallas_call(
        _experts_kernel,
        grid_spec=grid_spec,
        out_shape=jax.ShapeDtypeStruct(xs.shape, xs.dtype),
        compiler_params=_cparams("arbitrary"),
        name="experts",
    )(block_e, n_used, xs, wg, wu, wd)


_GROUP = 32


def _combine_kernel(slot_ref, nxt_ref, tw_ref, xs_ref, mod_ref, fn_ref, y_hbm, o_ref, gbuf0, gbuf1, wt_ref, sems, *, tm):
    gbuf = (gbuf0, gbuf1)
    j = pl.program_id(0)
    last = pl.num_programs(0) - 1
    d = xs_ref.shape[1]

    def issue(src_ref, bi, b, t):
        for k in range(TOP_K):
            pltpu.make_async_copy(_tile_row(y_hbm, src_ref[bi, k, t]), _tile_row(gbuf[b], (k * tm + t) * SUBLANES),
                                  sems.at[b]).start(priority=k % 2)

    def drain(b):
        pltpu.make_async_copy(y_hbm.at[pl.ds(0, gbuf[b].shape[0]), :], gbuf[b], sems.at[b]).wait()

    def reduce_rows(b, r0):
        w8 = wt_ref[b, pl.ds(r0, _GROUP), :]
        acc = [jnp.zeros((_GROUP, LANES), F32) for _ in range(d // LANES)]
        for k in range(TOP_K):
            wk = w8[:, k:k + 1]
            for c in range(SUBLANES):
                word = gbuf[b][pl.ds((k * tm + r0) * SUBLANES + c, _GROUP, stride=SUBLANES), :]
                acc[c] = acc[c] + wk * pltpu.bitcast(word << 16, F32)
                acc[SUBLANES + c] = acc[SUBLANES + c] + wk * pltpu.bitcast(word & jnp.uint32(0xFFFF0000), F32)
        rows = pl.ds(b * tm + r0, _GROUP)
        g2 = mod_ref[0][5:6]
        x2 = jnp.concatenate(acc, axis=1) * g2 + xs_ref[rows, :]
        o_ref[rows, :] = _rms(x2, fn_ref[...])

    def phase(fin_b, issue_args):
        def body(g, carry):
            r0 = pl.multiple_of(g * _GROUP, _GROUP)
            if issue_args is not None:
                for dt in range(_GROUP):
                    issue(*issue_args, r0 + dt)
            reduce_rows(fin_b, r0)
            return carry

        lax.fori_loop(0, tm // _GROUP, body, 0)

    @pl.when(j == 0)
    def _():
        lax.fori_loop(0, tm, lambda t, c: (issue(slot_ref, 0, 0, t), c)[1], 0)

    for b in range(2):
        wt_ref[b] = jnp.concatenate([tw_ref[b], jnp.zeros((LANES - TOP_K, tm), F32)], axis=0).T
    drain(0)
    phase(0, (slot_ref, 1, 1))
    drain(1)
    pl.when(j < last)(lambda: phase(1, (nxt_ref, 0, 0)))
    pl.when(j == last)(lambda: phase(1, None))


def _combine(slot, tw, xs, mod8, fn, y, *, seq, tm):
    n, d = xs.shape
    nb = slot.shape[0]
    assert nb % 2 == 0 and (seq // tm) % 2 == 0
    per_b = seq // (2 * tm)
    pair = lambda j: (j, 0, 0)
    return pl.pallas_call(
        functools.partial(_combine_kernel, tm=tm),
        grid=(nb // 2,),
        in_specs=[pl.BlockSpec((2, TOP_K, tm), pair, memory_space=pltpu.SMEM),
                  pl.BlockSpec((1, TOP_K, tm), lambda j: (jnp.minimum(2 * j + 2, nb - 1), 0, 0),
                               memory_space=pltpu.SMEM),
                  pl.BlockSpec((2, TOP_K, tm), pair),
                  pl.BlockSpec((2 * tm, d), lambda j: (j, 0)),
                  pl.BlockSpec((1, 8, d), lambda j: (j // per_b, 0, 0)),
                  pl.BlockSpec((1, d), lambda j: (0, 0)),
                  pl.BlockSpec(memory_space=pl.ANY)],
        out_specs=pl.BlockSpec((2 * tm, d), lambda j: (j, 0)),
        out_shape=jax.ShapeDtypeStruct((n, d), F32),
        scratch_shapes=[pltpu.VMEM((TOP_K * tm * SUBLANES, y.shape[1]), y.dtype),
                        pltpu.VMEM((TOP_K * tm * SUBLANES, y.shape[1]), y.dtype),
                        pltpu.VMEM((2, tm, LANES), F32), pltpu.SemaphoreType.DMA((2,))],
        compiler_params=_cparams("arbitrary"),
        name="combine",
    )(slot, slot, tw, xs, mod8, fn, y)


def _expert_runs(counts, n_blocks, blk):
    padded = (counts + blk - 1) // blk * blk
    pad_end = jnp.cumsum(padded)
    run_start = jnp.concatenate([pad_end - padded, pad_end[-1:]]).astype(jnp.int32)
    block_row = jnp.arange(n_blocks, dtype=jnp.int32) * blk
    block_e = jnp.minimum(jnp.sum(pad_end[None, :] <= block_row[:, None], axis=1), N_EXPERTS - 1).astype(jnp.int32)
    n_used = (pad_end[-1:] // blk).astype(jnp.int32)
    return run_start, block_e, n_used


def _pad_lanes(w, width):
    return jnp.pad(w, ((0, 0), (0, width - w.shape[1])))


def _tiles(seq):
    t = dict(tm_proj=512, tq=512, tt=256, tm_moe=512, blk=512)
    assert all(seq % v == 0 for k, v in t.items() if k != "blk")
    assert seq % (_Q_TILES * t["tq"]) == 0 and (seq // t["tm_moe"]) % 2 == 0
    return t


def _layer(x, c, positions, ada_w, ada_b, norm1_w, w_in, q_a_norm_w, wq_b, kv_a_norm_w, wkv_b,
           gdn_conv_w, gdn_a_log, gdn_dt_bias, gdn_norm_w, w_out, norm2_w, router_w, router_bias,
           exp_w_gate, exp_w_up, exp_w_down, sh_w_gate, sh_w_up, sh_w_down, final_norm_w,
           *, tm_proj, tq, tt, tm_moe, blk):
    batch, seq, d = x.shape
    n = batch * seq
    x2 = x.reshape(n, d)

    c8 = jnp.zeros((8, d), F32).at[:batch].set(c)
    mod = _adaln(c8, ada_w, ada_b.reshape(1, -1))
    mod8 = jnp.pad(mod[:batch].reshape(batch, 6, d), ((0, 0), (0, 2), (0, 0)))

    hq = MLA_NOPE + MLA_ROPE
    w_cq, w_ckv, w_kpe, w_qkv, w_z, w_a, w_b = jnp.split(
        w_in, [_C_KV, _C_KV + MLA_KV_RANK, _C_KV + MLA_KV_RANK + MLA_ROPE,
               _C_KV + MLA_KV_RANK + MLA_ROPE + _GDN_QKV,
               _C_KV + MLA_KV_RANK + MLA_ROPE + _GDN_QKV + GDN_HEADS * GDN_DV,
               _C_KV + MLA_KV_RANK + MLA_ROPE + _GDN_QKV + GDN_HEADS * GDN_DV + GDN_HEADS], axis=1)
    win = jnp.concatenate([w_cq, w_ckv, _pad_lanes(w_kpe, LANES), w_qkv, w_z,
                           _pad_lanes(jnp.concatenate([w_a, w_b], axis=1), LANES)], axis=1).astype(BF16)
    wq = jnp.pad(wq_b.reshape(MLA_Q_RANK, MLA_HEADS, hq),
                 ((0, 0), (0, 0), (0, MLA_QK_PAD - hq))).reshape(MLA_Q_RANK, MLA_HEADS * MLA_QK_PAD).astype(BF16)
    wkv4 = wkv_b.reshape(MLA_KV_RANK, MLA_HEADS, MLA_NOPE + MLA_V)
    wkv = jnp.concatenate([wkv4[:, :, :MLA_NOPE].reshape(MLA_KV_RANK, -1),
                           wkv4[:, :, MLA_NOPE:].reshape(MLA_KV_RANK, -1)], axis=1).astype(BF16)
    inv_freq = 1.0 / (ROPE_THETA ** (jnp.arange(0, MLA_ROPE, 2, dtype=F32) / MLA_ROPE))
    invf = _pad_lanes(jnp.concatenate([inv_freq, inv_freq])[None, :], LANES)

    q, k, v, qkv, z, ab = _proj(x2, mod8, positions.reshape(n, 1), norm1_w.reshape(1, d), win,
                                q_a_norm_w.reshape(1, -1), wq, kv_a_norm_w.reshape(1, -1), wkv, invf,
                                seq=seq, tm=tm_proj)
    out_a = _attn(q, k, v, batch=batch, seq=seq, tq=tq)

    cw8 = jnp.pad(gdn_conv_w, ((0, 8 - GDN_CONV), (0, 0)))
    hp = _pad_lanes(jnp.stack([gdn_a_log, gdn_dt_bias]), LANES)
    hp = jnp.pad(hp, ((0, 6), (0, 0)))
    out_b, wg_b, wu_b, wd_b = _gdn(qkv, z, ab, cw8, hp, gdn_norm_w.reshape(1, -1),
                                   (exp_w_gate, exp_w_up, exp_w_down), batch=batch, seq=seq, tt=tt)

    ha = MLA_HEADS * MLA_V
    xs, h2t, ids, tw, pos, counts = _mix(
        out_a, out_b, x2, mod8, w_out[:ha].astype(BF16), w_out[ha:].astype(BF16), norm2_w.reshape(1, d),
        router_w.T, router_bias.reshape(-1, 1), sh_w_gate.astype(BF16), sh_w_up.astype(BF16),
        sh_w_down.astype(BF16), seq=seq, tm=tm_moe)

    n_blocks = n * TOP_K // blk + N_EXPERTS
    run_start, block_e, n_used = _expert_runs(counts[:, 0].astype(jnp.int32), n_blocks, blk)
    slot = _slots(run_start, ids, pos)
    xsort = _dispatch(run_start, slot, h2t, n_slots=n_blocks * blk, tm=tm_moe, blk=blk)
    y = _experts(block_e, n_used, xsort, wg_b, wu_b, wd_b, blk=blk)
    out = _combine(slot, tw, xs, mod8, final_norm_w.reshape(1, d), y, seq=seq, tm=tm_moe)
    return out.reshape(batch, seq, d)


def kernel(x, c, positions, ada_w, ada_b, norm1_w, w_in, q_a_norm_w, wq_b, kv_a_norm_w, wkv_b, gdn_conv_w,
           gdn_a_log, gdn_dt_bias, gdn_norm_w, w_out, norm2_w, router_w, router_bias, exp_w_gate, exp_w_up,
           exp_w_down, sh_w_gate, sh_w_up, sh_w_down, final_norm_w):
    assert ada_w.shape[0] == 1, "single layer"
    return _layer(
        x, c, positions, ada_w[0], ada_b[0], norm1_w[0], w_in[0], q_a_norm_w[0], wq_b[0], kv_a_norm_w[0],
        wkv_b[0], gdn_conv_w[0], gdn_a_log[0], gdn_dt_bias[0], gdn_norm_w[0], w_out[0], norm2_w[0],
        router_w[0], router_bias[0], exp_w_gate[0], exp_w_up[0], exp_w_down[0], sh_w_gate[0], sh_w_up[0],
        sh_w_down[0], final_norm_w, **_tiles(x.shape[1]))
```

```python
import functools

import jax
import jax.numpy as jnp
from jax import lax
from jax.experimental import pallas as pl
from jax.experimental.pallas import tpu as pltpu

F32 = jnp.float32
BF16 = jnp.bfloat16

CHUNK = 64
NORM_EPS = 1e-6
MLA_HEADS = 4
MLA_Q_RANK = 384
MLA_KV_RANK = 256
MLA_NOPE = 128
MLA_ROPE = 64
MLA_V = 128
ROPE_THETA = 10000.0
GDN_HEADS = 4
GDN_DK = 128
GDN_DV = 128
GDN_CONV = 4
N_EXPERTS = 64
N_GROUPS = 8
TOPK_GROUPS = 4
TOP_K = 8
ROUTED_SCALE = 2.5
LOG2E = 1.4426950408889634

LANES = 128
MLA_QK_PAD = 256
VMEM_LIMIT = 56 * 1024 * 1024


def _cparams(*sem):
    return pltpu.CompilerParams(dimension_semantics=sem, vmem_limit_bytes=VMEM_LIMIT)


def _dot(a, b):
    return jnp.dot(a, b, preferred_element_type=F32)


def _split3(a):
    hi = a.astype(BF16)
    r = a - hi.astype(F32)
    mid = r.astype(BF16)
    return hi, mid, (r - mid.astype(F32)).astype(BF16)


def _dot_x3(a, b):
    ah, am, _ = a
    bh, bm, _ = b
    return _dot(ah, bh) + (_dot(ah, bm) + _dot(am, bh))


def _dot_nt(a, b):
    return lax.dot_general(a, b, (((1,), (1,)), ((), ())), preferred_element_type=F32)


def _dot_tn(a, b):
    return lax.dot_general(a, b, (((0,), (0,)), ((), ())), preferred_element_type=F32)


def _silu(x):
    return x * jax.nn.sigmoid(x)


def _rms(x, w):
    return x * lax.rsqrt(jnp.mean(x * x, axis=-1, keepdims=True) + NORM_EPS) * w


def _adaln_kernel(c_ref, w_ref, b_ref, o_ref):
    a = _silu(c_ref[...]).astype(BF16)
    o_ref[...] = _dot(a, w_ref[...].astype(BF16)) + b_ref[...]


def _adaln(c8, ada_w, ada_b):
    d = c8.shape[1]
    n_out = ada_w.shape[1]
    return pl.pallas_call(
        _adaln_kernel,
        grid=(n_out // d,),
        in_specs=[pl.BlockSpec((8, d), lambda j: (0, 0)),
                  pl.BlockSpec((d, d), lambda j: (0, j)),
                  pl.BlockSpec((1, d), lambda j: (0, j))],
        out_specs=pl.BlockSpec((8, d), lambda j: (0, j)),
        out_shape=jax.ShapeDtypeStruct((8, n_out), F32),
        compiler_params=_cparams("parallel"),
        name="adaln",
    )(c8, ada_w, ada_b)


_C_Q = 0
_C_KV = _C_Q + MLA_Q_RANK
_C_KPE = _C_KV + MLA_KV_RANK
_C_QKV = _C_KPE + LANES
_GDN_QKV = GDN_HEADS * (2 * GDN_DK + GDN_DV)
_C_Z = _C_QKV + _GDN_QKV
_C_AB = _C_Z + GDN_HEADS * GDN_DV
_D_IN_PAD = _C_AB + LANES


def _rope(xb, cos, sin):
    half = MLA_ROPE // 2
    lane = lax.broadcasted_iota(jnp.int32, xb.shape, 1)
    rot = jnp.where(lane < half, -pltpu.roll(xb, LANES - half, 1), pltpu.roll(xb, half, 1))
    return xb * cos + rot * sin


def _proj_kernel(x_ref, mod_ref, pos_ref, n1_ref, win_ref, qan_ref, wq_ref, kvan_ref, wkv_ref, invf_ref,
                 q_ref, k_ref, v_ref, qkv_ref, z_ref, ab_ref):
    x = x_ref[...]
    mod = mod_ref[0]
    sh1, sc1 = mod[0:1], mod[1:2]
    h = (_rms(x, n1_ref[...]) * (1.0 + sc1) + sh1).astype(BF16)
    proj = _dot(h, win_ref[...])

    ang = pos_ref[...].astype(F32) * invf_ref[...]
    cos, sin = jnp.cos(ang), jnp.sin(ang)

    cq = proj[:, _C_Q:_C_Q + MLA_Q_RANK]
    qn = _rms(cq, qan_ref[...]).astype(BF16)
    q = _dot(qn, wq_ref[...]) * ((MLA_NOPE + MLA_ROPE) ** -0.5 * LOG2E)
    for hh in range(MLA_HEADS):
        c0 = hh * MLA_QK_PAD
        q_ref[:, c0:c0 + MLA_NOPE] = q[:, c0:c0 + MLA_NOPE].astype(BF16)
        q_ref[:, c0 + MLA_NOPE:c0 + MLA_QK_PAD] = _rope(q[:, c0 + MLA_NOPE:c0 + MLA_QK_PAD], cos, sin).astype(BF16)

    ckv = proj[:, _C_KV:_C_KV + MLA_KV_RANK]
    kvn = _rms(ckv, kvan_ref[...]).astype(BF16)
    kv = _dot(kvn, wkv_ref[...])
    kpe = _rope(proj[:, _C_KPE:_C_KPE + LANES], cos, sin).astype(BF16)
    for hh in range(MLA_HEADS):
        c0 = hh * MLA_QK_PAD
        k_ref[:, c0:c0 + MLA_NOPE] = kv[:, hh * MLA_NOPE:(hh + 1) * MLA_NOPE].astype(BF16)
        k_ref[:, c0 + MLA_NOPE:c0 + MLA_QK_PAD] = kpe
    v_ref[...] = kv[:, MLA_HEADS * MLA_NOPE:].astype(BF16)

    qkv_ref[...] = proj[:, _C_QKV:_C_QKV + _GDN_QKV].astype(qkv_ref.dtype)
    z_ref[...] = proj[:, _C_Z:_C_Z + GDN_HEADS * GDN_DV].astype(z_ref.dtype)
    ab_ref[...] = proj[:, _C_AB:_C_AB + LANES]


def _proj(x2, mod8, pos2, n1, win, qan, wq, kvan, wkv, invf, *, seq, tm):
    n, d = x2.shape
    per_b = seq // tm
    row = lambda i: (i, 0)
    const = lambda i: (0, 0)
    hq = MLA_HEADS * MLA_QK_PAD
    hv = MLA_HEADS * MLA_V
    return pl.pallas_call(
        _proj_kernel,
        grid=(n // tm,),
        in_specs=[pl.BlockSpec((tm, d), row),
                  pl.BlockSpec((1, 8, d), lambda i: (i // per_b, 0, 0)),
                  pl.BlockSpec((tm, 1), row),
                  pl.BlockSpec((1, d), const),
                  pl.BlockSpec(win.shape, const),
                  pl.BlockSpec(qan.shape, const),
                  pl.BlockSpec(wq.shape, const),
                  pl.BlockSpec(kvan.shape, const),
                  pl.BlockSpec(wkv.shape, const),
                  pl.BlockSpec(invf.shape, const)],
        out_specs=[pl.BlockSpec((tm, hq), row), pl.BlockSpec((tm, hq), row), pl.BlockSpec((tm, hv), row),
                   pl.BlockSpec((tm, _GDN_QKV), row), pl.BlockSpec((tm, GDN_HEADS * GDN_DV), row),
                   pl.BlockSpec((tm, LANES), row)],
        out_shape=[jax.ShapeDtypeStruct((n, hq), BF16), jax.ShapeDtypeStruct((n, hq), BF16),
                   jax.ShapeDtypeStruct((n, hv), BF16), jax.ShapeDtypeStruct((n, _GDN_QKV), BF16),
                   jax.ShapeDtypeStruct((n, GDN_HEADS * GDN_DV), BF16), jax.ShapeDtypeStruct((n, LANES), F32)],
        compiler_params=_cparams("parallel"),
        name="proj",
    )(x2, mod8, pos2, n1, win, qan, wq, kvan, wkv, invf)


_Q_TILES = 2


def _attn_kernel(q_ref, k_ref, v_ref, o_ref, *, tq):
    i = pl.program_id(1)
    chains = [(qt, hh) for qt in range(_Q_TILES) for hh in range(MLA_HEADS)]

    ones_col = jnp.where(lax.broadcasted_iota(jnp.int32, (tq, MLA_V), 1) == 0, 1.0, 0.0).astype(BF16)

    def chain_step(qt, hh, r0, carry, mask):
        m, acc = carry
        s = _dot_nt(q_ref[qt * tq:(qt + 1) * tq, hh * MLA_QK_PAD:(hh + 1) * MLA_QK_PAD],
                    k_ref[pl.ds(r0, tq), hh * MLA_QK_PAD:(hh + 1) * MLA_QK_PAD])
        if mask is not None:
            s = jnp.where(mask, s, -jnp.inf)
        m_new = jnp.maximum(m, jnp.max(s, axis=-1, keepdims=True))
        alpha = jnp.exp2(m - m_new)
        p = jnp.exp2((s - m_new).astype(BF16))
        v_ext = jnp.concatenate([v_ref[pl.ds(r0, tq), hh * MLA_V:(hh + 1) * MLA_V], ones_col], axis=1)
        return m_new, alpha * acc + _dot(p, v_ext)

    def block(r0, carry, masks):
        return tuple(carry[c] if masks[qt] is False else chain_step(qt, hh, r0, carry[c], masks[qt])
                     for c, (qt, hh) in enumerate(chains))

    init = tuple((jnp.full((tq, 1), -jnp.inf, F32), jnp.zeros((tq, 2 * MLA_V), F32)) for _ in chains)
    def full_tiles(j, c):
        for kt in range(_Q_TILES):
            c = block(pl.multiple_of((_Q_TILES * j + kt) * tq, tq), c, (None,) * _Q_TILES)
        return c

    carry = lax.fori_loop(0, i, full_tiles, init)
    rq = lax.broadcasted_iota(jnp.int32, (tq, tq), 0) // CHUNK
    ck = lax.broadcasted_iota(jnp.int32, (tq, tq), 1) // CHUNK
    diag = ck <= rq
    for kt in range(_Q_TILES):
        masks = tuple(False if qt < kt else (diag if qt == kt else None) for qt in range(_Q_TILES))
        carry = block(pl.multiple_of((_Q_TILES * i + kt) * tq, tq), carry, masks)
    for c, (qt, hh) in enumerate(chains):
        _, acc = carry[c]
        o_ref[qt * tq:(qt + 1) * tq, hh * MLA_V:(hh + 1) * MLA_V] = (
            acc[:, :MLA_V] / acc[:, MLA_V:MLA_V + 1]).astype(o_ref.dtype)


def _attn(q, k, v, *, batch, seq, tq):
    n = q.shape[0]
    rows = _Q_TILES * tq
    nq = seq // rows
    return pl.pallas_call(
        functools.partial(_attn_kernel, tq=tq),
        grid=(batch, nq),
        in_specs=[pl.BlockSpec((rows, q.shape[1]), lambda b, i: (b * nq + i, 0)),
                  pl.BlockSpec((seq, k.shape[1]), lambda b, i: (b, 0)),
                  pl.BlockSpec((seq, v.shape[1]), lambda b, i: (b, 0))],
        out_specs=pl.BlockSpec((rows, v.shape[1]), lambda b, i: (b * nq + i, 0)),
        out_shape=jax.ShapeDtypeStruct((n, v.shape[1]), BF16),
        compiler_params=_cparams("parallel", "arbitrary"),
        name="attn",
    )(q, k, v)


_SUPER = 2 * CHUNK


def _unit_lower_inverses(lows):
    n = lows[0].shape[0]
    eye = (lax.broadcasted_iota(jnp.int32, (n, n), 0) == lax.broadcasted_iota(jnp.int32, (n, n), 1)).astype(F32)
    levels = CHUNK.bit_length() - 1
    ps = [-low for low in lows]
    ts = [eye + p for p in ps]
    for level in range(1, levels):
        parts = [_split3(p) for p in ps]
        ps = [_dot_x3(s, s) for s in parts]
        if level == 1:
            continue
        ts = [t + _dot_x3(s, _split3(t)) for t, s in zip(ts, parts)]
    return [t + _dot_x3(_split3(p), _split3(t)) for t, p in zip(ts, ps)]


def _gdn_kernel(qkv_ref, z_ref, ab_ref, cw_ref, hp_ref, nw_ref, wg_ref, wu_ref, wd_ref,
                o_ref, wgb_ref, wub_ref, wdb_ref, xe_ref, st_ref, *, tt):
    j = pl.program_id(1)
    hd = GDN_HEADS * GDN_DK
    for src, dst in ((wg_ref, wgb_ref), (wu_ref, wub_ref), (wd_ref, wdb_ref)):
        dst[...] = src[...].astype(BF16)

    @pl.when(j == 0)
    def _():
        xe_ref[0:8, :] = jnp.zeros((8, xe_ref.shape[1]), F32)
        st_ref[...] = jnp.zeros(st_ref.shape, F32)

    xe_ref[8:8 + tt, :] = qkv_ref[...].astype(F32)
    cw = cw_ref[...]
    y = xe_ref[8:8 + tt, :] * cw[GDN_CONV - 1:GDN_CONV]
    for i in range(1, GDN_CONV):
        y = y + xe_ref[8 - i:8 - i + tt, :] * cw[GDN_CONV - 1 - i:GDN_CONV - i]
    xe_ref[0:8, :] = xe_ref[tt:tt + 8, :]
    act = _silu(y)

    ab = ab_ref[...]
    hp = hp_ref[...]
    pre = ab + hp[1:2]
    softplus = jnp.maximum(pre, 0.0) + jnp.log1p(jnp.exp(-jnp.abs(pre)))
    g_all = -jnp.exp(hp[0:1]) * softplus
    beta_all = jax.nn.sigmoid(ab)

    ri = lax.broadcasted_iota(jnp.int32, (tt, tt), 0)
    ci = lax.broadcasted_iota(jnp.int32, (tt, tt), 1)
    tri = ((ri // CHUNK == ci // CHUNK) & (ci <= ri)).astype(BF16)
    gc = sum(_dot(tri, part) for part in _split3(g_all))
    gct = gc.T

    rs = lax.broadcasted_iota(jnp.int32, (_SUPER, _SUPER), 0)
    cs = lax.broadcasted_iota(jnp.int32, (_SUPER, _SUPER), 1)
    same = rs // CHUNK == cs // CHUNK
    incl = same & (cs <= rs)
    strict = same & (cs < rs)

    tiles = [(s * _SUPER, hh) for s in range(tt // _SUPER) for hh in range(GDN_HEADS)]
    pre_t = []
    for r0, hh in tiles:
        qh = act[r0:r0 + _SUPER, hh * GDN_DK:(hh + 1) * GDN_DK]
        kh = act[r0:r0 + _SUPER, hd + hh * GDN_DK:hd + (hh + 1) * GDN_DK]
        vh = act[r0:r0 + _SUPER, 2 * hd + hh * GDN_DV:2 * hd + (hh + 1) * GDN_DV]
        qh = qh * lax.rsqrt(jnp.sum(qh * qh, axis=-1, keepdims=True) + 1e-6) * (GDN_DK ** -0.5)
        kh = kh * lax.rsqrt(jnp.sum(kh * kh, axis=-1, keepdims=True) + 1e-6)
        beta = beta_all[r0:r0 + _SUPER, GDN_HEADS + hh:GDN_HEADS + hh + 1]
        gcol = gc[r0:r0 + _SUPER, hh:hh + 1]
        grow = gct[hh:hh + 1, r0:r0 + _SUPER]
        decay = jnp.exp(jnp.where(incl, gcol - grow, -jnp.inf))
        eg = jnp.exp(gcol)
        kb = kh * beta
        pre_t.append(dict(kh=kh, khb=kh.astype(BF16), kbb=kb.astype(BF16), vbb=(vh * beta).astype(BF16),
                          kgb=(kb * eg).astype(BF16), qb=qh.astype(BF16), qg=qh * eg, gcol=gcol, decay=decay))
    lows = [jnp.where(strict, _dot_nt(p["kbb"], p["khb"]) * p["decay"], 0.0) for p in pre_t]
    tinvs = [t.astype(BF16) for t in _unit_lower_inverses(lows)]
    us = [_dot(t, p["vbb"]).astype(BF16) for t, p in zip(tinvs, pre_t)]
    ws = [_dot(t, p["kgb"]).astype(BF16) for t, p in zip(tinvs, pre_t)]
    atts = [(_dot_nt(p["qb"], p["khb"]) * p["decay"]).astype(BF16) for p in pre_t]

    steps = []
    for cc in range(tt // CHUNK):
        r0 = cc * CHUNK
        a0 = r0 % _SUPER
        for hh in range(GDN_HEADS):
            ti = (r0 // _SUPER) * GDN_HEADS + hh
            p = pre_t[ti]
            gl = gc[r0 + CHUNK - 1:r0 + CHUNK, hh:hh + 1]
            kd = (p["kh"][a0:a0 + CHUNK] * jnp.exp(gl - p["gcol"][a0:a0 + CHUNK])).astype(BF16)
            u_c, w_c = us[ti][a0:a0 + CHUNK], ws[ti][a0:a0 + CHUNK]
            att_c = atts[ti][a0:a0 + CHUNK, a0:a0 + CHUNK]
            steps.append(dict(hh=hh, r0=r0, gain=jnp.exp(gl), kw=_dot_tn(kd, w_c).astype(BF16), ku=_dot_tn(kd, u_c),
                              qs=(p["qg"][a0:a0 + CHUNK] - _dot(att_c, w_c)).astype(BF16), ou=_dot(att_c, u_c)))

    states = [st_ref[hh] for hh in range(GDN_HEADS)]
    for st in steps:
        hh, r0 = st["hh"], st["r0"]
        sb = states[hh].astype(BF16)
        o = _dot(st["qs"], sb) + st["ou"]
        states[hh] = states[hh] * st["gain"] + (st["ku"] - _dot(st["kw"], sb))
        zz = z_ref[r0:r0 + CHUNK, hh * GDN_DV:(hh + 1) * GDN_DV].astype(F32)
        o_ref[r0:r0 + CHUNK, hh * GDN_DV:(hh + 1) * GDN_DV] = (_rms(o, nw_ref[...]) * _silu(zz)).astype(o_ref.dtype)
    for hh in range(GDN_HEADS):
        st_ref[hh] = states[hh]


def _gdn(qkv, z, ab, cw8, hp, nw, expert_ws, *, batch, seq, tt):
    n = qkv.shape[0]
    per_b = seq // tt
    n_steps = batch * per_b
    assert N_EXPERTS % n_steps == 0
    share = N_EXPERTS // n_steps
    row = lambda b, j: (b * per_b + j, 0)
    const = lambda b, j: (0, 0)
    wrow = lambda b, j: (b * per_b + j, 0, 0)
    hv = GDN_HEADS * GDN_DV
    wspecs = [pl.BlockSpec((share,) + w.shape[1:], wrow) for w in expert_ws]
    return pl.pallas_call(
        functools.partial(_gdn_kernel, tt=tt),
        grid=(batch, per_b),
        in_specs=[pl.BlockSpec((tt, _GDN_QKV), row), pl.BlockSpec((tt, hv), row), pl.BlockSpec((tt, LANES), row),
                  pl.BlockSpec(cw8.shape, const), pl.BlockSpec(hp.shape, const), pl.BlockSpec(nw.shape, const)]
        + wspecs,
        out_specs=[pl.BlockSpec((tt, hv), row)] + wspecs,
        out_shape=[jax.ShapeDtypeStruct((n, hv), BF16)] + [jax.ShapeDtypeStruct(w.shape, BF16) for w in expert_ws],
        scratch_shapes=[pltpu.VMEM((tt + 8, _GDN_QKV), F32), pltpu.VMEM((GDN_HEADS, GDN_DK, GDN_DV), F32)],
        compiler_params=_cparams("parallel", "arbitrary"),
        name="gdn",
    )(qkv, z, ab, cw8, hp, nw, *expert_ws)


def _route(logits_t, bias_col):
    e, tm = logits_t.shape
    gsz = e // N_GROUPS
    scores = jax.nn.sigmoid(logits_t)
    biased = scores + bias_col
    sub = lax.broadcasted_iota(jnp.int32, (gsz, tm), 0)
    rows = []
    for g in range(N_GROUPS):
        blk = biased[g * gsz:(g + 1) * gsz]
        m1 = jnp.max(blk, axis=0, keepdims=True)
        first = jnp.min(jnp.where(blk == m1, sub, gsz), axis=0, keepdims=True)
        m2 = jnp.max(jnp.where(sub == first, -jnp.inf, blk), axis=0, keepdims=True)
        rows.append(m1 + m2)
    gs = jnp.concatenate(rows, axis=0)
    gi = lax.broadcasted_iota(jnp.int32, (N_GROUPS, tm), 0)
    grank = jnp.zeros((N_GROUPS, tm), F32)
    for g in range(N_GROUPS):
        r = gs[g:g + 1]
        grank = grank + jnp.where(r > gs, 1.0, jnp.where(r == gs, (gi > g).astype(F32), 0.0))
    gsel = grank < TOPK_GROUPS
    masked = jnp.concatenate(
        [jnp.where(gsel[g:g + 1], biased[g * gsz:(g + 1) * gsz], -jnp.inf) for g in range(N_GROUPS)], axis=0)
    ei = lax.broadcasted_iota(jnp.int32, (e, tm), 0)
    picks = []
    for _ in range(TOP_K):
        best = jnp.max(masked, axis=0, keepdims=True)
        pick = jnp.min(jnp.where(masked == best, ei, e), axis=0, keepdims=True)
        picks.append(pick)
        masked = jnp.where(ei == pick, -jnp.inf, masked)
    return scores, picks


SUBLANES = 4


def _store_tile_rows(ref, x):
    c = x.shape[1] // 2
    lo = pltpu.bitcast(x[:, :c].astype(BF16).astype(F32), jnp.uint32)
    hi = pltpu.bitcast(x[:, c:].astype(BF16).astype(F32), jnp.uint32)
    w = (hi & jnp.uint32(0xFFFF0000)) | (lo >> 16)
    for j in range(SUBLANES):
        ref[pl.ds(j, x.shape[0], stride=SUBLANES), :] = w[:, j * LANES:(j + 1) * LANES]


def _load_tile_rows(ref, r0, rows):
    w = jnp.concatenate(
        [ref[pl.ds(r0 * SUBLANES + j, rows, stride=SUBLANES), :] for j in range(SUBLANES)], axis=1)
    return jnp.concatenate([pltpu.bitcast(w << 16, F32), pltpu.bitcast(w & jnp.uint32(0xFFFF0000), F32)], axis=1)


def _tile_row(ref, sublane_offset):
    return ref.at[pl.ds(pl.multiple_of(sublane_offset, SUBLANES), SUBLANES), :]


def _mix_kernel(oa_ref, ob_ref, x_ref, mod_ref, woa_ref, wob_ref, n2_ref, rwt_ref, rb_ref, sg_ref, su_ref, sd_ref,
                xs_ref, h2t_ref, te_ref, tw_ref, tp_ref, cnt_ref, run_ref):
    @pl.when(pl.program_id(0) == 0)
    def _():
        run_ref[...] = jnp.zeros(run_ref.shape, F32)

    mod = mod_ref[0]
    g1, sh2, sc2, g2 = mod[2:3], mod[3:4], mod[4:5], mod[5:6]
    mix = _dot(oa_ref[...], woa_ref[...]) + _dot(ob_ref[...], wob_ref[...])
    x1 = x_ref[...] + g1 * mix
    h2 = _rms(x1, n2_ref[...]) * (1.0 + sc2) + sh2
    _store_tile_rows(h2t_ref, h2)
    h2b = h2.astype(BF16)
    hs = (_silu(_dot(h2b, sg_ref[...])) * _dot(h2b, su_ref[...])).astype(BF16)
    xs_ref[...] = x1 + g2 * _dot(hs, sd_ref[...])

    rh, rm, _ = _split3(rwt_ref[...])
    hh, hm, _ = _split3(h2)
    scores, picks = _route(_dot_nt(rh, hh) + (_dot_nt(rh, hm) + _dot_nt(rm, hh)), rb_ref[...])
    e, tm = scores.shape
    ei = lax.broadcasted_iota(jnp.int32, (e, tm), 0)
    hits = [ei == pick for pick in picks]
    sel = jnp.zeros((e, tm), F32)
    for hit in hits:
        sel = sel + jnp.where(hit, 1.0, 0.0)
    earlier = (lax.broadcasted_iota(jnp.int32, (tm, tm), 0) < lax.broadcasted_iota(jnp.int32, (tm, tm), 1))
    posmat = _dot(sel.astype(BF16), earlier.astype(BF16)) + run_ref[...]
    wts = jnp.concatenate([jnp.sum(jnp.where(hit, scores, 0.0), axis=0, keepdims=True) for hit in hits], axis=0)
    pos = [jnp.sum(jnp.where(hit, posmat, 0.0), axis=0, keepdims=True) for hit in hits]
    te_ref[0] = jnp.concatenate(picks, axis=0)
    tw_ref[0] = wts / jnp.sum(wts, axis=0, keepdims=True) * ROUTED_SCALE
    tp_ref[0] = jnp.concatenate(pos, axis=0).astype(jnp.int32)
    run_ref[...] = run_ref[...] + jnp.sum(sel, axis=1, keepdims=True)
    cnt_ref[...] = jnp.broadcast_to(run_ref[...], cnt_ref.shape)


def _mix(oa, ob, x2, mod8, woa, wob, n2, rwt, rb, sg, su, sd, *, seq, tm):
    n, d = x2.shape
    per_b = seq // tm
    nb = n // tm
    row = lambda i: (i, 0)
    blk3 = lambda i: (i, 0, 0)
    const = lambda i: (0, 0)
    return pl.pallas_call(
        _mix_kernel,
        grid=(nb,),
        in_specs=[pl.BlockSpec((tm, oa.shape[1]), row), pl.BlockSpec((tm, ob.shape[1]), row),
                  pl.BlockSpec((tm, d), row), pl.BlockSpec((1, 8, d), lambda i: (i // per_b, 0, 0)),
                  pl.BlockSpec(woa.shape, const), pl.BlockSpec(wob.shape, const), pl.BlockSpec((1, d), const),
                  pl.BlockSpec(rwt.shape, const), pl.BlockSpec(rb.shape, const),
                  pl.BlockSpec(sg.shape, const), pl.BlockSpec(su.shape, const), pl.BlockSpec(sd.shape, const)],
        out_specs=[pl.BlockSpec((tm, d), row), pl.BlockSpec((tm * SUBLANES, LANES), row),
                   pl.BlockSpec((1, TOP_K, tm), blk3), pl.BlockSpec((1, TOP_K, tm), blk3),
                   pl.BlockSpec((1, TOP_K, tm), blk3), pl.BlockSpec((N_EXPERTS, LANES), const)],
        out_shape=[jax.ShapeDtypeStruct((n, d), F32), jax.ShapeDtypeStruct((n * SUBLANES, LANES), jnp.uint32),
                   jax.ShapeDtypeStruct((nb, TOP_K, tm), jnp.int32), jax.ShapeDtypeStruct((nb, TOP_K, tm), F32),
                   jax.ShapeDtypeStruct((nb, TOP_K, tm), jnp.int32),
                   jax.ShapeDtypeStruct((N_EXPERTS, LANES), F32)],
        scratch_shapes=[pltpu.VMEM((N_EXPERTS, 1), F32)],
        compiler_params=_cparams("arbitrary"),
        name="mix",
    )(oa, ob, x2, mod8, woa, wob, n2, rwt, rb, sg, su, sd)


def _slots_kernel(ps_ref, ids_ref, pos_ref, o_ref):
    ids = ids_ref[...]
    start = jnp.zeros(ids.shape, jnp.int32)
    for e in range(N_EXPERTS):
        start = jnp.where(ids == e, ps_ref[e], start)
    o_ref[...] = (start + pos_ref[...]) * SUBLANES


def _slots(ps, ids, pos):
    spec = pl.BlockSpec(ids.shape, lambda i, ps: (0, 0, 0))
    return pl.pallas_call(
        _slots_kernel,
        grid_spec=pltpu.PrefetchScalarGridSpec(num_scalar_prefetch=1, grid=(1,),
                                               in_specs=[spec, spec], out_specs=spec),
        out_shape=jax.ShapeDtypeStruct(ids.shape, jnp.int32),
        compiler_params=_cparams("parallel"),
        name="slots",
    )(ps, ids, pos)


def _dispatch_kernel(ps_ref, cnt_ref, slot_ref, h2t_ref, xs_hbm, zbuf, ring, zsem, sems, *, tm, blk, n_steps):
    i = pl.program_id(0)

    def zero_fills():
        def rows_at(row, n):
            return xs_hbm.at[pl.ds(pl.multiple_of(row * SUBLANES, SUBLANES), n * SUBLANES), :]

        fills = []
        for e in range(N_EXPERTS):
            first = ps_ref[e] + cnt_ref[e]
            pad = ps_ref[e + 1] - first
            size = blk // 2
            while size:
                done = pad & jnp.int32(-2 * size)
                fills.append(((pad & size) != 0, pltpu.make_async_copy(
                    zbuf.at[pl.ds(0, size * SUBLANES), :], rows_at(first + done, size), zsem)))
                size //= 2
            row = ps_ref[N_EXPERTS] + e * blk
            fills.append((row * SUBLANES < xs_hbm.shape[0], pltpu.make_async_copy(zbuf, rows_at(row, blk), zsem)))
        return fills

    @pl.when(i == 0)
    def _():
        zbuf[...] = jnp.zeros(zbuf.shape, zbuf.dtype)
        for needed, cp in zero_fills():
            pl.when(needed)(cp.start)

    def send(b):
        ring[b] = h2t_ref[...]

        def issue(t, carry):
            for k in range(TOP_K):
                pltpu.make_async_copy(_tile_row(ring.at[b], t * SUBLANES), _tile_row(xs_hbm, slot_ref[0, k, t]),
                                      sems.at[b]).start(priority=k % 2)
            return carry

        lax.fori_loop(0, tm, issue, 0)

    def drain(b):
        for k in range(TOP_K):
            pltpu.make_async_copy(ring.at[b], xs_hbm.at[pl.ds(0, tm * SUBLANES), :], sems.at[b]).wait()

    for b in range(2):
        @pl.when(i % 2 == b)
        def _(b=b):
            send(b)

            @pl.when(i > 0)
            def _():
                drain(1 - b)

    @pl.when(i == 0)
    def _():
        for needed, cp in zero_fills():
            pl.when(needed)(cp.wait)

    @pl.when(i == n_steps - 1)
    def _():
        drain((n_steps - 1) % 2)


def _dispatch(ps, counts, slot, h2t, *, n_slots, tm, blk):
    lanes = h2t.shape[1]
    n_steps = h2t.shape[0] // (tm * SUBLANES)
    grid_spec = pltpu.PrefetchScalarGridSpec(
        num_scalar_prefetch=2,
        grid=(n_steps,),
        in_specs=[pl.BlockSpec((1, TOP_K, tm), lambda i, ps, cnt: (i, 0, 0), memory_space=pltpu.SMEM),
                  pl.BlockSpec((tm * SUBLANES, lanes), lambda i, ps, cnt: (i, 0))],
        out_specs=pl.BlockSpec(memory_space=pl.ANY),
        scratch_shapes=[pltpu.VMEM((blk * SUBLANES, lanes), jnp.uint32),
                        pltpu.VMEM((2, tm * SUBLANES, lanes), jnp.uint32),
                        pltpu.SemaphoreType.DMA(()), pltpu.SemaphoreType.DMA((2,))],
    )
    return pl.pallas_call(
        functools.partial(_dispatch_kernel, tm=tm, blk=blk, n_steps=n_steps),
        grid_spec=grid_spec,
        out_shape=jax.ShapeDtypeStruct((n_slots * SUBLANES, lanes), jnp.uint32),
        compiler_params=_cparams("arbitrary"),
        name="dispatch",
    )(ps, counts, slot, h2t)


def _experts_kernel(be_ref, nu_ref, x_ref, wg_ref, wu_ref, wd_ref, y_ref):
    del be_ref
    used = pl.program_id(0) < nu_ref[0]

    @pl.when(used)
    def _():
        xb = _load_tile_rows(x_ref, 0, x_ref.shape[0] // SUBLANES).astype(BF16)
        hb = (_silu(_dot(xb, wg_ref[0])) * _dot(xb, wu_ref[0])).astype(BF16)
        _store_tile_rows(y_ref, _dot(hb, wd_ref[0]))

    @pl.when(jnp.logical_not(used))
    def _():
        y_ref[...] = jnp.zeros(y_ref.shape, y_ref.dtype)


def _experts(block_e, n_used, xs, wg, wu, wd, *, blk):
    n_blocks = block_e.shape[0]
    rows_blk = (blk * SUBLANES, xs.shape[1])
    wsel = lambda i, be, nu: (be[i], 0, 0)
    grid_spec = pltpu.PrefetchScalarGridSpec(
        num_scalar_prefetch=2,
        grid=(n_blocks,),
        in_specs=[pl.BlockSpec(rows_blk, lambda i, be, nu: (jnp.minimum(i, nu[0] - 1), 0)),
                  pl.BlockSpec((1,) + wg.shape[1:], wsel), pl.BlockSpec((1,) + wu.shape[1:], wsel),
                  pl.BlockSpec((1,) + wd.shape[1:], wsel)],
        out_specs=pl.BlockSpec(rows_blk, lambda i, be, nu: (i, 0)),
    )
    return pl.pallas_call(
        _experts_kernel,
        grid_spec=grid_spec,
        out_shape=jax.ShapeDtypeStruct(xs.shape, xs.dtype),
        compiler_params=_cparams("arbitrary"),
        name="experts",
    )(block_e, n_used, xs, wg, wu, wd)


_GROUP = 32


def _combine_kernel(slot_ref, nxt_ref, tw_ref, xs_ref, mod_ref, fn_ref, y_hbm, o_ref, gbuf0, gbuf1, wt_ref, sems, *, tm):
    gbuf = (gbuf0, gbuf1)
    j = pl.program_id(0)
    last = pl.num_programs(0) - 1
    d = xs_ref.shape[1]

    def issue(src_ref, bi, b, t):
        for k in range(TOP_K):
            pltpu.make_async_copy(_tile_row(y_hbm, src_ref[bi, k, t]), _tile_row(gbuf[b], (k * tm + t) * SUBLANES),
                                  sems.at[b]).start(priority=k % 2)

    def drain(b):
        pltpu.make_async_copy(y_hbm.at[pl.ds(0, gbuf[b].shape[0]), :], gbuf[b], sems.at[b]).wait()

    def reduce_rows(b, r0):
        w8 = wt_ref[b, pl.ds(r0, _GROUP), :]
        acc = [jnp.zeros((_GROUP, LANES), F32) for _ in range(d // LANES)]
        for k in range(TOP_K):
            wk = w8[:, k:k + 1]
            for c in range(SUBLANES):
                word = gbuf[b][pl.ds((k * tm + r0) * SUBLANES + c, _GROUP, stride=SUBLANES), :]
                acc[c] = acc[c] + wk * pltpu.bitcast(word << 16, F32)
                acc[SUBLANES + c] = acc[SUBLANES + c] + wk * pltpu.bitcast(word & jnp.uint32(0xFFFF0000), F32)
        rows = pl.ds(b * tm + r0, _GROUP)
        g2 = mod_ref[0][5:6]
        x2 = jnp.concatenate(acc, axis=1) * g2 + xs_ref[rows, :]
        o_ref[rows, :] = _rms(x2, fn_ref[...])

    def phase(fin_b, issue_args):
        def body(g, carry):
            r0 = pl.multiple_of(g * _GROUP, _GROUP)
            if issue_args is not None:
                for dt in range(_GROUP):
                    issue(*issue_args, r0 + dt)
            reduce_rows(fin_b, r0)
            return carry

        lax.fori_loop(0, tm // _GROUP, body, 0)

    @pl.when(j == 0)
    def _():
        lax.fori_loop(0, tm, lambda t, c: (issue(slot_ref, 0, 0, t), c)[1], 0)

    for b in range(2):
        wt_ref[b] = jnp.concatenate([tw_ref[b], jnp.zeros((LANES - TOP_K, tm), F32)], axis=0).T
    drain(0)
    phase(0, (slot_ref, 1, 1))
    drain(1)
    pl.when(j < last)(lambda: phase(1, (nxt_ref, 0, 0)))
    pl.when(j == last)(lambda: phase(1, None))


def _combine(slot, tw, xs, mod8, fn, y, *, seq, tm):
    n, d = xs.shape
    nb = slot.shape[0]
    assert nb % 2 == 0 and (seq // tm) % 2 == 0
    per_b = seq // (2 * tm)
    pair = lambda j: (j, 0, 0)
    return pl.pallas_call(
        functools.partial(_combine_kernel, tm=tm),
        grid=(nb // 2,),
        in_specs=[pl.BlockSpec((2, TOP_K, tm), pair, memory_space=pltpu.SMEM),
                  pl.BlockSpec((1, TOP_K, tm), lambda j: (jnp.minimum(2 * j + 2, nb - 1), 0, 0),
                               memory_space=pltpu.SMEM),
                  pl.BlockSpec((2, TOP_K, tm), pair),
                  pl.BlockSpec((2 * tm, d), lambda j: (j, 0)),
                  pl.BlockSpec((1, 8, d), lambda j: (j // per_b, 0, 0)),
                  pl.BlockSpec((1, d), lambda j: (0, 0)),
                  pl.BlockSpec(memory_space=pl.ANY)],
        out_specs=pl.BlockSpec((2 * tm, d), lambda j: (j, 0)),
        out_shape=jax.ShapeDtypeStruct((n, d), F32),
        scratch_shapes=[pltpu.VMEM((TOP_K * tm * SUBLANES, y.shape[1]), y.dtype),
                        pltpu.VMEM((TOP_K * tm * SUBLANES, y.shape[1]), y.dtype),
                        pltpu.VMEM((2, tm, LANES), F32), pltpu.SemaphoreType.DMA((2,))],
        compiler_params=_cparams("arbitrary"),
        name="combine",
    )(slot, slot, tw, xs, mod8, fn, y)


def _expert_runs(counts, n_blocks, blk):
    padded = (counts + blk - 1) // blk * blk
    pad_end = jnp.cumsum(padded)
    run_start = jnp.concatenate([pad_end - padded, pad_end[-1:]]).astype(jnp.int32)
    block_row = jnp.arange(n_blocks, dtype=jnp.int32) * blk
    block_e = jnp.minimum(jnp.sum(pad_end[None, :] <= block_row[:, None], axis=1), N_EXPERTS - 1).astype(jnp.int32)
    n_used = (pad_end[-1:] // blk).astype(jnp.int32)
    return run_start, block_e, n_used


def _pad_lanes(w, width):
    return jnp.pad(w, ((0, 0), (0, width - w.shape[1])))


def _tiles(seq):
    t = dict(tm_proj=512, tq=512, tt=256, tm_moe=512, blk=512)
    assert all(seq % v == 0 for k, v in t.items() if k != "blk")
    assert seq % (_Q_TILES * t["tq"]) == 0 and (seq // t["tm_moe"]) % 2 == 0
    return t


def _layer(x, c, positions, ada_w, ada_b, norm1_w, w_in, q_a_norm_w, wq_b, kv_a_norm_w, wkv_b,
           gdn_conv_w, gdn_a_log, gdn_dt_bias, gdn_norm_w, w_out, norm2_w, router_w, router_bias,
           exp_w_gate, exp_w_up, exp_w_down, sh_w_gate, sh_w_up, sh_w_down, final_norm_w,
           *, tm_proj, tq, tt, tm_moe, blk):
    batch, seq, d = x.shape
    n = batch * seq
    x2 = x.reshape(n, d)

    c8 = jnp.zeros((8, d), F32).at[:batch].set(c)
    mod = _adaln(c8, ada_w, ada_b.reshape(1, -1))
    mod8 = jnp.pad(mod[:batch].reshape(batch, 6, d), ((0, 0), (0, 2), (0, 0)))

    hq = MLA_NOPE + MLA_ROPE
    w_cq, w_ckv, w_kpe, w_qkv, w_z, w_a, w_b = jnp.split(
        w_in, [_C_KV, _C_KV + MLA_KV_RANK, _C_KV + MLA_KV_RANK + MLA_ROPE,
               _C_KV + MLA_KV_RANK + MLA_ROPE + _GDN_QKV,
               _C_KV + MLA_KV_RANK + MLA_ROPE + _GDN_QKV + GDN_HEADS * GDN_DV,
               _C_KV + MLA_KV_RANK + MLA_ROPE + _GDN_QKV + GDN_HEADS * GDN_DV + GDN_HEADS], axis=1)
    win = jnp.concatenate([w_cq, w_ckv, _pad_lanes(w_kpe, LANES), w_qkv, w_z,
                           _pad_lanes(jnp.concatenate([w_a, w_b], axis=1), LANES)], axis=1).astype(BF16)
    wq = jnp.pad(wq_b.reshape(MLA_Q_RANK, MLA_HEADS, hq),
                 ((0, 0), (0, 0), (0, MLA_QK_PAD - hq))).reshape(MLA_Q_RANK, MLA_HEADS * MLA_QK_PAD).astype(BF16)
    wkv4 = wkv_b.reshape(MLA_KV_RANK, MLA_HEADS, MLA_NOPE + MLA_V)
    wkv = jnp.concatenate([wkv4[:, :, :MLA_NOPE].reshape(MLA_KV_RANK, -1),
                           wkv4[:, :, MLA_NOPE:].reshape(MLA_KV_RANK, -1)], axis=1).astype(BF16)
    inv_freq = 1.0 / (ROPE_THETA ** (jnp.arange(0, MLA_ROPE, 2, dtype=F32) / MLA_ROPE))
    invf = _pad_lanes(jnp.concatenate([inv_freq, inv_freq])[None, :], LANES)

    q, k, v, qkv, z, ab = _proj(x2, mod8, positions.reshape(n, 1), norm1_w.reshape(1, d), win,
                                q_a_norm_w.reshape(1, -1), wq, kv_a_norm_w.reshape(1, -1), wkv, invf,
                                seq=seq, tm=tm_proj)
    out_a = _attn(q, k, v, batch=batch, seq=seq, tq=tq)

    cw8 = jnp.pad(gdn_conv_w, ((0, 8 - GDN_CONV), (0, 0)))
    hp = _pad_lanes(jnp.stack([gdn_a_log, gdn_dt_bias]), LANES)
    hp = jnp.pad(hp, ((0, 6), (0, 0)))
    out_b, wg_b, wu_b, wd_b = _gdn(qkv, z, ab, cw8, hp, gdn_norm_w.reshape(1, -1),
                                   (exp_w_gate, exp_w_up, exp_w_down), batch=batch, seq=seq, tt=tt)

    ha = MLA_HEADS * MLA_V
    xs, h2t, ids, tw, pos, counts = _mix(
        out_a, out_b, x2, mod8, w_out[:ha].astype(BF16), w_out[ha:].astype(BF16), norm2_w.reshape(1, d),
        router_w.T, router_bias.reshape(-1, 1), sh_w_gate.astype(BF16), sh_w_up.astype(BF16),
        sh_w_down.astype(BF16), seq=seq, tm=tm_moe)

    n_blocks = n * TOP_K // blk + N_EXPERTS
    counts = counts[:, 0].astype(jnp.int32)
    run_start, block_e, n_used = _expert_runs(counts, n_blocks, blk)
    slot = _slots(run_start, ids, pos)
    xsort = _dispatch(run_start, counts, slot, h2t, n_slots=n_blocks * blk, tm=tm_moe, blk=blk)
    y = _experts(block_e, n_used, xsort, wg_b, wu_b, wd_b, blk=blk)
    out = _combine(slot, tw, xs, mod8, final_norm_w.reshape(1, d), y, seq=seq, tm=tm_moe)
    return out.reshape(batch, seq, d)


def kernel(x, c, positions, ada_w, ada_b, norm1_w, w_in, q_a_norm_w, wq_b, kv_a_norm_w, wkv_b, gdn_conv_w,
           gdn_a_log, gdn_dt_bias, gdn_norm_w, w_out, norm2_w, router_w, router_bias, exp_w_gate, exp_w_up,
           exp_w_down, sh_w_gate, sh_w_up, sh_w_down, final_norm_w):
    assert ada_w.shape[0] == 1, "single layer"
    return _layer(
        x, c, positions, ada_w[0], ada_b[0], norm1_w[0], w_in[0], q_a_norm_w[0], wq_b[0], kv_a_norm_w[0],
        wkv_b[0], gdn_conv_w[0], gdn_a_log[0], gdn_dt_bias[0], gdn_norm_w[0], w_out[0], norm2_w[0],
        router_w[0], router_bias[0], exp_w_gate[0], exp_w_up[0], exp_w_down[0], sh_w_gate[0], sh_w_up[0],
        sh_w_down[0], final_norm_w, **_tiles(x.shape[1]))
```
